```python
import math, functools
import jax, jax.numpy as jnp
from jax import lax
import numpy as np

D_MODEL = 1024
BATCH = 4
SEQ = 4096
DEPTH = 2

CTX_LEN = 256
GRID_W = 64
HEAD_DIM = 64
N_DIFF_HEADS = 8
DIFF_V_DIM = 2 * HEAD_DIM
N_GQA_HEADS = 16
N_GQA_KV = 4
GQA_REP = N_GQA_HEADS // N_GQA_KV
D_FF = 2816
N_EXPERTS = 8
TOP_K = 2
D_FF_EXPERT = 3584
Q_BLOCK = 128
ROPE_THETA = 10000.0
EPS = 1e-6

DIFF_Q_W = N_DIFF_HEADS * 2 * HEAD_DIM
GQA_Q_W = N_GQA_HEADS * HEAD_DIM
DIFF_K_W = N_DIFF_HEADS * 2 * HEAD_DIM
DIFF_V_W = N_DIFF_HEADS * DIFF_V_DIM
GQA_K_W = N_GQA_KV * HEAD_DIM
GQA_V_W = N_GQA_KV * HEAD_DIM
GATE_W = 2 * D_MODEL
Q_END = DIFF_Q_W + GQA_Q_W
KV_END = Q_END + DIFF_K_W + DIFF_V_W + GQA_K_W + GQA_V_W
IN_W = KV_END + GATE_W

kernel_name = "hybrid_diff_gqa_prefix_dit"


def rms_norm(x, g):
    xf = x.astype(jnp.float32)
    y = xf * lax.rsqrt(jnp.mean(xf * xf, axis=-1, keepdims=True) + EPS)
    return (y * g.astype(jnp.float32)).astype(x.dtype)


def modulate(x, g, shift, scale):
    return rms_norm(x, g) * (1.0 + scale) + shift


def axial_rope_tables(n_tok):
    rows = n_tok // GRID_W
    row = jnp.repeat(jnp.arange(rows, dtype=jnp.float32), GRID_W)
    col = jnp.tile(jnp.arange(GRID_W, dtype=jnp.float32), rows)
    half = HEAD_DIM // 2
    inv_freq = ROPE_THETA ** (-jnp.arange(0, half, 2, dtype=jnp.float32) / half)
    ang = jnp.concatenate([row[:, None] * inv_freq, col[:, None] * inv_freq], axis=-1)
    return jnp.cos(ang), jnp.sin(ang)


def apply_rope(x, cos, sin):
    xf = x.astype(jnp.float32).reshape(*x.shape[:-1], x.shape[-1] // 2, 2)
    x0, x1 = xf[..., 0], xf[..., 1]
    out = jnp.stack([x0 * cos - x1 * sin, x0 * sin + x1 * cos], axis=-1)
    return out.reshape(x.shape).astype(x.dtype)


def split_queries(p_q, q_norm, rope):
    B, T = p_q.shape[:2]
    dq = p_q[..., :DIFF_Q_W].reshape(B, T, N_DIFF_HEADS, 2, HEAD_DIM)
    gq = rms_norm(p_q[..., DIFF_Q_W:Q_END].reshape(B, T, N_GQA_HEADS, HEAD_DIM), q_norm)
    if rope is not None:
        cos, sin = rope
        dq = apply_rope(dq, cos[:, None, None, :], sin[:, None, None, :])
        gq = apply_rope(gq, cos[:, None, :], sin[:, None, :])
    return dq, gq.reshape(B, T, N_GQA_KV, GQA_REP, HEAD_DIM)


def split_keys_values(p_kv, k_norm, rope):
    B, T = p_kv.shape[:2]
    o1 = DIFF_K_W
    o2 = o1 + DIFF_V_W
    o3 = o2 + GQA_K_W
    dk = p_kv[..., :o1].reshape(B, T, N_DIFF_HEADS, 2, HEAD_DIM)
    dv = p_kv[..., o1:o2].reshape(B, T, N_DIFF_HEADS, DIFF_V_DIM)
    gk = rms_norm(p_kv[..., o2:o3].reshape(B, T, N_GQA_KV, HEAD_DIM), k_norm)
    gv = p_kv[..., o3:].reshape(B, T, N_GQA_KV, HEAD_DIM)
    if rope is not None:
        cos, sin = rope
        dk = apply_rope(dk, cos[:, None, None, :], sin[:, None, None, :])
        gk = apply_rope(gk, cos[:, None, :], sin[:, None, :])
    return dk, dv, gk, gv


def diff_attention(q, k, v, lam):
    s = jnp.einsum('bqhcd,bkhcd->bhcqk', q, k, preferred_element_type=jnp.float32) * (HEAD_DIM ** -0.5)
    p = jax.nn.softmax(s, axis=-1)
    w = p[:, :, 0] - lam * p[:, :, 1]
    return jnp.einsum('bhqk,bkhe->bqhe', w, v.astype(jnp.float32)).astype(v.dtype)


def gqa_attention(q, k, v):
    s = jnp.einsum('bqgrd,bkgd->bgrqk', q, k, preferred_element_type=jnp.float32) * (HEAD_DIM ** -0.5)
    p = jax.nn.softmax(s, axis=-1)
    return jnp.einsum('bgrqk,bkgd->bqgrd', p, v.astype(jnp.float32)).astype(v.dtype)


def sweep_query_blocks(attn, q, *kv):
    B, S = q.shape[:2]
    nb = S // Q_BLOCK
    qb = jnp.moveaxis(q.reshape(B, nb, Q_BLOCK, *q.shape[2:]), 1, 0)
    ob = lax.map(lambda qi: attn(qi, *kv), qb)
    return jnp.moveaxis(ob, 0, 1).reshape(B, S, *ob.shape[3:])


def mixer_output(y_diff, y_gqa, gates, lam_init, subln_g, w_pd, w_pg, w_o):
    B, T = y_diff.shape[:2]
    yd = (rms_norm(y_diff, subln_g) * (1.0 - lam_init)).reshape(B, T, DIFF_V_W)
    yg = y_gqa.reshape(B, T, GQA_Q_W)
    m = jax.nn.sigmoid(gates[..., :D_MODEL]) * (yd @ w_pd) + jax.nn.sigmoid(gates[..., D_MODEL:]) * (yg @ w_pg)
    return m @ w_o


def swiglu(t, w1, w3, w2):
    return (jax.nn.silu(t @ w1) * (t @ w3)) @ w2


def moe_swiglu(h, router, w1, w3, w2):
    B, S, D = h.shape
    t = h.reshape(-1, D)
    logits = jnp.dot(t, router, preferred_element_type=jnp.float32)
    top_v, top_i = lax.top_k(logits, TOP_K)
    top_w = jax.nn.softmax(top_v, axis=-1)
    gate = jnp.sum(jax.nn.one_hot(top_i, N_EXPERTS, dtype=jnp.float32) * top_w[..., None], axis=1)
    out = jnp.zeros(t.shape, jnp.float32)
    for e in range(N_EXPERTS):
        out = out + gate[:, e:e + 1] * swiglu(t, w1[e], w3[e], w2[e]).astype(jnp.float32)
    return out.astype(h.dtype).reshape(B, S, D)


def setup_inputs(seed: int = 0) -> dict:
    key = jax.random.key(seed)
    ks = jax.random.split(key, 32)
    n_dense = (DEPTH + 1) // 2
    n_moe = DEPTH // 2
    f32 = jnp.float32

    def nrm(k, shape, scale):
        return jax.random.normal(k, shape, f32) * scale

    return {
        'x': nrm(ks[0], (BATCH, SEQ, D_MODEL), 1.0),
        'c': nrm(ks[1], (BATCH, D_MODEL), 1.0),
        'ctx': nrm(ks[2], (BATCH, CTX_LEN, D_MODEL), 1.0),
        'c_ctx': nrm(ks[3], (D_MODEL,), 1.0),
        'ada_w': nrm(ks[4], (DEPTH, D_MODEL, 6 * D_MODEL), D_MODEL ** -0.5),
        'ada_b': nrm(ks[5], (DEPTH, 6 * D_MODEL), 0.02),
        'norm_attn_g': 1.0 + nrm(ks[6], (DEPTH, D_MODEL), 0.1),
        'norm_ffn_g': 1.0 + nrm(ks[7], (DEPTH, D_MODEL), 0.1),
        'w_in': nrm(ks[8], (DEPTH, D_MODEL, IN_W), D_MODEL ** -0.5),
        'q_norm_g': 1.0 + nrm(ks[9], (DEPTH, HEAD_DIM), 0.1),
        'k_norm_g': 1.0 + nrm(ks[10], (DEPTH, HEAD_DIM), 0.1),
        'diff_lambda': nrm(ks[11], (DEPTH, 4, HEAD_DIM), 0.1),
        'diff_subln_g': 1.0 + nrm(ks[12], (DEPTH, DIFF_V_DIM), 0.1),
        'w_proj_diff': nrm(ks[13], (DEPTH, DIFF_V_W, D_MODEL), DIFF_V_W ** -0.5),
        'w_proj_gqa': nrm(ks[14], (DEPTH, GQA_Q_W, D_MODEL), GQA_Q_W ** -0.5),
        'w_out': nrm(ks[15], (DEPTH, D_MODEL, D_MODEL), D_MODEL ** -0.5),
        'ffn_w1': nrm(ks[16], (n_dense, D_MODEL, D_FF), D_MODEL ** -0.5),
        'ffn_w3': nrm(ks[17], (n_dense, D_MODEL, D_FF), D_MODEL ** -0.5),
        'ffn_w2': nrm(ks[18], (n_dense, D_FF, D_MODEL), D_FF ** -0.5),
        'moe_router': nrm(ks[19], (n_moe, D_MODEL, N_EXPERTS), D_MODEL ** -0.5),
        'moe_w1': nrm(ks[20], (n_moe, N_EXPERTS, D_MODEL, D_FF_EXPERT), D_MODEL ** -0.5),
        'moe_w3': nrm(ks[21], (n_moe, N_EXPERTS, D_MODEL, D_FF_EXPERT), D_MODEL ** -0.5),
        'moe_w2': nrm(ks[22], (n_moe, N_EXPERTS, D_FF_EXPERT, D_MODEL), D_FF_EXPERT ** -0.5),
        'final_norm_g': 1.0 + nrm(ks[23], (D_MODEL,), 0.1),
    }


def reference(x, c, ctx, c_ctx, ada_w, ada_b, norm_attn_g, norm_ffn_g, w_in, q_norm_g, k_norm_g,
              diff_lambda, diff_subln_g, w_proj_diff, w_proj_gqa, w_out, ffn_w1, ffn_w3, ffn_w2,
              moe_router, moe_w1, moe_w3, moe_w2, final_norm_g):
    B, S, _ = x.shape
    rope = axial_rope_tables(S)
    silu_c = jax.nn.silu(c)
    silu_cc = jax.nn.silu(c_ctx)
    xc = ctx

    for l in range(DEPTH):
        last = l == DEPTH - 1
        sh1, sc1, gt1, sh2, sc2, gt2 = jnp.split((silu_c @ ada_w[l] + ada_b[l])[:, None, :], 6, axis=-1)
        csh1, csc1, cgt1, csh2, csc2, cgt2 = jnp.split(silu_cc @ ada_w[l] + ada_b[l], 6, axis=-1)

        lam_init = 0.8 - 0.6 * math.exp(-0.3 * l)
        lq1, lk1, lq2, lk2 = diff_lambda[l]
        lam = (jnp.exp(jnp.sum(lq1 * lk1).astype(jnp.float32))
               - jnp.exp(jnp.sum(lq2 * lk2).astype(jnp.float32)) + lam_init)

        h = modulate(x, norm_attn_g[l], sh1, sc1)
        hc = modulate(xc, norm_attn_g[l], csh1, csc1)
        p = h @ w_in[l]
        if last:
            pc_kv = hc @ w_in[l, :, Q_END:KV_END]
        else:
            pc = hc @ w_in[l]
            pc_kv = pc[..., Q_END:KV_END]

        dq, gq = split_queries(p[..., :Q_END], q_norm_g[l], rope)
        dk, dv, gk, gv = split_keys_values(p[..., Q_END:KV_END], k_norm_g[l], rope)
        cdk, cdv, cgk, cgv = split_keys_values(pc_kv, k_norm_g[l], None)

        dk_all = jnp.concatenate([dk, cdk], axis=1)
        dv_all = jnp.concatenate([dv, cdv], axis=1)
        gk_all = jnp.concatenate([gk, cgk], axis=1)
        gv_all = jnp.concatenate([gv, cgv], axis=1)
        y_diff = sweep_query_blocks(functools.partial(diff_attention, lam=lam), dq, dk_all, dv_all)
        y_gqa = sweep_query_blocks(gqa_attention, gq, gk_all, gv_all)
        x = x + gt1 * mixer_output(y_diff, y_gqa, p[..., KV_END:], lam_init, diff_subln_g[l],
                                   w_proj_diff[l], w_proj_gqa[l], w_out[l])

        if not last:
            cdq, cgq = split_queries(pc[..., :Q_END], q_norm_g[l], None)
            yc_diff = diff_attention(cdq, cdk, cdv, lam)
            yc_gqa = gqa_attention(cgq, cgk, cgv)
            xc = xc + cgt1 * mixer_output(yc_diff, yc_gqa, pc[..., KV_END:], lam_init, diff_subln_g[l],
                                          w_proj_diff[l], w_proj_gqa[l], w_out[l])

        h2 = modulate(x, norm_ffn_g[l], sh2, sc2)
        if l % 2 == 0:
            i = l // 2
            x = x + gt2 * swiglu(h2, ffn_w1[i], ffn_w3[i], ffn_w2[i])
            if not last:
                hc2 = modulate(xc, norm_ffn_g[l], csh2, csc2)
                xc = xc + cgt2 * swiglu(hc2, ffn_w1[i], ffn_w3[i], ffn_w2[i])
        else:
            i = l // 2
            x = x + gt2 * moe_swiglu(h2, moe_router[i], moe_w1[i], moe_w3[i], moe_w2[i])
            if not last:
                hc2 = modulate(xc, norm_ffn_g[l], csh2, csc2)
                xc = xc + cgt2 * moe_swiglu(hc2, moe_router[i], moe_w1[i], moe_w3[i], moe_w2[i])

    return rms_norm(x, final_norm_g)
```

```python
import functools
import math
from typing import NamedTuple

import numpy as np
import jax
import jax.numpy as jnp
from jax import lax
from jax.experimental import pallas as pl
from jax.experimental.pallas import tpu as pltpu

F32 = jnp.float32
BF16 = jnp.bfloat16

D_MODEL = 1024
HEAD_DIM = 64
N_DIFF_HEADS = 8
N_GQA_HEADS = 16
N_GQA_KV = 4
GQA_REP = N_GQA_HEADS // N_GQA_KV
N_EXPERTS = 8
GRID_W = 64
ROPE_THETA = 10000.0
EPS = 1e-6
N_MODS = 6
MOD_ROWS = 16
NEG_BIG = -1e30

GATE_OFF = 0
DQ_OFF = 2048
GQ_OFF = 3072
DK_OFF = 4096
DV_OFF = 5120
GKV_OFF = 6144
IN_W = 6656
PROJ_TN = 512
ROPE_TILES = (4, 5, 8, 9)
QNORM_TILES = (6, 7)
KV_TILE = 12

VMEM_LIMIT = 52 * 1024 * 1024


class Tiles(NamedTuple):
    tm_proj: int
    tq: int
    tk: int
    tm_mix: int
    tm_ffn: int
    tf_ffn: int
    tf_moe: int


def _largest_divisor(n, candidates):
    for c in candidates:
        if n % c == 0:
            return c
    raise ValueError(f"no tile in {candidates} divides {n}")


def _pick_tiles(B, S, C, d_ff, d_ff_e):
    rows_common = math.gcd(S, B * C)
    tm = _largest_divisor(rows_common, (512, 256, 128))
    return Tiles(
        tm_proj=tm,
        tq=C,
        tk=_largest_divisor(S, (512, 256, 128)),
        tm_mix=tm,
        tm_ffn=_largest_divisor(rows_common, (1024, 512, 256, 128)),
        tf_ffn=_largest_divisor(d_ff, (1408, 1024, 512, 256, 128)),
        tf_moe=_largest_divisor(d_ff_e, (896, 512, 256, 128)),
    )


def _split_bf16(v):
    hi = v.astype(BF16)
    lo = (v - hi.astype(F32)).astype(BF16)
    return hi, lo


def _ada_kernel(c_ref, w_ref, b_ref, o_ref):
    c = c_ref[...]
    s = c / (1.0 + jnp.exp(-c))
    s_hi, s_lo = _split_bf16(s)
    w_hi, w_lo = _split_bf16(w_ref[...])
    acc = jnp.dot(s_hi, w_hi, preferred_element_type=F32)
    acc += jnp.dot(s_lo, w_hi, preferred_element_type=F32)
    acc += jnp.dot(s_hi, w_lo, preferred_element_type=F32)
    o_ref[...] = acc + b_ref[...]


def _ada_mods(cvec, ada_w, ada_b):
    depth, d, n = ada_w.shape
    tn = _largest_divisor(n, (1536, 1024, 512))
    return pl.pallas_call(
        _ada_kernel,
        grid=(depth, n // tn),
        in_specs=[
            pl.BlockSpec((MOD_ROWS, d), lambda l, j: (0, 0)),
            pl.BlockSpec((None, d, tn), lambda l, j: (l, 0, j)),
            pl.BlockSpec((None, 1, tn), lambda l, j: (l, 0, j)),
        ],
        out_specs=pl.BlockSpec((None, MOD_ROWS, tn), lambda l, j: (l, 0, j)),
        out_shape=jax.ShapeDtypeStruct((depth, MOD_ROWS, n), F32),
        compiler_params=pltpu.CompilerParams(
            dimension_semantics=("arbitrary", "arbitrary"), vmem_limit_bytes=VMEM_LIMIT),
        name="ada_mods",
    )(cvec, ada_w, ada_b.reshape(depth, 1, n))


def _modulated_norm(x, g, scale, shift):
    ms = jnp.mean(x * x, axis=-1, keepdims=True)
    return (x * lax.rsqrt(ms + EPS) * g) * (1.0 + scale) + shift


def _silu(a):
    return a / (1.0 + jnp.exp(-a))


def _mod_spec(k, row_to_mod):
    return pl.BlockSpec((None, 1, D_MODEL), lambda i, *_: (row_to_mod(i) * N_MODS + k, 0, 0))


def _rope(z, cos, sin_signed, lane):
    first_half = (lane & (HEAD_DIM // 2)) == 0
    n = z.shape[1]
    partner = jnp.where(first_half, pltpu.roll(z, n - HEAD_DIM // 2, 1), pltpu.roll(z, HEAD_DIM // 2, 1))
    return z * cos + partner * sin_signed


def _head_rms(z, ones_ref, gain):
    zz_hi, zz_lo = _split_bf16(z * z)
    ss = jnp.dot(zz_hi, ones_ref[...], preferred_element_type=F32)
    ss += jnp.dot(zz_lo, ones_ref[...], preferred_element_type=F32)
    return z * lax.rsqrt(ss * (1.0 / HEAD_DIM) + EPS) * gain


def _proj_kernel(x_ref, g_ref, sc_ref, sh_ref, w_ref, cos_ref, sin_ref, ones_ref, qg_ref, kg_ref,
                 o_ref, h_scr):
    j = pl.program_id(1)

    @pl.when(j == 0)
    def _():
        h_scr[...] = _modulated_norm(x_ref[...], g_ref[...], sc_ref[...], sh_ref[...]).astype(BF16)

    z = jnp.dot(h_scr[...], w_ref[...], preferred_element_type=F32)
    lane = lax.broadcasted_iota(jnp.int32, z.shape, 1)
    is_rope = functools.reduce(jnp.logical_or, [j == t for t in ROPE_TILES])
    is_qnorm = functools.reduce(jnp.logical_or, [j == t for t in QNORM_TILES])
    is_kv = j == KV_TILE
    is_plain = jnp.logical_not(is_rope | is_qnorm | is_kv)

    @pl.when(is_plain)
    def _():
        o_ref[...] = z.astype(BF16)

    @pl.when(is_rope)
    def _():
        o_ref[...] = _rope(z, cos_ref[...], sin_ref[...], lane).astype(BF16)

    @pl.when(is_qnorm)
    def _():
        zn = _head_rms(z, ones_ref, qg_ref[...])
        o_ref[...] = _rope(zn, cos_ref[...], sin_ref[...], lane).astype(BF16)

    @pl.when(is_kv)
    def _():
        zn = _head_rms(z, ones_ref, kg_ref[...])
        zr = _rope(zn, cos_ref[...], sin_ref[...], lane)
        is_key_lane = (lane & HEAD_DIM) == 0
        o_ref[...] = jnp.where(is_key_lane, zr, z).astype(BF16)


def _project(xs, mods, norm_g, w, cos_t, sin_t, ones_bd, qg, kg, *, tm, lat_tiles, tiles_per_seq, n_batch):
    rows = xs.shape[0]
    row_to_mod = lambda i: jnp.minimum(i // tiles_per_seq, n_batch)
    rope_row = lambda i: jnp.where(i < lat_tiles, i % tiles_per_seq, tiles_per_seq)
    const = lambda i, j: (0, 0)
    return pl.pallas_call(
        _proj_kernel,
        grid=(rows // tm, IN_W // PROJ_TN),
        in_specs=[
            pl.BlockSpec((tm, D_MODEL), lambda i, j: (i, 0)),
            pl.BlockSpec((1, D_MODEL), const),
            _mod_spec(1, row_to_mod),
            _mod_spec(0, row_to_mod),
            pl.BlockSpec((D_MODEL, PROJ_TN), lambda i, j: (0, j)),
            pl.BlockSpec((tm, PROJ_TN), lambda i, j: (rope_row(i), 0)),
            pl.BlockSpec((tm, PROJ_TN), lambda i, j: (rope_row(i), 0)),
            pl.BlockSpec((PROJ_TN, PROJ_TN), const),
            pl.BlockSpec((1, PROJ_TN), const),
            pl.BlockSpec((1, PROJ_TN), const),
        ],
        out_specs=pl.BlockSpec((tm, PROJ_TN), lambda i, j: (i, j)),
        out_shape=jax.ShapeDtypeStruct((rows, IN_W), BF16),
        scratch_shapes=[pltpu.VMEM((tm, D_MODEL), BF16)],
        compiler_params=pltpu.CompilerParams(
            dimension_semantics=("arbitrary", "arbitrary"), vmem_limit_bytes=VMEM_LIMIT),
        name="in_proj",
    )(xs, norm_g, mods, mods, w, cos_t, sin_t, ones_bd, qg, kg)


def _online_softmax_step(q, k, v, m, l, acc):
    s = lax.dot_general(q, k, (((1,), (1,)), ((), ())), preferred_element_type=F32)
    m_new = jnp.maximum(m, jnp.max(s, axis=-1, keepdims=True))
    alpha = jnp.exp(m - m_new)
    p = jnp.exp(s - m_new)
    l_new = alpha * l + jnp.sum(p, axis=-1, keepdims=True)
    acc_new = alpha * acc + jnp.dot(p.astype(BF16), v, preferred_element_type=F32)
    return m_new, l_new, acc_new


def _softmax_init(tq, dv):
    return (jnp.full((tq, 1), NEG_BIG, F32), jnp.zeros((tq, 1), F32), jnp.zeros((tq, dv), F32))


def _latent_chunks(nq_lat, n_chunks, ctx_queries):
    if not ctx_queries:
        return n_chunks
    return jnp.where(pl.program_id(2) < nq_lat, n_chunks, 0)


def _diff_attn_kernel(lam_ref, q_ref, kl_ref, vl_ref, kc_ref, vc_ref, g_ref, o_ref,
                      *, nq_lat, n_chunks, tk, ctx_queries):
    q = q_ref[...]
    q1, q2 = q[:, :HEAD_DIM], q[:, HEAD_DIM:]
    tq = q.shape[0]

    def step(k, v, carry):
        s1 = _online_softmax_step(q1, k[:, :HEAD_DIM], v, *carry[:3])
        s2 = _online_softmax_step(q2, k[:, HEAD_DIM:], v, *carry[3:])
        return s1 + s2

    def body(c, carry):
        off = pl.multiple_of(c * tk, tk)
        return step(kl_ref[pl.ds(off, tk), :], vl_ref[pl.ds(off, tk), :], carry)

    init = _softmax_init(tq, 2 * HEAD_DIM) * 2
    carry = lax.fori_loop(0, _latent_chunks(nq_lat, n_chunks, ctx_queries), body, init)
    _, l1, a1, _, l2, a2 = step(kc_ref[...], vc_ref[...], carry)
    y = a1 / l1 - lam_ref[0] * (a2 / l2)
    ms = jnp.mean(y * y, axis=-1, keepdims=True)
    o_ref[...] = (y * lax.rsqrt(ms + EPS) * g_ref[...]).astype(BF16)


def _gqa_attn_kernel(q_ref, kvl_ref, kvc_ref, o_ref, *, nq_lat, n_chunks, tk, ctx_queries):
    q = q_ref[...]
    tq = q.shape[0]
    qs = [q[:, r * HEAD_DIM:(r + 1) * HEAD_DIM] for r in range(GQA_REP)]

    def step(kv, carry):
        k, v = kv[:, :HEAD_DIM], kv[:, HEAD_DIM:]
        out = ()
        for r in range(GQA_REP):
            out += _online_softmax_step(qs[r], k, v, *carry[3 * r:3 * r + 3])
        return out

    def body(c, carry):
        off = pl.multiple_of(c * tk, tk)
        return step(kvl_ref[pl.ds(off, tk), :], carry)

    init = _softmax_init(tq, HEAD_DIM) * GQA_REP
    carry = lax.fori_loop(0, _latent_chunks(nq_lat, n_chunks, ctx_queries), body, init)
    carry = step(kvc_ref[...], carry)
    for r in range(GQA_REP):
        _, l, a = carry[3 * r:3 * r + 3]
        o_ref[:, r * HEAD_DIM:(r + 1) * HEAD_DIM] = (a / l).astype(BF16)


def _attention(p, lam, subln_gain, *, B, S, C, tiles, ctx_queries):
    tq, tk = tiles.tq, tiles.tk
    nq_lat = S // tq
    nq = nq_lat + (1 if ctx_queries else 0)
    ctx_blk0 = (B * S) // C
    rows_out = B * S + (B * C if ctx_queries else 0)
    statics = dict(nq_lat=nq_lat, n_chunks=S // tk, tk=tk, ctx_queries=ctx_queries)

    def q_row(b, qi):
        if not ctx_queries:
            return b * nq_lat + qi
        return jnp.where(qi < nq_lat, b * nq_lat + qi, ctx_blk0 + b)

    cparams = pltpu.CompilerParams(
        dimension_semantics=("arbitrary", "arbitrary", "arbitrary"), vmem_limit_bytes=VMEM_LIMIT)
    w2 = 2 * HEAD_DIM
    yd = pl.pallas_call(
        functools.partial(_diff_attn_kernel, **statics),
        grid=(B, N_DIFF_HEADS, nq),
        in_specs=[
            pl.BlockSpec(memory_space=pltpu.SMEM),
            pl.BlockSpec((tq, w2), lambda b, h, qi: (q_row(b, qi), DQ_OFF // w2 + h)),
            pl.BlockSpec((S, w2), lambda b, h, qi: (b, DK_OFF // w2 + h)),
            pl.BlockSpec((S, w2), lambda b, h, qi: (b, DV_OFF // w2 + h)),
            pl.BlockSpec((C, w2), lambda b, h, qi: (ctx_blk0 + b, DK_OFF // w2 + h)),
            pl.BlockSpec((C, w2), lambda b, h, qi: (ctx_blk0 + b, DV_OFF // w2 + h)),
            pl.BlockSpec((1, w2), lambda b, h, qi: (0, 0)),
        ],
        out_specs=pl.BlockSpec((tq, w2), lambda b, h, qi: (q_row(b, qi), h)),
        out_shape=jax.ShapeDtypeStruct((rows_out, N_DIFF_HEADS * w2), BF16),
        compiler_params=cparams,
        name="diff_attn",
    )(lam, p, p, p, p, p, subln_gain)

    wq = GQA_REP * HEAD_DIM
    yg = pl.pallas_call(
        functools.partial(_gqa_attn_kernel, **statics),
        grid=(B, N_GQA_KV, nq),
        in_specs=[
            pl.BlockSpec((tq, wq), lambda b, g, qi: (q_row(b, qi), GQ_OFF // wq + g)),
            pl.BlockSpec((S, w2), lambda b, g, qi: (b, GKV_OFF // w2 + g)),
            pl.BlockSpec((C, w2), lambda b, g, qi: (ctx_blk0 + b, GKV_OFF // w2 + g)),
        ],
        out_specs=pl.BlockSpec((tq, wq), lambda b, g, qi: (q_row(b, qi), g)),
        out_shape=jax.ShapeDtypeStruct((rows_out, N_GQA_HEADS * HEAD_DIM), BF16),
        compiler_params=cparams,
        name="gqa_attn",
    )(p, p, p)
    return yd, yg


def _mixer_kernel(x_ref, yd_ref, yg_ref, gd_ref, gg_ref, gt_ref, wpd_ref, wpg_ref, wo_ref, o_ref):
    pd = jnp.dot(yd_ref[...], wpd_ref[...], preferred_element_type=F32)
    pg = jnp.dot(yg_ref[...], wpg_ref[...], preferred_element_type=F32)
    m = jax.nn.sigmoid(gd_ref[...].astype(F32)) * pd + jax.nn.sigmoid(gg_ref[...].astype(F32)) * pg
    o_ref[...] = x_ref[...] + gt_ref[...] * jnp.dot(m.astype(BF16), wo_ref[...], preferred_element_type=F32)


def _mixer_out(xs, yd, yg, p, mods, wpd, wpg, wo, *, rows, tm, tiles_per_seq, n_batch):
    row_to_mod = lambda i: jnp.minimum(i // tiles_per_seq, n_batch)
    row_tile = pl.BlockSpec((tm, D_MODEL), lambda i: (i, 0))
    weight = pl.BlockSpec((D_MODEL, D_MODEL), lambda i: (0, 0))
    return pl.pallas_call(
        _mixer_kernel,
        grid=(rows // tm,),
        in_specs=[
            row_tile, row_tile, row_tile,
            pl.BlockSpec((tm, D_MODEL), lambda i: (i, GATE_OFF // D_MODEL)),
            pl.BlockSpec((tm, D_MODEL), lambda i: (i, GATE_OFF // D_MODEL + 1)),
            _mod_spec(2, row_to_mod),
            weight, weight, weight,
        ],
        out_specs=row_tile,
        out_shape=jax.ShapeDtypeStruct((rows, D_MODEL), F32),
        compiler_params=pltpu.CompilerParams(
            dimension_semantics=("arbitrary",), vmem_limit_bytes=VMEM_LIMIT),
        name="mixer_out",
    )(xs, yd, yg, p, p, mods, wpd, wpg, wo)


def _ffn_kernel(x_ref, g_ref, sc_ref, sh_ref, gt_ref, w1_ref, w3_ref, w2_ref, o_ref, h_scr, acc_scr):
    f = pl.program_id(1)

    @pl.when(f == 0)
    def _():
        h_scr[...] = _modulated_norm(x_ref[...], g_ref[...], sc_ref[...], sh_ref[...]).astype(BF16)
        acc_scr[...] = jnp.zeros_like(acc_scr)

    h = h_scr[...]
    a = jnp.dot(h, w1_ref[...], preferred_element_type=F32)
    b = jnp.dot(h, w3_ref[...], preferred_element_type=F32)
    acc_scr[...] += jnp.dot((_silu(a) * b).astype(BF16), w2_ref[...], preferred_element_type=F32)

    @pl.when(f == pl.num_programs(1) - 1)
    def _():
        o_ref[...] = x_ref[...] + gt_ref[...] * acc_scr[...]


def _ffn(xs, mods, norm_g, w1, w3, w2, *, tm, tf, tiles_per_seq, n_batch):
    rows = xs.shape[0]
    d_ff = w1.shape[1]
    row_to_mod = lambda i: jnp.minimum(i // tiles_per_seq, n_batch)
    row_tile = pl.BlockSpec((tm, D_MODEL), lambda i, f: (i, 0))
    return pl.pallas_call(
        _ffn_kernel,
        grid=(rows // tm, d_ff // tf),
        in_specs=[
            row_tile,
            pl.BlockSpec((1, D_MODEL), lambda i, f: (0, 0)),
            _mod_spec(4, row_to_mod), _mod_spec(3, row_to_mod), _mod_spec(5, row_to_mod),
            pl.BlockSpec((D_MODEL, tf), lambda i, f: (0, f)),
            pl.BlockSpec((D_MODEL, tf), lambda i, f: (0, f)),
            pl.BlockSpec((tf, D_MODEL), lambda i, f: (f, 0)),
        ],
        out_specs=row_tile,
        out_shape=jax.ShapeDtypeStruct((rows, D_MODEL), F32),
        scratch_shapes=[pltpu.VMEM((tm, D_MODEL), BF16), pltpu.VMEM((tm, D_MODEL), F32)],
        compiler_params=pltpu.CompilerParams(
            dimension_semantics=("arbitrary", "arbitrary"), vmem_limit_bytes=VMEM_LIMIT),
        name="ffn_swiglu",
    )(xs, norm_g, mods, mods, mods, w1, w3, w2)


def _top2_gates(logits):
    lane = lax.broadcasted_iota(jnp.int32, logits.shape, 1)
    n_lanes = logits.shape[1]
    lg = jnp.where(lane < N_EXPERTS, logits, NEG_BIG)
    m1 = jnp.max(lg, axis=-1, keepdims=True)
    i1 = jnp.min(jnp.where(lg == m1, lane, n_lanes), axis=-1, keepdims=True)
    lg2 = jnp.where(lane == i1, NEG_BIG, lg)
    m2 = jnp.max(lg2, axis=-1, keepdims=True)
    i2 = jnp.min(jnp.where(lg2 == m2, lane, n_lanes), axis=-1, keepdims=True)
    e2 = jnp.exp(m2 - m1)
    w_top = 1.0 / (1.0 + e2)
    return jnp.where(lane == i1, w_top, 0.0) + jnp.where(lane == i2, e2 * w_top, 0.0)


def _moe_kernel(x_ref, g_ref, sc_ref, sh_ref, gt_ref, rhi_ref, rlo_ref, fg_ref, w1_ref, w3_ref, w2_ref,
                o_ref, h_scr, gate_scr, acc_scr):
    e = pl.program_id(1)
    f = pl.program_id(2)

    @pl.when((e == 0) & (f == 0))
    def _():
        h = _modulated_norm(x_ref[...], g_ref[...], sc_ref[...], sh_ref[...])
        h_hi, h_lo = _split_bf16(h)
        h_scr[...] = h_hi
        logits = jnp.dot(h_hi, rhi_ref[...], preferred_element_type=F32)
        logits += jnp.dot(h_lo, rhi_ref[...], preferred_element_type=F32)
        logits += jnp.dot(h_hi, rlo_ref[...], preferred_element_type=F32)
        gate = _top2_gates(logits)
        lane = lax.broadcasted_iota(jnp.int32, gate.shape, 1)
        for k in range(N_EXPERTS):
            col = jnp.sum(jnp.where(lane == k, gate, 0.0), axis=-1, keepdims=True)
            gate_scr[k] = jnp.broadcast_to(col, gate.shape)
        acc_scr[...] = jnp.zeros_like(acc_scr)

    h = h_scr[...]
    a = jnp.dot(h, w1_ref[...], preferred_element_type=F32)
    b = jnp.dot(h, w3_ref[...], preferred_element_type=F32)
    hid = _silu(a) * b * gate_scr[e][:, :1]
    acc_scr[...] += jnp.dot(hid.astype(BF16), w2_ref[...], preferred_element_type=F32)

    @pl.when((e == pl.num_programs(1) - 1) & (f == pl.num_programs(2) - 1))
    def _():
        y = x_ref[...] + gt_ref[...] * acc_scr[...]
        ms = jnp.mean(y * y, axis=-1, keepdims=True)
        o_ref[...] = y * lax.rsqrt(ms + EPS) * fg_ref[...]


def _moe(xs, mods, norm_g, r_hi, r_lo, final_g, w1, w3, w2, *, rows, tm, tf, tiles_per_seq, n_batch):
    d_ff = w1.shape[2]
    row_to_mod = lambda i: jnp.minimum(i // tiles_per_seq, n_batch)
    row_tile = pl.BlockSpec((tm, D_MODEL), lambda i, e, f: (i, 0))
    vec = pl.BlockSpec((1, D_MODEL), lambda i, e, f: (0, 0))
    router = pl.BlockSpec((D_MODEL, 128), lambda i, e, f: (0, 0))
    return pl.pallas_call(
        _moe_kernel,
        grid=(rows // tm, N_EXPERTS, d_ff // tf),
        in_specs=[
            row_tile, vec,
            _mod_spec(4, row_to_mod), _mod_spec(3, row_to_mod), _mod_spec(5, row_to_mod),
            router, router, vec,
            pl.BlockSpec((None, D_MODEL, tf), lambda i, e, f: (e, 0, f)),
            pl.BlockSpec((None, D_MODEL, tf), lambda i, e, f: (e, 0, f)),
            pl.BlockSpec((None, tf, D_MODEL), lambda i, e, f: (e, f, 0)),
        ],
        out_specs=row_tile,
        out_shape=jax.ShapeDtypeStruct((rows, D_MODEL), F32),
        scratch_shapes=[
            pltpu.VMEM((tm, D_MODEL), BF16),
            pltpu.VMEM((N_EXPERTS, tm, 128), F32),
            pltpu.VMEM((tm, D_MODEL), F32),
        ],
        compiler_params=pltpu.CompilerParams(
            dimension_semantics=("arbitrary", "arbitrary", "arbitrary"), vmem_limit_bytes=VMEM_LIMIT),
        name="moe_swiglu",
    )(xs, norm_g, mods, mods, mods, r_hi, r_lo, final_g, w1, w3, w2)


def _deinterleave(n=HEAD_DIM):
    return np.concatenate([np.arange(0, n, 2), np.arange(1, n, 2)])


def _proj_columns():
    de = _deinterleave()
    o_dq, o_gq, o_dk, o_dv, o_gk, o_gv, o_gate = 0, 1024, 2048, 3072, 4096, 4352, 4608
    cols = np.zeros(IN_W, np.int32)
    scale = np.ones(IN_W, np.float32)
    cols[GATE_OFF:GATE_OFF + 2048] = o_gate + np.arange(2048)
    for h in range(N_DIFF_HEADS):
        for c in range(2):
            dst = 128 * h + 64 * c
            cols[DQ_OFF + dst:DQ_OFF + dst + 64] = o_dq + dst + de
            cols[DK_OFF + dst:DK_OFF + dst + 64] = o_dk + dst + de
    scale[DQ_OFF:DQ_OFF + 1024] = HEAD_DIM ** -0.5
    for j in range(N_GQA_HEADS):
        cols[GQ_OFF + 64 * j:GQ_OFF + 64 * j + 64] = o_gq + 64 * j + de
    cols[DV_OFF:DV_OFF + 1024] = o_dv + np.arange(1024)
    for g in range(N_GQA_KV):
        cols[GKV_OFF + 128 * g:GKV_OFF + 128 * g + 64] = o_gk + 64 * g + de
        cols[GKV_OFF + 128 * g + 64:GKV_OFF + 128 * g + 128] = o_gv + 64 * g + np.arange(64)
    return cols, scale


def _rope_tables(S, pad_rows):
    rows = S // GRID_W
    row = jnp.repeat(jnp.arange(rows, dtype=F32), GRID_W)
    col = jnp.tile(jnp.arange(GRID_W, dtype=F32), rows)
    half = HEAD_DIM // 2
    inv_freq = ROPE_THETA ** (-jnp.arange(0, half, 2, dtype=F32) / half)
    ang = jnp.concatenate([row[:, None] * inv_freq, col[:, None] * inv_freq], axis=-1)
    cos, sin = jnp.cos(ang), jnp.sin(ang)
    reps = PROJ_TN // HEAD_DIM
    cos_t = jnp.tile(jnp.concatenate([cos, cos], axis=-1), (1, reps))
    sin_t = jnp.tile(jnp.concatenate([-sin, sin], axis=-1), (1, reps))
    cos_t = jnp.concatenate([cos_t, jnp.ones((pad_rows, PROJ_TN), F32)], axis=0)
    sin_t = jnp.concatenate([sin_t, jnp.zeros((pad_rows, PROJ_TN), F32)], axis=0)
    return cos_t, sin_t


def kernel(x, c, ctx, c_ctx, ada_w, ada_b, norm_attn_g, norm_ffn_g, w_in, q_norm_g, k_norm_g, diff_lambda,
           diff_subln_g, w_proj_diff, w_proj_gqa, w_out, ffn_w1, ffn_w3, ffn_w2, moe_router, moe_w1, moe_w3,
           moe_w2, final_norm_g):
    B, S, D = x.shape
    C = ctx.shape[1]
    depth = ada_w.shape[0]
    assert D == D_MODEL and depth == 2 and B + 1 <= MOD_ROWS
    assert w_in.shape[2] == IN_W and moe_router.shape[2] == N_EXPERTS
    tiles = _pick_tiles(B, S, C, ffn_w1.shape[2], moe_w1.shape[3])
    n_lat = B * S

    cvec = jnp.zeros((MOD_ROWS, D), F32).at[:B].set(c).at[B].set(c_ctx)
    mods_all = _ada_mods(cvec, ada_w, ada_b).reshape(depth, MOD_ROWS * N_MODS, 1, D)

    cols, col_scale = _proj_columns()
    de = _deinterleave()
    cos_t, sin_t = _rope_tables(S, tiles.tm_proj)
    blk = np.arange(PROJ_TN) // HEAD_DIM
    ones_bd = jnp.asarray(blk[:, None] == blk[None, :], BF16)
    is_key_lane = (np.arange(PROJ_TN) % (2 * HEAD_DIM)) < HEAD_DIM

    xs = jnp.concatenate([x.reshape(n_lat, D), ctx.reshape(B * C, D)], axis=0)
    for l in range(depth):
        last = l == depth - 1
        mods = mods_all[l]
        lam_init = 0.8 - 0.6 * math.exp(-0.3 * l)
        lq1, lk1, lq2, lk2 = diff_lambda[l]
        lam = (jnp.exp(jnp.sum(lq1 * lk1)) - jnp.exp(jnp.sum(lq2 * lk2)) + lam_init).reshape(1).astype(F32)
        w = (jnp.take(w_in[l], cols, axis=1) * col_scale).astype(BF16)
        qg = jnp.tile(q_norm_g[l][de] * HEAD_DIM ** -0.5, PROJ_TN // HEAD_DIM).reshape(1, PROJ_TN)
        kg = jnp.where(is_key_lane, jnp.tile(k_norm_g[l][de], PROJ_TN // HEAD_DIM), 1.0).reshape(1, PROJ_TN)
        subln = (diff_subln_g[l] * (1.0 - lam_init)).reshape(1, 2 * HEAD_DIM)

        p = _project(xs, mods, norm_attn_g[l].reshape(1, D), w, cos_t, sin_t, ones_bd, qg, kg,
                     tm=tiles.tm_proj, lat_tiles=n_lat // tiles.tm_proj, tiles_per_seq=S // tiles.tm_proj,
                     n_batch=B)
        yd, yg = _attention(p, lam, subln, B=B, S=S, C=C, tiles=tiles, ctx_queries=not last)
        rows = n_lat if last else n_lat + B * C
        xs = _mixer_out(xs, yd, yg, p, mods, w_proj_diff[l].astype(BF16), w_proj_gqa[l].astype(BF16),
                        w_out[l].astype(BF16), rows=rows, tm=tiles.tm_mix, tiles_per_seq=S // tiles.tm_mix,
                        n_batch=B)
        i = l // 2
        if l % 2 == 0:
            xs = _ffn(xs, mods, norm_ffn_g[l].reshape(1, D), ffn_w1[i].astype(BF16), ffn_w3[i].astype(BF16),
                      ffn_w2[i].astype(BF16), tm=tiles.tm_ffn, tf=tiles.tf_ffn,
                      tiles_per_seq=S // tiles.tm_ffn, n_batch=B)
        else:
            r_pad = jnp.zeros((D, 128), F32).at[:, :N_EXPERTS].set(moe_router[i])
            r_hi = r_pad.astype(BF16)
            r_lo = (r_pad - r_hi.astype(F32)).astype(BF16)
            xs = _moe(xs, mods, norm_ffn_g[l].reshape(1, D), r_hi, r_lo, final_norm_g.reshape(1, D),
                      moe_w1[i].astype(BF16), moe_w3[i].astype(BF16), moe_w2[i].astype(BF16),
                      rows=rows, tm=tiles.tm_ffn, tf=tiles.tf_moe, tiles_per_seq=S // tiles.tm_ffn, n_batch=B)
    return xs.reshape(B, S, D)
```

```python
import functools
import math
from typing import NamedTuple

import numpy as np
import jax
import jax.numpy as jnp
from jax import lax
from jax.experimental import pallas as pl
from jax.experimental.pallas import tpu as pltpu

F32 = jnp.float32
BF16 = jnp.bfloat16

D_MODEL = 1024
HEAD_DIM = 64
N_DIFF_HEADS = 8
N_GQA_HEADS = 16
N_GQA_KV = 4
GQA_REP = N_GQA_HEADS // N_GQA_KV
N_EXPERTS = 8
GRID_W = 64
ROPE_THETA = 10000.0
EPS = 1e-6
N_MODS = 6
MOD_ROWS = 16
NEG_BIG = -1e30

GATE_OFF = 0
DQ_OFF = 2048
GQ_OFF = 3072
DK_OFF = 4096
DV_OFF = 5120
GKV_OFF = 6144
IN_W = 6656
PROJ_TN = 512
ROPE_TILES = (4, 5, 8, 9)
QNORM_TILES = (6, 7)
KV_TILE = 12

VMEM_LIMIT = 52 * 1024 * 1024


class Tiles(NamedTuple):
    tm_proj: int
    tq: int
    tk: int
    tm_mix: int
    tm_ffn: int
    tf_ffn: int
    tf_moe: int


def _largest_divisor(n, candidates):
    for c in candidates:
        if n % c == 0:
            return c
    raise ValueError(f"no tile in {candidates} divides {n}")


def _pick_tiles(B, S, C, d_ff, d_ff_e):
    rows_common = math.gcd(S, B * C)
    tm = _largest_divisor(rows_common, (512, 256, 128))
    return Tiles(
        tm_proj=tm,
        tq=C,
        tk=_largest_divisor(math.gcd(S, C), (256, 128)),
        tm_mix=tm,
        tm_ffn=_largest_divisor(rows_common, (1024, 512, 256, 128)),
        tf_ffn=_largest_divisor(d_ff, (1408, 1024, 512, 256, 128)),
        tf_moe=_largest_divisor(d_ff_e, (896, 512, 256, 128)),
    )


def _split_bf16(v):
    hi = v.astype(BF16)
    lo = (v - hi.astype(F32)).astype(BF16)
    return hi, lo


def _ada_kernel(c_ref, w_ref, b_ref, o_ref):
    c = c_ref[...]
    s = c / (1.0 + jnp.exp(-c))
    s_hi, s_lo = _split_bf16(s)
    w_hi, w_lo = _split_bf16(w_ref[...])
    acc = jnp.dot(s_hi, w_hi, preferred_element_type=F32)
    acc += jnp.dot(s_lo, w_hi, preferred_element_type=F32)
    acc += jnp.dot(s_hi, w_lo, preferred_element_type=F32)
    o_ref[...] = acc + b_ref[...]


def _ada_mods(cvec, ada_w, ada_b):
    depth, d, n = ada_w.shape
    tn = _largest_divisor(n, (1536, 1024, 512))
    return pl.pallas_call(
        _ada_kernel,
        grid=(depth, n // tn),
        in_specs=[
            pl.BlockSpec((MOD_ROWS, d), lambda l, j: (0, 0)),
            pl.BlockSpec((None, d, tn), lambda l, j: (l, 0, j)),
            pl.BlockSpec((None, 1, tn), lambda l, j: (l, 0, j)),
        ],
        out_specs=pl.BlockSpec((None, MOD_ROWS, tn), lambda l, j: (l, 0, j)),
        out_shape=jax.ShapeDtypeStruct((depth, MOD_ROWS, n), F32),
        compiler_params=pltpu.CompilerParams(
            dimension_semantics=("arbitrary", "arbitrary"), vmem_limit_bytes=VMEM_LIMIT),
        name="ada_mods",
    )(cvec, ada_w, ada_b.reshape(depth, 1, n))


def _modulated_norm(x, g, scale, shift):
    ms = jnp.mean(x * x, axis=-1, keepdims=True)
    return (x * lax.rsqrt(ms + EPS) * g) * (1.0 + scale) + shift


def _silu(a):
    return a / (1.0 + jnp.exp(-a))


def _mod_spec(k, row_to_mod):
    return pl.BlockSpec((None, 1, D_MODEL), lambda i, *_: (row_to_mod(i) * N_MODS + k, 0, 0))


def _rope(z, cos, sin_signed, lane):
    first_half = (lane & (HEAD_DIM // 2)) == 0
    n = z.shape[1]
    partner = jnp.where(first_half, pltpu.roll(z, n - HEAD_DIM // 2, 1), pltpu.roll(z, HEAD_DIM // 2, 1))
    return z * cos + partner * sin_signed


def _head_rms(z, ones_ref, gain):
    zz_hi, zz_lo = _split_bf16(z * z)
    ss = jnp.dot(zz_hi, ones_ref[...], preferred_element_type=F32)
    ss += jnp.dot(zz_lo, ones_ref[...], preferred_element_type=F32)
    return z * lax.rsqrt(ss * (1.0 / HEAD_DIM) + EPS) * gain


def _proj_kernel(x_ref, g_ref, sc_ref, sh_ref, w_ref, cos_ref, sin_ref, ones_ref, qg_ref, kg_ref,
                 o_ref, h_scr):
    j = pl.program_id(1)

    @pl.when(j == 0)
    def _():
        h_scr[...] = _modulated_norm(x_ref[...], g_ref[...], sc_ref[...], sh_ref[...]).astype(BF16)

    z = jnp.dot(h_scr[...], w_ref[...], preferred_element_type=F32)
    lane = lax.broadcasted_iota(jnp.int32, z.shape, 1)
    is_rope = functools.reduce(jnp.logical_or, [j == t for t in ROPE_TILES])
    is_qnorm = functools.reduce(jnp.logical_or, [j == t for t in QNORM_TILES])
    is_kv = j == KV_TILE
    is_plain = jnp.logical_not(is_rope | is_qnorm | is_kv)

    @pl.when(is_plain)
    def _():
        o_ref[...] = z.astype(BF16)

    @pl.when(is_rope)
    def _():
        o_ref[...] = _rope(z, cos_ref[...], sin_ref[...], lane).astype(BF16)

    @pl.when(is_qnorm)
    def _():
        zn = _head_rms(z, ones_ref, qg_ref[...])
        o_ref[...] = _rope(zn, cos_ref[...], sin_ref[...], lane).astype(BF16)

    @pl.when(is_kv)
    def _():
        zn = _head_rms(z, ones_ref, kg_ref[...])
        zr = _rope(zn, cos_ref[...], sin_ref[...], lane)
        is_key_lane = (lane & HEAD_DIM) == 0
        o_ref[...] = jnp.where(is_key_lane, zr, z).astype(BF16)


def _project(xs, mods, norm_g, w, cos_t, sin_t, ones_bd, qg, kg, *, tm, lat_tiles, tiles_per_seq, n_batch):
    rows = xs.shape[0]
    row_to_mod = lambda i: jnp.minimum(i // tiles_per_seq, n_batch)
    rope_row = lambda i: jnp.where(i < lat_tiles, i % tiles_per_seq, tiles_per_seq)
    const = lambda i, j: (0, 0)
    return pl.pallas_call(
        _proj_kernel,
        grid=(rows // tm, IN_W // PROJ_TN),
        in_specs=[
            pl.BlockSpec((tm, D_MODEL), lambda i, j: (i, 0)),
            pl.BlockSpec((1, D_MODEL), const),
            _mod_spec(1, row_to_mod),
            _mod_spec(0, row_to_mod),
            pl.BlockSpec((D_MODEL, PROJ_TN), lambda i, j: (0, j)),
            pl.BlockSpec((tm, PROJ_TN), lambda i, j: (rope_row(i), 0)),
            pl.BlockSpec((tm, PROJ_TN), lambda i, j: (rope_row(i), 0)),
            pl.BlockSpec((PROJ_TN, PROJ_TN), const),
            pl.BlockSpec((1, PROJ_TN), const),
            pl.BlockSpec((1, PROJ_TN), const),
        ],
        out_specs=pl.BlockSpec((tm, PROJ_TN), lambda i, j: (i, j)),
        out_shape=jax.ShapeDtypeStruct((rows, IN_W), BF16),
        scratch_shapes=[pltpu.VMEM((tm, D_MODEL), BF16)],
        compiler_params=pltpu.CompilerParams(
            dimension_semantics=("arbitrary", "arbitrary"), vmem_limit_bytes=VMEM_LIMIT),
        name="in_proj",
    )(xs, norm_g, mods, mods, w, cos_t, sin_t, ones_bd, qg, kg)


STAB_LANE = HEAD_DIM
L_FLOOR = 1e-26
DIFF_VT_ROWS = 2 * HEAD_DIM + 16


def _aug_keys(k):
    lane = lax.broadcasted_iota(jnp.int32, k.shape, 1)
    in_key = lane < HEAD_DIM
    sq = jnp.where(in_key, k * k, 0.0).astype(BF16)
    norms = jnp.dot(sq, jnp.ones((k.shape[1], k.shape[1]), BF16), preferred_element_type=F32)
    ka = jnp.where(in_key, k, jnp.where(lane == STAB_LANE, 1.0, 0.0)).astype(BF16)
    return ka, jnp.max(norms, axis=0, keepdims=True)


def _query_mats(slab, kmax2):
    lane = lax.broadcasted_iota(jnp.int32, slab.shape, 1)
    in_key = lane < HEAD_DIM
    sq = jnp.where(in_key, slab * slab, 0.0).astype(BF16)
    norms = jnp.dot(sq, jnp.ones((slab.shape[1], slab.shape[1]), BF16), preferred_element_type=F32)
    bound = jnp.sqrt(norms * kmax2)
    plain = jnp.where(in_key, slab, 0.0)
    return jnp.where(lane == STAB_LANE, -bound, plain).astype(BF16), plain.astype(BF16)


def _attend(ka_scr, vt_scr, mats, acc_scr, *, first, n_chunks, tk):
    def scores(r, c, qmat):
        off = pl.multiple_of(c * tk, tk)
        return lax.dot_general(ka_scr[r % ka_scr.shape[0], pl.ds(off, tk), :], qmat, (((1,), (1,)), ((), ())),
                               preferred_element_type=F32)

    acc_scr[...] = jnp.zeros(acc_scr.shape, F32)

    def all_scores(c):
        return tuple(scores(r, c, shifted) for r, (shifted, _) in enumerate(mats))

    def accumulate(c, s_all):
        for r, s in enumerate(s_all):
            acc_scr[r] += jnp.dot(vt_scr[c], jnp.exp(s).astype(BF16), preferred_element_type=F32)

    def fast_body(c, s_cur):
        s_next = all_scores(c + 1)
        accumulate(c, s_cur)
        return s_next

    trips = n_chunks - 1 - first
    s_last = lax.fori_loop(first, n_chunks - 1, fast_body, all_scores(first),
                           unroll=_largest_divisor(trips, (4, 2, 1)) if trips > 0 else 1)
    accumulate(n_chunks - 1, s_last)


def _attend_fallback(ka_scr, vt_scr, mats, m_scr, acc_scr, *, first, n_chunks, tk):
    m_scr[...] = jnp.full(m_scr.shape, NEG_BIG, F32)
    acc_scr[...] = jnp.zeros(acc_scr.shape, F32)

    def body(c, carry):
        off = pl.multiple_of(c * tk, tk)
        for r, (_, plain) in enumerate(mats):
            s = lax.dot_general(ka_scr[r % ka_scr.shape[0], pl.ds(off, tk), :], plain, (((1,), (1,)), ((), ())),
                                preferred_element_type=F32)
            m_old = m_scr[r]
            m_new = jnp.maximum(m_old, jnp.max(s, axis=0, keepdims=True))
            p = jnp.exp(s - m_new).astype(BF16)
            acc_scr[r] = acc_scr[r] * jnp.exp(m_old - m_new) + jnp.dot(vt_scr[c], p, preferred_element_type=F32)
            m_scr[r] = m_new
        return carry

    lax.fori_loop(first, n_chunks, body, 0)


def _softmax_tile(ka_scr, vt_scr, kmax_scr, slabs, m_scr, acc_scr, l_row, *, n_lat_chunks, n_chunks, tk, ctx_tile):
    mats = [_query_mats(slab, kmax_scr[r % kmax_scr.shape[0]]) for r, slab in enumerate(slabs)]

    def run(first):
        _attend(ka_scr, vt_scr, mats, acc_scr, first=first, n_chunks=n_chunks, tk=tk)
        l_min = functools.reduce(
            jnp.minimum, [jnp.min(acc_scr[r, l_row:l_row + 1, :]) for r in range(len(slabs))])

        @pl.when(jnp.logical_not(l_min >= L_FLOOR))
        def _():
            _attend_fallback(ka_scr, vt_scr, mats, m_scr, acc_scr, first=first, n_chunks=n_chunks, tk=tk)

    if ctx_tile is None:
        run(0)
    else:
        pl.when(jnp.logical_not(ctx_tile))(functools.partial(run, 0))
        pl.when(ctx_tile)(functools.partial(run, n_lat_chunks))


def _chunk_rows(c, n_lat_chunks, tk, lat_ref, ctx_ref):
    if c < n_lat_chunks:
        return lat_ref[c * tk:(c + 1) * tk, :]
    return ctx_ref[(c - n_lat_chunks) * tk:(c - n_lat_chunks + 1) * tk, :]


def _diff_attn_kernel(lam_ref, q_ref, kl_ref, vl_ref, kc_ref, vc_ref, g_ref, o_ref,
                      ka_scr, vt_scr, kmax_scr, m_scr, acc_scr, *, nq_lat, n_lat_chunks, n_chunks, tk, ctx_queries):
    dv = 2 * HEAD_DIM

    @pl.when(pl.program_id(2) == 0)
    def _():
        kmax = [None, None]
        for c in range(n_chunks):
            k = _chunk_rows(c, n_lat_chunks, tk, kl_ref, kc_ref).astype(F32)
            for r, keys in enumerate((k, pltpu.roll(k, HEAD_DIM, 1))):
                ka, n2 = _aug_keys(keys)
                ka_scr[r, c * tk:(c + 1) * tk, :] = ka
                kmax[r] = n2 if kmax[r] is None else jnp.maximum(kmax[r], n2)
            vt_scr[c, :dv, :] = _chunk_rows(c, n_lat_chunks, tk, vl_ref, vc_ref).T
            vt_scr[c, dv:, :] = jnp.ones((DIFF_VT_ROWS - dv, tk), BF16)
        kmax_scr[0] = kmax[0]
        kmax_scr[1] = kmax[1]

    q = q_ref[...].astype(F32)
    ctx_tile = (pl.program_id(2) == nq_lat) if ctx_queries else None
    _softmax_tile(ka_scr, vt_scr, kmax_scr, [q, pltpu.roll(q, HEAD_DIM, 1)], m_scr, acc_scr, dv,
                  n_lat_chunks=n_lat_chunks, n_chunks=n_chunks, tk=tk, ctx_tile=ctx_tile)
    a1, a2 = acc_scr[0], acc_scr[1]
    yt = a1[:dv] * (1.0 / a1[dv:dv + 1]) - lam_ref[0] * (a2[:dv] * (1.0 / a2[dv:dv + 1]))
    y = yt.T
    ms = jnp.mean(y * y, axis=-1, keepdims=True)
    o_ref[...] = (y * lax.rsqrt(ms + EPS) * g_ref[...]).astype(BF16)


def _gqa_attn_kernel(q_ref, kvl_ref, kvc_ref, o_ref, ka_scr, vt_scr, kmax_scr, m_scr, acc_scr,
                     *, nq_lat, n_lat_chunks, n_chunks, tk, ctx_queries):
    @pl.when(pl.program_id(2) == 0)
    def _():
        kmax = None
        for c in range(n_chunks):
            kv = _chunk_rows(c, n_lat_chunks, tk, kvl_ref, kvc_ref)
            ka, n2 = _aug_keys(kv.astype(F32))
            ka_scr[0, c * tk:(c + 1) * tk, :] = ka
            kmax = n2 if kmax is None else jnp.maximum(kmax, n2)
            t = kv.T
            row = lax.broadcasted_iota(jnp.int32, t.shape, 0)
            vt_scr[c] = jnp.where(row < HEAD_DIM, jnp.ones_like(t), t)
        kmax_scr[0] = kmax

    qf = q_ref[...].astype(F32)
    tq = qf.shape[0]
    lane = lax.broadcasted_iota(jnp.int32, (tq, 2 * HEAD_DIM), 1)
    slabs = []
    for r in range(GQA_REP):
        slab = qf[:, (r // 2) * 2 * HEAD_DIM:(r // 2 + 1) * 2 * HEAD_DIM]
        slabs.append(pltpu.roll(slab, HEAD_DIM, 1) if r % 2 else slab)
    ctx_tile = (pl.program_id(2) == nq_lat) if ctx_queries else None
    _softmax_tile(ka_scr, vt_scr, kmax_scr, slabs, m_scr, acc_scr, 0,
                  n_lat_chunks=n_lat_chunks, n_chunks=n_chunks, tk=tk, ctx_tile=ctx_tile)

    def head_out(r):
        acc = acc_scr[r]
        return (acc * (1.0 / acc[0:1])).T

    for j in range(GQA_REP // 2):
        pair = jnp.where(lane < HEAD_DIM, pltpu.roll(head_out(2 * j), HEAD_DIM, 1), head_out(2 * j + 1))
        o_ref[:, j * 2 * HEAD_DIM:(j + 1) * 2 * HEAD_DIM] = pair.astype(BF16)


def _attention(p, lam, subln_gain, *, B, S, C, tiles, ctx_queries):
    tq, tk = tiles.tq, tiles.tk
    nq_lat = S // tq
    nq = nq_lat + (1 if ctx_queries else 0)
    ctx_blk0 = (B * S) // C
    rows_out = B * S + (B * C if ctx_queries else 0)
    T = S + C
    statics = dict(nq_lat=nq_lat, n_lat_chunks=S // tk, n_chunks=T // tk, tk=tk, ctx_queries=ctx_queries)

    def q_row(b, qi):
        if not ctx_queries:
            return b * nq_lat + qi
        return jnp.where(qi < nq_lat, b * nq_lat + qi, ctx_blk0 + b)

    cparams = pltpu.CompilerParams(
        dimension_semantics=("arbitrary", "arbitrary", "arbitrary"), vmem_limit_bytes=VMEM_LIMIT)
    w2 = 2 * HEAD_DIM
    yd = pl.pallas_call(
        functools.partial(_diff_attn_kernel, **statics),
        grid=(B, N_DIFF_HEADS, nq),
        in_specs=[
            pl.BlockSpec(memory_space=pltpu.SMEM),
            pl.BlockSpec((tq, w2), lambda b, h, qi: (q_row(b, qi), DQ_OFF // w2 + h)),
            pl.BlockSpec((S, w2), lambda b, h, qi: (b, DK_OFF // w2 + h)),
            pl.BlockSpec((S, w2), lambda b, h, qi: (b, DV_OFF // w2 + h)),
            pl.BlockSpec((C, w2), lambda b, h, qi: (ctx_blk0 + b, DK_OFF // w2 + h)),
            pl.BlockSpec((C, w2), lambda b, h, qi: (ctx_blk0 + b, DV_OFF // w2 + h)),
            pl.BlockSpec((1, w2), lambda b, h, qi: (0, 0)),
        ],
        out_specs=pl.BlockSpec((tq, w2), lambda b, h, qi: (q_row(b, qi), h)),
        out_shape=jax.ShapeDtypeStruct((rows_out, N_DIFF_HEADS * w2), BF16),
        scratch_shapes=[
            pltpu.VMEM((2, T, w2), BF16),
            pltpu.VMEM((T // tk, DIFF_VT_ROWS, tk), BF16),
            pltpu.VMEM((2, 1, w2), F32),
            pltpu.VMEM((2, 1, tq), F32),
            pltpu.VMEM((2, DIFF_VT_ROWS, tq), F32),
        ],
        compiler_params=cparams,
        name="diff_attn",
    )(lam, p, p, p, p, p, subln_gain)

    wq = GQA_REP * HEAD_DIM
    yg = pl.pallas_call(
        functools.partial(_gqa_attn_kernel, **statics),
        grid=(B, N_GQA_KV, nq),
        in_specs=[
            pl.BlockSpec((tq, wq), lambda b, g, qi: (q_row(b, qi), GQ_OFF // wq + g)),
            pl.BlockSpec((S, w2), lambda b, g, qi: (b, GKV_OFF // w2 + g)),
            pl.BlockSpec((C, w2), lambda b, g, qi: (ctx_blk0 + b, GKV_OFF // w2 + g)),
        ],
        out_specs=pl.BlockSpec((tq, wq), lambda b, g, qi: (q_row(b, qi), g)),
        out_shape=jax.ShapeDtypeStruct((rows_out, N_GQA_HEADS * HEAD_DIM), BF16),
        scratch_shapes=[
            pltpu.VMEM((1, T, w2), BF16),
            pltpu.VMEM((T // tk, w2, tk), BF16),
            pltpu.VMEM((1, 1, w2), F32),
            pltpu.VMEM((GQA_REP, 1, tq), F32),
            pltpu.VMEM((GQA_REP, w2, tq), F32),
        ],
        compiler_params=cparams,
        name="gqa_attn",
    )(p, p, p)
    return yd, yg


def _mixer_kernel(x_ref, yd_ref, yg_ref, gd_ref, gg_ref, gt_ref, wpd_ref, wpg_ref, wo_ref, o_ref):
    pd = jnp.dot(yd_ref[...], wpd_ref[...], preferred_element_type=F32)
    pg = jnp.dot(yg_ref[...], wpg_ref[...], preferred_element_type=F32)
    m = jax.nn.sigmoid(gd_ref[...].astype(F32)) * pd + jax.nn.sigmoid(gg_ref[...].astype(F32)) * pg
    o_ref[...] = x_ref[...] + gt_ref[...] * jnp.dot(m.astype(BF16), wo_ref[...], preferred_element_type=F32)


def _mixer_out(xs, yd, yg, p, mods, wpd, wpg, wo, *, rows, tm, tiles_per_seq, n_batch):
    row_to_mod = lambda i: jnp.minimum(i // tiles_per_seq, n_batch)
    row_tile = pl.BlockSpec((tm, D_MODEL), lambda i: (i, 0))
    weight = pl.BlockSpec((D_MODEL, D_MODEL), lambda i: (0, 0))
    return pl.pallas_call(
        _mixer_kernel,
        grid=(rows // tm,),
        in_specs=[
            row_tile, row_tile, row_tile,
            pl.BlockSpec((tm, D_MODEL), lambda i: (i, GATE_OFF // D_MODEL)),
            pl.BlockSpec((tm, D_MODEL), lambda i: (i, GATE_OFF // D_MODEL + 1)),
            _mod_spec(2, row_to_mod),
            weight, weight, weight,
        ],
        out_specs=row_tile,
        out_shape=jax.ShapeDtypeStruct((rows, D_MODEL), F32),
        compiler_params=pltpu.CompilerParams(
            dimension_semantics=("arbitrary",), vmem_limit_bytes=VMEM_LIMIT),
        name="mixer_out",
    )(xs, yd, yg, p, p, mods, wpd, wpg, wo)


def _ffn_kernel(x_ref, g_ref, sc_ref, sh_ref, gt_ref, w1_ref, w3_ref, w2_ref, o_ref, h_scr, acc_scr):
    f = pl.program_id(1)

    @pl.when(f == 0)
    def _():
        h_scr[...] = _modulated_norm(x_ref[...], g_ref[...], sc_ref[...], sh_ref[...]).astype(BF16)
        acc_scr[...] = jnp.zeros_like(acc_scr)

    h = h_scr[...]
    a = jnp.dot(h, w1_ref[...], preferred_element_type=F32)
    b = jnp.dot(h, w3_ref[...], preferred_element_type=F32)
    acc_scr[...] += jnp.dot((_silu(a) * b).astype(BF16), w2_ref[...], preferred_element_type=F32)

    @pl.when(f == pl.num_programs(1) - 1)
    def _():
        o_ref[...] = x_ref[...] + gt_ref[...] * acc_scr[...]


def _ffn(xs, mods, norm_g, w1, w3, w2, *, tm, tf, tiles_per_seq, n_batch):
    rows = xs.shape[0]
    d_ff = w1.shape[1]
    row_to_mod = lambda i: jnp.minimum(i // tiles_per_seq, n_batch)
    row_tile = pl.BlockSpec((tm, D_MODEL), lambda i, f: (i, 0))
    return pl.pallas_call(
        _ffn_kernel,
        grid=(rows // tm, d_ff // tf),
        in_specs=[
            row_tile,
            pl.BlockSpec((1, D_MODEL), lambda i, f: (0, 0)),
            _mod_spec(4, row_to_mod), _mod_spec(3, row_to_mod), _mod_spec(5, row_to_mod),
            pl.BlockSpec((D_MODEL, tf), lambda i, f: (0, f)),
            pl.BlockSpec((D_MODEL, tf), lambda i, f: (0, f)),
            pl.BlockSpec((tf, D_MODEL), lambda i, f: (f, 0)),
        ],
        out_specs=row_tile,
        out_shape=jax.ShapeDtypeStruct((rows, D_MODEL), F32),
        scratch_shapes=[pltpu.VMEM((tm, D_MODEL), BF16), pltpu.VMEM((tm, D_MODEL), F32)],
        compiler_params=pltpu.CompilerParams(
            dimension_semantics=("arbitrary", "arbitrary"), vmem_limit_bytes=VMEM_LIMIT),
        name="ffn_swiglu",
    )(xs, norm_g, mods, mods, mods, w1, w3, w2)


def _top2_gates(logits):
    lane = lax.broadcasted_iota(jnp.int32, logits.shape, 1)
    n_lanes = logits.shape[1]
    lg = jnp.where(lane < N_EXPERTS, logits, NEG_BIG)
    m1 = jnp.max(lg, axis=-1, keepdims=True)
    i1 = jnp.min(jnp.where(lg == m1, lane, n_lanes), axis=-1, keepdims=True)
    lg2 = jnp.where(lane == i1, NEG_BIG, lg)
    m2 = jnp.max(lg2, axis=-1, keepdims=True)
    i2 = jnp.min(jnp.where(lg2 == m2, lane, n_lanes), axis=-1, keepdims=True)
    e2 = jnp.exp(m2 - m1)
    w_top = 1.0 / (1.0 + e2)
    return jnp.where(lane == i1, w_top, 0.0) + jnp.where(lane == i2, e2 * w_top, 0.0)


def _moe_kernel(x_ref, g_ref, sc_ref, sh_ref, gt_ref, rhi_ref, rlo_ref, fg_ref, w1_ref, w3_ref, w2_ref,
                o_ref, h_scr, gate_scr, acc_scr):
    e = pl.program_id(1)
    f = pl.program_id(2)

    @pl.when((e == 0) & (f == 0))
    def _():
        h = _modulated_norm(x_ref[...], g_ref[...], sc_ref[...], sh_ref[...])
        h_hi, h_lo = _split_bf16(h)
        h_scr[...] = h_hi
        logits = jnp.dot(h_hi, rhi_ref[...], preferred_element_type=F32)
        logits += jnp.dot(h_lo, rhi_ref[...], preferred_element_type=F32)
        logits += jnp.dot(h_hi, rlo_ref[...], preferred_element_type=F32)
        gate = _top2_gates(logits)
        lane = lax.broadcasted_iota(jnp.int32, gate.shape, 1)
        for k in range(N_EXPERTS):
            col = jnp.sum(jnp.where(lane == k, gate, 0.0), axis=-1, keepdims=True)
            gate_scr[k] = jnp.broadcast_to(col, gate.shape)
        acc_scr[...] = jnp.zeros_like(acc_scr)

    h = h_scr[...]
    a = jnp.dot(h, w1_ref[...], preferred_element_type=F32)
    b = jnp.dot(h, w3_ref[...], preferred_element_type=F32)
    hid = _silu(a) * b * gate_scr[e][:, :1]
    acc_scr[...] += jnp.dot(hid.astype(BF16), w2_ref[...], preferred_element_type=F32)

    @pl.when((e == pl.num_programs(1) - 1) & (f == pl.num_programs(2) - 1))
    def _():
        y = x_ref[...] + gt_ref[...] * acc_scr[...]
        ms = jnp.mean(y * y, axis=-1, keepdims=True)
        o_ref[...] = y * lax.rsqrt(ms + EPS) * fg_ref[...]


def _moe(xs, mods, norm_g, r_hi, r_lo, final_g, w1, w3, w2, *, rows, tm, tf, tiles_per_seq, n_batch):
    d_ff = w1.shape[2]
    row_to_mod = lambda i: jnp.minimum(i // tiles_per_seq, n_batch)
    row_tile = pl.BlockSpec((tm, D_MODEL), lambda i, e, f: (i, 0))
    vec = pl.BlockSpec((1, D_MODEL), lambda i, e, f: (0, 0))
    router = pl.BlockSpec((D_MODEL, 128), lambda i, e, f: (0, 0))
    return pl.pallas_call(
        _moe_kernel,
        grid=(rows // tm, N_EXPERTS, d_ff // tf),
        in_specs=[
            row_tile, vec,
            _mod_spec(4, row_to_mod), _mod_spec(3, row_to_mod), _mod_spec(5, row_to_mod),
            router, router, vec,
            pl.BlockSpec((None, D_MODEL, tf), lambda i, e, f: (e, 0, f)),
            pl.BlockSpec((None, D_MODEL, tf), lambda i, e, f: (e, 0, f)),
            pl.BlockSpec((None, tf, D_MODEL), lambda i, e, f: (e, f, 0)),
        ],
        out_specs=row_tile,
        out_shape=jax.ShapeDtypeStruct((rows, D_MODEL), F32),
        scratch_shapes=[
            pltpu.VMEM((tm, D_MODEL), BF16),
            pltpu.VMEM((N_EXPERTS, tm, 128), F32),
            pltpu.VMEM((tm, D_MODEL), F32),
        ],
        compiler_params=pltpu.CompilerParams(
            dimension_semantics=("arbitrary", "arbitrary", "arbitrary"), vmem_limit_bytes=VMEM_LIMIT),
        name="moe_swiglu",
    )(xs, norm_g, mods, mods, mods, r_hi, r_lo, final_g, w1, w3, w2)


def _deinterleave(n=HEAD_DIM):
    return np.concatenate([np.arange(0, n, 2), np.arange(1, n, 2)])


def _proj_columns():
    de = _deinterleave()
    o_dq, o_gq, o_dk, o_dv, o_gk, o_gv, o_gate = 0, 1024, 2048, 3072, 4096, 4352, 4608
    cols = np.zeros(IN_W, np.int32)
    scale = np.ones(IN_W, np.float32)
    cols[GATE_OFF:GATE_OFF + 2048] = o_gate + np.arange(2048)
    for h in range(N_DIFF_HEADS):
        for c in range(2):
            dst = 128 * h + 64 * c
            cols[DQ_OFF + dst:DQ_OFF + dst + 64] = o_dq + dst + de
            cols[DK_OFF + dst:DK_OFF + dst + 64] = o_dk + dst + de
    scale[DQ_OFF:DQ_OFF + 1024] = HEAD_DIM ** -0.5
    for j in range(N_GQA_HEADS):
        cols[GQ_OFF + 64 * j:GQ_OFF + 64 * j + 64] = o_gq + 64 * j + de
    cols[DV_OFF:DV_OFF + 1024] = o_dv + np.arange(1024)
    for g in range(N_GQA_KV):
        cols[GKV_OFF + 128 * g:GKV_OFF + 128 * g + 64] = o_gk + 64 * g + de
        cols[GKV_OFF + 128 * g + 64:GKV_OFF + 128 * g + 128] = o_gv + 64 * g + np.arange(64)
    return cols, scale


def _rope_tables(S, pad_rows):
    rows = S // GRID_W
    row = jnp.repeat(jnp.arange(rows, dtype=F32), GRID_W)
    col = jnp.tile(jnp.arange(GRID_W, dtype=F32), rows)
    half = HEAD_DIM // 2
    inv_freq = ROPE_THETA ** (-jnp.arange(0, half, 2, dtype=F32) / half)
    ang = jnp.concatenate([row[:, None] * inv_freq, col[:, None] * inv_freq], axis=-1)
    cos, sin = jnp.cos(ang), jnp.sin(ang)
    reps = PROJ_TN // HEAD_DIM
    cos_t = jnp.tile(jnp.concatenate([cos, cos], axis=-1), (1, reps))
    sin_t = jnp.tile(jnp.concatenate([-sin, sin], axis=-1), (1, reps))
    cos_t = jnp.concatenate([cos_t, jnp.ones((pad_rows, PROJ_TN), F32)], axis=0)
    sin_t = jnp.concatenate([sin_t, jnp.zeros((pad_rows, PROJ_TN), F32)], axis=0)
    return cos_t, sin_t


def kernel(x, c, ctx, c_ctx, ada_w, ada_b, norm_attn_g, norm_ffn_g, w_in, q_norm_g, k_norm_g, diff_lambda,
           diff_subln_g, w_proj_diff, w_proj_gqa, w_out, ffn_w1, ffn_w3, ffn_w2, moe_router, moe_w1, moe_w3,
           moe_w2, final_norm_g):
    B, S, D = x.shape
    C = ctx.shape[1]
    depth = ada_w.shape[0]
    assert D == D_MODEL and depth == 2 and B + 1 <= MOD_ROWS
    assert w_in.shape[2] == IN_W and moe_router.shape[2] == N_EXPERTS
    tiles = _pick_tiles(B, S, C, ffn_w1.shape[2], moe_w1.shape[3])
    n_lat = B * S

    cvec = jnp.zeros((MOD_ROWS, D), F32).at[:B].set(c).at[B].set(c_ctx)
    mods_all = _ada_mods(cvec, ada_w, ada_b).reshape(depth, MOD_ROWS * N_MODS, 1, D)

    cols, col_scale = _proj_columns()
    de = _deinterleave()
    cos_t, sin_t = _rope_tables(S, tiles.tm_proj)
    blk = np.arange(PROJ_TN) // HEAD_DIM
    ones_bd = jnp.asarray(blk[:, None] == blk[None, :], BF16)
    is_key_lane = (np.arange(PROJ_TN) % (2 * HEAD_DIM)) < HEAD_DIM

    xs = jnp.concatenate([x.reshape(n_lat, D), ctx.reshape(B * C, D)], axis=0)
    for l in range(depth):
        last = l == depth - 1
        mods = mods_all[l]
        lam_init = 0.8 - 0.6 * math.exp(-0.3 * l)
        lq1, lk1, lq2, lk2 = diff_lambda[l]
        lam = (jnp.exp(jnp.sum(lq1 * lk1)) - jnp.exp(jnp.sum(lq2 * lk2)) + lam_init).reshape(1).astype(F32)
        w = (jnp.take(w_in[l], cols, axis=1) * col_scale).astype(BF16)
        qg = jnp.tile(q_norm_g[l][de] * HEAD_DIM ** -0.5, PROJ_TN // HEAD_DIM).reshape(1, PROJ_TN)
        kg = jnp.where(is_key_lane, jnp.tile(k_norm_g[l][de], PROJ_TN // HEAD_DIM), 1.0).reshape(1, PROJ_TN)
        subln = (diff_subln_g[l] * (1.0 - lam_init)).reshape(1, 2 * HEAD_DIM)

        p = _project(xs, mods, norm_attn_g[l].reshape(1, D), w, cos_t, sin_t, ones_bd, qg, kg,
                     tm=tiles.tm_proj, lat_tiles=n_lat // tiles.tm_proj, tiles_per_seq=S // tiles.tm_proj,
                     n_batch=B)
        yd, yg = _attention(p, lam, subln, B=B, S=S, C=C, tiles=tiles, ctx_queries=not last)
        rows = n_lat if last else n_lat + B * C
        xs = _mixer_out(xs, yd, yg, p, mods, w_proj_diff[l].astype(BF16), w_proj_gqa[l].astype(BF16),
                        w_out[l].astype(BF16), rows=rows, tm=tiles.tm_mix, tiles_per_seq=S // tiles.tm_mix,
                        n_batch=B)
        i = l // 2
        if l % 2 == 0:
            xs = _ffn(xs, mods, norm_ffn_g[l].reshape(1, D), ffn_w1[i].astype(BF16), ffn_w3[i].astype(BF16),
                      ffn_w2[i].astype(BF16), tm=tiles.tm_ffn, tf=tiles.tf_ffn,
                      tiles_per_seq=S // tiles.tm_ffn, n_batch=B)
        else:
            r_pad = jnp.zeros((D, 128), F32).at[:, :N_EXPERTS].set(moe_router[i])
            r_hi = r_pad.astype(BF16)
            r_lo = (r_pad - r_hi.astype(F32)).astype(BF16)
            xs = _moe(xs, mods, norm_ffn_g[l].reshape(1, D), r_hi, r_lo, final_norm_g.reshape(1, D),
                      moe_w1[i].astype(BF16), moe_w3[i].astype(BF16), moe_w2[i].astype(BF16),
                      rows=rows, tm=tiles.tm_ffn, tf=tiles.tf_moe, tiles_per_seq=S // tiles.tm_ffn, n_batch=B)
    return xs.reshape(B, S, D)
```

```python
import functools
import math
from typing import NamedTuple

import numpy as np
import jax
import jax.numpy as jnp
from jax import lax
from jax.experimental import pallas as pl
from jax.experimental.pallas import tpu as pltpu

F32 = jnp.float32
BF16 = jnp.bfloat16

D_MODEL = 1024
HEAD_DIM = 64
N_DIFF_HEADS = 8
N_GQA_HEADS = 16
N_GQA_KV = 4
GQA_REP = N_GQA_HEADS // N_GQA_KV
N_EXPERTS = 8
GRID_W = 64
ROPE_THETA = 10000.0
EPS = 1e-6
N_MODS = 6
MOD_ROWS = 16
NEG_BIG = -1e30

GATE_OFF = 0
DQ_OFF = 2048
GQ_OFF = 3072
DK_OFF = 4096
DV_OFF = 5120
GKV_OFF = 6144
IN_W = 6656
PROJ_TN = 512
ROPE_TILES = (4, 5, 8, 9)
QNORM_TILES = (6, 7)
KV_TILE = 12

VMEM_LIMIT = 52 * 1024 * 1024


class Tiles(NamedTuple):
    tm_proj: int
    tq: int
    tk: int
    tm_mix: int
    tm_ffn: int
    tf_ffn: int
    tf_moe: int


def _largest_divisor(n, candidates):
    for c in candidates:
        if n % c == 0:
            return c
    raise ValueError(f"no tile in {candidates} divides {n}")


def _pick_tiles(B, S, C, d_ff, d_ff_e):
    rows_common = math.gcd(S, B * C)
    tm = _largest_divisor(rows_common, (512, 256, 128))
    return Tiles(
        tm_proj=tm,
        tq=C,
        tk=_largest_divisor(math.gcd(S, C), (256, 128)),
        tm_mix=tm,
        tm_ffn=_largest_divisor(rows_common, (1024, 512, 256, 128)),
        tf_ffn=_largest_divisor(d_ff, (1408, 1024, 512, 256, 128)),
        tf_moe=_largest_divisor(d_ff_e, (896, 512, 256, 128)),
    )


def _split_bf16(v):
    hi = v.astype(BF16)
    lo = (v - hi.astype(F32)).astype(BF16)
    return hi, lo


def _ada_kernel(c_ref, w_ref, b_ref, o_ref):
    c = c_ref[...]
    s = c / (1.0 + jnp.exp(-c))
    s_hi, s_lo = _split_bf16(s)
    w_hi, w_lo = _split_bf16(w_ref[...])
    acc = jnp.dot(s_hi, w_hi, preferred_element_type=F32)
    acc += jnp.dot(s_lo, w_hi, preferred_element_type=F32)
    acc += jnp.dot(s_hi, w_lo, preferred_element_type=F32)
    o_ref[...] = acc + b_ref[...]


def _ada_mods(cvec, ada_w, ada_b):
    depth, d, n = ada_w.shape
    tn = _largest_divisor(n, (1536, 1024, 512))
    return pl.pallas_call(
        _ada_kernel,
        grid=(depth, n // tn),
        in_specs=[
            pl.BlockSpec((MOD_ROWS, d), lambda l, j: (0, 0)),
            pl.BlockSpec((None, d, tn), lambda l, j: (l, 0, j)),
            pl.BlockSpec((None, 1, tn), lambda l, j: (l, 0, j)),
        ],
        out_specs=pl.BlockSpec((None, MOD_ROWS, tn), lambda l, j: (l, 0, j)),
        out_shape=jax.ShapeDtypeStruct((depth, MOD_ROWS, n), F32),
        compiler_params=pltpu.CompilerParams(
            dimension_semantics=("arbitrary", "arbitrary"), vmem_limit_bytes=VMEM_LIMIT),
        name="ada_mods",
    )(cvec, ada_w, ada_b.reshape(depth, 1, n))


def _modulated_norm(x, g, scale, shift):
    ms = jnp.mean(x * x, axis=-1, keepdims=True)
    return (x * lax.rsqrt(ms + EPS) * g) * (1.0 + scale) + shift


def _silu(a):
    return a / (1.0 + jnp.exp(-a))


def _mod_spec(k, row_to_mod):
    return pl.BlockSpec((None, 1, D_MODEL), lambda i, *_: (row_to_mod(i) * N_MODS + k, 0, 0))


def _rope(z, cos, sin_signed, lane):
    first_half = (lane & (HEAD_DIM // 2)) == 0
    n = z.shape[1]
    partner = jnp.where(first_half, pltpu.roll(z, n - HEAD_DIM // 2, 1), pltpu.roll(z, HEAD_DIM // 2, 1))
    return z * cos + partner * sin_signed


def _head_rms(z, ones_ref, gain):
    zz_hi, zz_lo = _split_bf16(z * z)
    ss = jnp.dot(zz_hi, ones_ref[...], preferred_element_type=F32)
    ss += jnp.dot(zz_lo, ones_ref[...], preferred_element_type=F32)
    return z * lax.rsqrt(ss * (1.0 / HEAD_DIM) + EPS) * gain


def _proj_kernel(x_ref, g_ref, sc_ref, sh_ref, w_ref, cos_ref, sin_ref, ones_ref, qg_ref, kg_ref,
                 o_ref, h_scr):
    j = pl.program_id(1)

    @pl.when(j == 0)
    def _():
        h_scr[...] = _modulated_norm(x_ref[...], g_ref[...], sc_ref[...], sh_ref[...]).astype(BF16)

    z = jnp.dot(h_scr[...], w_ref[...], preferred_element_type=F32)
    lane = lax.broadcasted_iota(jnp.int32, z.shape, 1)
    is_rope = functools.reduce(jnp.logical_or, [j == t for t in ROPE_TILES])
    is_qnorm = functools.reduce(jnp.logical_or, [j == t for t in QNORM_TILES])
    is_kv = j == KV_TILE
    is_plain = jnp.logical_not(is_rope | is_qnorm | is_kv)

    @pl.when(is_plain)
    def _():
        o_ref[...] = z.astype(BF16)

    @pl.when(is_rope)
    def _():
        o_ref[...] = _rope(z, cos_ref[...], sin_ref[...], lane).astype(BF16)

    @pl.when(is_qnorm)
    def _():
        zn = _head_rms(z, ones_ref, qg_ref[...])
        o_ref[...] = _rope(zn, cos_ref[...], sin_ref[...], lane).astype(BF16)

    @pl.when(is_kv)
    def _():
        zn = _head_rms(z, ones_ref, kg_ref[...])
        zr = _rope(zn, cos_ref[...], sin_ref[...], lane)
        is_key_lane = (lane & HEAD_DIM) == 0
        o_ref[...] = jnp.where(is_key_lane, zr, z).astype(BF16)


def _project(xs, mods, norm_g, w, cos_t, sin_t, ones_bd, qg, kg, *, tm, lat_tiles, tiles_per_seq, n_batch):
    rows = xs.shape[0]
    row_to_mod = lambda i: jnp.minimum(i // tiles_per_seq, n_batch)
    rope_row = lambda i: jnp.where(i < lat_tiles, i % tiles_per_seq, tiles_per_seq)
    const = lambda i, j: (0, 0)
    return pl.pallas_call(
        _proj_kernel,
        grid=(rows // tm, IN_W // PROJ_TN),
        in_specs=[
            pl.BlockSpec((tm, D_MODEL), lambda i, j: (i, 0)),
            pl.BlockSpec((1, D_MODEL), const),
            _mod_spec(1, row_to_mod),
            _mod_spec(0, row_to_mod),
            pl.BlockSpec((D_MODEL, PROJ_TN), lambda i, j: (0, j)),
            pl.BlockSpec((tm, PROJ_TN), lambda i, j: (rope_row(i), 0)),
            pl.BlockSpec((tm, PROJ_TN), lambda i, j: (rope_row(i), 0)),
            pl.BlockSpec((PROJ_TN, PROJ_TN), const),
            pl.BlockSpec((1, PROJ_TN), const),
            pl.BlockSpec((1, PROJ_TN), const),
        ],
        out_specs=pl.BlockSpec((tm, PROJ_TN), lambda i, j: (i, j)),
        out_shape=jax.ShapeDtypeStruct((rows, IN_W), BF16),
        scratch_shapes=[pltpu.VMEM((tm, D_MODEL), BF16)],
        compiler_params=pltpu.CompilerParams(
            dimension_semantics=("arbitrary", "arbitrary"), vmem_limit_bytes=VMEM_LIMIT),
        name="in_proj",
    )(xs, norm_g, mods, mods, w, cos_t, sin_t, ones_bd, qg, kg)


STAB_LANE = HEAD_DIM
L_FLOOR = 1e-26
DIFF_VT_ROWS = 2 * HEAD_DIM + 16


def _aug_keys(k):
    lane = lax.broadcasted_iota(jnp.int32, k.shape, 1)
    in_key = lane < HEAD_DIM
    sq = jnp.where(in_key, k * k, 0.0).astype(BF16)
    norms = jnp.dot(sq, jnp.ones((k.shape[1], k.shape[1]), BF16), preferred_element_type=F32)
    ka = jnp.where(in_key, k, jnp.where(lane == STAB_LANE, 1.0, 0.0)).astype(BF16)
    return ka, jnp.max(norms, axis=0, keepdims=True)


def _query_mats(slab, kmax2):
    lane = lax.broadcasted_iota(jnp.int32, slab.shape, 1)
    in_key = lane < HEAD_DIM
    sq = jnp.where(in_key, slab * slab, 0.0).astype(BF16)
    norms = jnp.dot(sq, jnp.ones((slab.shape[1], slab.shape[1]), BF16), preferred_element_type=F32)
    nk = norms * kmax2
    bound = nk * lax.rsqrt(nk + 1e-30)
    plain = jnp.where(in_key, slab, 0.0)
    return jnp.where(lane == STAB_LANE, -bound, plain).astype(BF16), plain.astype(BF16)


def _attend(ka_scr, vt_scr, mats, acc_scr, *, first, n_chunks, tk):
    def scores(r, c, qmat):
        off = pl.multiple_of(c * tk, tk)
        return lax.dot_general(ka_scr[r % ka_scr.shape[0], pl.ds(off, tk), :], qmat, (((1,), (1,)), ((), ())),
                               preferred_element_type=F32)

    acc_scr[...] = jnp.zeros(acc_scr.shape, F32)

    def all_scores(c):
        return tuple(scores(r, c, shifted) for r, (shifted, _) in enumerate(mats))

    def accumulate(c, s_all):
        for r, s in enumerate(s_all):
            acc_scr[r] += jnp.dot(vt_scr[c], jnp.exp(s).astype(BF16), preferred_element_type=F32)

    def fast_body(c, s_cur):
        s_next = all_scores(c + 1)
        accumulate(c, s_cur)
        return s_next

    trips = n_chunks - 1 - first
    s_last = lax.fori_loop(first, n_chunks - 1, fast_body, all_scores(first),
                           unroll=_largest_divisor(trips, (16, 8, 4, 2, 1)) if trips > 0 else 1)
    accumulate(n_chunks - 1, s_last)


def _attend_fallback(ka_scr, vt_scr, mats, m_scr, acc_scr, *, first, n_chunks, tk):
    m_scr[...] = jnp.full(m_scr.shape, NEG_BIG, F32)
    acc_scr[...] = jnp.zeros(acc_scr.shape, F32)

    def body(c, carry):
        off = pl.multiple_of(c * tk, tk)
        for r, (_, plain) in enumerate(mats):
            s = lax.dot_general(ka_scr[r % ka_scr.shape[0], pl.ds(off, tk), :], plain, (((1,), (1,)), ((), ())),
                                preferred_element_type=F32)
            m_old = m_scr[r]
            m_new = jnp.maximum(m_old, jnp.max(s, axis=0, keepdims=True))
            p = jnp.exp(s - m_new).astype(BF16)
            acc_scr[r] = acc_scr[r] * jnp.exp(m_old - m_new) + jnp.dot(vt_scr[c], p, preferred_element_type=F32)
            m_scr[r] = m_new
        return carry

    lax.fori_loop(first, n_chunks, body, 0)


def _softmax_tile(ka_scr, vt_scr, kmax_scr, slabs, m_scr, acc_scr, l_row, *, n_lat_chunks, n_chunks, tk, ctx_tile):
    mats = [_query_mats(slab, kmax_scr[r % kmax_scr.shape[0]]) for r, slab in enumerate(slabs)]

    def run(first):
        _attend(ka_scr, vt_scr, mats, acc_scr, first=first, n_chunks=n_chunks, tk=tk)
        l_min = functools.reduce(
            jnp.minimum, [jnp.min(acc_scr[r, l_row:l_row + 1, :]) for r in range(len(slabs))])

        @pl.when(jnp.logical_not(l_min >= L_FLOOR))
        def _():
            _attend_fallback(ka_scr, vt_scr, mats, m_scr, acc_scr, first=first, n_chunks=n_chunks, tk=tk)

    if ctx_tile is None:
        run(0)
    else:
        pl.when(jnp.logical_not(ctx_tile))(functools.partial(run, 0))
        pl.when(ctx_tile)(functools.partial(run, n_lat_chunks))


def _chunk_rows(c, n_lat_chunks, tk, lat_ref, ctx_ref):
    if c < n_lat_chunks:
        return lat_ref[c * tk:(c + 1) * tk, :]
    return ctx_ref[(c - n_lat_chunks) * tk:(c - n_lat_chunks + 1) * tk, :]


def _diff_attn_kernel(lam_ref, q_ref, kl_ref, vl_ref, kc_ref, vc_ref, g_ref, o_ref,
                      ka_scr, vt_scr, kmax_scr, m_scr, acc_scr, *, nq_lat, n_lat_chunks, n_chunks, tk, ctx_queries):
    dv = 2 * HEAD_DIM

    @pl.when(pl.program_id(2) == 0)
    def _():
        kmax = [None, None]
        for c in range(n_chunks):
            k = _chunk_rows(c, n_lat_chunks, tk, kl_ref, kc_ref).astype(F32)
            for r, keys in enumerate((k, pltpu.roll(k, HEAD_DIM, 1))):
                ka, n2 = _aug_keys(keys)
                ka_scr[r, c * tk:(c + 1) * tk, :] = ka
                kmax[r] = n2 if kmax[r] is None else jnp.maximum(kmax[r], n2)
            vt_scr[c, :dv, :] = _chunk_rows(c, n_lat_chunks, tk, vl_ref, vc_ref).T
            vt_scr[c, dv:, :] = jnp.ones((DIFF_VT_ROWS - dv, tk), BF16)
        kmax_scr[0] = kmax[0]
        kmax_scr[1] = kmax[1]

    q = q_ref[...].astype(F32)
    ctx_tile = (pl.program_id(2) == nq_lat) if ctx_queries else None
    _softmax_tile(ka_scr, vt_scr, kmax_scr, [q, pltpu.roll(q, HEAD_DIM, 1)], m_scr, acc_scr, dv,
                  n_lat_chunks=n_lat_chunks, n_chunks=n_chunks, tk=tk, ctx_tile=ctx_tile)
    a1, a2 = acc_scr[0], acc_scr[1]
    yt = a1[:dv] * (1.0 / a1[dv:dv + 1]) - lam_ref[0] * (a2[:dv] * (1.0 / a2[dv:dv + 1]))
    y = yt.T
    ms = jnp.mean(y * y, axis=-1, keepdims=True)
    o_ref[...] = (y * lax.rsqrt(ms + EPS) * g_ref[...]).astype(BF16)


def _gqa_attn_kernel(q_ref, kvl_ref, kvc_ref, o_ref, ka_scr, vt_scr, kmax_scr, m_scr, acc_scr,
                     *, nq_lat, n_lat_chunks, n_chunks, tk, ctx_queries):
    @pl.when(pl.program_id(2) == 0)
    def _():
        kmax = None
        for c in range(n_chunks):
            kv = _chunk_rows(c, n_lat_chunks, tk, kvl_ref, kvc_ref)
            ka, n2 = _aug_keys(kv.astype(F32))
            ka_scr[0, c * tk:(c + 1) * tk, :] = ka
            kmax = n2 if kmax is None else jnp.maximum(kmax, n2)
            t = kv.T
            row = lax.broadcasted_iota(jnp.int32, t.shape, 0)
            vt_scr[c] = jnp.where(row < HEAD_DIM, jnp.ones_like(t), t)
        kmax_scr[0] = kmax

    qf = q_ref[...].astype(F32)
    tq = qf.shape[0]
    lane = lax.broadcasted_iota(jnp.int32, (tq, 2 * HEAD_DIM), 1)
    slabs = []
    for r in range(GQA_REP):
        slab = qf[:, (r // 2) * 2 * HEAD_DIM:(r // 2 + 1) * 2 * HEAD_DIM]
        slabs.append(pltpu.roll(slab, HEAD_DIM, 1) if r % 2 else slab)
    ctx_tile = (pl.program_id(2) == nq_lat) if ctx_queries else None
    _softmax_tile(ka_scr, vt_scr, kmax_scr, slabs, m_scr, acc_scr, 0,
                  n_lat_chunks=n_lat_chunks, n_chunks=n_chunks, tk=tk, ctx_tile=ctx_tile)

    def head_out(r):
        acc = acc_scr[r]
        return (acc * (1.0 / acc[0:1])).T

    for j in range(GQA_REP // 2):
        pair = jnp.where(lane < HEAD_DIM, pltpu.roll(head_out(2 * j), HEAD_DIM, 1), head_out(2 * j + 1))
        o_ref[:, j * 2 * HEAD_DIM:(j + 1) * 2 * HEAD_DIM] = pair.astype(BF16)


def _attention(p, lam, subln_gain, *, B, S, C, tiles, ctx_queries):
    tq, tk = tiles.tq, tiles.tk
    nq_lat = S // tq
    nq = nq_lat + (1 if ctx_queries else 0)
    ctx_blk0 = (B * S) // C
    rows_out = B * S + (B * C if ctx_queries else 0)
    T = S + C
    statics = dict(nq_lat=nq_lat, n_lat_chunks=S // tk, n_chunks=T // tk, tk=tk, ctx_queries=ctx_queries)

    def q_row(b, qi):
        if not ctx_queries:
            return b * nq_lat + qi
        return jnp.where(qi < nq_lat, b * nq_lat + qi, ctx_blk0 + b)

    cparams = pltpu.CompilerParams(
        dimension_semantics=("arbitrary", "arbitrary", "arbitrary"), vmem_limit_bytes=VMEM_LIMIT)
    w2 = 2 * HEAD_DIM
    yd = pl.pallas_call(
        functools.partial(_diff_attn_kernel, **statics),
        grid=(B, N_DIFF_HEADS, nq),
        in_specs=[
            pl.BlockSpec(memory_space=pltpu.SMEM),
            pl.BlockSpec((tq, w2), lambda b, h, qi: (q_row(b, qi), DQ_OFF // w2 + h)),
            pl.BlockSpec((S, w2), lambda b, h, qi: (b, DK_OFF // w2 + h)),
            pl.BlockSpec((S, w2), lambda b, h, qi: (b, DV_OFF // w2 + h)),
            pl.BlockSpec((C, w2), lambda b, h, qi: (ctx_blk0 + b, DK_OFF // w2 + h)),
            pl.BlockSpec((C, w2), lambda b, h, qi: (ctx_blk0 + b, DV_OFF // w2 + h)),
            pl.BlockSpec((1, w2), lambda b, h, qi: (0, 0)),
        ],
        out_specs=pl.BlockSpec((tq, w2), lambda b, h, qi: (q_row(b, qi), h)),
        out_shape=jax.ShapeDtypeStruct((rows_out, N_DIFF_HEADS * w2), BF16),
        scratch_shapes=[
            pltpu.VMEM((2, T, w2), BF16),
            pltpu.VMEM((T // tk, DIFF_VT_ROWS, tk), BF16),
            pltpu.VMEM((2, 1, w2), F32),
            pltpu.VMEM((2, 1, tq), F32),
            pltpu.VMEM((2, DIFF_VT_ROWS, tq), F32),
        ],
        compiler_params=cparams,
        name="diff_attn",
    )(lam, p, p, p, p, p, subln_gain)

    wq = GQA_REP * HEAD_DIM
    yg = pl.pallas_call(
        functools.partial(_gqa_attn_kernel, **statics),
        grid=(B, N_GQA_KV, nq),
        in_specs=[
            pl.BlockSpec((tq, wq), lambda b, g, qi: (q_row(b, qi), GQ_OFF // wq + g)),
            pl.BlockSpec((S, w2), lambda b, g, qi: (b, GKV_OFF // w2 + g)),
            pl.BlockSpec((C, w2), lambda b, g, qi: (ctx_blk0 + b, GKV_OFF // w2 + g)),
        ],
        out_specs=pl.BlockSpec((tq, wq), lambda b, g, qi: (q_row(b, qi), g)),
        out_shape=jax.ShapeDtypeStruct((rows_out, N_GQA_HEADS * HEAD_DIM), BF16),
        scratch_shapes=[
            pltpu.VMEM((1, T, w2), BF16),
            pltpu.VMEM((T // tk, w2, tk), BF16),
            pltpu.VMEM((1, 1, w2), F32),
            pltpu.VMEM((GQA_REP, 1, tq), F32),
            pltpu.VMEM((GQA_REP, w2, tq), F32),
        ],
        compiler_params=cparams,
        name="gqa_attn",
    )(p, p, p)
    return yd, yg


def _mixer_kernel(x_ref, yd_ref, yg_ref, gd_ref, gg_ref, gt_ref, wpd_ref, wpg_ref, wo_ref, o_ref):
    pd = jnp.dot(yd_ref[...], wpd_ref[...], preferred_element_type=F32)
    pg = jnp.dot(yg_ref[...], wpg_ref[...], preferred_element_type=F32)
    m = jax.nn.sigmoid(gd_ref[...].astype(F32)) * pd + jax.nn.sigmoid(gg_ref[...].astype(F32)) * pg
    o_ref[...] = x_ref[...] + gt_ref[...] * jnp.dot(m.astype(BF16), wo_ref[...], preferred_element_type=F32)


def _mixer_out(xs, yd, yg, p, mods, wpd, wpg, wo, *, rows, tm, tiles_per_seq, n_batch):
    row_to_mod = lambda i: jnp.minimum(i // tiles_per_seq, n_batch)
    row_tile = pl.BlockSpec((tm, D_MODEL), lambda i: (i, 0))
    weight = pl.BlockSpec((D_MODEL, D_MODEL), lambda i: (0, 0))
    return pl.pallas_call(
        _mixer_kernel,
        grid=(rows // tm,),
        in_specs=[
            row_tile, row_tile, row_tile,
            pl.BlockSpec((tm, D_MODEL), lambda i: (i, GATE_OFF // D_MODEL)),
            pl.BlockSpec((tm, D_MODEL), lambda i: (i, GATE_OFF // D_MODEL + 1)),
            _mod_spec(2, row_to_mod),
            weight, weight, weight,
        ],
        out_specs=row_tile,
        out_shape=jax.ShapeDtypeStruct((rows, D_MODEL), F32),
        compiler_params=pltpu.CompilerParams(
            dimension_semantics=("arbitrary",), vmem_limit_bytes=VMEM_LIMIT),
        name="mixer_out",
    )(xs, yd, yg, p, p, mods, wpd, wpg, wo)


def _ffn_kernel(x_ref, g_ref, sc_ref, sh_ref, gt_ref, w1_ref, w3_ref, w2_ref, o_ref, h_scr, acc_scr):
    f = pl.program_id(1)

    @pl.when(f == 0)
    def _():
        h_scr[...] = _modulated_norm(x_ref[...], g_ref[...], sc_ref[...], sh_ref[...]).astype(BF16)
        acc_scr[...] = jnp.zeros_like(acc_scr)

    h = h_scr[...]
    a = jnp.dot(h, w1_ref[...], preferred_element_type=F32)
    b = jnp.dot(h, w3_ref[...], preferred_element_type=F32)
    acc_scr[...] += jnp.dot((_silu(a) * b).astype(BF16), w2_ref[...], preferred_element_type=F32)

    @pl.when(f == pl.num_programs(1) - 1)
    def _():
        o_ref[...] = x_ref[...] + gt_ref[...] * acc_scr[...]


def _ffn(xs, mods, norm_g, w1, w3, w2, *, tm, tf, tiles_per_seq, n_batch):
    rows = xs.shape[0]
    d_ff = w1.shape[1]
    row_to_mod = lambda i: jnp.minimum(i // tiles_per_seq, n_batch)
    row_tile = pl.BlockSpec((tm, D_MODEL), lambda i, f: (i, 0))
    return pl.pallas_call(
        _ffn_kernel,
        grid=(rows // tm, d_ff // tf),
        in_specs=[
            row_tile,
            pl.BlockSpec((1, D_MODEL), lambda i, f: (0, 0)),
            _mod_spec(4, row_to_mod), _mod_spec(3, row_to_mod), _mod_spec(5, row_to_mod),
            pl.BlockSpec((D_MODEL, tf), lambda i, f: (0, f)),
            pl.BlockSpec((D_MODEL, tf), lambda i, f: (0, f)),
            pl.BlockSpec((tf, D_MODEL), lambda i, f: (f, 0)),
        ],
        out_specs=row_tile,
        out_shape=jax.ShapeDtypeStruct((rows, D_MODEL), F32),
        scratch_shapes=[pltpu.VMEM((tm, D_MODEL), BF16), pltpu.VMEM((tm, D_MODEL), F32)],
        compiler_params=pltpu.CompilerParams(
            dimension_semantics=("arbitrary", "arbitrary"), vmem_limit_bytes=VMEM_LIMIT),
        name="ffn_swiglu",
    )(xs, norm_g, mods, mods, mods, w1, w3, w2)


def _top2_gates(logits):
    lane = lax.broadcasted_iota(jnp.int32, logits.shape, 1)
    n_lanes = logits.shape[1]
    lg = jnp.where(lane < N_EXPERTS, logits, NEG_BIG)
    m1 = jnp.max(lg, axis=-1, keepdims=True)
    i1 = jnp.min(jnp.where(lg == m1, lane, n_lanes), axis=-1, keepdims=True)
    lg2 = jnp.where(lane == i1, NEG_BIG, lg)
    m2 = jnp.max(lg2, axis=-1, keepdims=True)
    i2 = jnp.min(jnp.where(lg2 == m2, lane, n_lanes), axis=-1, keepdims=True)
    e2 = jnp.exp(m2 - m1)
    w_top = 1.0 / (1.0 + e2)
    return jnp.where(lane == i1, w_top, 0.0) + jnp.where(lane == i2, e2 * w_top, 0.0)


def _moe_kernel(x_ref, g_ref, sc_ref, sh_ref, gt_ref, rhi_ref, rlo_ref, fg_ref, w1_ref, w3_ref, w2_ref,
                o_ref, h_scr, gate_scr, acc_scr):
    e = pl.program_id(1)
    f = pl.program_id(2)

    @pl.when((e == 0) & (f == 0))
    def _():
        h = _modulated_norm(x_ref[...], g_ref[...], sc_ref[...], sh_ref[...])
        h_hi, h_lo = _split_bf16(h)
        h_scr[...] = h_hi
        logits = jnp.dot(h_hi, rhi_ref[...], preferred_element_type=F32)
        logits += jnp.dot(h_lo, rhi_ref[...], preferred_element_type=F32)
        logits += jnp.dot(h_hi, rlo_ref[...], preferred_element_type=F32)
        gate = _top2_gates(logits)
        lane = lax.broadcasted_iota(jnp.int32, gate.shape, 1)
        for k in range(N_EXPERTS):
            col = jnp.sum(jnp.where(lane == k, gate, 0.0), axis=-1, keepdims=True)
            gate_scr[k] = jnp.broadcast_to(col, gate.shape)
        acc_scr[...] = jnp.zeros_like(acc_scr)

    h = h_scr[...]
    a = jnp.dot(h, w1_ref[...], preferred_element_type=F32)
    b = jnp.dot(h, w3_ref[...], preferred_element_type=F32)
    hid = _silu(a) * b * gate_scr[e][:, :1]
    acc_scr[...] += jnp.dot(hid.astype(BF16), w2_ref[...], preferred_element_type=F32)

    @pl.when((e == pl.num_programs(1) - 1) & (f == pl.num_programs(2) - 1))
    def _():
        y = x_ref[...] + gt_ref[...] * acc_scr[...]
        ms = jnp.mean(y * y, axis=-1, keepdims=True)
        o_ref[...] = y * lax.rsqrt(ms + EPS) * fg_ref[...]


def _moe(xs, mods, norm_g, r_hi, r_lo, final_g, w1, w3, w2, *, rows, tm, tf, tiles_per_seq, n_batch):
    d_ff = w1.shape[2]
    row_to_mod = lambda i: jnp.minimum(i // tiles_per_seq, n_batch)
    row_tile = pl.BlockSpec((tm, D_MODEL), lambda i, e, f: (i, 0))
    vec = pl.BlockSpec((1, D_MODEL), lambda i, e, f: (0, 0))
    router = pl.BlockSpec((D_MODEL, 128), lambda i, e, f: (0, 0))
    return pl.pallas_call(
        _moe_kernel,
        grid=(rows // tm, N_EXPERTS, d_ff // tf),
        in_specs=[
            row_tile, vec,
            _mod_spec(4, row_to_mod), _mod_spec(3, row_to_mod), _mod_spec(5, row_to_mod),
            router, router, vec,
            pl.BlockSpec((None, D_MODEL, tf), lambda i, e, f: (e, 0, f)),
            pl.BlockSpec((None, D_MODEL, tf), lambda i, e, f: (e, 0, f)),
            pl.BlockSpec((None, tf, D_MODEL), lambda i, e, f: (e, f, 0)),
        ],
        out_specs=row_tile,
        out_shape=jax.ShapeDtypeStruct((rows, D_MODEL), F32),
        scratch_shapes=[
            pltpu.VMEM((tm, D_MODEL), BF16),
            pltpu.VMEM((N_EXPERTS, tm, 128), F32),
            pltpu.VMEM((tm, D_MODEL), F32),
        ],
        compiler_params=pltpu.CompilerParams(
            dimension_semantics=("arbitrary", "arbitrary", "arbitrary"), vmem_limit_bytes=VMEM_LIMIT),
        name="moe_swiglu",
    )(xs, norm_g, mods, mods, mods, r_hi, r_lo, final_g, w1, w3, w2)


def _deinterleave(n=HEAD_DIM):
    return np.concatenate([np.arange(0, n, 2), np.arange(1, n, 2)])


def _proj_columns():
    de = _deinterleave()
    o_dq, o_gq, o_dk, o_dv, o_gk, o_gv, o_gate = 0, 1024, 2048, 3072, 4096, 4352, 4608
    cols = np.zeros(IN_W, np.int32)
    scale = np.ones(IN_W, np.float32)
    cols[GATE_OFF:GATE_OFF + 2048] = o_gate + np.arange(2048)
    for h in range(N_DIFF_HEADS):
        for c in range(2):
            dst = 128 * h + 64 * c
            cols[DQ_OFF + dst:DQ_OFF + dst + 64] = o_dq + dst + de
            cols[DK_OFF + dst:DK_OFF + dst + 64] = o_dk + dst + de
    scale[DQ_OFF:DQ_OFF + 1024] = HEAD_DIM ** -0.5
    for j in range(N_GQA_HEADS):
        cols[GQ_OFF + 64 * j:GQ_OFF + 64 * j + 64] = o_gq + 64 * j + de
    cols[DV_OFF:DV_OFF + 1024] = o_dv + np.arange(1024)
    for g in range(N_GQA_KV):
        cols[GKV_OFF + 128 * g:GKV_OFF + 128 * g + 64] = o_gk + 64 * g + de
        cols[GKV_OFF + 128 * g + 64:GKV_OFF + 128 * g + 128] = o_gv + 64 * g + np.arange(64)
    return cols, scale


def _rope_tables(S, pad_rows):
    rows = S // GRID_W
    row = jnp.repeat(jnp.arange(rows, dtype=F32), GRID_W)
    col = jnp.tile(jnp.arange(GRID_W, dtype=F32), rows)
    half = HEAD_DIM // 2
    inv_freq = ROPE_THETA ** (-jnp.arange(0, half, 2, dtype=F32) / half)
    ang = jnp.concatenate([row[:, None] * inv_freq, col[:, None] * inv_freq], axis=-1)
    cos, sin = jnp.cos(ang), jnp.sin(ang)
    reps = PROJ_TN // HEAD_DIM
    cos_t = jnp.tile(jnp.concatenate([cos, cos], axis=-1), (1, reps))
    sin_t = jnp.tile(jnp.concatenate([-sin, sin], axis=-1), (1, reps))
    cos_t = jnp.concatenate([cos_t, jnp.ones((pad_rows, PROJ_TN), F32)], axis=0)
    sin_t = jnp.concatenate([sin_t, jnp.zeros((pad_rows, PROJ_TN), F32)], axis=0)
    return cos_t, sin_t


def kernel(x, c, ctx, c_ctx, ada_w, ada_b, norm_attn_g, norm_ffn_g, w_in, q_norm_g, k_norm_g, diff_lambda,
           diff_subln_g, w_proj_diff, w_proj_gqa, w_out, ffn_w1, ffn_w3, ffn_w2, moe_router, moe_w1, moe_w3,
           moe_w2, final_norm_g):
    B, S, D = x.shape
    C = ctx.shape[1]
    depth = ada_w.shape[0]
    assert D == D_MODEL and depth == 2 and B + 1 <= MOD_ROWS
    assert w_in.shape[2] == IN_W and moe_router.shape[2] == N_EXPERTS
    tiles = _pick_tiles(B, S, C, ffn_w1.shape[2], moe_w1.shape[3])
    n_lat = B * S

    cvec = jnp.zeros((MOD_ROWS, D), F32).at[:B].set(c).at[B].set(c_ctx)
    mods_all = _ada_mods(cvec, ada_w, ada_b).reshape(depth, MOD_ROWS * N_MODS, 1, D)

    cols, col_scale = _proj_columns()
    de = _deinterleave()
    cos_t, sin_t = _rope_tables(S, tiles.tm_proj)
    blk = np.arange(PROJ_TN) // HEAD_DIM
    ones_bd = jnp.asarray(blk[:, None] == blk[None, :], BF16)
    is_key_lane = (np.arange(PROJ_TN) % (2 * HEAD_DIM)) < HEAD_DIM

    xs = jnp.concatenate([x.reshape(n_lat, D), ctx.reshape(B * C, D)], axis=0)
    for l in range(depth):
        last = l == depth - 1
        mods = mods_all[l]
        lam_init = 0.8 - 0.6 * math.exp(-0.3 * l)
        lq1, lk1, lq2, lk2 = diff_lambda[l]
        lam = (jnp.exp(jnp.sum(lq1 * lk1)) - jnp.exp(jnp.sum(lq2 * lk2)) + lam_init).reshape(1).astype(F32)
        w = (jnp.take(w_in[l], cols, axis=1) * col_scale).astype(BF16)
        qg = jnp.tile(q_norm_g[l][de] * HEAD_DIM ** -0.5, PROJ_TN // HEAD_DIM).reshape(1, PROJ_TN)
        kg = jnp.where(is_key_lane, jnp.tile(k_norm_g[l][de], PROJ_TN // HEAD_DIM), 1.0).reshape(1, PROJ_TN)
        subln = (diff_subln_g[l] * (1.0 - lam_init)).reshape(1, 2 * HEAD_DIM)

        p = _project(xs, mods, norm_attn_g[l].reshape(1, D), w, cos_t, sin_t, ones_bd, qg, kg,
                     tm=tiles.tm_proj, lat_tiles=n_lat // tiles.tm_proj, tiles_per_seq=S // tiles.tm_proj,
                     n_batch=B)
        yd, yg = _attention(p, lam, subln, B=B, S=S, C=C, tiles=tiles, ctx_queries=not last)
        rows = n_lat if last else n_lat + B * C
        xs = _mixer_out(xs, yd, yg, p, mods, w_proj_diff[l].astype(BF16), w_proj_gqa[l].astype(BF16),
                        w_out[l].astype(BF16), rows=rows, tm=tiles.tm_mix, tiles_per_seq=S // tiles.tm_mix,
                        n_batch=B)
        i = l // 2
        if l % 2 == 0:
            xs = _ffn(xs, mods, norm_ffn_g[l].reshape(1, D), ffn_w1[i].astype(BF16), ffn_w3[i].astype(BF16),
                      ffn_w2[i].astype(BF16), tm=tiles.tm_ffn, tf=tiles.tf_ffn,
                      tiles_per_seq=S // tiles.tm_ffn, n_batch=B)
        else:
            r_pad = jnp.zeros((D, 128), F32).at[:, :N_EXPERTS].set(moe_router[i])
            r_hi = r_pad.astype(BF16)
            r_lo = (r_pad - r_hi.astype(F32)).astype(BF16)
            xs = _moe(xs, mods, norm_ffn_g[l].reshape(1, D), r_hi, r_lo, final_norm_g.reshape(1, D),
                      moe_w1[i].astype(BF16), moe_w3[i].astype(BF16), moe_w2[i].astype(BF16),
                      rows=rows, tm=tiles.tm_ffn, tf=tiles.tf_moe, tiles_per_seq=S // tiles.tm_ffn, n_batch=B)
    return xs.reshape(B, S, D)
```

```python
import functools
import math
from typing import NamedTuple

import numpy as np
import jax
import jax.numpy as jnp
from jax import lax
from jax.experimental import pallas as pl
from jax.experimental.pallas import tpu as pltpu

F32 = jnp.float32
BF16 = jnp.bfloat16

D_MODEL = 1024
HEAD_DIM = 64
N_DIFF_HEADS = 8
N_GQA_HEADS = 16
N_GQA_KV = 4
GQA_REP = N_GQA_HEADS // N_GQA_KV
N_EXPERTS = 8
GRID_W = 64
ROPE_THETA = 10000.0
EPS = 1e-6
N_MODS = 6
MOD_ROWS = 16
NEG_BIG = -1e30

GATE_OFF = 0
DQ_OFF = 2048
GQ_OFF = 3072
DK_OFF = 4096
DV_OFF = 5120
GKV_OFF = 6144
IN_W = 6656
PROJ_TN = 512
ROPE_TILES = (4, 5, 8, 9)
QNORM_TILES = (6, 7)
KV_TILE = 12

VMEM_LIMIT = 52 * 1024 * 1024


class Tiles(NamedTuple):
    tm_proj: int
    tq: int
    tk: int
    tm_mix: int
    tm_ffn: int
    tf_ffn: int
    tf_moe: int
    tm_moe: int


def _largest_divisor(n, candidates):
    for c in candidates:
        if n % c == 0:
            return c
    raise ValueError(f"no tile in {candidates} divides {n}")


def _pick_tiles(B, S, C, d_ff, d_ff_e):
    rows_common = math.gcd(S, B * C)
    tm = _largest_divisor(rows_common, (512, 256, 128))
    return Tiles(
        tm_proj=tm,
        tq=C,
        tk=_largest_divisor(math.gcd(S, C), (256, 128)),
        tm_mix=tm,
        tm_ffn=_largest_divisor(rows_common, (1024, 512, 256, 128)),
        tf_ffn=_largest_divisor(d_ff, (1408, 1024, 512, 256, 128)),
        tf_moe=_largest_divisor(d_ff_e, (896, 512, 256, 128)),
        tm_moe=_largest_divisor(2 * B * S, (512, 256)),
    )


def _split_bf16(v):
    hi = v.astype(BF16)
    lo = (v - hi.astype(F32)).astype(BF16)
    return hi, lo


def _ada_kernel(c_ref, w_ref, b_ref, o_ref):
    c = c_ref[...]
    s = c / (1.0 + jnp.exp(-c))
    s_hi, s_lo = _split_bf16(s)
    w_hi, w_lo = _split_bf16(w_ref[...])
    acc = jnp.dot(s_hi, w_hi, preferred_element_type=F32)
    acc += jnp.dot(s_lo, w_hi, preferred_element_type=F32)
    acc += jnp.dot(s_hi, w_lo, preferred_element_type=F32)
    o_ref[...] = acc + b_ref[...]


def _ada_mods(cvec, ada_w, ada_b):
    depth, d, n = ada_w.shape
    tn = _largest_divisor(n, (1536, 1024, 512))
    return pl.pallas_call(
        _ada_kernel,
        grid=(depth, n // tn),
        in_specs=[
            pl.BlockSpec((MOD_ROWS, d), lambda l, j: (0, 0)),
            pl.BlockSpec((None, d, tn), lambda l, j: (l, 0, j)),
            pl.BlockSpec((None, 1, tn), lambda l, j: (l, 0, j)),
        ],
        out_specs=pl.BlockSpec((None, MOD_ROWS, tn), lambda l, j: (l, 0, j)),
        out_shape=jax.ShapeDtypeStruct((depth, MOD_ROWS, n), F32),
        compiler_params=pltpu.CompilerParams(
            dimension_semantics=("arbitrary", "arbitrary"), vmem_limit_bytes=VMEM_LIMIT),
        name="ada_mods",
    )(cvec, ada_w, ada_b.reshape(depth, 1, n))


def _modulated_norm(x, g, scale, shift):
    ms = jnp.mean(x * x, axis=-1, keepdims=True)
    return (x * lax.rsqrt(ms + EPS) * g) * (1.0 + scale) + shift


def _silu(a):
    return a / (1.0 + jnp.exp(-a))


def _mod_spec(k, row_to_mod):
    return pl.BlockSpec((None, 1, D_MODEL), lambda i, *_: (row_to_mod(i) * N_MODS + k, 0, 0))


def _rope(z, cos, sin_signed, lane):
    first_half = (lane & (HEAD_DIM // 2)) == 0
    n = z.shape[1]
    partner = jnp.where(first_half, pltpu.roll(z, n - HEAD_DIM // 2, 1), pltpu.roll(z, HEAD_DIM // 2, 1))
    return z * cos + partner * sin_signed


def _head_rms(z, ones_ref, gain):
    zz_hi, zz_lo = _split_bf16(z * z)
    ss = jnp.dot(zz_hi, ones_ref[...], preferred_element_type=F32)
    ss += jnp.dot(zz_lo, ones_ref[...], preferred_element_type=F32)
    return z * lax.rsqrt(ss * (1.0 / HEAD_DIM) + EPS) * gain


def _proj_kernel(x_ref, g_ref, sc_ref, sh_ref, w_ref, cos_ref, sin_ref, ones_ref, qg_ref, kg_ref,
                 o_ref, h_scr):
    j = pl.program_id(1)

    @pl.when(j == 0)
    def _():
        h_scr[...] = _modulated_norm(x_ref[...], g_ref[...], sc_ref[...], sh_ref[...]).astype(BF16)

    z = jnp.dot(h_scr[...], w_ref[...], preferred_element_type=F32)
    lane = lax.broadcasted_iota(jnp.int32, z.shape, 1)
    is_rope = functools.reduce(jnp.logical_or, [j == t for t in ROPE_TILES])
    is_qnorm = functools.reduce(jnp.logical_or, [j == t for t in QNORM_TILES])
    is_kv = j == KV_TILE
    is_plain = jnp.logical_not(is_rope | is_qnorm | is_kv)

    @pl.when(is_plain)
    def _():
        o_ref[...] = z.astype(BF16)

    @pl.when(is_rope)
    def _():
        o_ref[...] = _rope(z, cos_ref[...], sin_ref[...], lane).astype(BF16)

    @pl.when(is_qnorm)
    def _():
        zn = _head_rms(z, ones_ref, qg_ref[...])
        o_ref[...] = _rope(zn, cos_ref[...], sin_ref[...], lane).astype(BF16)

    @pl.when(is_kv)
    def _():
        zn = _head_rms(z, ones_ref, kg_ref[...])
        zr = _rope(zn, cos_ref[...], sin_ref[...], lane)
        is_key_lane = (lane & HEAD_DIM) == 0
        o_ref[...] = jnp.where(is_key_lane, zr, z).astype(BF16)


def _project(xs, mods, norm_g, w, cos_t, sin_t, ones_bd, qg, kg, *, tm, lat_tiles, tiles_per_seq, n_batch):
    rows = xs.shape[0]
    row_to_mod = lambda i: jnp.minimum(i // tiles_per_seq, n_batch)
    rope_row = lambda i: jnp.where(i < lat_tiles, i % tiles_per_seq, tiles_per_seq)
    const = lambda i, j: (0, 0)
    return pl.pallas_call(
        _proj_kernel,
        grid=(rows // tm, IN_W // PROJ_TN),
        in_specs=[
            pl.BlockSpec((tm, D_MODEL), lambda i, j: (i, 0)),
            pl.BlockSpec((1, D_MODEL), const),
            _mod_spec(1, row_to_mod),
            _mod_spec(0, row_to_mod),
            pl.BlockSpec((D_MODEL, PROJ_TN), lambda i, j: (0, j)),
            pl.BlockSpec((tm, PROJ_TN), lambda i, j: (rope_row(i), 0)),
            pl.BlockSpec((tm, PROJ_TN), lambda i, j: (rope_row(i), 0)),
            pl.BlockSpec((PROJ_TN, PROJ_TN), const),
            pl.BlockSpec((1, PROJ_TN), const),
            pl.BlockSpec((1, PROJ_TN), const),
        ],
        out_specs=pl.BlockSpec((tm, PROJ_TN), lambda i, j: (i, j)),
        out_shape=jax.ShapeDtypeStruct((rows, IN_W), BF16),
        scratch_shapes=[pltpu.VMEM((tm, D_MODEL), BF16)],
        compiler_params=pltpu.CompilerParams(
            dimension_semantics=("arbitrary", "arbitrary"), vmem_limit_bytes=VMEM_LIMIT),
        name="in_proj",
    )(xs, norm_g, mods, mods, w, cos_t, sin_t, ones_bd, qg, kg)


STAB_LANE = HEAD_DIM
L_FLOOR = 1e-26
DIFF_VT_ROWS = 2 * HEAD_DIM + 16


def _aug_keys(k):
    lane = lax.broadcasted_iota(jnp.int32, k.shape, 1)
    in_key = lane < HEAD_DIM
    sq = jnp.where(in_key, k * k, 0.0).astype(BF16)
    norms = jnp.dot(sq, jnp.ones((k.shape[1], k.shape[1]), BF16), preferred_element_type=F32)
    ka = jnp.where(in_key, k, jnp.where(lane == STAB_LANE, 1.0, 0.0)).astype(BF16)
    return ka, jnp.max(norms, axis=0, keepdims=True)


def _query_mats(slab, kmax2):
    lane = lax.broadcasted_iota(jnp.int32, slab.shape, 1)
    in_key = lane < HEAD_DIM
    sq = jnp.where(in_key, slab * slab, 0.0).astype(BF16)
    norms = jnp.dot(sq, jnp.ones((slab.shape[1], slab.shape[1]), BF16), preferred_element_type=F32)
    nk = norms * kmax2
    bound = nk * lax.rsqrt(nk + 1e-30)
    plain = jnp.where(in_key, slab, 0.0)
    return jnp.where(lane == STAB_LANE, -bound, plain).astype(BF16), plain.astype(BF16)


def _attend(ka_scr, vt_scr, mats, acc_scr, *, first, n_chunks, tk):
    def scores(r, c, qmat):
        off = pl.multiple_of(c * tk, tk)
        return lax.dot_general(ka_scr[r % ka_scr.shape[0], pl.ds(off, tk), :], qmat, (((1,), (1,)), ((), ())),
                               preferred_element_type=F32)

    acc_scr[...] = jnp.zeros(acc_scr.shape, F32)

    def all_scores(c):
        return tuple(scores(r, c, shifted) for r, (shifted, _) in enumerate(mats))

    def accumulate(c, s_all):
        for r, s in enumerate(s_all):
            acc_scr[r] += jnp.dot(vt_scr[c], jnp.exp(s).astype(BF16), preferred_element_type=F32)

    def fast_body(c, s_cur):
        s_next = all_scores(c + 1)
        accumulate(c, s_cur)
        return s_next

    trips = n_chunks - 1 - first
    s_last = lax.fori_loop(first, n_chunks - 1, fast_body, all_scores(first),
                           unroll=_largest_divisor(trips, (16, 8, 4, 2, 1)) if trips > 0 else 1)
    accumulate(n_chunks - 1, s_last)


def _attend_fallback(ka_scr, vt_scr, mats, m_scr, acc_scr, *, first, n_chunks, tk):
    m_scr[...] = jnp.full(m_scr.shape, NEG_BIG, F32)
    acc_scr[...] = jnp.zeros(acc_scr.shape, F32)

    def body(c, carry):
        off = pl.multiple_of(c * tk, tk)
        for r, (_, plain) in enumerate(mats):
            s = lax.dot_general(ka_scr[r % ka_scr.shape[0], pl.ds(off, tk), :], plain, (((1,), (1,)), ((), ())),
                                preferred_element_type=F32)
            m_old = m_scr[r]
            m_new = jnp.maximum(m_old, jnp.max(s, axis=0, keepdims=True))
            p = jnp.exp(s - m_new).astype(BF16)
            acc_scr[r] = acc_scr[r] * jnp.exp(m_old - m_new) + jnp.dot(vt_scr[c], p, preferred_element_type=F32)
            m_scr[r] = m_new
        return carry

    lax.fori_loop(first, n_chunks, body, 0)


def _softmax_tile(ka_scr, vt_scr, kmax_scr, slabs, m_scr, acc_scr, l_row, *, n_lat_chunks, n_chunks, tk, ctx_tile):
    mats = [_query_mats(slab, kmax_scr[r % kmax_scr.shape[0]]) for r, slab in enumerate(slabs)]

    def run(first):
        _attend(ka_scr, vt_scr, mats, acc_scr, first=first, n_chunks=n_chunks, tk=tk)
        l_min = functools.reduce(
            jnp.minimum, [jnp.min(acc_scr[r, l_row:l_row + 1, :]) for r in range(len(slabs))])

        @pl.when(jnp.logical_not(l_min >= L_FLOOR))
        def _():
            _attend_fallback(ka_scr, vt_scr, mats, m_scr, acc_scr, first=first, n_chunks=n_chunks, tk=tk)

    if ctx_tile is None:
        run(0)
    else:
        pl.when(jnp.logical_not(ctx_tile))(functools.partial(run, 0))
        pl.when(ctx_tile)(functools.partial(run, n_lat_chunks))


def _chunk_rows(c, n_lat_chunks, tk, lat_ref, ctx_ref):
    if c < n_lat_chunks:
        return lat_ref[c * tk:(c + 1) * tk, :]
    return ctx_ref[(c - n_lat_chunks) * tk:(c - n_lat_chunks + 1) * tk, :]


def _diff_attn_kernel(lam_ref, q_ref, kl_ref, vl_ref, kc_ref, vc_ref, g_ref, o_ref,
                      ka_scr, vt_scr, kmax_scr, m_scr, acc_scr, *, nq_lat, n_lat_chunks, n_chunks, tk, ctx_queries):
    dv = 2 * HEAD_DIM

    @pl.when(pl.program_id(2) == 0)
    def _():
        kmax = [None, None]
        for c in range(n_chunks):
            k = _chunk_rows(c, n_lat_chunks, tk, kl_ref, kc_ref).astype(F32)
            for r, keys in enumerate((k, pltpu.roll(k, HEAD_DIM, 1))):
                ka, n2 = _aug_keys(keys)
                ka_scr[r, c * tk:(c + 1) * tk, :] = ka
                kmax[r] = n2 if kmax[r] is None else jnp.maximum(kmax[r], n2)
            vt_scr[c, :dv, :] = _chunk_rows(c, n_lat_chunks, tk, vl_ref, vc_ref).T
            vt_scr[c, dv:, :] = jnp.ones((DIFF_VT_ROWS - dv, tk), BF16)
        kmax_scr[0] = kmax[0]
        kmax_scr[1] = kmax[1]

    q = q_ref[...].astype(F32)
    ctx_tile = (pl.program_id(2) == nq_lat) if ctx_queries else None
    _softmax_tile(ka_scr, vt_scr, kmax_scr, [q, pltpu.roll(q, HEAD_DIM, 1)], m_scr, acc_scr, dv,
                  n_lat_chunks=n_lat_chunks, n_chunks=n_chunks, tk=tk, ctx_tile=ctx_tile)
    a1, a2 = acc_scr[0], acc_scr[1]
    yt = a1[:dv] * (1.0 / a1[dv:dv + 1]) - lam_ref[0] * (a2[:dv] * (1.0 / a2[dv:dv + 1]))
    y = yt.T
    ms = jnp.mean(y * y, axis=-1, keepdims=True)
    o_ref[...] = (y * lax.rsqrt(ms + EPS) * g_ref[...]).astype(BF16)


def _gqa_attn_kernel(q_ref, kvl_ref, kvc_ref, o_ref, ka_scr, vt_scr, kmax_scr, m_scr, acc_scr,
                     *, nq_lat, n_lat_chunks, n_chunks, tk, ctx_queries):
    @pl.when(pl.program_id(2) == 0)
    def _():
        kmax = None
        for c in range(n_chunks):
            kv = _chunk_rows(c, n_lat_chunks, tk, kvl_ref, kvc_ref)
            ka, n2 = _aug_keys(kv.astype(F32))
            ka_scr[0, c * tk:(c + 1) * tk, :] = ka
            kmax = n2 if kmax is None else jnp.maximum(kmax, n2)
            t = kv.T
            row = lax.broadcasted_iota(jnp.int32, t.shape, 0)
            vt_scr[c] = jnp.where(row < HEAD_DIM, jnp.ones_like(t), t)
        kmax_scr[0] = kmax

    qf = q_ref[...].astype(F32)
    tq = qf.shape[0]
    lane = lax.broadcasted_iota(jnp.int32, (tq, 2 * HEAD_DIM), 1)
    slabs = []
    for r in range(GQA_REP):
        slab = qf[:, (r // 2) * 2 * HEAD_DIM:(r // 2 + 1) * 2 * HEAD_DIM]
        slabs.append(pltpu.roll(slab, HEAD_DIM, 1) if r % 2 else slab)
    ctx_tile = (pl.program_id(2) == nq_lat) if ctx_queries else None
    _softmax_tile(ka_scr, vt_scr, kmax_scr, slabs, m_scr, acc_scr, 0,
                  n_lat_chunks=n_lat_chunks, n_chunks=n_chunks, tk=tk, ctx_tile=ctx_tile)

    def head_out(r):
        acc = acc_scr[r]
        return (acc * (1.0 / acc[0:1])).T

    for j in range(GQA_REP // 2):
        pair = jnp.where(lane < HEAD_DIM, pltpu.roll(head_out(2 * j), HEAD_DIM, 1), head_out(2 * j + 1))
        o_ref[:, j * 2 * HEAD_DIM:(j + 1) * 2 * HEAD_DIM] = pair.astype(BF16)


def _attention(p, lam, subln_gain, *, B, S, C, tiles, ctx_queries):
    tq, tk = tiles.tq, tiles.tk
    nq_lat = S // tq
    nq = nq_lat + (1 if ctx_queries else 0)
    ctx_blk0 = (B * S) // C
    rows_out = B * S + (B * C if ctx_queries else 0)
    T = S + C
    statics = dict(nq_lat=nq_lat, n_lat_chunks=S // tk, n_chunks=T // tk, tk=tk, ctx_queries=ctx_queries)

    def q_row(b, qi):
        if not ctx_queries:
            return b * nq_lat + qi
        return jnp.where(qi < nq_lat, b * nq_lat + qi, ctx_blk0 + b)

    cparams = pltpu.CompilerParams(
        dimension_semantics=("arbitrary", "arbitrary", "arbitrary"), vmem_limit_bytes=VMEM_LIMIT)
    w2 = 2 * HEAD_DIM
    yd = pl.pallas_call(
        functools.partial(_diff_attn_kernel, **statics),
        grid=(B, N_DIFF_HEADS, nq),
        in_specs=[
            pl.BlockSpec(memory_space=pltpu.SMEM),
            pl.BlockSpec((tq, w2), lambda b, h, qi: (q_row(b, qi), DQ_OFF // w2 + h)),
            pl.BlockSpec((S, w2), lambda b, h, qi: (b, DK_OFF // w2 + h)),
            pl.BlockSpec((S, w2), lambda b, h, qi: (b, DV_OFF // w2 + h)),
            pl.BlockSpec((C, w2), lambda b, h, qi: (ctx_blk0 + b, DK_OFF // w2 + h)),
            pl.BlockSpec((C, w2), lambda b, h, qi: (ctx_blk0 + b, DV_OFF // w2 + h)),
            pl.BlockSpec((1, w2), lambda b, h, qi: (0, 0)),
        ],
        out_specs=pl.BlockSpec((tq, w2), lambda b, h, qi: (q_row(b, qi), h)),
        out_shape=jax.ShapeDtypeStruct((rows_out, N_DIFF_HEADS * w2), BF16),
        scratch_shapes=[
            pltpu.VMEM((2, T, w2), BF16),
            pltpu.VMEM((T // tk, DIFF_VT_ROWS, tk), BF16),
            pltpu.VMEM((2, 1, w2), F32),
            pltpu.VMEM((2, 1, tq), F32),
            pltpu.VMEM((2, DIFF_VT_ROWS, tq), F32),
        ],
        compiler_params=cparams,
        name="diff_attn",
    )(lam, p, p, p, p, p, subln_gain)

    wq = GQA_REP * HEAD_DIM
    yg = pl.pallas_call(
        functools.partial(_gqa_attn_kernel, **statics),
        grid=(B, N_GQA_KV, nq),
        in_specs=[
            pl.BlockSpec((tq, wq), lambda b, g, qi: (q_row(b, qi), GQ_OFF // wq + g)),
            pl.BlockSpec((S, w2), lambda b, g, qi: (b, GKV_OFF // w2 + g)),
            pl.BlockSpec((C, w2), lambda b, g, qi: (ctx_blk0 + b, GKV_OFF // w2 + g)),
        ],
        out_specs=pl.BlockSpec((tq, wq), lambda b, g, qi: (q_row(b, qi), g)),
        out_shape=jax.ShapeDtypeStruct((rows_out, N_GQA_HEADS * HEAD_DIM), BF16),
        scratch_shapes=[
            pltpu.VMEM((1, T, w2), BF16),
            pltpu.VMEM((T // tk, w2, tk), BF16),
            pltpu.VMEM((1, 1, w2), F32),
            pltpu.VMEM((GQA_REP, 1, tq), F32),
            pltpu.VMEM((GQA_REP, w2, tq), F32),
        ],
        compiler_params=cparams,
        name="gqa_attn",
    )(p, p, p)
    return yd, yg


def _mixer_kernel(x_ref, yd_ref, yg_ref, gd_ref, gg_ref, gt_ref, wpd_ref, wpg_ref, wo_ref, o_ref):
    pd = jnp.dot(yd_ref[...], wpd_ref[...], preferred_element_type=F32)
    pg = jnp.dot(yg_ref[...], wpg_ref[...], preferred_element_type=F32)
    m = jax.nn.sigmoid(gd_ref[...].astype(F32)) * pd + jax.nn.sigmoid(gg_ref[...].astype(F32)) * pg
    o_ref[...] = x_ref[...] + gt_ref[...] * jnp.dot(m.astype(BF16), wo_ref[...], preferred_element_type=F32)


def _mixer_out(xs, yd, yg, p, mods, wpd, wpg, wo, *, rows, tm, tiles_per_seq, n_batch):
    row_to_mod = lambda i: jnp.minimum(i // tiles_per_seq, n_batch)
    row_tile = pl.BlockSpec((tm, D_MODEL), lambda i: (i, 0))
    weight = pl.BlockSpec((D_MODEL, D_MODEL), lambda i: (0, 0))
    return pl.pallas_call(
        _mixer_kernel,
        grid=(rows // tm,),
        in_specs=[
            row_tile, row_tile, row_tile,
            pl.BlockSpec((tm, D_MODEL), lambda i: (i, GATE_OFF // D_MODEL)),
            pl.BlockSpec((tm, D_MODEL), lambda i: (i, GATE_OFF // D_MODEL + 1)),
            _mod_spec(2, row_to_mod),
            weight, weight, weight,
        ],
        out_specs=row_tile,
        out_shape=jax.ShapeDtypeStruct((rows, D_MODEL), F32),
        compiler_params=pltpu.CompilerParams(
            dimension_semantics=("arbitrary",), vmem_limit_bytes=VMEM_LIMIT),
        name="mixer_out",
    )(xs, yd, yg, p, p, mods, wpd, wpg, wo)


def _ffn_kernel(x_ref, g_ref, sc_ref, sh_ref, gt_ref, w1_ref, w3_ref, w2_ref, o_ref, h_scr, acc_scr):
    f = pl.program_id(1)

    @pl.when(f == 0)
    def _():
        h_scr[...] = _modulated_norm(x_ref[...], g_ref[...], sc_ref[...], sh_ref[...]).astype(BF16)
        acc_scr[...] = jnp.zeros_like(acc_scr)

    h = h_scr[...]
    a = jnp.dot(h, w1_ref[...], preferred_element_type=F32)
    b = jnp.dot(h, w3_ref[...], preferred_element_type=F32)
    acc_scr[...] += jnp.dot((_silu(a) * b).astype(BF16), w2_ref[...], preferred_element_type=F32)

    @pl.when(f == pl.num_programs(1) - 1)
    def _():
        o_ref[...] = x_ref[...] + gt_ref[...] * acc_scr[...]


def _ffn(xs, mods, norm_g, w1, w3, w2, *, tm, tf, tiles_per_seq, n_batch):
    rows = xs.shape[0]
    d_ff = w1.shape[1]
    row_to_mod = lambda i: jnp.minimum(i // tiles_per_seq, n_batch)
    row_tile = pl.BlockSpec((tm, D_MODEL), lambda i, f: (i, 0))
    return pl.pallas_call(
        _ffn_kernel,
        grid=(rows // tm, d_ff // tf),
        in_specs=[
            row_tile,
            pl.BlockSpec((1, D_MODEL), lambda i, f: (0, 0)),
            _mod_spec(4, row_to_mod), _mod_spec(3, row_to_mod), _mod_spec(5, row_to_mod),
            pl.BlockSpec((D_MODEL, tf), lambda i, f: (0, f)),
            pl.BlockSpec((D_MODEL, tf), lambda i, f: (0, f)),
            pl.BlockSpec((tf, D_MODEL), lambda i, f: (f, 0)),
        ],
        out_specs=row_tile,
        out_shape=jax.ShapeDtypeStruct((rows, D_MODEL), F32),
        scratch_shapes=[pltpu.VMEM((tm, D_MODEL), BF16), pltpu.VMEM((tm, D_MODEL), F32)],
        compiler_params=pltpu.CompilerParams(
            dimension_semantics=("arbitrary", "arbitrary"), vmem_limit_bytes=VMEM_LIMIT),
        name="ffn_swiglu",
    )(xs, norm_g, mods, mods, mods, w1, w3, w2)


def _top2_gates(logits):
    lane = lax.broadcasted_iota(jnp.int32, logits.shape, 1)
    n_lanes = logits.shape[1]
    lg = jnp.where(lane < N_EXPERTS, logits, NEG_BIG)
    m1 = jnp.max(lg, axis=-1, keepdims=True)
    i1 = jnp.min(jnp.where(lg == m1, lane, n_lanes), axis=-1, keepdims=True)
    lg2 = jnp.where(lane == i1, NEG_BIG, lg)
    m2 = jnp.max(lg2, axis=-1, keepdims=True)
    i2 = jnp.min(jnp.where(lg2 == m2, lane, n_lanes), axis=-1, keepdims=True)
    e2 = jnp.exp(m2 - m1)
    w_top = 1.0 / (1.0 + e2)
    idx = jnp.where(lane == 0, i1, jnp.where(lane == 1, i2, 0))
    wts = jnp.where(lane == 0, w_top, jnp.where(lane == 1, e2 * w_top, 0.0))
    return idx, wts


def _router_kernel(x_ref, g_ref, sc_ref, sh_ref, rhi_ref, rlo_ref, h_ref, idx_ref, wts_ref):
    h = _modulated_norm(x_ref[...], g_ref[...], sc_ref[...], sh_ref[...])
    h_ref[...] = h
    h_hi, h_lo = _split_bf16(h)
    logits = jnp.dot(h_hi, rhi_ref[...], preferred_element_type=F32)
    logits += jnp.dot(h_lo, rhi_ref[...], preferred_element_type=F32)
    logits += jnp.dot(h_hi, rlo_ref[...], preferred_element_type=F32)
    idx_ref[...], wts_ref[...] = _top2_gates(logits)


def _router(xs, mods, norm_g, r_hi, r_lo, *, rows, tm, tiles_per_seq, n_batch):
    row_to_mod = lambda i: jnp.minimum(i // tiles_per_seq, n_batch)
    row_tile = pl.BlockSpec((tm, D_MODEL), lambda i: (i, 0))
    lanes = pl.BlockSpec((tm, 128), lambda i: (i, 0))
    router = pl.BlockSpec((D_MODEL, 128), lambda i: (0, 0))
    return pl.pallas_call(
        _router_kernel,
        grid=(rows // tm,),
        in_specs=[row_tile, pl.BlockSpec((1, D_MODEL), lambda i: (0, 0)),
                  _mod_spec(4, row_to_mod), _mod_spec(3, row_to_mod), router, router],
        out_specs=[row_tile, lanes, lanes],
        out_shape=[jax.ShapeDtypeStruct((rows, D_MODEL), F32),
                   jax.ShapeDtypeStruct((rows, 128), jnp.int32),
                   jax.ShapeDtypeStruct((rows, 128), F32)],
        compiler_params=pltpu.CompilerParams(
            dimension_semantics=("arbitrary",), vmem_limit_bytes=VMEM_LIMIT),
        name="moe_router",
    )(xs, norm_g, mods, mods, r_hi, r_lo)


def _route_plan(idx, n_tok, tm_e):
    n_pairs = 2 * n_tok
    n_tiles = n_pairs // tm_e + N_EXPERTS
    n_rows = n_tiles * tm_e
    e_flat = idx.reshape(n_pairs)
    onehot = (e_flat[:, None] == jnp.arange(N_EXPERTS, dtype=jnp.int32)[None, :]).astype(jnp.int32)
    csum = jnp.cumsum(onehot, axis=0)
    rank = jnp.sum(csum * onehot, axis=1) - 1
    counts = csum[-1]
    padded = ((counts + tm_e - 1) // tm_e) * tm_e
    ends = jnp.cumsum(padded)
    dest = (ends - padded)[e_flat] + rank
    pair = jnp.arange(n_pairs, dtype=jnp.int32)
    tok, choice = pair // 2, pair % 2
    src = jnp.zeros((n_rows,), jnp.int32).at[dest].set(tok)
    valid = jnp.zeros((n_rows,), jnp.bool_).at[dest].set(True)
    dump = n_pairs + jnp.cumsum(jnp.logical_not(valid).astype(jnp.int32)) - 1
    dst = jnp.where(valid, jnp.zeros((n_rows,), jnp.int32).at[dest].set(choice * n_tok + tok), dump)
    tile_start = jnp.arange(n_tiles, dtype=jnp.int32) * tm_e
    tile_expert = jnp.minimum(jnp.sum((tile_start[:, None] >= ends[None, :]).astype(jnp.int32), axis=1),
                              N_EXPERTS - 1).astype(jnp.int32)
    n_used = (ends[-1] // tm_e).astype(jnp.int32).reshape(1)
    return tile_expert, n_used, src.reshape(n_tiles, 1, tm_e), dst.reshape(n_tiles, 1, tm_e)


def _experts_kernel(te_ref, nu_ref, src_ref, src_next_ref, dst_ref, h_hbm, w1_ref, w3_ref, w2_ref, y_hbm,
                    xbuf, xb_scr, acc_scr, ybuf, gsem, ssem, *, tm_e):
    t = pl.program_id(0)
    f = pl.program_id(1)
    nf = pl.num_programs(1)
    n_used = nu_ref[0]
    slot = t % 2

    def gather_rows(idx_ref, s):
        def body(r, carry):
            pltpu.make_async_copy(h_hbm.at[pl.ds(idx_ref[0, r], 1)], xbuf.at[s, pl.ds(r, 1)], gsem.at[s]).start()
            return carry
        lax.fori_loop(0, tm_e, body, 0, unroll=8)

    def wait_gather(s):
        pltpu.make_async_copy(h_hbm.at[pl.ds(0, tm_e)], xbuf.at[s], gsem.at[s]).wait()

    def scatter_rows(s):
        def body(r, carry):
            pltpu.make_async_copy(ybuf.at[s, pl.ds(r, 1)], y_hbm.at[pl.ds(dst_ref[0, r], 1)], ssem.at[s]).start()
            return carry
        lax.fori_loop(0, tm_e, body, 0, unroll=8)

    def wait_scatter(s):
        pltpu.make_async_copy(ybuf.at[s], y_hbm.at[pl.ds(0, tm_e)], ssem.at[s]).wait()

    @pl.when(t < n_used)
    def _():
        @pl.when(f == 0)
        def _():
            @pl.when(t == 0)
            def _():
                gather_rows(src_ref, 0)
            for s in range(2):
                @pl.when(slot == s)
                def _():
                    wait_gather(s)
                    xb_scr[...] = xbuf[s].astype(BF16)

                    @pl.when(t + 1 < n_used)
                    def _():
                        gather_rows(src_next_ref, 1 - s)
            acc_scr[...] = jnp.zeros_like(acc_scr)

        x = xb_scr[...]
        a = jnp.dot(x, w1_ref[...], preferred_element_type=F32)
        b = jnp.dot(x, w3_ref[...], preferred_element_type=F32)
        acc_scr[...] += jnp.dot((_silu(a) * b).astype(BF16), w2_ref[...], preferred_element_type=F32)

    @pl.when(f == nf - 1)
    def _():
        for s in range(2):
            @pl.when(slot == s)
            def _():
                @pl.when(t >= 2)
                def _():
                    wait_scatter(s)

                @pl.when(t < n_used)
                def _():
                    ybuf[s] = acc_scr[...]

                @pl.when(t >= n_used)
                def _():
                    ybuf[s] = jnp.zeros(ybuf.shape[1:], F32)

                scatter_rows(s)

                @pl.when(t == pl.num_programs(0) - 1)
                def _():
                    wait_scatter(s)
                    wait_scatter(1 - s)


def _experts(h, plan, w1, w3, w2, *, n_tok, tm_e, tf):
    tile_expert, n_used, src, dst = plan
    n_tiles = src.shape[0]
    d_ff = w1.shape[2]
    nf = d_ff // tf
    n_out = n_tiles * tm_e

    def chunk(t, f, nu):
        return jnp.where(t < nu[0], f, nf - 1)

    smem_tile = lambda shift: pl.BlockSpec(
        (None, 1, tm_e), lambda t, f, te, nu: (jnp.minimum(t + shift, n_tiles - 1), 0, 0),
        memory_space=pltpu.SMEM)
    grid_spec = pltpu.PrefetchScalarGridSpec(
        num_scalar_prefetch=2,
        grid=(n_tiles, nf),
        in_specs=[
            smem_tile(0), smem_tile(1), smem_tile(0),
            pl.BlockSpec(memory_space=pl.ANY),
            pl.BlockSpec((None, D_MODEL, tf), lambda t, f, te, nu: (te[t], 0, chunk(t, f, nu))),
            pl.BlockSpec((None, D_MODEL, tf), lambda t, f, te, nu: (te[t], 0, chunk(t, f, nu))),
            pl.BlockSpec((None, tf, D_MODEL), lambda t, f, te, nu: (te[t], chunk(t, f, nu), 0)),
        ],
        out_specs=pl.BlockSpec(memory_space=pl.ANY),
        scratch_shapes=[
            pltpu.VMEM((2, tm_e, D_MODEL), F32),
            pltpu.VMEM((tm_e, D_MODEL), BF16),
            pltpu.VMEM((tm_e, D_MODEL), F32),
            pltpu.VMEM((2, tm_e, D_MODEL), F32),
            pltpu.SemaphoreType.DMA((2,)),
            pltpu.SemaphoreType.DMA((2,)),
        ],
    )
    return pl.pallas_call(
        functools.partial(_experts_kernel, tm_e=tm_e),
        grid_spec=grid_spec,
        out_shape=jax.ShapeDtypeStruct((n_out, D_MODEL), F32),
        compiler_params=pltpu.CompilerParams(
            dimension_semantics=("arbitrary", "arbitrary"), vmem_limit_bytes=VMEM_LIMIT),
        name="moe_experts",
    )(tile_expert, n_used, src, src, dst, h, w1, w3, w2)


def _combine_kernel(x_ref, y0_ref, y1_ref, wts_ref, gt_ref, fg_ref, o_ref):
    wts = wts_ref[...]
    moe = wts[:, 0:1] * y0_ref[...] + wts[:, 1:2] * y1_ref[...]
    y = x_ref[...] + gt_ref[...] * moe
    ms = jnp.mean(y * y, axis=-1, keepdims=True)
    o_ref[...] = y * lax.rsqrt(ms + EPS) * fg_ref[...]


def _combine(xs, y, wts, mods, final_g, *, rows, tm, tiles_per_seq, n_batch):
    row_to_mod = lambda i: jnp.minimum(i // tiles_per_seq, n_batch)
    row_tile = pl.BlockSpec((tm, D_MODEL), lambda i: (i, 0))
    return pl.pallas_call(
        _combine_kernel,
        grid=(rows // tm,),
        in_specs=[
            row_tile, row_tile,
            pl.BlockSpec((tm, D_MODEL), lambda i: (i + rows // tm, 0)),
            pl.BlockSpec((tm, 128), lambda i: (i, 0)),
            _mod_spec(5, row_to_mod),
            pl.BlockSpec((1, D_MODEL), lambda i: (0, 0)),
        ],
        out_specs=row_tile,
        out_shape=jax.ShapeDtypeStruct((rows, D_MODEL), F32),
        compiler_params=pltpu.CompilerParams(
            dimension_semantics=("arbitrary",), vmem_limit_bytes=VMEM_LIMIT),
        name="moe_combine",
    )(xs, y, y, wts, mods, final_g)


def _moe(xs, mods, norm_g, r_hi, r_lo, final_g, w1, w3, w2, *, rows, tm, tm_e, tf, tiles_per_seq, n_batch):
    h, idx, wts = _router(xs, mods, norm_g, r_hi, r_lo, rows=rows, tm=tm, tiles_per_seq=tiles_per_seq,
                          n_batch=n_batch)
    plan = _route_plan(idx[:, :2], rows, tm_e)
    y = _experts(h, plan, w1, w3, w2, n_tok=rows, tm_e=tm_e, tf=tf)
    return _combine(xs, y, wts, mods, final_g, rows=rows, tm=tm, tiles_per_seq=tiles_per_seq, n_batch=n_batch)


def _deinterleave(n=HEAD_DIM):
    return np.concatenate([np.arange(0, n, 2), np.arange(1, n, 2)])


def _proj_columns():
    de = _deinterleave()
    o_dq, o_gq, o_dk, o_dv, o_gk, o_gv, o_gate = 0, 1024, 2048, 3072, 4096, 4352, 4608
    cols = np.zeros(IN_W, np.int32)
    scale = np.ones(IN_W, np.float32)
    cols[GATE_OFF:GATE_OFF + 2048] = o_gate + np.arange(2048)
    for h in range(N_DIFF_HEADS):
        for c in range(2):
            dst = 128 * h + 64 * c
            cols[DQ_OFF + dst:DQ_OFF + dst + 64] = o_dq + dst + de
            cols[DK_OFF + dst:DK_OFF + dst + 64] = o_dk + dst + de
    scale[DQ_OFF:DQ_OFF + 1024] = HEAD_DIM ** -0.5
    for j in range(N_GQA_HEADS):
        cols[GQ_OFF + 64 * j:GQ_OFF + 64 * j + 64] = o_gq + 64 * j + de
    cols[DV_OFF:DV_OFF + 1024] = o_dv + np.arange(1024)
    for g in range(N_GQA_KV):
        cols[GKV_OFF + 128 * g:GKV_OFF + 128 * g + 64] = o_gk + 64 * g + de
        cols[GKV_OFF + 128 * g + 64:GKV_OFF + 128 * g + 128] = o_gv + 64 * g + np.arange(64)
    return cols, scale


def _rope_tables(S, pad_rows):
    rows = S // GRID_W
    row = jnp.repeat(jnp.arange(rows, dtype=F32), GRID_W)
    col = jnp.tile(jnp.arange(GRID_W, dtype=F32), rows)
    half = HEAD_DIM // 2
    inv_freq = ROPE_THETA ** (-jnp.arange(0, half, 2, dtype=F32) / half)
    ang = jnp.concatenate([row[:, None] * inv_freq, col[:, None] * inv_freq], axis=-1)
    cos, sin = jnp.cos(ang), jnp.sin(ang)
    reps = PROJ_TN // HEAD_DIM
    cos_t = jnp.tile(jnp.concatenate([cos, cos], axis=-1), (1, reps))
    sin_t = jnp.tile(jnp.concatenate([-sin, sin], axis=-1), (1, reps))
    cos_t = jnp.concatenate([cos_t, jnp.ones((pad_rows, PROJ_TN), F32)], axis=0)
    sin_t = jnp.concatenate([sin_t, jnp.zeros((pad_rows, PROJ_TN), F32)], axis=0)
    return cos_t, sin_t


def kernel(x, c, ctx, c_ctx, ada_w, ada_b, norm_attn_g, norm_ffn_g, w_in, q_norm_g, k_norm_g, diff_lambda,
           diff_subln_g, w_proj_diff, w_proj_gqa, w_out, ffn_w1, ffn_w3, ffn_w2, moe_router, moe_w1, moe_w3,
           moe_w2, final_norm_g):
    B, S, D = x.shape
    C = ctx.shape[1]
    depth = ada_w.shape[0]
    assert D == D_MODEL and depth == 2 and B + 1 <= MOD_ROWS
    assert w_in.shape[2] == IN_W and moe_router.shape[2] == N_EXPERTS
    tiles = _pick_tiles(B, S, C, ffn_w1.shape[2], moe_w1.shape[3])
    n_lat = B * S

    cvec = jnp.zeros((MOD_ROWS, D), F32).at[:B].set(c).at[B].set(c_ctx)
    mods_all = _ada_mods(cvec, ada_w, ada_b).reshape(depth, MOD_ROWS * N_MODS, 1, D)

    cols, col_scale = _proj_columns()
    de = _deinterleave()
    cos_t, sin_t = _rope_tables(S, tiles.tm_proj)
    blk = np.arange(PROJ_TN) // HEAD_DIM
    ones_bd = jnp.asarray(blk[:, None] == blk[None, :], BF16)
    is_key_lane = (np.arange(PROJ_TN) % (2 * HEAD_DIM)) < HEAD_DIM

    xs = jnp.concatenate([x.reshape(n_lat, D), ctx.reshape(B * C, D)], axis=0)
    for l in range(depth):
        last = l == depth - 1
        mods = mods_all[l]
        lam_init = 0.8 - 0.6 * math.exp(-0.3 * l)
        lq1, lk1, lq2, lk2 = diff_lambda[l]
        lam = (jnp.exp(jnp.sum(lq1 * lk1)) - jnp.exp(jnp.sum(lq2 * lk2)) + lam_init).reshape(1).astype(F32)
        w = (jnp.take(w_in[l], cols, axis=1) * col_scale).astype(BF16)
        qg = jnp.tile(q_norm_g[l][de] * HEAD_DIM ** -0.5, PROJ_TN // HEAD_DIM).reshape(1, PROJ_TN)
        kg = jnp.where(is_key_lane, jnp.tile(k_norm_g[l][de], PROJ_TN // HEAD_DIM), 1.0).reshape(1, PROJ_TN)
        subln = (diff_subln_g[l] * (1.0 - lam_init)).reshape(1, 2 * HEAD_DIM)

        p = _project(xs, mods, norm_attn_g[l].reshape(1, D), w, cos_t, sin_t, ones_bd, qg, kg,
                     tm=tiles.tm_proj, lat_tiles=n_lat // tiles.tm_proj, tiles_per_seq=S // tiles.tm_proj,
                     n_batch=B)
        yd, yg = _attention(p, lam, subln, B=B, S=S, C=C, tiles=tiles, ctx_queries=not last)
        rows = n_lat if last else n_lat + B * C
        xs = _mixer_out(xs, yd, yg, p, mods, w_proj_diff[l].astype(BF16), w_proj_gqa[l].astype(BF16),
                        w_out[l].astype(BF16), rows=rows, tm=tiles.tm_mix, tiles_per_seq=S // tiles.tm_mix,
                        n_batch=B)
        i = l // 2
        if l % 2 == 0:
            xs = _ffn(xs, mods, norm_ffn_g[l].reshape(1, D), ffn_w1[i].astype(BF16), ffn_w3[i].astype(BF16),
                      ffn_w2[i].astype(BF16), tm=tiles.tm_ffn, tf=tiles.tf_ffn,
                      tiles_per_seq=S // tiles.tm_ffn, n_batch=B)
        else:
            r_pad = jnp.zeros((D, 128), F32).at[:, :N_EXPERTS].set(moe_router[i])
            r_hi = r_pad.astype(BF16)
            r_lo = (r_pad - r_hi.astype(F32)).astype(BF16)
            xs = _moe(xs, mods, norm_ffn_g[l].reshape(1, D), r_hi, r_lo, final_norm_g.reshape(1, D),
                      moe_w1[i].astype(BF16), moe_w3[i].astype(BF16), moe_w2[i].astype(BF16),
                      rows=rows, tm=tiles.tm_mix, tm_e=tiles.tm_moe, tf=tiles.tf_moe,
                      tiles_per_seq=S // tiles.tm_mix, n_batch=B)
    return xs.reshape(B, S, D)
```

```python
import functools
import math
from typing import NamedTuple

import numpy as np
import jax
import jax.numpy as jnp
from jax import lax
from jax.experimental import pallas as pl
from jax.experimental.pallas import tpu as pltpu

F32 = jnp.float32
BF16 = jnp.bfloat16

D_MODEL = 1024
HEAD_DIM = 64
N_DIFF_HEADS = 8
N_GQA_HEADS = 16
N_GQA_KV = 4
GQA_REP = N_GQA_HEADS // N_GQA_KV
N_EXPERTS = 8
GRID_W = 64
ROPE_THETA = 10000.0
EPS = 1e-6
N_MODS = 6
MOD_ROWS = 16
NEG_BIG = -1e30

GATE_OFF = 0
DQ_OFF = 2048
GQ_OFF = 3072
DK_OFF = 4096
DV_OFF = 5120
GKV_OFF = 6144
IN_W = 6656
PROJ_TN = 512
ROPE_TILES = (4, 5, 8, 9)
QNORM_TILES = (6, 7)
KV_TILE = 12

VMEM_LIMIT = 52 * 1024 * 1024


class Tiles(NamedTuple):
    tm_proj: int
    tq: int
    tk: int
    tm_mix: int
    tm_ffn: int
    tf_ffn: int
    tf_moe: int
    tm_moe: int


def _largest_divisor(n, candidates):
    for c in candidates:
        if n % c == 0:
            return c
    raise ValueError(f"no tile in {candidates} divides {n}")


def _pick_tiles(B, S, C, d_ff, d_ff_e):
    rows_common = math.gcd(S, B * C)
    tm = _largest_divisor(rows_common, (512, 256, 128))
    return Tiles(
        tm_proj=tm,
        tq=C,
        tk=_largest_divisor(math.gcd(S, C), (256, 128)),
        tm_mix=tm,
        tm_ffn=_largest_divisor(rows_common, (1024, 512, 256, 128)),
        tf_ffn=_largest_divisor(d_ff, (1408, 1024, 512, 256, 128)),
        tf_moe=_largest_divisor(d_ff_e, (896, 512, 256, 128)),
        tm_moe=_largest_divisor(2 * B * S, (512, 256)),
    )


def _split_bf16(v):
    hi = v.astype(BF16)
    lo = (v - hi.astype(F32)).astype(BF16)
    return hi, lo


def _ada_kernel(c_ref, w_ref, b_ref, o_ref):
    c = c_ref[...]
    s = c / (1.0 + jnp.exp(-c))
    s_hi, s_lo = _split_bf16(s)
    w_hi, w_lo = _split_bf16(w_ref[...])
    acc = jnp.dot(s_hi, w_hi, preferred_element_type=F32)
    acc += jnp.dot(s_lo, w_hi, preferred_element_type=F32)
    acc += jnp.dot(s_hi, w_lo, preferred_element_type=F32)
    o_ref[...] = acc + b_ref[...]


def _ada_mods(cvec, ada_w, ada_b):
    depth, d, n = ada_w.shape
    tn = _largest_divisor(n, (1536, 1024, 512))
    return pl.pallas_call(
        _ada_kernel,
        grid=(depth, n // tn),
        in_specs=[
            pl.BlockSpec((MOD_ROWS, d), lambda l, j: (0, 0)),
            pl.BlockSpec((None, d, tn), lambda l, j: (l, 0, j)),
            pl.BlockSpec((None, 1, tn), lambda l, j: (l, 0, j)),
        ],
        out_specs=pl.BlockSpec((None, MOD_ROWS, tn), lambda l, j: (l, 0, j)),
        out_shape=jax.ShapeDtypeStruct((depth, MOD_ROWS, n), F32),
        compiler_params=pltpu.CompilerParams(
            dimension_semantics=("arbitrary", "arbitrary"), vmem_limit_bytes=VMEM_LIMIT),
        name="ada_mods",
    )(cvec, ada_w, ada_b.reshape(depth, 1, n))


def _modulated_norm(x, g, scale, shift):
    ms = jnp.mean(x * x, axis=-1, keepdims=True)
    return (x * lax.rsqrt(ms + EPS) * g) * (1.0 + scale) + shift


def _silu(a):
    return a / (1.0 + jnp.exp(-a))


def _mod_spec(k, row_to_mod):
    return pl.BlockSpec((None, 1, D_MODEL), lambda i, *_: (row_to_mod(i) * N_MODS + k, 0, 0))


def _rope(z, cos, sin_signed, lane):
    first_half = (lane & (HEAD_DIM // 2)) == 0
    n = z.shape[1]
    partner = jnp.where(first_half, pltpu.roll(z, n - HEAD_DIM // 2, 1), pltpu.roll(z, HEAD_DIM // 2, 1))
    return z * cos + partner * sin_signed


def _head_rms(z, ones_ref, gain):
    zz_hi, zz_lo = _split_bf16(z * z)
    ss = jnp.dot(zz_hi, ones_ref[...], preferred_element_type=F32)
    ss += jnp.dot(zz_lo, ones_ref[...], preferred_element_type=F32)
    return z * lax.rsqrt(ss * (1.0 / HEAD_DIM) + EPS) * gain


def _proj_kernel(x_ref, g_ref, sc_ref, sh_ref, w_ref, cos_ref, sin_ref, ones_ref, qg_ref, kg_ref,
                 o_ref, h_scr):
    j = pl.program_id(1)

    @pl.when(j == 0)
    def _():
        h_scr[...] = _modulated_norm(x_ref[...], g_ref[...], sc_ref[...], sh_ref[...]).astype(BF16)

    z = jnp.dot(h_scr[...], w_ref[...], preferred_element_type=F32)
    lane = lax.broadcasted_iota(jnp.int32, z.shape, 1)
    is_rope = functools.reduce(jnp.logical_or, [j == t for t in ROPE_TILES])
    is_qnorm = functools.reduce(jnp.logical_or, [j == t for t in QNORM_TILES])
    is_kv = j == KV_TILE
    is_plain = jnp.logical_not(is_rope | is_qnorm | is_kv)

    @pl.when(is_plain)
    def _():
        o_ref[...] = z.astype(BF16)

    @pl.when(is_rope)
    def _():
        o_ref[...] = _rope(z, cos_ref[...], sin_ref[...], lane).astype(BF16)

    @pl.when(is_qnorm)
    def _():
        zn = _head_rms(z, ones_ref, qg_ref[...])
        o_ref[...] = _rope(zn, cos_ref[...], sin_ref[...], lane).astype(BF16)

    @pl.when(is_kv)
    def _():
        zn = _head_rms(z, ones_ref, kg_ref[...])
        zr = _rope(zn, cos_ref[...], sin_ref[...], lane)
        is_key_lane = (lane & HEAD_DIM) == 0
        o_ref[...] = jnp.where(is_key_lane, zr, z).astype(BF16)


def _project(xs, mods, norm_g, w, cos_t, sin_t, ones_bd, qg, kg, *, tm, lat_tiles, tiles_per_seq, n_batch):
    rows = xs.shape[0]
    row_to_mod = lambda i: jnp.minimum(i // tiles_per_seq, n_batch)
    rope_row = lambda i: jnp.where(i < lat_tiles, i % tiles_per_seq, tiles_per_seq)
    const = lambda i, j: (0, 0)
    return pl.pallas_call(
        _proj_kernel,
        grid=(rows // tm, IN_W // PROJ_TN),
        in_specs=[
            pl.BlockSpec((tm, D_MODEL), lambda i, j: (i, 0)),
            pl.BlockSpec((1, D_MODEL), const),
            _mod_spec(1, row_to_mod),
            _mod_spec(0, row_to_mod),
            pl.BlockSpec((D_MODEL, PROJ_TN), lambda i, j: (0, j)),
            pl.BlockSpec((tm, PROJ_TN), lambda i, j: (rope_row(i), 0)),
            pl.BlockSpec((tm, PROJ_TN), lambda i, j: (rope_row(i), 0)),
            pl.BlockSpec((PROJ_TN, PROJ_TN), const),
            pl.BlockSpec((1, PROJ_TN), const),
            pl.BlockSpec((1, PROJ_TN), const),
        ],
        out_specs=pl.BlockSpec((tm, PROJ_TN), lambda i, j: (i, j)),
        out_shape=jax.ShapeDtypeStruct((rows, IN_W), BF16),
        scratch_shapes=[pltpu.VMEM((tm, D_MODEL), BF16)],
        compiler_params=pltpu.CompilerParams(
            dimension_semantics=("arbitrary", "arbitrary"), vmem_limit_bytes=VMEM_LIMIT),
        name="in_proj",
    )(xs, norm_g, mods, mods, w, cos_t, sin_t, ones_bd, qg, kg)


STAB_LANE = HEAD_DIM
L_FLOOR = 1e-26
DIFF_VT_ROWS = 2 * HEAD_DIM + 16
DIFF_HEADS_PER_STEP = 2


def _aug_keys(k):
    lane = lax.broadcasted_iota(jnp.int32, k.shape, 1)
    in_key = lane < HEAD_DIM
    sq = jnp.where(in_key, k * k, 0.0).astype(BF16)
    norms = jnp.dot(sq, jnp.ones((k.shape[1], k.shape[1]), BF16), preferred_element_type=F32)
    ka = jnp.where(in_key, k, jnp.where(lane == STAB_LANE, 1.0, 0.0)).astype(BF16)
    return ka, jnp.max(norms, axis=0, keepdims=True)


def _query_mats(slab, kmax2):
    lane = lax.broadcasted_iota(jnp.int32, slab.shape, 1)
    in_key = lane < HEAD_DIM
    sq = jnp.where(in_key, slab * slab, 0.0).astype(BF16)
    norms = jnp.dot(sq, jnp.ones((slab.shape[1], slab.shape[1]), BF16), preferred_element_type=F32)
    nk = norms * kmax2
    bound = nk * lax.rsqrt(nk + 1e-30)
    plain = jnp.where(in_key, slab, 0.0)
    return jnp.where(lane == STAB_LANE, -bound, plain).astype(BF16), plain.astype(BF16)


def _attend(ka_scr, vt_scr, mats, acc_scr, *, first, n_chunks, tk):
    def scores(r, c, qmat):
        off = pl.multiple_of(c * tk, tk)
        return lax.dot_general(ka_scr[r % ka_scr.shape[0], pl.ds(off, tk), :], qmat, (((1,), (1,)), ((), ())),
                               preferred_element_type=F32)

    acc_scr[...] = jnp.zeros(acc_scr.shape, F32)

    def all_scores(c):
        return tuple(scores(r, c, shifted) for r, (shifted, _) in enumerate(mats))

    def accumulate(c, s_all):
        for r, s in enumerate(s_all):
            vt = vt_scr[r * vt_scr.shape[0] // len(mats), c]
            acc_scr[r] += jnp.dot(vt, jnp.exp(s).astype(BF16), preferred_element_type=F32)

    def fast_body(c, s_cur):
        s_next = all_scores(c + 1)
        accumulate(c, s_cur)
        return s_next

    trips = n_chunks - 1 - first
    s_last = lax.fori_loop(first, n_chunks - 1, fast_body, all_scores(first),
                           unroll=_largest_divisor(trips, (16, 8, 4, 2, 1)) if trips > 0 else 1)
    accumulate(n_chunks - 1, s_last)


def _attend_fallback(ka_scr, vt_scr, mats, m_scr, acc_scr, *, first, n_chunks, tk):
    m_scr[...] = jnp.full(m_scr.shape, NEG_BIG, F32)
    acc_scr[...] = jnp.zeros(acc_scr.shape, F32)

    def body(c, carry):
        off = pl.multiple_of(c * tk, tk)
        for r, (_, plain) in enumerate(mats):
            s = lax.dot_general(ka_scr[r % ka_scr.shape[0], pl.ds(off, tk), :], plain, (((1,), (1,)), ((), ())),
                                preferred_element_type=F32)
            m_old = m_scr[r]
            m_new = jnp.maximum(m_old, jnp.max(s, axis=0, keepdims=True))
            p = jnp.exp(s - m_new).astype(BF16)
            vt = vt_scr[r * vt_scr.shape[0] // len(mats), c]
            acc_scr[r] = acc_scr[r] * jnp.exp(m_old - m_new) + jnp.dot(vt, p, preferred_element_type=F32)
            m_scr[r] = m_new
        return carry

    lax.fori_loop(first, n_chunks, body, 0)


def _softmax_tile(ka_scr, vt_scr, kmax_scr, slabs, m_scr, acc_scr, l_row, *, n_lat_chunks, n_chunks, tk, ctx_tile):
    mats = [_query_mats(slab, kmax_scr[r % kmax_scr.shape[0]]) for r, slab in enumerate(slabs)]

    def run(first):
        _attend(ka_scr, vt_scr, mats, acc_scr, first=first, n_chunks=n_chunks, tk=tk)
        l_min = functools.reduce(
            jnp.minimum, [jnp.min(acc_scr[r, l_row:l_row + 1, :]) for r in range(len(slabs))])

        @pl.when(jnp.logical_not(l_min >= L_FLOOR))
        def _():
            _attend_fallback(ka_scr, vt_scr, mats, m_scr, acc_scr, first=first, n_chunks=n_chunks, tk=tk)

    if ctx_tile is None:
        run(0)
    else:
        pl.when(jnp.logical_not(ctx_tile))(functools.partial(run, 0))
        pl.when(ctx_tile)(functools.partial(run, n_lat_chunks))


def _chunk_rows(c, n_lat_chunks, tk, lat_ref, ctx_ref):
    if c < n_lat_chunks:
        return lat_ref[c * tk:(c + 1) * tk, :]
    return ctx_ref[(c - n_lat_chunks) * tk:(c - n_lat_chunks + 1) * tk, :]


def _diff_attn_kernel(lam_ref, q_ref, kl_ref, vl_ref, kc_ref, vc_ref, g_ref, o_ref,
                      ka_scr, vt_scr, kmax_scr, m_scr, acc_scr, *, nq_lat, n_lat_chunks, n_chunks, tk, ctx_queries):
    dv = 2 * HEAD_DIM
    heads = [slice(a * dv, (a + 1) * dv) for a in range(DIFF_HEADS_PER_STEP)]

    @pl.when(pl.program_id(2) == 0)
    def _():
        kmax = [None] * (2 * len(heads))
        for c in range(n_chunks):
            k_all = _chunk_rows(c, n_lat_chunks, tk, kl_ref, kc_ref).astype(F32)
            v_all = _chunk_rows(c, n_lat_chunks, tk, vl_ref, vc_ref)
            for a, cols in enumerate(heads):
                k = k_all[:, cols]
                for r, keys in ((2 * a, k), (2 * a + 1, pltpu.roll(k, HEAD_DIM, 1))):
                    ka, n2 = _aug_keys(keys)
                    ka_scr[r, c * tk:(c + 1) * tk, :] = ka
                    kmax[r] = n2 if kmax[r] is None else jnp.maximum(kmax[r], n2)
                vt_scr[a, c, :dv, :] = v_all[:, cols].T
                vt_scr[a, c, dv:, :] = jnp.ones((DIFF_VT_ROWS - dv, tk), BF16)
        for r, n2 in enumerate(kmax):
            kmax_scr[r] = n2

    q_all = q_ref[...].astype(F32)
    slabs = []
    for cols in heads:
        slabs += [q_all[:, cols], pltpu.roll(q_all[:, cols], HEAD_DIM, 1)]
    ctx_tile = (pl.program_id(2) == nq_lat) if ctx_queries else None
    _softmax_tile(ka_scr, vt_scr, kmax_scr, slabs, m_scr, acc_scr, dv,
                  n_lat_chunks=n_lat_chunks, n_chunks=n_chunks, tk=tk, ctx_tile=ctx_tile)
    for a, cols in enumerate(heads):
        a1, a2 = acc_scr[2 * a], acc_scr[2 * a + 1]
        yt = a1[:dv] * (1.0 / a1[dv:dv + 1]) - lam_ref[0] * (a2[:dv] * (1.0 / a2[dv:dv + 1]))
        y = yt.T
        ms = jnp.mean(y * y, axis=-1, keepdims=True)
        o_ref[:, cols] = (y * lax.rsqrt(ms + EPS) * g_ref[...]).astype(BF16)


def _gqa_attn_kernel(q_ref, kvl_ref, kvc_ref, o_ref, ka_scr, vt_scr, kmax_scr, m_scr, acc_scr,
                     *, nq_lat, n_lat_chunks, n_chunks, tk, ctx_queries):
    @pl.when(pl.program_id(2) == 0)
    def _():
        kmax = None
        for c in range(n_chunks):
            kv = _chunk_rows(c, n_lat_chunks, tk, kvl_ref, kvc_ref)
            ka, n2 = _aug_keys(kv.astype(F32))
            ka_scr[0, c * tk:(c + 1) * tk, :] = ka
            kmax = n2 if kmax is None else jnp.maximum(kmax, n2)
            t = kv.T
            row = lax.broadcasted_iota(jnp.int32, t.shape, 0)
            vt_scr[0, c] = jnp.where(row < HEAD_DIM, jnp.ones_like(t), t)
        kmax_scr[0] = kmax

    qf = q_ref[...].astype(F32)
    tq = qf.shape[0]
    lane = lax.broadcasted_iota(jnp.int32, (tq, 2 * HEAD_DIM), 1)
    slabs = []
    for r in range(GQA_REP):
        slab = qf[:, (r // 2) * 2 * HEAD_DIM:(r // 2 + 1) * 2 * HEAD_DIM]
        slabs.append(pltpu.roll(slab, HEAD_DIM, 1) if r % 2 else slab)
    ctx_tile = (pl.program_id(2) == nq_lat) if ctx_queries else None
    _softmax_tile(ka_scr, vt_scr, kmax_scr, slabs, m_scr, acc_scr, 0,
                  n_lat_chunks=n_lat_chunks, n_chunks=n_chunks, tk=tk, ctx_tile=ctx_tile)

    def head_out(r):
        acc = acc_scr[r]
        return (acc * (1.0 / acc[0:1])).T

    for j in range(GQA_REP // 2):
        pair = jnp.where(lane < HEAD_DIM, pltpu.roll(head_out(2 * j), HEAD_DIM, 1), head_out(2 * j + 1))
        o_ref[:, j * 2 * HEAD_DIM:(j + 1) * 2 * HEAD_DIM] = pair.astype(BF16)


def _attention(p, lam, subln_gain, *, B, S, C, tiles, ctx_queries):
    tq, tk = tiles.tq, tiles.tk
    nq_lat = S // tq
    nq = nq_lat + (1 if ctx_queries else 0)
    ctx_blk0 = (B * S) // C
    rows_out = B * S + (B * C if ctx_queries else 0)
    T = S + C
    statics = dict(nq_lat=nq_lat, n_lat_chunks=S // tk, n_chunks=T // tk, tk=tk, ctx_queries=ctx_queries)

    def q_row(b, qi):
        if not ctx_queries:
            return b * nq_lat + qi
        return jnp.where(qi < nq_lat, b * nq_lat + qi, ctx_blk0 + b)

    cparams = pltpu.CompilerParams(
        dimension_semantics=("arbitrary", "arbitrary", "arbitrary"), vmem_limit_bytes=VMEM_LIMIT)
    w2 = 2 * HEAD_DIM
    hps = DIFF_HEADS_PER_STEP
    wd = hps * w2
    yd = pl.pallas_call(
        functools.partial(_diff_attn_kernel, **statics),
        grid=(B, N_DIFF_HEADS // hps, nq),
        in_specs=[
            pl.BlockSpec(memory_space=pltpu.SMEM),
            pl.BlockSpec((tq, wd), lambda b, h, qi: (q_row(b, qi), DQ_OFF // wd + h)),
            pl.BlockSpec((S, wd), lambda b, h, qi: (b, DK_OFF // wd + h)),
            pl.BlockSpec((S, wd), lambda b, h, qi: (b, DV_OFF // wd + h)),
            pl.BlockSpec((C, wd), lambda b, h, qi: (ctx_blk0 + b, DK_OFF // wd + h)),
            pl.BlockSpec((C, wd), lambda b, h, qi: (ctx_blk0 + b, DV_OFF // wd + h)),
            pl.BlockSpec((1, w2), lambda b, h, qi: (0, 0)),
        ],
        out_specs=pl.BlockSpec((tq, wd), lambda b, h, qi: (q_row(b, qi), h)),
        out_shape=jax.ShapeDtypeStruct((rows_out, N_DIFF_HEADS * w2), BF16),
        scratch_shapes=[
            pltpu.VMEM((2 * hps, T, w2), BF16),
            pltpu.VMEM((hps, T // tk, DIFF_VT_ROWS, tk), BF16),
            pltpu.VMEM((2 * hps, 1, w2), F32),
            pltpu.VMEM((2 * hps, 1, tq), F32),
            pltpu.VMEM((2 * hps, DIFF_VT_ROWS, tq), F32),
        ],
        compiler_params=cparams,
        name="diff_attn",
    )(lam, p, p, p, p, p, subln_gain)

    wq = GQA_REP * HEAD_DIM
    yg = pl.pallas_call(
        functools.partial(_gqa_attn_kernel, **statics),
        grid=(B, N_GQA_KV, nq),
        in_specs=[
            pl.BlockSpec((tq, wq), lambda b, g, qi: (q_row(b, qi), GQ_OFF // wq + g)),
            pl.BlockSpec((S, w2), lambda b, g, qi: (b, GKV_OFF // w2 + g)),
            pl.BlockSpec((C, w2), lambda b, g, qi: (ctx_blk0 + b, GKV_OFF // w2 + g)),
        ],
        out_specs=pl.BlockSpec((tq, wq), lambda b, g, qi: (q_row(b, qi), g)),
        out_shape=jax.ShapeDtypeStruct((rows_out, N_GQA_HEADS * HEAD_DIM), BF16),
        scratch_shapes=[
            pltpu.VMEM((1, T, w2), BF16),
            pltpu.VMEM((1, T // tk, w2, tk), BF16),
            pltpu.VMEM((1, 1, w2), F32),
            pltpu.VMEM((GQA_REP, 1, tq), F32),
            pltpu.VMEM((GQA_REP, w2, tq), F32),
        ],
        compiler_params=cparams,
        name="gqa_attn",
    )(p, p, p)
    return yd, yg


def _mixer_kernel(x_ref, yd_ref, yg_ref, gd_ref, gg_ref, gt_ref, wpd_ref, wpg_ref, wo_ref, o_ref):
    pd = jnp.dot(yd_ref[...], wpd_ref[...], preferred_element_type=F32)
    pg = jnp.dot(yg_ref[...], wpg_ref[...], preferred_element_type=F32)
    m = jax.nn.sigmoid(gd_ref[...].astype(F32)) * pd + jax.nn.sigmoid(gg_ref[...].astype(F32)) * pg
    o_ref[...] = x_ref[...] + gt_ref[...] * jnp.dot(m.astype(BF16), wo_ref[...], preferred_element_type=F32)


def _mixer_out(xs, yd, yg, p, mods, wpd, wpg, wo, *, rows, tm, tiles_per_seq, n_batch):
    row_to_mod = lambda i: jnp.minimum(i // tiles_per_seq, n_batch)
    row_tile = pl.BlockSpec((tm, D_MODEL), lambda i: (i, 0))
    weight = pl.BlockSpec((D_MODEL, D_MODEL), lambda i: (0, 0))
    return pl.pallas_call(
        _mixer_kernel,
        grid=(rows // tm,),
        in_specs=[
            row_tile, row_tile, row_tile,
            pl.BlockSpec((tm, D_MODEL), lambda i: (i, GATE_OFF // D_MODEL)),
            pl.BlockSpec((tm, D_MODEL), lambda i: (i, GATE_OFF // D_MODEL + 1)),
            _mod_spec(2, row_to_mod),
            weight, weight, weight,
        ],
        out_specs=row_tile,
        out_shape=jax.ShapeDtypeStruct((rows, D_MODEL), F32),
        compiler_params=pltpu.CompilerParams(
            dimension_semantics=("arbitrary",), vmem_limit_bytes=VMEM_LIMIT),
        name="mixer_out",
    )(xs, yd, yg, p, p, mods, wpd, wpg, wo)


def _ffn_kernel(x_ref, g_ref, sc_ref, sh_ref, gt_ref, w1_ref, w3_ref, w2_ref, o_ref, h_scr, acc_scr):
    f = pl.program_id(1)

    @pl.when(f == 0)
    def _():
        h_scr[...] = _modulated_norm(x_ref[...], g_ref[...], sc_ref[...], sh_ref[...]).astype(BF16)
        acc_scr[...] = jnp.zeros_like(acc_scr)

    h = h_scr[...]
    a = jnp.dot(h, w1_ref[...], preferred_element_type=F32)
    b = jnp.dot(h, w3_ref[...], preferred_element_type=F32)
    acc_scr[...] += jnp.dot((_silu(a) * b).astype(BF16), w2_ref[...], preferred_element_type=F32)

    @pl.when(f == pl.num_programs(1) - 1)
    def _():
        o_ref[...] = x_ref[...] + gt_ref[...] * acc_scr[...]


def _ffn(xs, mods, norm_g, w1, w3, w2, *, tm, tf, tiles_per_seq, n_batch):
    rows = xs.shape[0]
    d_ff = w1.shape[1]
    row_to_mod = lambda i: jnp.minimum(i // tiles_per_seq, n_batch)
    row_tile = pl.BlockSpec((tm, D_MODEL), lambda i, f: (i, 0))
    return pl.pallas_call(
        _ffn_kernel,
        grid=(rows // tm, d_ff // tf),
        in_specs=[
            row_tile,
            pl.BlockSpec((1, D_MODEL), lambda i, f: (0, 0)),
            _mod_spec(4, row_to_mod), _mod_spec(3, row_to_mod), _mod_spec(5, row_to_mod),
            pl.BlockSpec((D_MODEL, tf), lambda i, f: (0, f)),
            pl.BlockSpec((D_MODEL, tf), lambda i, f: (0, f)),
            pl.BlockSpec((tf, D_MODEL), lambda i, f: (f, 0)),
        ],
        out_specs=row_tile,
        out_shape=jax.ShapeDtypeStruct((rows, D_MODEL), F32),
        scratch_shapes=[pltpu.VMEM((tm, D_MODEL), BF16), pltpu.VMEM((tm, D_MODEL), F32)],
        compiler_params=pltpu.CompilerParams(
            dimension_semantics=("arbitrary", "arbitrary"), vmem_limit_bytes=VMEM_LIMIT),
        name="ffn_swiglu",
    )(xs, norm_g, mods, mods, mods, w1, w3, w2)


def _top2_gates(logits):
    lane = lax.broadcasted_iota(jnp.int32, logits.shape, 1)
    n_lanes = logits.shape[1]
    lg = jnp.where(lane < N_EXPERTS, logits, NEG_BIG)
    m1 = jnp.max(lg, axis=-1, keepdims=True)
    i1 = jnp.min(jnp.where(lg == m1, lane, n_lanes), axis=-1, keepdims=True)
    lg2 = jnp.where(lane == i1, NEG_BIG, lg)
    m2 = jnp.max(lg2, axis=-1, keepdims=True)
    i2 = jnp.min(jnp.where(lg2 == m2, lane, n_lanes), axis=-1, keepdims=True)
    e2 = jnp.exp(m2 - m1)
    w_top = 1.0 / (1.0 + e2)
    idx = jnp.where(lane == 0, i1, jnp.where(lane == 1, i2, 0))
    wts = jnp.where(lane == 0, w_top, jnp.where(lane == 1, e2 * w_top, 0.0))
    return idx, wts


def _router_kernel(x_ref, g_ref, sc_ref, sh_ref, rhi_ref, rlo_ref, h_ref, idx_ref, wts_ref):
    h = _modulated_norm(x_ref[...], g_ref[...], sc_ref[...], sh_ref[...])
    h_ref[...] = h
    h_hi, h_lo = _split_bf16(h)
    logits = jnp.dot(h_hi, rhi_ref[...], preferred_element_type=F32)
    logits += jnp.dot(h_lo, rhi_ref[...], preferred_element_type=F32)
    logits += jnp.dot(h_hi, rlo_ref[...], preferred_element_type=F32)
    idx_ref[...], wts_ref[...] = _top2_gates(logits)


def _router(xs, mods, norm_g, r_hi, r_lo, *, rows, tm, tiles_per_seq, n_batch):
    row_to_mod = lambda i: jnp.minimum(i // tiles_per_seq, n_batch)
    row_tile = pl.BlockSpec((tm, D_MODEL), lambda i: (i, 0))
    lanes = pl.BlockSpec((tm, 128), lambda i: (i, 0))
    router = pl.BlockSpec((D_MODEL, 128), lambda i: (0, 0))
    return pl.pallas_call(
        _router_kernel,
        grid=(rows // tm,),
        in_specs=[row_tile, pl.BlockSpec((1, D_MODEL), lambda i: (0, 0)),
                  _mod_spec(4, row_to_mod), _mod_spec(3, row_to_mod), router, router],
        out_specs=[row_tile, lanes, lanes],
        out_shape=[jax.ShapeDtypeStruct((rows, D_MODEL), F32),
                   jax.ShapeDtypeStruct((rows, 128), jnp.int32),
                   jax.ShapeDtypeStruct((rows, 128), F32)],
        compiler_params=pltpu.CompilerParams(
            dimension_semantics=("arbitrary",), vmem_limit_bytes=VMEM_LIMIT),
        name="moe_router",
    )(xs, norm_g, mods, mods, r_hi, r_lo)


def _route_plan(idx, n_tok, tm_e):
    n_pairs = 2 * n_tok
    n_tiles = n_pairs // tm_e + N_EXPERTS
    n_rows = n_tiles * tm_e
    e_flat = idx.reshape(n_pairs)
    order = jnp.argsort(e_flat, stable=True).astype(jnp.int32)
    counts = jnp.sum((e_flat[:, None] == jnp.arange(N_EXPERTS, dtype=jnp.int32)[None, :]).astype(jnp.int32),
                     axis=0)
    padded = ((counts + tm_e - 1) // tm_e) * tm_e
    ends = jnp.cumsum(padded)
    tile_start = jnp.arange(n_tiles, dtype=jnp.int32) * tm_e
    tile_expert = jnp.minimum(jnp.sum((tile_start[:, None] >= ends[None, :]).astype(jnp.int32), axis=1),
                              N_EXPERTS - 1).astype(jnp.int32)
    n_used = (ends[-1] // tm_e).astype(jnp.int32).reshape(1)
    row = jnp.arange(n_rows, dtype=jnp.int32)
    row_expert = jnp.repeat(tile_expert, tm_e)
    offset = row - (ends - padded)[row_expert]
    valid = (offset < counts[row_expert]) & (row < ends[-1])
    pair = order[jnp.clip((jnp.cumsum(counts) - counts)[row_expert] + offset, 0, n_pairs - 1)]
    tok, choice = pair // 2, pair % 2
    src = jnp.where(valid, tok, 0)
    dump = n_pairs + jnp.cumsum(jnp.logical_not(valid).astype(jnp.int32)) - 1
    dst = jnp.where(valid, choice * n_tok + tok, dump)
    return tile_expert, n_used, src.reshape(n_tiles, 1, tm_e), dst.reshape(n_tiles, 1, tm_e)


def _experts_kernel(te_ref, nu_ref, src_ref, src_next_ref, dst_ref, h_hbm, w1_ref, w3_ref, w2_ref, y_hbm,
                    xbuf, xb_scr, acc_scr, ybuf, gsem, ssem, *, tm_e):
    t = pl.program_id(0)
    f = pl.program_id(1)
    nf = pl.num_programs(1)
    n_used = nu_ref[0]
    slot = t % 2

    def gather_rows(idx_ref, s):
        def body(r, carry):
            pltpu.make_async_copy(h_hbm.at[pl.ds(idx_ref[0, r], 1)], xbuf.at[s, pl.ds(r, 1)], gsem.at[s]).start()
            return carry
        lax.fori_loop(0, tm_e, body, 0, unroll=8)

    def wait_gather(s):
        pltpu.make_async_copy(h_hbm.at[pl.ds(0, tm_e)], xbuf.at[s], gsem.at[s]).wait()

    def scatter_rows(s):
        def body(r, carry):
            pltpu.make_async_copy(ybuf.at[s, pl.ds(r, 1)], y_hbm.at[pl.ds(dst_ref[0, r], 1)], ssem.at[s]).start()
            return carry
        lax.fori_loop(0, tm_e, body, 0, unroll=8)

    def wait_scatter(s):
        pltpu.make_async_copy(ybuf.at[s], y_hbm.at[pl.ds(0, tm_e)], ssem.at[s]).wait()

    @pl.when(t < n_used)
    def _():
        @pl.when(f == 0)
        def _():
            @pl.when(t == 0)
            def _():
                gather_rows(src_ref, 0)
            for s in range(2):
                @pl.when(slot == s)
                def _():
                    wait_gather(s)
                    xb_scr[...] = xbuf[s].astype(BF16)

                    @pl.when(t + 1 < n_used)
                    def _():
                        gather_rows(src_next_ref, 1 - s)
            acc_scr[...] = jnp.zeros_like(acc_scr)

        x = xb_scr[...]
        a = jnp.dot(x, w1_ref[...], preferred_element_type=F32)
        b = jnp.dot(x, w3_ref[...], preferred_element_type=F32)
        acc_scr[...] += jnp.dot((_silu(a) * b).astype(BF16), w2_ref[...], preferred_element_type=F32)

    @pl.when(f == nf - 1)
    def _():
        for s in range(2):
            @pl.when(slot == s)
            def _():
                @pl.when(t >= 2)
                def _():
                    wait_scatter(s)

                @pl.when(t < n_used)
                def _():
                    ybuf[s] = acc_scr[...]

                @pl.when(t >= n_used)
                def _():
                    ybuf[s] = jnp.zeros(ybuf.shape[1:], F32)

                scatter_rows(s)

                @pl.when(t == pl.num_programs(0) - 1)
                def _():
                    wait_scatter(s)
                    wait_scatter(1 - s)


def _experts(h, plan, w1, w3, w2, *, n_tok, tm_e, tf):
    tile_expert, n_used, src, dst = plan
    n_tiles = src.shape[0]
    d_ff = w1.shape[2]
    nf = d_ff // tf
    n_out = n_tiles * tm_e

    def chunk(t, f, nu):
        return jnp.where(t < nu[0], f, nf - 1)

    smem_tile = lambda shift: pl.BlockSpec(
        (None, 1, tm_e), lambda t, f, te, nu: (jnp.minimum(t + shift, n_tiles - 1), 0, 0),
        memory_space=pltpu.SMEM)
    grid_spec = pltpu.PrefetchScalarGridSpec(
        num_scalar_prefetch=2,
        grid=(n_tiles, nf),
        in_specs=[
            smem_tile(0), smem_tile(1), smem_tile(0),
            pl.BlockSpec(memory_space=pl.ANY),
            pl.BlockSpec((None, D_MODEL, tf), lambda t, f, te, nu: (te[t], 0, chunk(t, f, nu))),
            pl.BlockSpec((None, D_MODEL, tf), lambda t, f, te, nu: (te[t], 0, chunk(t, f, nu))),
            pl.BlockSpec((None, tf, D_MODEL), lambda t, f, te, nu: (te[t], chunk(t, f, nu), 0)),
        ],
        out_specs=pl.BlockSpec(memory_space=pl.ANY),
        scratch_shapes=[
            pltpu.VMEM((2, tm_e, D_MODEL), F32),
            pltpu.VMEM((tm_e, D_MODEL), BF16),
            pltpu.VMEM((tm_e, D_MODEL), F32),
            pltpu.VMEM((2, tm_e, D_MODEL), F32),
            pltpu.SemaphoreType.DMA((2,)),
            pltpu.SemaphoreType.DMA((2,)),
        ],
    )
    return pl.pallas_call(
        functools.partial(_experts_kernel, tm_e=tm_e),
        grid_spec=grid_spec,
        out_shape=jax.ShapeDtypeStruct((n_out, D_MODEL), F32),
        compiler_params=pltpu.CompilerParams(
            dimension_semantics=("arbitrary", "arbitrary"), vmem_limit_bytes=VMEM_LIMIT),
        name="moe_experts",
    )(tile_expert, n_used, src, src, dst, h, w1, w3, w2)


def _combine_kernel(x_ref, y0_ref, y1_ref, wts_ref, gt_ref, fg_ref, o_ref):
    wts = wts_ref[...]
    moe = wts[:, 0:1] * y0_ref[...] + wts[:, 1:2] * y1_ref[...]
    y = x_ref[...] + gt_ref[...] * moe
    ms = jnp.mean(y * y, axis=-1, keepdims=True)
    o_ref[...] = y * lax.rsqrt(ms + EPS) * fg_ref[...]


def _combine(xs, y, wts, mods, final_g, *, rows, tm, tiles_per_seq, n_batch):
    row_to_mod = lambda i: jnp.minimum(i // tiles_per_seq, n_batch)
    row_tile = pl.BlockSpec((tm, D_MODEL), lambda i: (i, 0))
    return pl.pallas_call(
        _combine_kernel,
        grid=(rows // tm,),
        in_specs=[
            row_tile, row_tile,
            pl.BlockSpec((tm, D_MODEL), lambda i: (i + rows // tm, 0)),
            pl.BlockSpec((tm, 128), lambda i: (i, 0)),
            _mod_spec(5, row_to_mod),
            pl.BlockSpec((1, D_MODEL), lambda i: (0, 0)),
        ],
        out_specs=row_tile,
        out_shape=jax.ShapeDtypeStruct((rows, D_MODEL), F32),
        compiler_params=pltpu.CompilerParams(
            dimension_semantics=("arbitrary",), vmem_limit_bytes=VMEM_LIMIT),
        name="moe_combine",
    )(xs, y, y, wts, mods, final_g)


def _moe(xs, mods, norm_g, r_hi, r_lo, final_g, w1, w3, w2, *, rows, tm, tm_e, tf, tiles_per_seq, n_batch):
    h, idx, wts = _router(xs, mods, norm_g, r_hi, r_lo, rows=rows, tm=tm, tiles_per_seq=tiles_per_seq,
                          n_batch=n_batch)
    plan = _route_plan(idx[:, :2], rows, tm_e)
    y = _experts(h, plan, w1, w3, w2, n_tok=rows, tm_e=tm_e, tf=tf)
    return _combine(xs, y, wts, mods, final_g, rows=rows, tm=tm, tiles_per_seq=tiles_per_seq, n_batch=n_batch)


def _deinterleave(n=HEAD_DIM):
    return np.concatenate([np.arange(0, n, 2), np.arange(1, n, 2)])


def _proj_columns():
    de = _deinterleave()
    o_dq, o_gq, o_dk, o_dv, o_gk, o_gv, o_gate = 0, 1024, 2048, 3072, 4096, 4352, 4608
    cols = np.zeros(IN_W, np.int32)
    scale = np.ones(IN_W, np.float32)
    cols[GATE_OFF:GATE_OFF + 2048] = o_gate + np.arange(2048)
    for h in range(N_DIFF_HEADS):
        for c in range(2):
            dst = 128 * h + 64 * c
            cols[DQ_OFF + dst:DQ_OFF + dst + 64] = o_dq + dst + de
            cols[DK_OFF + dst:DK_OFF + dst + 64] = o_dk + dst + de
    scale[DQ_OFF:DQ_OFF + 1024] = HEAD_DIM ** -0.5
    for j in range(N_GQA_HEADS):
        cols[GQ_OFF + 64 * j:GQ_OFF + 64 * j + 64] = o_gq + 64 * j + de
    cols[DV_OFF:DV_OFF + 1024] = o_dv + np.arange(1024)
    for g in range(N_GQA_KV):
        cols[GKV_OFF + 128 * g:GKV_OFF + 128 * g + 64] = o_gk + 64 * g + de
        cols[GKV_OFF + 128 * g + 64:GKV_OFF + 128 * g + 128] = o_gv + 64 * g + np.arange(64)
    return cols, scale


def _rope_tables(S, pad_rows):
    rows = S // GRID_W
    row = jnp.repeat(jnp.arange(rows, dtype=F32), GRID_W)
    col = jnp.tile(jnp.arange(GRID_W, dtype=F32), rows)
    half = HEAD_DIM // 2
    inv_freq = ROPE_THETA ** (-jnp.arange(0, half, 2, dtype=F32) / half)
    ang = jnp.concatenate([row[:, None] * inv_freq, col[:, None] * inv_freq], axis=-1)
    cos, sin = jnp.cos(ang), jnp.sin(ang)
    reps = PROJ_TN // HEAD_DIM
    cos_t = jnp.tile(jnp.concatenate([cos, cos], axis=-1), (1, reps))
    sin_t = jnp.tile(jnp.concatenate([-sin, sin], axis=-1), (1, reps))
    cos_t = jnp.concatenate([cos_t, jnp.ones((pad_rows, PROJ_TN), F32)], axis=0)
    sin_t = jnp.concatenate([sin_t, jnp.zeros((pad_rows, PROJ_TN), F32)], axis=0)
    return cos_t, sin_t


def kernel(x, c, ctx, c_ctx, ada_w, ada_b, norm_attn_g, norm_ffn_g, w_in, q_norm_g, k_norm_g, diff_lambda,
           diff_subln_g, w_proj_diff, w_proj_gqa, w_out, ffn_w1, ffn_w3, ffn_w2, moe_router, moe_w1, moe_w3,
           moe_w2, final_norm_g):
    B, S, D = x.shape
    C = ctx.shape[1]
    depth = ada_w.shape[0]
    assert D == D_MODEL and depth == 2 and B + 1 <= MOD_ROWS
    assert w_in.shape[2] == IN_W and moe_router.shape[2] == N_EXPERTS
    tiles = _pick_tiles(B, S, C, ffn_w1.shape[2], moe_w1.shape[3])
    n_lat = B * S

    cvec = jnp.zeros((MOD_ROWS, D), F32).at[:B].set(c).at[B].set(c_ctx)
    mods_all = _ada_mods(cvec, ada_w, ada_b).reshape(depth, MOD_ROWS * N_MODS, 1, D)

    cols, col_scale = _proj_columns()
    de = _deinterleave()
    cos_t, sin_t = _rope_tables(S, tiles.tm_proj)
    blk = np.arange(PROJ_TN) // HEAD_DIM
    ones_bd = jnp.asarray(blk[:, None] == blk[None, :], BF16)
    is_key_lane = (np.arange(PROJ_TN) % (2 * HEAD_DIM)) < HEAD_DIM

    xs = jnp.concatenate([x.reshape(n_lat, D), ctx.reshape(B * C, D)], axis=0)
    for l in range(depth):
        last = l == depth - 1
        mods = mods_all[l]
        lam_init = 0.8 - 0.6 * math.exp(-0.3 * l)
        lq1, lk1, lq2, lk2 = diff_lambda[l]
        lam = (jnp.exp(jnp.sum(lq1 * lk1)) - jnp.exp(jnp.sum(lq2 * lk2)) + lam_init).reshape(1).astype(F32)
        w = (jnp.take(w_in[l], cols, axis=1) * col_scale).astype(BF16)
        qg = jnp.tile(q_norm_g[l][de] * HEAD_DIM ** -0.5, PROJ_TN // HEAD_DIM).reshape(1, PROJ_TN)
        kg = jnp.where(is_key_lane, jnp.tile(k_norm_g[l][de], PROJ_TN // HEAD_DIM), 1.0).reshape(1, PROJ_TN)
        subln = (diff_subln_g[l] * (1.0 - lam_init)).reshape(1, 2 * HEAD_DIM)

        p = _project(xs, mods, norm_attn_g[l].reshape(1, D), w, cos_t, sin_t, ones_bd, qg, kg,
                     tm=tiles.tm_proj, lat_tiles=n_lat // tiles.tm_proj, tiles_per_seq=S // tiles.tm_proj,
                     n_batch=B)
        yd, yg = _attention(p, lam, subln, B=B, S=S, C=C, tiles=tiles, ctx_queries=not last)
        rows = n_lat if last else n_lat + B * C
        xs = _mixer_out(xs, yd, yg, p, mods, w_proj_diff[l].astype(BF16), w_proj_gqa[l].astype(BF16),
                        w_out[l].astype(BF16), rows=rows, tm=tiles.tm_mix, tiles_per_seq=S // tiles.tm_mix,
                        n_batch=B)
        i = l // 2
        if l % 2 == 0:
            xs = _ffn(xs, mods, norm_ffn_g[l].reshape(1, D), ffn_w1[i].astype(BF16), ffn_w3[i].astype(BF16),
                      ffn_w2[i].astype(BF16), tm=tiles.tm_ffn, tf=tiles.tf_ffn,
                      tiles_per_seq=S // tiles.tm_ffn, n_batch=B)
        else:
            r_pad = jnp.zeros((D, 128), F32).at[:, :N_EXPERTS].set(moe_router[i])
            r_hi = r_pad.astype(BF16)
            r_lo = (r_pad - r_hi.astype(F32)).astype(BF16)
            xs = _moe(xs, mods, norm_ffn_g[l].reshape(1, D), r_hi, r_lo, final_norm_g.reshape(1, D),
                      moe_w1[i].astype(BF16), moe_w3[i].astype(BF16), moe_w2[i].astype(BF16),
                      rows=rows, tm=tiles.tm_mix, tm_e=tiles.tm_moe, tf=tiles.tf_moe,
                      tiles_per_seq=S // tiles.tm_mix, n_batch=B)
    return xs.reshape(B, S, D)
```

```python
import functools
import math
from typing import NamedTuple

import numpy as np
import jax
import jax.numpy as jnp
from jax import lax
from jax.experimental import pallas as pl
from jax.experimental.pallas import tpu as pltpu

F32 = jnp.float32
BF16 = jnp.bfloat16

D_MODEL = 1024
HEAD_DIM = 64
N_DIFF_HEADS = 8
N_GQA_HEADS = 16
N_GQA_KV = 4
GQA_REP = N_GQA_HEADS // N_GQA_KV
N_EXPERTS = 8
GRID_W = 64
ROPE_THETA = 10000.0
EPS = 1e-6
N_MODS = 6
MOD_ROWS = 16
NEG_BIG = -1e30

GATE_OFF = 0
DQ_OFF = 2048
GQ_OFF = 3072
DK_OFF = 4096
DV_OFF = 5120
GKV_OFF = 6144
IN_W = 6656
PROJ_TN = 512
ROPE_TILES = (4, 5, 8, 9)
QNORM_TILES = (6, 7)
KV_TILE = 12

VMEM_LIMIT = 52 * 1024 * 1024


class Tiles(NamedTuple):
    tm_proj: int
    tq: int
    tk: int
    tm_mix: int
    tm_ffn: int
    tf_ffn: int
    tf_moe: int
    tm_moe: int


def _largest_divisor(n, candidates):
    for c in candidates:
        if n % c == 0:
            return c
    raise ValueError(f"no tile in {candidates} divides {n}")


def _pick_tiles(B, S, C, d_ff, d_ff_e):
    rows_common = math.gcd(S, B * C)
    tm = _largest_divisor(rows_common, (512, 256, 128))
    return Tiles(
        tm_proj=_largest_divisor(rows_common, (1024, 512, 256, 128)),
        tq=C,
        tk=_largest_divisor(math.gcd(S, C), (256, 128)),
        tm_mix=tm,
        tm_ffn=_largest_divisor(rows_common, (1024, 512, 256, 128)),
        tf_ffn=_largest_divisor(d_ff, (1408, 1024, 512, 256, 128)),
        tf_moe=_largest_divisor(d_ff_e, (896, 512, 256, 128)),
        tm_moe=_largest_divisor(2 * B * S, (512, 256)),
    )


def _split_bf16(v):
    hi = v.astype(BF16)
    lo = (v - hi.astype(F32)).astype(BF16)
    return hi, lo


def _ada_kernel(c_ref, w_ref, b_ref, o_ref):
    c = c_ref[...]
    s = c / (1.0 + jnp.exp(-c))
    s_hi, s_lo = _split_bf16(s)
    w_hi, w_lo = _split_bf16(w_ref[...])
    acc = jnp.dot(s_hi, w_hi, preferred_element_type=F32)
    acc += jnp.dot(s_lo, w_hi, preferred_element_type=F32)
    acc += jnp.dot(s_hi, w_lo, preferred_element_type=F32)
    o_ref[...] = acc + b_ref[...]


def _ada_mods(cvec, ada_w, ada_b):
    depth, d, n = ada_w.shape
    tn = _largest_divisor(n, (1536, 1024, 512))
    return pl.pallas_call(
        _ada_kernel,
        grid=(depth, n // tn),
        in_specs=[
            pl.BlockSpec((MOD_ROWS, d), lambda l, j: (0, 0)),
            pl.BlockSpec((None, d, tn), lambda l, j: (l, 0, j)),
            pl.BlockSpec((None, 1, tn), lambda l, j: (l, 0, j)),
        ],
        out_specs=pl.BlockSpec((None, MOD_ROWS, tn), lambda l, j: (l, 0, j)),
        out_shape=jax.ShapeDtypeStruct((depth, MOD_ROWS, n), F32),
        compiler_params=pltpu.CompilerParams(
            dimension_semantics=("arbitrary", "arbitrary"), vmem_limit_bytes=VMEM_LIMIT),
        name="ada_mods",
    )(cvec, ada_w, ada_b.reshape(depth, 1, n))


def _modulated_norm(x, g, scale, shift):
    ms = jnp.mean(x * x, axis=-1, keepdims=True)
    return (x * lax.rsqrt(ms + EPS) * g) * (1.0 + scale) + shift


def _silu(a):
    return a / (1.0 + jnp.exp(-a))


def _mod_spec(k, row_to_mod):
    return pl.BlockSpec((None, 1, D_MODEL), lambda i, *_: (row_to_mod(i) * N_MODS + k, 0, 0))


def _rope(z, cos, sin_signed, lane):
    first_half = (lane & (HEAD_DIM // 2)) == 0
    n = z.shape[1]
    partner = jnp.where(first_half, pltpu.roll(z, n - HEAD_DIM // 2, 1), pltpu.roll(z, HEAD_DIM // 2, 1))
    return z * cos + partner * sin_signed


def _head_rms(z, ones_ref, gain):
    ss = jnp.dot((z * z).astype(BF16), ones_ref[...], preferred_element_type=F32)
    return z * lax.rsqrt(ss * (1.0 / HEAD_DIM) + EPS) * gain


def _proj_kernel(x_ref, g_ref, sc_ref, sh_ref, w_ref, cos_ref, sin_ref, ones_ref, qg_ref, kg_ref,
                 o_ref, h_scr):
    j = pl.program_id(1)

    @pl.when(j == 0)
    def _():
        h_scr[...] = _modulated_norm(x_ref[...], g_ref[...], sc_ref[...], sh_ref[...]).astype(BF16)

    def project():
        z = jnp.dot(h_scr[...], w_ref[...], preferred_element_type=F32)
        return z, lax.broadcasted_iota(jnp.int32, z.shape, 1)

    is_rope = functools.reduce(jnp.logical_or, [j == t for t in ROPE_TILES])
    is_qnorm = functools.reduce(jnp.logical_or, [j == t for t in QNORM_TILES])
    is_kv = j == KV_TILE
    is_plain = jnp.logical_not(is_rope | is_qnorm | is_kv)

    @pl.when(is_plain)
    def _():
        z, _ = project()
        o_ref[...] = z.astype(BF16)

    @pl.when(is_rope)
    def _():
        z, lane = project()
        o_ref[...] = _rope(z, cos_ref[...], sin_ref[...], lane).astype(BF16)

    @pl.when(is_qnorm)
    def _():
        z, lane = project()
        zn = _head_rms(z, ones_ref, qg_ref[...])
        o_ref[...] = _rope(zn, cos_ref[...], sin_ref[...], lane).astype(BF16)

    @pl.when(is_kv)
    def _():
        z, lane = project()
        zn = _head_rms(z, ones_ref, kg_ref[...])
        zr = _rope(zn, cos_ref[...], sin_ref[...], lane)
        is_key_lane = (lane & HEAD_DIM) == 0
        o_ref[...] = jnp.where(is_key_lane, zr, z).astype(BF16)


def _project(xs, mods, norm_g, w, cos_t, sin_t, ones_bd, qg, kg, *, tm, lat_tiles, tiles_per_seq, n_batch):
    rows = xs.shape[0]
    row_to_mod = lambda i: jnp.minimum(i // tiles_per_seq, n_batch)
    rope_row = lambda i: jnp.where(i < lat_tiles, i % tiles_per_seq, tiles_per_seq)
    const = lambda i, j: (0, 0)
    return pl.pallas_call(
        _proj_kernel,
        grid=(rows // tm, IN_W // PROJ_TN),
        in_specs=[
            pl.BlockSpec((tm, D_MODEL), lambda i, j: (i, 0)),
            pl.BlockSpec((1, D_MODEL), const),
            _mod_spec(1, row_to_mod),
            _mod_spec(0, row_to_mod),
            pl.BlockSpec((D_MODEL, PROJ_TN), lambda i, j: (0, j)),
            pl.BlockSpec((tm, PROJ_TN), lambda i, j: (rope_row(i), 0)),
            pl.BlockSpec((tm, PROJ_TN), lambda i, j: (rope_row(i), 0)),
            pl.BlockSpec((PROJ_TN, PROJ_TN), const),
            pl.BlockSpec((1, PROJ_TN), const),
            pl.BlockSpec((1, PROJ_TN), const),
        ],
        out_specs=pl.BlockSpec((tm, PROJ_TN), lambda i, j: (i, j)),
        out_shape=jax.ShapeDtypeStruct((rows, IN_W), BF16),
        scratch_shapes=[pltpu.VMEM((tm, D_MODEL), BF16)],
        compiler_params=pltpu.CompilerParams(
            dimension_semantics=("arbitrary", "arbitrary"), vmem_limit_bytes=VMEM_LIMIT),
        name="in_proj",
    )(xs, norm_g, mods, mods, w, cos_t, sin_t, ones_bd, qg, kg)


STAB_LANE = HEAD_DIM
L_FLOOR = 1e-26
DIFF_VT_ROWS = 2 * HEAD_DIM + 16
DIFF_HEADS_PER_STEP = 2


def _aug_keys(k):
    lane = lax.broadcasted_iota(jnp.int32, k.shape, 1)
    in_key = lane < HEAD_DIM
    sq = jnp.where(in_key, k * k, 0.0).astype(BF16)
    norms = jnp.dot(sq, jnp.ones((k.shape[1], k.shape[1]), BF16), preferred_element_type=F32)
    ka = jnp.where(in_key, k, jnp.where(lane == STAB_LANE, 1.0, 0.0)).astype(BF16)
    return ka, jnp.max(norms, axis=0, keepdims=True)


def _query_mats(slab, kmax2):
    lane = lax.broadcasted_iota(jnp.int32, slab.shape, 1)
    in_key = lane < HEAD_DIM
    sq = jnp.where(in_key, slab * slab, 0.0).astype(BF16)
    norms = jnp.dot(sq, jnp.ones((slab.shape[1], slab.shape[1]), BF16), preferred_element_type=F32)
    nk = norms * kmax2
    bound = nk * lax.rsqrt(nk + 1e-30)
    plain = jnp.where(in_key, slab, 0.0)
    return jnp.where(lane == STAB_LANE, -bound, plain).astype(BF16), plain.astype(BF16)


def _attend(ka_scr, vt_scr, mats, acc_scr, *, first, n_chunks, tk):
    def scores(r, c, qmat):
        off = pl.multiple_of(c * tk, tk)
        return lax.dot_general(ka_scr[r % ka_scr.shape[0], pl.ds(off, tk), :], qmat, (((1,), (1,)), ((), ())),
                               preferred_element_type=F32)

    acc_scr[...] = jnp.zeros(acc_scr.shape, F32)

    def all_scores(c):
        return tuple(scores(r, c, shifted) for r, (shifted, _) in enumerate(mats))

    def accumulate(c, s_all):
        for r, s in enumerate(s_all):
            vt = vt_scr[r * vt_scr.shape[0] // len(mats), c]
            acc_scr[r] += jnp.dot(vt, jnp.exp(s).astype(BF16), preferred_element_type=F32)

    def fast_body(c, s_cur):
        s_next = all_scores(c + 1)
        accumulate(c, s_cur)
        return s_next

    trips = n_chunks - 1 - first
    s_last = lax.fori_loop(first, n_chunks - 1, fast_body, all_scores(first),
                           unroll=_largest_divisor(trips, (16, 8, 4, 2, 1)) if trips > 0 else 1)
    accumulate(n_chunks - 1, s_last)


def _attend_fallback(ka_scr, vt_scr, mats, m_scr, acc_scr, *, first, n_chunks, tk):
    m_scr[...] = jnp.full(m_scr.shape, NEG_BIG, F32)
    acc_scr[...] = jnp.zeros(acc_scr.shape, F32)

    def body(c, carry):
        off = pl.multiple_of(c * tk, tk)
        for r, (_, plain) in enumerate(mats):
            s = lax.dot_general(ka_scr[r % ka_scr.shape[0], pl.ds(off, tk), :], plain, (((1,), (1,)), ((), ())),
                                preferred_element_type=F32)
            m_old = m_scr[r]
            m_new = jnp.maximum(m_old, jnp.max(s, axis=0, keepdims=True))
            p = jnp.exp(s - m_new).astype(BF16)
            vt = vt_scr[r * vt_scr.shape[0] // len(mats), c]
            acc_scr[r] = acc_scr[r] * jnp.exp(m_old - m_new) + jnp.dot(vt, p, preferred_element_type=F32)
            m_scr[r] = m_new
        return carry

    lax.fori_loop(first, n_chunks, body, 0)


def _softmax_tile(ka_scr, vt_scr, kmax_scr, slabs, m_scr, acc_scr, l_row, *, n_lat_chunks, n_chunks, tk, ctx_tile):
    mats = [_query_mats(slab, kmax_scr[r % kmax_scr.shape[0]]) for r, slab in enumerate(slabs)]

    def run(first):
        _attend(ka_scr, vt_scr, mats, acc_scr, first=first, n_chunks=n_chunks, tk=tk)
        l_min = functools.reduce(
            jnp.minimum, [jnp.min(acc_scr[r, l_row:l_row + 1, :]) for r in range(len(slabs))])

        @pl.when(jnp.logical_not(l_min >= L_FLOOR))
        def _():
            _attend_fallback(ka_scr, vt_scr, mats, m_scr, acc_scr, first=first, n_chunks=n_chunks, tk=tk)

    if ctx_tile is None:
        run(0)
    else:
        pl.when(jnp.logical_not(ctx_tile))(functools.partial(run, 0))
        pl.when(ctx_tile)(functools.partial(run, n_lat_chunks))


def _chunk_rows(c, n_lat_chunks, tk, lat_ref, ctx_ref):
    if c < n_lat_chunks:
        return lat_ref[c * tk:(c + 1) * tk, :]
    return ctx_ref[(c - n_lat_chunks) * tk:(c - n_lat_chunks + 1) * tk, :]


def _diff_attn_kernel(lam_ref, q_ref, kl_ref, vl_ref, kc_ref, vc_ref, g_ref, o_ref,
                      ka_scr, vt_scr, kmax_scr, m_scr, acc_scr, *, nq_lat, n_lat_chunks, n_chunks, tk, ctx_queries):
    dv = 2 * HEAD_DIM
    heads = [slice(a * dv, (a + 1) * dv) for a in range(DIFF_HEADS_PER_STEP)]

    @pl.when(pl.program_id(2) == 0)
    def _():
        kmax = [None] * (2 * len(heads))
        for c in range(n_chunks):
            k_all = _chunk_rows(c, n_lat_chunks, tk, kl_ref, kc_ref).astype(F32)
            v_all = _chunk_rows(c, n_lat_chunks, tk, vl_ref, vc_ref)
            for a, cols in enumerate(heads):
                k = k_all[:, cols]
                for r, keys in ((2 * a, k), (2 * a + 1, pltpu.roll(k, HEAD_DIM, 1))):
                    ka, n2 = _aug_keys(keys)
                    ka_scr[r, c * tk:(c + 1) * tk, :] = ka
                    kmax[r] = n2 if kmax[r] is None else jnp.maximum(kmax[r], n2)
                vt_scr[a, c, :dv, :] = v_all[:, cols].T
                vt_scr[a, c, dv:, :] = jnp.ones((DIFF_VT_ROWS - dv, tk), BF16)
        for r, n2 in enumerate(kmax):
            kmax_scr[r] = n2

    q_all = q_ref[...].astype(F32)
    slabs = []
    for cols in heads:
        slabs += [q_all[:, cols], pltpu.roll(q_all[:, cols], HEAD_DIM, 1)]
    ctx_tile = (pl.program_id(2) == nq_lat) if ctx_queries else None
    _softmax_tile(ka_scr, vt_scr, kmax_scr, slabs, m_scr, acc_scr, dv,
                  n_lat_chunks=n_lat_chunks, n_chunks=n_chunks, tk=tk, ctx_tile=ctx_tile)
    for a, cols in enumerate(heads):
        a1, a2 = acc_scr[2 * a], acc_scr[2 * a + 1]
        yt = a1[:dv] * (1.0 / a1[dv:dv + 1]) - lam_ref[0] * (a2[:dv] * (1.0 / a2[dv:dv + 1]))
        y = yt.T
        ms = jnp.mean(y * y, axis=-1, keepdims=True)
        o_ref[:, cols] = (y * lax.rsqrt(ms + EPS) * g_ref[...]).astype(BF16)


def _gqa_attn_kernel(q_ref, kvl_ref, kvc_ref, o_ref, ka_scr, vt_scr, kmax_scr, m_scr, acc_scr,
                     *, nq_lat, n_lat_chunks, n_chunks, tk, ctx_queries):
    @pl.when(pl.program_id(2) == 0)
    def _():
        kmax = None
        for c in range(n_chunks):
            kv = _chunk_rows(c, n_lat_chunks, tk, kvl_ref, kvc_ref)
            ka, n2 = _aug_keys(kv.astype(F32))
            ka_scr[0, c * tk:(c + 1) * tk, :] = ka
            kmax = n2 if kmax is None else jnp.maximum(kmax, n2)
            t = kv.T
            row = lax.broadcasted_iota(jnp.int32, t.shape, 0)
            vt_scr[0, c] = jnp.where(row < HEAD_DIM, jnp.ones_like(t), t)
        kmax_scr[0] = kmax

    qf = q_ref[...].astype(F32)
    tq = qf.shape[0]
    lane = lax.broadcasted_iota(jnp.int32, (tq, 2 * HEAD_DIM), 1)
    slabs = []
    for r in range(GQA_REP):
        slab = qf[:, (r // 2) * 2 * HEAD_DIM:(r // 2 + 1) * 2 * HEAD_DIM]
        slabs.append(pltpu.roll(slab, HEAD_DIM, 1) if r % 2 else slab)
    ctx_tile = (pl.program_id(2) == nq_lat) if ctx_queries else None
    _softmax_tile(ka_scr, vt_scr, kmax_scr, slabs, m_scr, acc_scr, 0,
                  n_lat_chunks=n_lat_chunks, n_chunks=n_chunks, tk=tk, ctx_tile=ctx_tile)

    def head_out(r):
        acc = acc_scr[r]
        return (acc * (1.0 / acc[0:1])).T

    for j in range(GQA_REP // 2):
        pair = jnp.where(lane < HEAD_DIM, pltpu.roll(head_out(2 * j), HEAD_DIM, 1), head_out(2 * j + 1))
        o_ref[:, j * 2 * HEAD_DIM:(j + 1) * 2 * HEAD_DIM] = pair.astype(BF16)


def _attention(p, lam, subln_gain, *, B, S, C, tiles, ctx_queries):
    tq, tk = tiles.tq, tiles.tk
    nq_lat = S // tq
    nq = nq_lat + (1 if ctx_queries else 0)
    ctx_blk0 = (B * S) // C
    rows_out = B * S + (B * C if ctx_queries else 0)
    T = S + C
    statics = dict(nq_lat=nq_lat, n_lat_chunks=S // tk, n_chunks=T // tk, tk=tk, ctx_queries=ctx_queries)

    def q_row(b, qi):
        if not ctx_queries:
            return b * nq_lat + qi
        return jnp.where(qi < nq_lat, b * nq_lat + qi, ctx_blk0 + b)

    cparams = pltpu.CompilerParams(
        dimension_semantics=("arbitrary", "arbitrary", "arbitrary"), vmem_limit_bytes=VMEM_LIMIT)
    w2 = 2 * HEAD_DIM
    hps = DIFF_HEADS_PER_STEP
    wd = hps * w2
    yd = pl.pallas_call(
        functools.partial(_diff_attn_kernel, **statics),
        grid=(B, N_DIFF_HEADS // hps, nq),
        in_specs=[
            pl.BlockSpec(memory_space=pltpu.SMEM),
            pl.BlockSpec((tq, wd), lambda b, h, qi: (q_row(b, qi), DQ_OFF // wd + h)),
            pl.BlockSpec((S, wd), lambda b, h, qi: (b, DK_OFF // wd + h)),
            pl.BlockSpec((S, wd), lambda b, h, qi: (b, DV_OFF // wd + h)),
            pl.BlockSpec((C, wd), lambda b, h, qi: (ctx_blk0 + b, DK_OFF // wd + h)),
            pl.BlockSpec((C, wd), lambda b, h, qi: (ctx_blk0 + b, DV_OFF // wd + h)),
            pl.BlockSpec((1, w2), lambda b, h, qi: (0, 0)),
        ],
        out_specs=pl.BlockSpec((tq, wd), lambda b, h, qi: (q_row(b, qi), h)),
        out_shape=jax.ShapeDtypeStruct((rows_out, N_DIFF_HEADS * w2), BF16),
        scratch_shapes=[
            pltpu.VMEM((2 * hps, T, w2), BF16),
            pltpu.VMEM((hps, T // tk, DIFF_VT_ROWS, tk), BF16),
            pltpu.VMEM((2 * hps, 1, w2), F32),
            pltpu.VMEM((2 * hps, 1, tq), F32),
            pltpu.VMEM((2 * hps, DIFF_VT_ROWS, tq), F32),
        ],
        compiler_params=cparams,
        name="diff_attn",
    )(lam, p, p, p, p, p, subln_gain)

    wq = GQA_REP * HEAD_DIM
    yg = pl.pallas_call(
        functools.partial(_gqa_attn_kernel, **statics),
        grid=(B, N_GQA_KV, nq),
        in_specs=[
            pl.BlockSpec((tq, wq), lambda b, g, qi: (q_row(b, qi), GQ_OFF // wq + g)),
            pl.BlockSpec((S, w2), lambda b, g, qi: (b, GKV_OFF // w2 + g)),
            pl.BlockSpec((C, w2), lambda b, g, qi: (ctx_blk0 + b, GKV_OFF // w2 + g)),
        ],
        out_specs=pl.BlockSpec((tq, wq), lambda b, g, qi: (q_row(b, qi), g)),
        out_shape=jax.ShapeDtypeStruct((rows_out, N_GQA_HEADS * HEAD_DIM), BF16),
        scratch_shapes=[
            pltpu.VMEM((1, T, w2), BF16),
            pltpu.VMEM((1, T // tk, w2, tk), BF16),
            pltpu.VMEM((1, 1, w2), F32),
            pltpu.VMEM((GQA_REP, 1, tq), F32),
            pltpu.VMEM((GQA_REP, w2, tq), F32),
        ],
        compiler_params=cparams,
        name="gqa_attn",
    )(p, p, p)
    return yd, yg


def _mixer_kernel(x_ref, yd_ref, yg_ref, gd_ref, gg_ref, gt_ref, wpd_ref, wpg_ref, wo_ref, o_ref):
    pd = jnp.dot(yd_ref[...], wpd_ref[...], preferred_element_type=F32)
    pg = jnp.dot(yg_ref[...], wpg_ref[...], preferred_element_type=F32)
    m = jax.nn.sigmoid(gd_ref[...].astype(F32)) * pd + jax.nn.sigmoid(gg_ref[...].astype(F32)) * pg
    o_ref[...] = x_ref[...] + gt_ref[...] * jnp.dot(m.astype(BF16), wo_ref[...], preferred_element_type=F32)


def _mixer_out(xs, yd, yg, p, mods, wpd, wpg, wo, *, rows, tm, tiles_per_seq, n_batch):
    row_to_mod = lambda i: jnp.minimum(i // tiles_per_seq, n_batch)
    row_tile = pl.BlockSpec((tm, D_MODEL), lambda i: (i, 0))
    weight = pl.BlockSpec((D_MODEL, D_MODEL), lambda i: (0, 0))
    return pl.pallas_call(
        _mixer_kernel,
        grid=(rows // tm,),
        in_specs=[
            row_tile, row_tile, row_tile,
            pl.BlockSpec((tm, D_MODEL), lambda i: (i, GATE_OFF // D_MODEL)),
            pl.BlockSpec((tm, D_MODEL), lambda i: (i, GATE_OFF // D_MODEL + 1)),
            _mod_spec(2, row_to_mod),
            weight, weight, weight,
        ],
        out_specs=row_tile,
        out_shape=jax.ShapeDtypeStruct((rows, D_MODEL), F32),
        compiler_params=pltpu.CompilerParams(
            dimension_semantics=("arbitrary",), vmem_limit_bytes=VMEM_LIMIT),
        name="mixer_out",
    )(xs, yd, yg, p, p, mods, wpd, wpg, wo)


def _ffn_kernel(x_ref, g_ref, sc_ref, sh_ref, gt_ref, w1_ref, w3_ref, w2_ref, o_ref, h_scr, acc_scr):
    f = pl.program_id(1)

    @pl.when(f == 0)
    def _():
        h_scr[...] = _modulated_norm(x_ref[...], g_ref[...], sc_ref[...], sh_ref[...]).astype(BF16)
        acc_scr[...] = jnp.zeros_like(acc_scr)

    h = h_scr[...]
    a = jnp.dot(h, w1_ref[...], preferred_element_type=F32)
    b = jnp.dot(h, w3_ref[...], preferred_element_type=F32)
    acc_scr[...] += jnp.dot((_silu(a) * b).astype(BF16), w2_ref[...], preferred_element_type=F32)

    @pl.when(f == pl.num_programs(1) - 1)
    def _():
        o_ref[...] = x_ref[...] + gt_ref[...] * acc_scr[...]


def _ffn(xs, mods, norm_g, w1, w3, w2, *, tm, tf, tiles_per_seq, n_batch):
    rows = xs.shape[0]
    d_ff = w1.shape[1]
    row_to_mod = lambda i: jnp.minimum(i // tiles_per_seq, n_batch)
    row_tile = pl.BlockSpec((tm, D_MODEL), lambda i, f: (i, 0))
    return pl.pallas_call(
        _ffn_kernel,
        grid=(rows // tm, d_ff // tf),
        in_specs=[
            row_tile,
            pl.BlockSpec((1, D_MODEL), lambda i, f: (0, 0)),
            _mod_spec(4, row_to_mod), _mod_spec(3, row_to_mod), _mod_spec(5, row_to_mod),
            pl.BlockSpec((D_MODEL, tf), lambda i, f: (0, f)),
            pl.BlockSpec((D_MODEL, tf), lambda i, f: (0, f)),
            pl.BlockSpec((tf, D_MODEL), lambda i, f: (f, 0)),
        ],
        out_specs=row_tile,
        out_shape=jax.ShapeDtypeStruct((rows, D_MODEL), F32),
        scratch_shapes=[pltpu.VMEM((tm, D_MODEL), BF16), pltpu.VMEM((tm, D_MODEL), F32)],
        compiler_params=pltpu.CompilerParams(
            dimension_semantics=("arbitrary", "arbitrary"), vmem_limit_bytes=VMEM_LIMIT),
        name="ffn_swiglu",
    )(xs, norm_g, mods, mods, mods, w1, w3, w2)


def _top2_gates(logits):
    lane = lax.broadcasted_iota(jnp.int32, logits.shape, 1)
    n_lanes = logits.shape[1]
    lg = jnp.where(lane < N_EXPERTS, logits, NEG_BIG)
    m1 = jnp.max(lg, axis=-1, keepdims=True)
    i1 = jnp.min(jnp.where(lg == m1, lane, n_lanes), axis=-1, keepdims=True)
    lg2 = jnp.where(lane == i1, NEG_BIG, lg)
    m2 = jnp.max(lg2, axis=-1, keepdims=True)
    i2 = jnp.min(jnp.where(lg2 == m2, lane, n_lanes), axis=-1, keepdims=True)
    e2 = jnp.exp(m2 - m1)
    w_top = 1.0 / (1.0 + e2)
    idx = jnp.where(lane == 0, i1, jnp.where(lane == 1, i2, 0))
    wts = jnp.where(lane == 0, w_top, jnp.where(lane == 1, e2 * w_top, 0.0))
    return idx, wts


def _router_kernel(x_ref, g_ref, sc_ref, sh_ref, rhi_ref, rlo_ref, h_ref, idx_ref, wts_ref):
    h = _modulated_norm(x_ref[...], g_ref[...], sc_ref[...], sh_ref[...])
    h_ref[...] = h
    h_hi, h_lo = _split_bf16(h)
    logits = jnp.dot(h_hi, rhi_ref[...], preferred_element_type=F32)
    logits += jnp.dot(h_lo, rhi_ref[...], preferred_element_type=F32)
    logits += jnp.dot(h_hi, rlo_ref[...], preferred_element_type=F32)
    idx_ref[...], wts_ref[...] = _top2_gates(logits)


def _router(xs, mods, norm_g, r_hi, r_lo, *, rows, tm, tiles_per_seq, n_batch):
    row_to_mod = lambda i: jnp.minimum(i // tiles_per_seq, n_batch)
    row_tile = pl.BlockSpec((tm, D_MODEL), lambda i: (i, 0))
    lanes = pl.BlockSpec((tm, 128), lambda i: (i, 0))
    router = pl.BlockSpec((D_MODEL, 128), lambda i: (0, 0))
    return pl.pallas_call(
        _router_kernel,
        grid=(rows // tm,),
        in_specs=[row_tile, pl.BlockSpec((1, D_MODEL), lambda i: (0, 0)),
                  _mod_spec(4, row_to_mod), _mod_spec(3, row_to_mod), router, router],
        out_specs=[row_tile, lanes, lanes],
        out_shape=[jax.ShapeDtypeStruct((rows, D_MODEL), F32),
                   jax.ShapeDtypeStruct((rows, 128), jnp.int32),
                   jax.ShapeDtypeStruct((rows, 128), F32)],
        compiler_params=pltpu.CompilerParams(
            dimension_semantics=("arbitrary",), vmem_limit_bytes=VMEM_LIMIT),
        name="moe_router",
    )(xs, norm_g, mods, mods, r_hi, r_lo)


def _route_plan(idx, n_tok, tm_e):
    n_pairs = 2 * n_tok
    n_tiles = n_pairs // tm_e + N_EXPERTS + 1
    n_rows = n_tiles * tm_e
    e_flat = idx.reshape(n_pairs)
    order = jnp.argsort(e_flat, stable=True).astype(jnp.int32)
    counts = jnp.sum((e_flat[:, None] == jnp.arange(N_EXPERTS, dtype=jnp.int32)[None, :]).astype(jnp.int32),
                     axis=0)
    padded = ((counts + tm_e - 1) // tm_e) * tm_e
    ends = jnp.cumsum(padded)
    tile_start = jnp.arange(n_tiles, dtype=jnp.int32) * tm_e
    tile_expert = jnp.minimum(jnp.sum((tile_start[:, None] >= ends[None, :]).astype(jnp.int32), axis=1),
                              N_EXPERTS - 1).astype(jnp.int32)
    n_used = (ends[-1] // tm_e).astype(jnp.int32).reshape(1)
    row = jnp.arange(n_rows, dtype=jnp.int32)
    row_expert = jnp.repeat(tile_expert, tm_e)
    offset = row - (ends - padded)[row_expert]
    valid = (offset < counts[row_expert]) & (row < ends[-1])
    pair = order[jnp.clip((jnp.cumsum(counts) - counts)[row_expert] + offset, 0, n_pairs - 1)]
    tok, choice = pair // 2, pair % 2
    src = jnp.where(valid, tok, 0)
    dump = n_pairs + jnp.cumsum(jnp.logical_not(valid).astype(jnp.int32)) - 1
    dst = jnp.where(valid, choice * n_tok + tok, dump)
    return tile_expert, n_used, src.reshape(n_tiles, 1, tm_e), dst.reshape(n_tiles, 1, tm_e)


def _experts_kernel(te_ref, nu_ref, src_ref, src_next_ref, dst_ref, h_hbm, w1_ref, w3_ref, w2_ref, y_hbm,
                    xbuf, xb_scr, acc_scr, ybuf, gsem, ssem, *, tm_e, n_chunks):
    t = pl.program_id(0)
    f = pl.program_id(1)
    nf = pl.num_programs(1)
    n_used = nu_ref[0]
    slot = t % 2

    def gather_rows(idx_ref, s):
        def body(r, carry):
            pltpu.make_async_copy(h_hbm.at[pl.ds(idx_ref[0, r], 1)], xbuf.at[s, pl.ds(r, 1)], gsem.at[s]).start()
            return carry
        lax.fori_loop(0, tm_e, body, 0, unroll=8)

    def wait_gather(s):
        pltpu.make_async_copy(h_hbm.at[pl.ds(0, tm_e)], xbuf.at[s], gsem.at[s]).wait()

    def scatter_rows(s):
        def body(r, carry):
            pltpu.make_async_copy(ybuf.at[s, pl.ds(r, 1)], y_hbm.at[pl.ds(dst_ref[0, r], 1)], ssem.at[s]).start()
            return carry
        lax.fori_loop(0, tm_e, body, 0, unroll=8)

    def wait_scatter(s):
        pltpu.make_async_copy(ybuf.at[s], y_hbm.at[pl.ds(0, tm_e)], ssem.at[s]).wait()

    @pl.when((f == 0) & (t <= n_used))
    def _():
        @pl.when(t == 0)
        def _():
            gather_rows(src_ref, 0)
        for s in range(2):
            @pl.when(slot == s)
            def _():
                wait_gather(s)

                @pl.when(t < n_used)
                def _():
                    xb_scr[...] = xbuf[s].astype(BF16)
                    acc_scr[...] = jnp.zeros_like(acc_scr)

    @pl.when(t < n_used)
    def _():
        x = xb_scr[...]
        a = jnp.dot(x, w1_ref[...], preferred_element_type=F32)
        b = jnp.dot(x, w3_ref[...], preferred_element_type=F32)
        acc_scr[...] += jnp.dot((_silu(a) * b).astype(BF16), w2_ref[...], preferred_element_type=F32)
        share = tm_e // n_chunks
        for i in range(share):
            r = f * share + i
            pltpu.make_async_copy(h_hbm.at[pl.ds(src_next_ref[0, r], 1)],
                                  xbuf.at[1 - slot, pl.ds(r, 1)], gsem.at[1 - slot]).start()

    @pl.when(f == nf - 1)
    def _():
        for s in range(2):
            @pl.when(slot == s)
            def _():
                @pl.when(t >= 2)
                def _():
                    wait_scatter(s)

                @pl.when(t < n_used)
                def _():
                    ybuf[s] = acc_scr[...]

                @pl.when(t >= n_used)
                def _():
                    ybuf[s] = jnp.zeros(ybuf.shape[1:], F32)

                scatter_rows(s)

                @pl.when(t == pl.num_programs(0) - 1)
                def _():
                    wait_scatter(s)
                    wait_scatter(1 - s)


def _experts(h, plan, w1, w3, w2, *, n_tok, tm_e, tf):
    tile_expert, n_used, src, dst = plan
    n_tiles = src.shape[0]
    d_ff = w1.shape[2]
    nf = d_ff // tf
    n_out = n_tiles * tm_e

    def chunk(t, f, nu):
        return jnp.where(t < nu[0], f, nf - 1)

    smem_tile = lambda shift: pl.BlockSpec(
        (None, 1, tm_e), lambda t, f, te, nu: (jnp.minimum(t + shift, n_tiles - 1), 0, 0),
        memory_space=pltpu.SMEM)
    grid_spec = pltpu.PrefetchScalarGridSpec(
        num_scalar_prefetch=2,
        grid=(n_tiles, nf),
        in_specs=[
            smem_tile(0), smem_tile(1), smem_tile(0),
            pl.BlockSpec(memory_space=pl.ANY),
            pl.BlockSpec((None, D_MODEL, tf), lambda t, f, te, nu: (te[t], 0, chunk(t, f, nu))),
            pl.BlockSpec((None, D_MODEL, tf), lambda t, f, te, nu: (te[t], 0, chunk(t, f, nu))),
            pl.BlockSpec((None, tf, D_MODEL), lambda t, f, te, nu: (te[t], chunk(t, f, nu), 0)),
        ],
        out_specs=pl.BlockSpec(memory_space=pl.ANY),
        scratch_shapes=[
            pltpu.VMEM((2, tm_e, D_MODEL), F32),
            pltpu.VMEM((tm_e, D_MODEL), BF16),
            pltpu.VMEM((tm_e, D_MODEL), F32),
            pltpu.VMEM((2, tm_e, D_MODEL), F32),
            pltpu.SemaphoreType.DMA((2,)),
            pltpu.SemaphoreType.DMA((2,)),
        ],
    )
    return pl.pallas_call(
        functools.partial(_experts_kernel, tm_e=tm_e, n_chunks=nf),
        grid_spec=grid_spec,
        out_shape=jax.ShapeDtypeStruct((n_out, D_MODEL), F32),
        compiler_params=pltpu.CompilerParams(
            dimension_semantics=("arbitrary", "arbitrary"), vmem_limit_bytes=VMEM_LIMIT),
        name="moe_experts",
    )(tile_expert, n_used, src, src, dst, h, w1, w3, w2)


def _combine_kernel(x_ref, y0_ref, y1_ref, wts_ref, gt_ref, fg_ref, o_ref):
    wts = wts_ref[...]
    moe = wts[:, 0:1] * y0_ref[...] + wts[:, 1:2] * y1_ref[...]
    y = x_ref[...] + gt_ref[...] * moe
    ms = jnp.mean(y * y, axis=-1, keepdims=True)
    o_ref[...] = y * lax.rsqrt(ms + EPS) * fg_ref[...]


def _combine(xs, y, wts, mods, final_g, *, rows, tm, tiles_per_seq, n_batch):
    row_to_mod = lambda i: jnp.minimum(i // tiles_per_seq, n_batch)
    row_tile = pl.BlockSpec((tm, D_MODEL), lambda i: (i, 0))
    return pl.pallas_call(
        _combine_kernel,
        grid=(rows // tm,),
        in_specs=[
            row_tile, row_tile,
            pl.BlockSpec((tm, D_MODEL), lambda i: (i + rows // tm, 0)),
            pl.BlockSpec((tm, 128), lambda i: (i, 0)),
            _mod_spec(5, row_to_mod),
            pl.BlockSpec((1, D_MODEL), lambda i: (0, 0)),
        ],
        out_specs=row_tile,
        out_shape=jax.ShapeDtypeStruct((rows, D_MODEL), F32),
        compiler_params=pltpu.CompilerParams(
            dimension_semantics=("arbitrary",), vmem_limit_bytes=VMEM_LIMIT),
        name="moe_combine",
    )(xs, y, y, wts, mods, final_g)


def _moe(xs, mods, norm_g, r_hi, r_lo, final_g, w1, w3, w2, *, rows, tm, tm_e, tf, tiles_per_seq, n_batch):
    h, idx, wts = _router(xs, mods, norm_g, r_hi, r_lo, rows=rows, tm=tm, tiles_per_seq=tiles_per_seq,
                          n_batch=n_batch)
    plan = _route_plan(idx[:, :2], rows, tm_e)
    y = _experts(h, plan, w1, w3, w2, n_tok=rows, tm_e=tm_e, tf=tf)
    return _combine(xs, y, wts, mods, final_g, rows=rows, tm=tm, tiles_per_seq=tiles_per_seq, n_batch=n_batch)


def _deinterleave(n=HEAD_DIM):
    return np.concatenate([np.arange(0, n, 2), np.arange(1, n, 2)])


def _proj_columns():
    de = _deinterleave()
    o_dq, o_gq, o_dk, o_dv, o_gk, o_gv, o_gate = 0, 1024, 2048, 3072, 4096, 4352, 4608
    cols = np.zeros(IN_W, np.int32)
    scale = np.ones(IN_W, np.float32)
    cols[GATE_OFF:GATE_OFF + 2048] = o_gate + np.arange(2048)
    for h in range(N_DIFF_HEADS):
        for c in range(2):
            dst = 128 * h + 64 * c
            cols[DQ_OFF + dst:DQ_OFF + dst + 64] = o_dq + dst + de
            cols[DK_OFF + dst:DK_OFF + dst + 64] = o_dk + dst + de
    scale[DQ_OFF:DQ_OFF + 1024] = HEAD_DIM ** -0.5
    for j in range(N_GQA_HEADS):
        cols[GQ_OFF + 64 * j:GQ_OFF + 64 * j + 64] = o_gq + 64 * j + de
    cols[DV_OFF:DV_OFF + 1024] = o_dv + np.arange(1024)
    for g in range(N_GQA_KV):
        cols[GKV_OFF + 128 * g:GKV_OFF + 128 * g + 64] = o_gk + 64 * g + de
        cols[GKV_OFF + 128 * g + 64:GKV_OFF + 128 * g + 128] = o_gv + 64 * g + np.arange(64)
    return cols, scale


def _rope_tables(S, pad_rows):
    rows = S // GRID_W
    row = jnp.repeat(jnp.arange(rows, dtype=F32), GRID_W)
    col = jnp.tile(jnp.arange(GRID_W, dtype=F32), rows)
    half = HEAD_DIM // 2
    inv_freq = ROPE_THETA ** (-jnp.arange(0, half, 2, dtype=F32) / half)
    ang = jnp.concatenate([row[:, None] * inv_freq, col[:, None] * inv_freq], axis=-1)
    cos, sin = jnp.cos(ang), jnp.sin(ang)
    reps = PROJ_TN // HEAD_DIM
    cos_t = jnp.tile(jnp.concatenate([cos, cos], axis=-1), (1, reps))
    sin_t = jnp.tile(jnp.concatenate([-sin, sin], axis=-1), (1, reps))
    cos_t = jnp.concatenate([cos_t, jnp.ones((pad_rows, PROJ_TN), F32)], axis=0)
    sin_t = jnp.concatenate([sin_t, jnp.zeros((pad_rows, PROJ_TN), F32)], axis=0)
    return cos_t, sin_t


def kernel(x, c, ctx, c_ctx, ada_w, ada_b, norm_attn_g, norm_ffn_g, w_in, q_norm_g, k_norm_g, diff_lambda,
           diff_subln_g, w_proj_diff, w_proj_gqa, w_out, ffn_w1, ffn_w3, ffn_w2, moe_router, moe_w1, moe_w3,
           moe_w2, final_norm_g):
    B, S, D = x.shape
    C = ctx.shape[1]
    depth = ada_w.shape[0]
    assert D == D_MODEL and depth == 2 and B + 1 <= MOD_ROWS
    assert w_in.shape[2] == IN_W and moe_router.shape[2] == N_EXPERTS
    tiles = _pick_tiles(B, S, C, ffn_w1.shape[2], moe_w1.shape[3])
    n_lat = B * S

    cvec = jnp.zeros((MOD_ROWS, D), F32).at[:B].set(c).at[B].set(c_ctx)
    mods_all = _ada_mods(cvec, ada_w, ada_b).reshape(depth, MOD_ROWS * N_MODS, 1, D)

    cols, col_scale = _proj_columns()
    de = _deinterleave()
    cos_t, sin_t = _rope_tables(S, tiles.tm_proj)
    blk = np.arange(PROJ_TN) // HEAD_DIM
    ones_bd = jnp.asarray(blk[:, None] == blk[None, :], BF16)
    is_key_lane = (np.arange(PROJ_TN) % (2 * HEAD_DIM)) < HEAD_DIM

    xs = jnp.concatenate([x.reshape(n_lat, D), ctx.reshape(B * C, D)], axis=0)
    for l in range(depth):
        last = l == depth - 1
        mods = mods_all[l]
        lam_init = 0.8 - 0.6 * math.exp(-0.3 * l)
        lq1, lk1, lq2, lk2 = diff_lambda[l]
        lam = (jnp.exp(jnp.sum(lq1 * lk1)) - jnp.exp(jnp.sum(lq2 * lk2)) + lam_init).reshape(1).astype(F32)
        w = (jnp.take(w_in[l], cols, axis=1) * col_scale).astype(BF16)
        qg = jnp.tile(q_norm_g[l][de] * HEAD_DIM ** -0.5, PROJ_TN // HEAD_DIM).reshape(1, PROJ_TN)
        kg = jnp.where(is_key_lane, jnp.tile(k_norm_g[l][de], PROJ_TN // HEAD_DIM), 1.0).reshape(1, PROJ_TN)
        subln = (diff_subln_g[l] * (1.0 - lam_init)).reshape(1, 2 * HEAD_DIM)

        p = _project(xs, mods, norm_attn_g[l].reshape(1, D), w, cos_t, sin_t, ones_bd, qg, kg,
                     tm=tiles.tm_proj, lat_tiles=n_lat // tiles.tm_proj, tiles_per_seq=S // tiles.tm_proj,
                     n_batch=B)
        yd, yg = _attention(p, lam, subln, B=B, S=S, C=C, tiles=tiles, ctx_queries=not last)
        rows = n_lat if last else n_lat + B * C
        xs = _mixer_out(xs, yd, yg, p, mods, w_proj_diff[l].astype(BF16), w_proj_gqa[l].astype(BF16),
                        w_out[l].astype(BF16), rows=rows, tm=tiles.tm_mix, tiles_per_seq=S // tiles.tm_mix,
                        n_batch=B)
        i = l // 2
        if l % 2 == 0:
            xs = _ffn(xs, mods, norm_ffn_g[l].reshape(1, D), ffn_w1[i].astype(BF16), ffn_w3[i].astype(BF16),
                      ffn_w2[i].astype(BF16), tm=tiles.tm_ffn, tf=tiles.tf_ffn,
                      tiles_per_seq=S // tiles.tm_ffn, n_batch=B)
        else:
            r_pad = jnp.zeros((D, 128), F32).at[:, :N_EXPERTS].set(moe_router[i])
            r_hi = r_pad.astype(BF16)
            r_lo = (r_pad - r_hi.astype(F32)).astype(BF16)
            xs = _moe(xs, mods, norm_ffn_g[l].reshape(1, D), r_hi, r_lo, final_norm_g.reshape(1, D),
                      moe_w1[i].astype(BF16), moe_w3[i].astype(BF16), moe_w2[i].astype(BF16),
                      rows=rows, tm=tiles.tm_mix, tm_e=tiles.tm_moe, tf=tiles.tf_moe,
                      tiles_per_seq=S // tiles.tm_mix, n_batch=B)
    return xs.reshape(B, S, D)
```

```python
import functools
import math
from typing import NamedTuple

import numpy as np
import jax
import jax.numpy as jnp
from jax import lax
from jax.experimental import pallas as pl
from jax.experimental.pallas import tpu as pltpu

F32 = jnp.float32
BF16 = jnp.bfloat16

D_MODEL = 1024
HEAD_DIM = 64
N_DIFF_HEADS = 8
N_GQA_HEADS = 16
N_GQA_KV = 4
GQA_REP = N_GQA_HEADS // N_GQA_KV
N_EXPERTS = 8
GRID_W = 64
ROPE_THETA = 10000.0
EPS = 1e-6
N_MODS = 6
MOD_ROWS = 16
NEG_BIG = -1e30

GATE_OFF = 0
DQ_OFF = 2048
GQ_OFF = 3072
DK_OFF = 4096
DV_OFF = 5120
GKV_OFF = 6144
IN_W = 6656
PROJ_TN = 512
ROPE_TILES = (4, 5, 8, 9)
QNORM_TILES = (6, 7)
KV_TILE = 12

VMEM_LIMIT = 52 * 1024 * 1024


class Tiles(NamedTuple):
    tm_proj: int
    tq: int
    tk: int
    tm_mix: int
    tm_ffn: int
    tf_ffn: int
    tf_moe: int
    tm_moe: int


def _largest_divisor(n, candidates):
    for c in candidates:
        if n % c == 0:
            return c
    raise ValueError(f"no tile in {candidates} divides {n}")


def _pick_tiles(B, S, C, d_ff, d_ff_e):
    rows_common = math.gcd(S, B * C)
    tm = _largest_divisor(rows_common, (512, 256, 128))
    return Tiles(
        tm_proj=_largest_divisor(rows_common, (1024, 512, 256, 128)),
        tq=C,
        tk=_largest_divisor(math.gcd(S, C), (256, 128)),
        tm_mix=tm,
        tm_ffn=_largest_divisor(rows_common, (1024, 512, 256, 128)),
        tf_ffn=_largest_divisor(d_ff, (1408, 1024, 512, 256, 128)),
        tf_moe=_largest_divisor(d_ff_e, (896, 512, 256, 128)),
        tm_moe=_largest_divisor(2 * B * S, (512, 256)),
    )


def _split_bf16(v):
    hi = v.astype(BF16)
    lo = (v - hi.astype(F32)).astype(BF16)
    return hi, lo


def _ada_kernel(c_ref, w_ref, b_ref, o_ref):
    c = c_ref[...]
    s = c / (1.0 + jnp.exp(-c))
    s_hi, s_lo = _split_bf16(s)
    w_hi, w_lo = _split_bf16(w_ref[...])
    acc = jnp.dot(s_hi, w_hi, preferred_element_type=F32)
    acc += jnp.dot(s_lo, w_hi, preferred_element_type=F32)
    acc += jnp.dot(s_hi, w_lo, preferred_element_type=F32)
    o_ref[...] = acc + b_ref[...]


def _ada_mods(cvec, ada_w, ada_b):
    depth, d, n = ada_w.shape
    tn = _largest_divisor(n, (1536, 1024, 512))
    return pl.pallas_call(
        _ada_kernel,
        grid=(depth, n // tn),
        in_specs=[
            pl.BlockSpec((MOD_ROWS, d), lambda l, j: (0, 0)),
            pl.BlockSpec((None, d, tn), lambda l, j: (l, 0, j)),
            pl.BlockSpec((None, 1, tn), lambda l, j: (l, 0, j)),
        ],
        out_specs=pl.BlockSpec((None, MOD_ROWS, tn), lambda l, j: (l, 0, j)),
        out_shape=jax.ShapeDtypeStruct((depth, MOD_ROWS, n), F32),
        compiler_params=pltpu.CompilerParams(
            dimension_semantics=("arbitrary", "arbitrary"), vmem_limit_bytes=VMEM_LIMIT),
        name="ada_mods",
    )(cvec, ada_w, ada_b.reshape(depth, 1, n))


def _modulated_norm(x, g, scale, shift):
    ms = jnp.mean(x * x, axis=-1, keepdims=True)
    return (x * lax.rsqrt(ms + EPS) * g) * (1.0 + scale) + shift


def _silu(a):
    return a / (1.0 + jnp.exp(-a))


def _mod_spec(k, row_to_mod):
    return pl.BlockSpec((None, 1, D_MODEL), lambda i, *_: (row_to_mod(i) * N_MODS + k, 0, 0))


def _rope(z, cos, sin_signed, swap_ref):
    partner = jnp.dot(z.astype(BF16), swap_ref[...], preferred_element_type=F32)
    return z * cos + partner * sin_signed


def _head_rms(z, ones_ref, gain):
    ss = jnp.dot((z * z).astype(BF16), ones_ref[...], preferred_element_type=F32)
    return z * lax.rsqrt(ss * (1.0 / HEAD_DIM) + EPS) * gain


def _proj_kernel(x_ref, g_ref, sc_ref, sh_ref, w_ref, cos_ref, sin_ref, ones_ref, swap_ref, qg_ref, kg_ref,
                 o_ref, h_scr):
    j = pl.program_id(1)

    @pl.when(j == 0)
    def _():
        h_scr[...] = _modulated_norm(x_ref[...], g_ref[...], sc_ref[...], sh_ref[...]).astype(BF16)

    def project():
        z = jnp.dot(h_scr[...], w_ref[...], preferred_element_type=F32)
        return z, lax.broadcasted_iota(jnp.int32, z.shape, 1)

    is_rope = functools.reduce(jnp.logical_or, [j == t for t in ROPE_TILES])
    is_qnorm = functools.reduce(jnp.logical_or, [j == t for t in QNORM_TILES])
    is_kv = j == KV_TILE
    is_plain = jnp.logical_not(is_rope | is_qnorm | is_kv)

    @pl.when(is_plain)
    def _():
        z, _ = project()
        o_ref[...] = z.astype(BF16)

    @pl.when(is_rope)
    def _():
        z, lane = project()
        o_ref[...] = _rope(z, cos_ref[...], sin_ref[...], swap_ref).astype(BF16)

    @pl.when(is_qnorm)
    def _():
        z, lane = project()
        zn = _head_rms(z, ones_ref, qg_ref[...])
        o_ref[...] = _rope(zn, cos_ref[...], sin_ref[...], swap_ref).astype(BF16)

    @pl.when(is_kv)
    def _():
        z, lane = project()
        zn = _head_rms(z, ones_ref, kg_ref[...])
        zr = _rope(zn, cos_ref[...], sin_ref[...], swap_ref)
        is_key_lane = (lane & HEAD_DIM) == 0
        o_ref[...] = jnp.where(is_key_lane, zr, z).astype(BF16)


def _project(xs, mods, norm_g, w, cos_t, sin_t, ones_bd, swap_bd, qg, kg, *, tm, lat_tiles, tiles_per_seq, n_batch):
    rows = xs.shape[0]
    row_to_mod = lambda i: jnp.minimum(i // tiles_per_seq, n_batch)
    rope_row = lambda i: jnp.where(i < lat_tiles, i % tiles_per_seq, tiles_per_seq)
    const = lambda i, j: (0, 0)
    return pl.pallas_call(
        _proj_kernel,
        grid=(rows // tm, IN_W // PROJ_TN),
        in_specs=[
            pl.BlockSpec((tm, D_MODEL), lambda i, j: (i, 0)),
            pl.BlockSpec((1, D_MODEL), const),
            _mod_spec(1, row_to_mod),
            _mod_spec(0, row_to_mod),
            pl.BlockSpec((D_MODEL, PROJ_TN), lambda i, j: (0, j)),
            pl.BlockSpec((tm, PROJ_TN), lambda i, j: (rope_row(i), 0)),
            pl.BlockSpec((tm, PROJ_TN), lambda i, j: (rope_row(i), 0)),
            pl.BlockSpec((PROJ_TN, PROJ_TN), const),
            pl.BlockSpec((PROJ_TN, PROJ_TN), const),
            pl.BlockSpec((1, PROJ_TN), const),
            pl.BlockSpec((1, PROJ_TN), const),
        ],
        out_specs=pl.BlockSpec((tm, PROJ_TN), lambda i, j: (i, j)),
        out_shape=jax.ShapeDtypeStruct((rows, IN_W), BF16),
        scratch_shapes=[pltpu.VMEM((tm, D_MODEL), BF16)],
        compiler_params=pltpu.CompilerParams(
            dimension_semantics=("arbitrary", "arbitrary"), vmem_limit_bytes=VMEM_LIMIT),
        name="in_proj",
    )(xs, norm_g, mods, mods, w, cos_t, sin_t, ones_bd, swap_bd, qg, kg)


STAB_LANE = HEAD_DIM
L_FLOOR = 1e-26
DIFF_VT_ROWS = 2 * HEAD_DIM + 16
DIFF_HEADS_PER_STEP = 2
GQA_GROUPS_PER_STEP = 1


def _aug_keys(k):
    lane = lax.broadcasted_iota(jnp.int32, k.shape, 1)
    in_key = lane < HEAD_DIM
    sq = jnp.where(in_key, k * k, 0.0).astype(BF16)
    norms = jnp.dot(sq, jnp.ones((k.shape[1], k.shape[1]), BF16), preferred_element_type=F32)
    ka = jnp.where(in_key, k, jnp.where(lane == STAB_LANE, 1.0, 0.0)).astype(BF16)
    return ka, jnp.max(norms, axis=0, keepdims=True)


def _query_mats(slab, kmax2):
    lane = lax.broadcasted_iota(jnp.int32, slab.shape, 1)
    in_key = lane < HEAD_DIM
    sq = jnp.where(in_key, slab * slab, 0.0).astype(BF16)
    norms = jnp.dot(sq, jnp.ones((slab.shape[1], slab.shape[1]), BF16), preferred_element_type=F32)
    nk = norms * kmax2
    bound = nk * lax.rsqrt(nk + 1e-30)
    plain = jnp.where(in_key, slab, 0.0)
    return jnp.where(lane == STAB_LANE, -bound, plain).astype(BF16), plain.astype(BF16)


def _attend(ka_scr, vt_scr, mats, acc_scr, *, first, n_chunks, tk):
    def scores(r, c, qmat):
        off = pl.multiple_of(c * tk, tk)
        return lax.dot_general(ka_scr[r * ka_scr.shape[0] // len(mats), pl.ds(off, tk), :], qmat, (((1,), (1,)), ((), ())),
                               preferred_element_type=F32)

    acc_scr[...] = jnp.zeros(acc_scr.shape, F32)

    def all_scores(c):
        return tuple(scores(r, c, shifted) for r, (shifted, _) in enumerate(mats))

    def accumulate(c, s_all):
        for r, s in enumerate(s_all):
            vt = vt_scr[r * vt_scr.shape[0] // len(mats), c]
            acc_scr[r] += jnp.dot(vt, jnp.exp(s).astype(BF16), preferred_element_type=F32)

    def fast_body(c, s_cur):
        s_next = all_scores(c + 1)
        accumulate(c, s_cur)
        return s_next

    trips = n_chunks - 1 - first
    s_last = lax.fori_loop(first, n_chunks - 1, fast_body, all_scores(first),
                           unroll=_largest_divisor(trips, (16, 8, 4, 2, 1)) if trips > 0 else 1)
    accumulate(n_chunks - 1, s_last)


def _attend_fallback(ka_scr, vt_scr, mats, m_scr, acc_scr, *, first, n_chunks, tk):
    m_scr[...] = jnp.full(m_scr.shape, NEG_BIG, F32)
    acc_scr[...] = jnp.zeros(acc_scr.shape, F32)

    def body(c, carry):
        off = pl.multiple_of(c * tk, tk)
        for r, (_, plain) in enumerate(mats):
            s = lax.dot_general(ka_scr[r * ka_scr.shape[0] // len(mats), pl.ds(off, tk), :], plain, (((1,), (1,)), ((), ())),
                                preferred_element_type=F32)
            m_old = m_scr[r]
            m_new = jnp.maximum(m_old, jnp.max(s, axis=0, keepdims=True))
            p = jnp.exp(s - m_new).astype(BF16)
            vt = vt_scr[r * vt_scr.shape[0] // len(mats), c]
            acc_scr[r] = acc_scr[r] * jnp.exp(m_old - m_new) + jnp.dot(vt, p, preferred_element_type=F32)
            m_scr[r] = m_new
        return carry

    lax.fori_loop(first, n_chunks, body, 0)


def _softmax_tile(ka_scr, vt_scr, kmax_scr, slabs, m_scr, acc_scr, l_row, *, n_lat_chunks, n_chunks, tk, ctx_tile):
    mats = [_query_mats(slab, kmax_scr[r * kmax_scr.shape[0] // len(slabs)]) for r, slab in enumerate(slabs)]

    def run(first):
        _attend(ka_scr, vt_scr, mats, acc_scr, first=first, n_chunks=n_chunks, tk=tk)
        l_min = functools.reduce(
            jnp.minimum, [jnp.min(acc_scr[r, l_row:l_row + 1, :]) for r in range(len(slabs))])

        @pl.when(jnp.logical_not(l_min >= L_FLOOR))
        def _():
            _attend_fallback(ka_scr, vt_scr, mats, m_scr, acc_scr, first=first, n_chunks=n_chunks, tk=tk)

    if ctx_tile is None:
        run(0)
    else:
        pl.when(jnp.logical_not(ctx_tile))(functools.partial(run, 0))
        pl.when(ctx_tile)(functools.partial(run, n_lat_chunks))


def _chunk_rows(c, n_lat_chunks, tk, lat_ref, ctx_ref):
    if c < n_lat_chunks:
        return lat_ref[c * tk:(c + 1) * tk, :]
    return ctx_ref[(c - n_lat_chunks) * tk:(c - n_lat_chunks + 1) * tk, :]


def _diff_attn_kernel(lam_ref, q_ref, kl_ref, vl_ref, kc_ref, vc_ref, g_ref, o_ref,
                      ka_scr, vt_scr, kmax_scr, m_scr, acc_scr, *, nq_lat, n_lat_chunks, n_chunks, tk, ctx_queries):
    dv = 2 * HEAD_DIM
    heads = [slice(a * dv, (a + 1) * dv) for a in range(DIFF_HEADS_PER_STEP)]

    @pl.when(pl.program_id(2) == 0)
    def _():
        kmax = [None] * (2 * len(heads))
        for c in range(n_chunks):
            k_all = _chunk_rows(c, n_lat_chunks, tk, kl_ref, kc_ref).astype(F32)
            v_all = _chunk_rows(c, n_lat_chunks, tk, vl_ref, vc_ref)
            for a, cols in enumerate(heads):
                k = k_all[:, cols]
                for r, keys in ((2 * a, k), (2 * a + 1, pltpu.roll(k, HEAD_DIM, 1))):
                    ka, n2 = _aug_keys(keys)
                    ka_scr[r, c * tk:(c + 1) * tk, :] = ka
                    kmax[r] = n2 if kmax[r] is None else jnp.maximum(kmax[r], n2)
                vt_scr[a, c, :dv, :] = v_all[:, cols].T
                vt_scr[a, c, dv:, :] = jnp.ones((DIFF_VT_ROWS - dv, tk), BF16)
        for r, n2 in enumerate(kmax):
            kmax_scr[r] = n2

    q_all = q_ref[...].astype(F32)
    slabs = []
    for cols in heads:
        slabs += [q_all[:, cols], pltpu.roll(q_all[:, cols], HEAD_DIM, 1)]
    ctx_tile = (pl.program_id(2) == nq_lat) if ctx_queries else None
    _softmax_tile(ka_scr, vt_scr, kmax_scr, slabs, m_scr, acc_scr, dv,
                  n_lat_chunks=n_lat_chunks, n_chunks=n_chunks, tk=tk, ctx_tile=ctx_tile)
    for a, cols in enumerate(heads):
        a1, a2 = acc_scr[2 * a], acc_scr[2 * a + 1]
        yt = a1[:dv] * (1.0 / a1[dv:dv + 1]) - lam_ref[0] * (a2[:dv] * (1.0 / a2[dv:dv + 1]))
        y = yt.T
        ms = jnp.mean(y * y, axis=-1, keepdims=True)
        o_ref[:, cols] = (y * lax.rsqrt(ms + EPS) * g_ref[...]).astype(BF16)


def _gqa_attn_kernel(q_ref, kvl_ref, kvc_ref, o_ref, ka_scr, vt_scr, kmax_scr, m_scr, acc_scr,
                     *, nq_lat, n_lat_chunks, n_chunks, tk, ctx_queries):
    w2 = 2 * HEAD_DIM
    groups = [slice(g * w2, (g + 1) * w2) for g in range(GQA_GROUPS_PER_STEP)]
    n_heads = GQA_REP * len(groups)

    @pl.when(pl.program_id(2) == 0)
    def _():
        kmax = [None] * len(groups)
        for c in range(n_chunks):
            kv_all = _chunk_rows(c, n_lat_chunks, tk, kvl_ref, kvc_ref)
            for g, cols in enumerate(groups):
                kv = kv_all[:, cols]
                ka, n2 = _aug_keys(kv.astype(F32))
                ka_scr[g, c * tk:(c + 1) * tk, :] = ka
                kmax[g] = n2 if kmax[g] is None else jnp.maximum(kmax[g], n2)
                t = kv.T
                row = lax.broadcasted_iota(jnp.int32, t.shape, 0)
                vt_scr[g, c] = jnp.where(row < HEAD_DIM, jnp.ones_like(t), t)
        for g, n2 in enumerate(kmax):
            kmax_scr[g] = n2

    qf = q_ref[...].astype(F32)
    tq = qf.shape[0]
    lane = lax.broadcasted_iota(jnp.int32, (tq, w2), 1)
    slabs = []
    for r in range(n_heads):
        slab = qf[:, (r // 2) * w2:(r // 2 + 1) * w2]
        slabs.append(pltpu.roll(slab, HEAD_DIM, 1) if r % 2 else slab)
    ctx_tile = (pl.program_id(2) == nq_lat) if ctx_queries else None
    _softmax_tile(ka_scr, vt_scr, kmax_scr, slabs, m_scr, acc_scr, 0,
                  n_lat_chunks=n_lat_chunks, n_chunks=n_chunks, tk=tk, ctx_tile=ctx_tile)

    def head_out(r):
        acc = acc_scr[r]
        return (acc * (1.0 / acc[0:1])).T

    for j in range(n_heads // 2):
        pair = jnp.where(lane < HEAD_DIM, pltpu.roll(head_out(2 * j), HEAD_DIM, 1), head_out(2 * j + 1))
        o_ref[:, j * w2:(j + 1) * w2] = pair.astype(BF16)


def _attention(p, lam, subln_gain, *, B, S, C, tiles, ctx_queries):
    tq, tk = tiles.tq, tiles.tk
    nq_lat = S // tq
    nq = nq_lat + (1 if ctx_queries else 0)
    ctx_blk0 = (B * S) // C
    rows_out = B * S + (B * C if ctx_queries else 0)
    T = S + C
    statics = dict(nq_lat=nq_lat, n_lat_chunks=S // tk, n_chunks=T // tk, tk=tk, ctx_queries=ctx_queries)

    def q_row(b, qi):
        if not ctx_queries:
            return b * nq_lat + qi
        return jnp.where(qi < nq_lat, b * nq_lat + qi, ctx_blk0 + b)

    cparams = pltpu.CompilerParams(
        dimension_semantics=("arbitrary", "arbitrary", "arbitrary"), vmem_limit_bytes=VMEM_LIMIT)
    w2 = 2 * HEAD_DIM
    hps = DIFF_HEADS_PER_STEP
    wd = hps * w2
    yd = pl.pallas_call(
        functools.partial(_diff_attn_kernel, **statics),
        grid=(B, N_DIFF_HEADS // hps, nq),
        in_specs=[
            pl.BlockSpec(memory_space=pltpu.SMEM),
            pl.BlockSpec((tq, wd), lambda b, h, qi: (q_row(b, qi), DQ_OFF // wd + h)),
            pl.BlockSpec((S, wd), lambda b, h, qi: (b, DK_OFF // wd + h)),
            pl.BlockSpec((S, wd), lambda b, h, qi: (b, DV_OFF // wd + h)),
            pl.BlockSpec((C, wd), lambda b, h, qi: (ctx_blk0 + b, DK_OFF // wd + h)),
            pl.BlockSpec((C, wd), lambda b, h, qi: (ctx_blk0 + b, DV_OFF // wd + h)),
            pl.BlockSpec((1, w2), lambda b, h, qi: (0, 0)),
        ],
        out_specs=pl.BlockSpec((tq, wd), lambda b, h, qi: (q_row(b, qi), h)),
        out_shape=jax.ShapeDtypeStruct((rows_out, N_DIFF_HEADS * w2), BF16),
        scratch_shapes=[
            pltpu.VMEM((2 * hps, T, w2), BF16),
            pltpu.VMEM((hps, T // tk, DIFF_VT_ROWS, tk), BF16),
            pltpu.VMEM((2 * hps, 1, w2), F32),
            pltpu.VMEM((2 * hps, 1, tq), F32),
            pltpu.VMEM((2 * hps, DIFF_VT_ROWS, tq), F32),
        ],
        compiler_params=cparams,
        name="diff_attn",
    )(lam, p, p, p, p, p, subln_gain)

    gps = GQA_GROUPS_PER_STEP
    wq = gps * GQA_REP * HEAD_DIM
    wkv = gps * w2
    yg = pl.pallas_call(
        functools.partial(_gqa_attn_kernel, **statics),
        grid=(B, N_GQA_KV // gps, nq),
        in_specs=[
            pl.BlockSpec((tq, wq), lambda b, g, qi: (q_row(b, qi), GQ_OFF // wq + g)),
            pl.BlockSpec((S, wkv), lambda b, g, qi: (b, GKV_OFF // wkv + g)),
            pl.BlockSpec((C, wkv), lambda b, g, qi: (ctx_blk0 + b, GKV_OFF // wkv + g)),
        ],
        out_specs=pl.BlockSpec((tq, wq), lambda b, g, qi: (q_row(b, qi), g)),
        out_shape=jax.ShapeDtypeStruct((rows_out, N_GQA_HEADS * HEAD_DIM), BF16),
        scratch_shapes=[
            pltpu.VMEM((gps, T, w2), BF16),
            pltpu.VMEM((gps, T // tk, w2, tk), BF16),
            pltpu.VMEM((gps, 1, w2), F32),
            pltpu.VMEM((gps * GQA_REP, 1, tq), F32),
            pltpu.VMEM((gps * GQA_REP, w2, tq), F32),
        ],
        compiler_params=cparams,
        name="gqa_attn",
    )(p, p, p)
    return yd, yg


def _mixer_kernel(x_ref, yd_ref, yg_ref, gd_ref, gg_ref, gt_ref, wpd_ref, wpg_ref, wo_ref, o_ref):
    pd = jnp.dot(yd_ref[...], wpd_ref[...], preferred_element_type=F32)
    pg = jnp.dot(yg_ref[...], wpg_ref[...], preferred_element_type=F32)
    m = jax.nn.sigmoid(gd_ref[...].astype(F32)) * pd + jax.nn.sigmoid(gg_ref[...].astype(F32)) * pg
    o_ref[...] = x_ref[...] + gt_ref[...] * jnp.dot(m.astype(BF16), wo_ref[...], preferred_element_type=F32)


def _mixer_out(xs, yd, yg, p, mods, wpd, wpg, wo, *, rows, tm, tiles_per_seq, n_batch):
    row_to_mod = lambda i: jnp.minimum(i // tiles_per_seq, n_batch)
    row_tile = pl.BlockSpec((tm, D_MODEL), lambda i: (i, 0))
    weight = pl.BlockSpec((D_MODEL, D_MODEL), lambda i: (0, 0))
    return pl.pallas_call(
        _mixer_kernel,
        grid=(rows // tm,),
        in_specs=[
            row_tile, row_tile, row_tile,
            pl.BlockSpec((tm, D_MODEL), lambda i: (i, GATE_OFF // D_MODEL)),
            pl.BlockSpec((tm, D_MODEL), lambda i: (i, GATE_OFF // D_MODEL + 1)),
            _mod_spec(2, row_to_mod),
            weight, weight, weight,
        ],
        out_specs=row_tile,
        out_shape=jax.ShapeDtypeStruct((rows, D_MODEL), F32),
        compiler_params=pltpu.CompilerParams(
            dimension_semantics=("arbitrary",), vmem_limit_bytes=VMEM_LIMIT),
        name="mixer_out",
    )(xs, yd, yg, p, p, mods, wpd, wpg, wo)


def _ffn_kernel(x_ref, g_ref, sc_ref, sh_ref, gt_ref, w1_ref, w3_ref, w2_ref, o_ref, h_scr, acc_scr):
    f = pl.program_id(1)

    @pl.when(f == 0)
    def _():
        h_scr[...] = _modulated_norm(x_ref[...], g_ref[...], sc_ref[...], sh_ref[...]).astype(BF16)
        acc_scr[...] = jnp.zeros_like(acc_scr)

    h = h_scr[...]
    a = jnp.dot(h, w1_ref[...], preferred_element_type=F32)
    b = jnp.dot(h, w3_ref[...], preferred_element_type=F32)
    acc_scr[...] += jnp.dot((_silu(a) * b).astype(BF16), w2_ref[...], preferred_element_type=F32)

    @pl.when(f == pl.num_programs(1) - 1)
    def _():
        o_ref[...] = x_ref[...] + gt_ref[...] * acc_scr[...]


def _ffn(xs, mods, norm_g, w1, w3, w2, *, tm, tf, tiles_per_seq, n_batch):
    rows = xs.shape[0]
    d_ff = w1.shape[1]
    row_to_mod = lambda i: jnp.minimum(i // tiles_per_seq, n_batch)
    row_tile = pl.BlockSpec((tm, D_MODEL), lambda i, f: (i, 0))
    return pl.pallas_call(
        _ffn_kernel,
        grid=(rows // tm, d_ff // tf),
        in_specs=[
            row_tile,
            pl.BlockSpec((1, D_MODEL), lambda i, f: (0, 0)),
            _mod_spec(4, row_to_mod), _mod_spec(3, row_to_mod), _mod_spec(5, row_to_mod),
            pl.BlockSpec((D_MODEL, tf), lambda i, f: (0, f)),
            pl.BlockSpec((D_MODEL, tf), lambda i, f: (0, f)),
            pl.BlockSpec((tf, D_MODEL), lambda i, f: (f, 0)),
        ],
        out_specs=row_tile,
        out_shape=jax.ShapeDtypeStruct((rows, D_MODEL), F32),
        scratch_shapes=[pltpu.VMEM((tm, D_MODEL), BF16), pltpu.VMEM((tm, D_MODEL), F32)],
        compiler_params=pltpu.CompilerParams(
            dimension_semantics=("arbitrary", "arbitrary"), vmem_limit_bytes=VMEM_LIMIT),
        name="ffn_swiglu",
    )(xs, norm_g, mods, mods, mods, w1, w3, w2)


def _top2_gates(logits):
    lane = lax.broadcasted_iota(jnp.int32, logits.shape, 1)
    n_lanes = logits.shape[1]
    lg = jnp.where(lane < N_EXPERTS, logits, NEG_BIG)
    m1 = jnp.max(lg, axis=-1, keepdims=True)
    i1 = jnp.min(jnp.where(lg == m1, lane, n_lanes), axis=-1, keepdims=True)
    lg2 = jnp.where(lane == i1, NEG_BIG, lg)
    m2 = jnp.max(lg2, axis=-1, keepdims=True)
    i2 = jnp.min(jnp.where(lg2 == m2, lane, n_lanes), axis=-1, keepdims=True)
    e2 = jnp.exp(m2 - m1)
    w_top = 1.0 / (1.0 + e2)
    idx = jnp.where(lane == 0, i1, jnp.where(lane == 1, i2, 0))
    wts = jnp.where(lane == 0, w_top, jnp.where(lane == 1, e2 * w_top, 0.0))
    return idx, wts


def _router_kernel(x_ref, g_ref, sc_ref, sh_ref, rhi_ref, rlo_ref, h_ref, idx_ref, wts_ref):
    h = _modulated_norm(x_ref[...], g_ref[...], sc_ref[...], sh_ref[...])
    h_ref[...] = h
    h_hi, h_lo = _split_bf16(h)
    logits = jnp.dot(h_hi, rhi_ref[...], preferred_element_type=F32)
    logits += jnp.dot(h_lo, rhi_ref[...], preferred_element_type=F32)
    logits += jnp.dot(h_hi, rlo_ref[...], preferred_element_type=F32)
    idx_ref[...], wts_ref[...] = _top2_gates(logits)


def _router(xs, mods, norm_g, r_hi, r_lo, *, rows, tm, tiles_per_seq, n_batch):
    row_to_mod = lambda i: jnp.minimum(i // tiles_per_seq, n_batch)
    row_tile = pl.BlockSpec((tm, D_MODEL), lambda i: (i, 0))
    lanes = pl.BlockSpec((tm, 128), lambda i: (i, 0))
    router = pl.BlockSpec((D_MODEL, 128), lambda i: (0, 0))
    return pl.pallas_call(
        _router_kernel,
        grid=(rows // tm,),
        in_specs=[row_tile, pl.BlockSpec((1, D_MODEL), lambda i: (0, 0)),
                  _mod_spec(4, row_to_mod), _mod_spec(3, row_to_mod), router, router],
        out_specs=[row_tile, lanes, lanes],
        out_shape=[jax.ShapeDtypeStruct((rows, D_MODEL), F32),
                   jax.ShapeDtypeStruct((rows, 128), jnp.int32),
                   jax.ShapeDtypeStruct((rows, 128), F32)],
        compiler_params=pltpu.CompilerParams(
            dimension_semantics=("arbitrary",), vmem_limit_bytes=VMEM_LIMIT),
        name="moe_router",
    )(xs, norm_g, mods, mods, r_hi, r_lo)


def _route_plan(idx, n_tok, tm_e):
    n_pairs = 2 * n_tok
    n_tiles = n_pairs // tm_e + N_EXPERTS + 1
    n_rows = n_tiles * tm_e
    e_flat = idx.reshape(n_pairs)
    order = jnp.argsort(e_flat, stable=True).astype(jnp.int32)
    counts = jnp.sum((e_flat[:, None] == jnp.arange(N_EXPERTS, dtype=jnp.int32)[None, :]).astype(jnp.int32),
                     axis=0)
    padded = ((counts + tm_e - 1) // tm_e) * tm_e
    ends = jnp.cumsum(padded)
    tile_start = jnp.arange(n_tiles, dtype=jnp.int32) * tm_e
    tile_expert = jnp.minimum(jnp.sum((tile_start[:, None] >= ends[None, :]).astype(jnp.int32), axis=1),
                              N_EXPERTS - 1).astype(jnp.int32)
    n_used = (ends[-1] // tm_e).astype(jnp.int32).reshape(1)
    row = jnp.arange(n_rows, dtype=jnp.int32)
    row_expert = jnp.repeat(tile_expert, tm_e)
    offset = row - (ends - padded)[row_expert]
    valid = (offset < counts[row_expert]) & (row < ends[-1])
    pair = order[jnp.clip((jnp.cumsum(counts) - counts)[row_expert] + offset, 0, n_pairs - 1)]
    tok, choice = pair // 2, pair % 2
    src = jnp.where(valid, tok, 0)
    dump = n_pairs + jnp.cumsum(jnp.logical_not(valid).astype(jnp.int32)) - 1
    dst = jnp.where(valid, choice * n_tok + tok, dump)
    return tile_expert, n_used, src.reshape(n_tiles, 1, tm_e), dst.reshape(n_tiles, 1, tm_e)


def _experts_kernel(te_ref, nu_ref, src_ref, src_next_ref, dst_ref, h_hbm, w1_ref, w3_ref, w2_ref, y_hbm,
                    xbuf, xb_scr, acc_scr, ybuf, gsem, ssem, *, tm_e, n_chunks):
    t = pl.program_id(0)
    f = pl.program_id(1)
    nf = pl.num_programs(1)
    n_used = nu_ref[0]
    slot = t % 2

    def gather_rows(idx_ref, s):
        def body(r, carry):
            pltpu.make_async_copy(h_hbm.at[pl.ds(idx_ref[0, r], 1)], xbuf.at[s, pl.ds(r, 1)], gsem.at[s]).start()
            return carry
        lax.fori_loop(0, tm_e, body, 0, unroll=8)

    def wait_gather(s):
        pltpu.make_async_copy(h_hbm.at[pl.ds(0, tm_e)], xbuf.at[s], gsem.at[s]).wait()

    def scatter_rows(s):
        def body(r, carry):
            pltpu.make_async_copy(ybuf.at[s, pl.ds(r, 1)], y_hbm.at[pl.ds(dst_ref[0, r], 1)], ssem.at[s]).start()
            return carry
        lax.fori_loop(0, tm_e, body, 0, unroll=8)

    def wait_scatter(s):
        pltpu.make_async_copy(ybuf.at[s], y_hbm.at[pl.ds(0, tm_e)], ssem.at[s]).wait()

    @pl.when((f == 0) & (t <= n_used))
    def _():
        @pl.when(t == 0)
        def _():
            gather_rows(src_ref, 0)
        for s in range(2):
            @pl.when(slot == s)
            def _():
                wait_gather(s)

                @pl.when(t < n_used)
                def _():
                    xb_scr[...] = xbuf[s].astype(BF16)
                    acc_scr[...] = jnp.zeros_like(acc_scr)

    @pl.when(t < n_used)
    def _():
        x = xb_scr[...]
        a = jnp.dot(x, w1_ref[...], preferred_element_type=F32)
        b = jnp.dot(x, w3_ref[...], preferred_element_type=F32)
        acc_scr[...] += jnp.dot((_silu(a) * b).astype(BF16), w2_ref[...], preferred_element_type=F32)
        share = tm_e // n_chunks
        for i in range(share):
            r = f * share + i
            pltpu.make_async_copy(h_hbm.at[pl.ds(src_next_ref[0, r], 1)],
                                  xbuf.at[1 - slot, pl.ds(r, 1)], gsem.at[1 - slot]).start()

    @pl.when(f == nf - 1)
    def _():
        for s in range(2):
            @pl.when(slot == s)
            def _():
                @pl.when(t >= 2)
                def _():
                    wait_scatter(s)

                @pl.when(t < n_used)
                def _():
                    ybuf[s] = acc_scr[...]

                @pl.when(t >= n_used)
                def _():
                    ybuf[s] = jnp.zeros(ybuf.shape[1:], F32)

                scatter_rows(s)

                @pl.when(t == pl.num_programs(0) - 1)
                def _():
                    wait_scatter(s)
                    wait_scatter(1 - s)


def _experts(h, plan, w1, w3, w2, *, n_tok, tm_e, tf):
    tile_expert, n_used, src, dst = plan
    n_tiles = src.shape[0]
    d_ff = w1.shape[2]
    nf = d_ff // tf
    n_out = n_tiles * tm_e

    def chunk(t, f, nu):
        return jnp.where(t < nu[0], f, nf - 1)

    smem_tile = lambda shift: pl.BlockSpec(
        (None, 1, tm_e), lambda t, f, te, nu: (jnp.minimum(t + shift, n_tiles - 1), 0, 0),
        memory_space=pltpu.SMEM)
    grid_spec = pltpu.PrefetchScalarGridSpec(
        num_scalar_prefetch=2,
        grid=(n_tiles, nf),
        in_specs=[
            smem_tile(0), smem_tile(1), smem_tile(0),
            pl.BlockSpec(memory_space=pl.ANY),
            pl.BlockSpec((None, D_MODEL, tf), lambda t, f, te, nu: (te[t], 0, chunk(t, f, nu))),
            pl.BlockSpec((None, D_MODEL, tf), lambda t, f, te, nu: (te[t], 0, chunk(t, f, nu))),
            pl.BlockSpec((None, tf, D_MODEL), lambda t, f, te, nu: (te[t], chunk(t, f, nu), 0)),
        ],
        out_specs=pl.BlockSpec(memory_space=pl.ANY),
        scratch_shapes=[
            pltpu.VMEM((2, tm_e, D_MODEL), F32),
            pltpu.VMEM((tm_e, D_MODEL), BF16),
            pltpu.VMEM((tm_e, D_MODEL), F32),
            pltpu.VMEM((2, tm_e, D_MODEL), F32),
            pltpu.SemaphoreType.DMA((2,)),
            pltpu.SemaphoreType.DMA((2,)),
        ],
    )
    return pl.pallas_call(
        functools.partial(_experts_kernel, tm_e=tm_e, n_chunks=nf),
        grid_spec=grid_spec,
        out_shape=jax.ShapeDtypeStruct((n_out, D_MODEL), F32),
        compiler_params=pltpu.CompilerParams(
            dimension_semantics=("arbitrary", "arbitrary"), vmem_limit_bytes=VMEM_LIMIT),
        name="moe_experts",
    )(tile_expert, n_used, src, src, dst, h, w1, w3, w2)


def _combine_kernel(x_ref, y0_ref, y1_ref, wts_ref, gt_ref, fg_ref, o_ref):
    wts = wts_ref[...]
    moe = wts[:, 0:1] * y0_ref[...] + wts[:, 1:2] * y1_ref[...]
    y = x_ref[...] + gt_ref[...] * moe
    ms = jnp.mean(y * y, axis=-1, keepdims=True)
    o_ref[...] = y * lax.rsqrt(ms + EPS) * fg_ref[...]


def _combine(xs, y, wts, mods, final_g, *, rows, tm, tiles_per_seq, n_batch):
    row_to_mod = lambda i: jnp.minimum(i // tiles_per_seq, n_batch)
    row_tile = pl.BlockSpec((tm, D_MODEL), lambda i: (i, 0))
    return pl.pallas_call(
        _combine_kernel,
        grid=(rows // tm,),
        in_specs=[
            row_tile, row_tile,
            pl.BlockSpec((tm, D_MODEL), lambda i: (i + rows // tm, 0)),
            pl.BlockSpec((tm, 128), lambda i: (i, 0)),
            _mod_spec(5, row_to_mod),
            pl.BlockSpec((1, D_MODEL), lambda i: (0, 0)),
        ],
        out_specs=row_tile,
        out_shape=jax.ShapeDtypeStruct((rows, D_MODEL), F32),
        compiler_params=pltpu.CompilerParams(
            dimension_semantics=("arbitrary",), vmem_limit_bytes=VMEM_LIMIT),
        name="moe_combine",
    )(xs, y, y, wts, mods, final_g)


def _moe(xs, mods, norm_g, r_hi, r_lo, final_g, w1, w3, w2, *, rows, tm, tm_e, tf, tiles_per_seq, n_batch):
    h, idx, wts = _router(xs, mods, norm_g, r_hi, r_lo, rows=rows, tm=tm, tiles_per_seq=tiles_per_seq,
                          n_batch=n_batch)
    plan = _route_plan(idx[:, :2], rows, tm_e)
    y = _experts(h, plan, w1, w3, w2, n_tok=rows, tm_e=tm_e, tf=tf)
    return _combine(xs, y, wts, mods, final_g, rows=rows, tm=tm, tiles_per_seq=tiles_per_seq, n_batch=n_batch)


def _deinterleave(n=HEAD_DIM):
    return np.concatenate([np.arange(0, n, 2), np.arange(1, n, 2)])


def _proj_columns():
    de = _deinterleave()
    o_dq, o_gq, o_dk, o_dv, o_gk, o_gv, o_gate = 0, 1024, 2048, 3072, 4096, 4352, 4608
    cols = np.zeros(IN_W, np.int32)
    scale = np.ones(IN_W, np.float32)
    cols[GATE_OFF:GATE_OFF + 2048] = o_gate + np.arange(2048)
    for h in range(N_DIFF_HEADS):
        for c in range(2):
            dst = 128 * h + 64 * c
            cols[DQ_OFF + dst:DQ_OFF + dst + 64] = o_dq + dst + de
            cols[DK_OFF + dst:DK_OFF + dst + 64] = o_dk + dst + de
    scale[DQ_OFF:DQ_OFF + 1024] = HEAD_DIM ** -0.5
    for j in range(N_GQA_HEADS):
        cols[GQ_OFF + 64 * j:GQ_OFF + 64 * j + 64] = o_gq + 64 * j + de
    cols[DV_OFF:DV_OFF + 1024] = o_dv + np.arange(1024)
    for g in range(N_GQA_KV):
        cols[GKV_OFF + 128 * g:GKV_OFF + 128 * g + 64] = o_gk + 64 * g + de
        cols[GKV_OFF + 128 * g + 64:GKV_OFF + 128 * g + 128] = o_gv + 64 * g + np.arange(64)
    return cols, scale


def _rope_tables(S, pad_rows):
    rows = S // GRID_W
    row = jnp.repeat(jnp.arange(rows, dtype=F32), GRID_W)
    col = jnp.tile(jnp.arange(GRID_W, dtype=F32), rows)
    half = HEAD_DIM // 2
    inv_freq = ROPE_THETA ** (-jnp.arange(0, half, 2, dtype=F32) / half)
    ang = jnp.concatenate([row[:, None] * inv_freq, col[:, None] * inv_freq], axis=-1)
    cos, sin = jnp.cos(ang), jnp.sin(ang)
    reps = PROJ_TN // HEAD_DIM
    cos_t = jnp.tile(jnp.concatenate([cos, cos], axis=-1), (1, reps))
    sin_t = jnp.tile(jnp.concatenate([-sin, sin], axis=-1), (1, reps))
    cos_t = jnp.concatenate([cos_t, jnp.ones((pad_rows, PROJ_TN), F32)], axis=0)
    sin_t = jnp.concatenate([sin_t, jnp.zeros((pad_rows, PROJ_TN), F32)], axis=0)
    return cos_t, sin_t


def kernel(x, c, ctx, c_ctx, ada_w, ada_b, norm_attn_g, norm_ffn_g, w_in, q_norm_g, k_norm_g, diff_lambda,
           diff_subln_g, w_proj_diff, w_proj_gqa, w_out, ffn_w1, ffn_w3, ffn_w2, moe_router, moe_w1, moe_w3,
           moe_w2, final_norm_g):
    B, S, D = x.shape
    C = ctx.shape[1]
    depth = ada_w.shape[0]
    assert D == D_MODEL and depth == 2 and B + 1 <= MOD_ROWS
    assert w_in.shape[2] == IN_W and moe_router.shape[2] == N_EXPERTS
    tiles = _pick_tiles(B, S, C, ffn_w1.shape[2], moe_w1.shape[3])
    n_lat = B * S

    cvec = jnp.zeros((MOD_ROWS, D), F32).at[:B].set(c).at[B].set(c_ctx)
    mods_all = _ada_mods(cvec, ada_w, ada_b).reshape(depth, MOD_ROWS * N_MODS, 1, D)

    cols, col_scale = _proj_columns()
    de = _deinterleave()
    cos_t, sin_t = _rope_tables(S, tiles.tm_proj)
    blk = np.arange(PROJ_TN) // HEAD_DIM
    ones_bd = jnp.asarray(blk[:, None] == blk[None, :], BF16)
    col = np.arange(PROJ_TN)
    swap_bd = jnp.asarray(col[:, None] == (col[None, :] ^ (HEAD_DIM // 2)), BF16)
    is_key_lane = (np.arange(PROJ_TN) % (2 * HEAD_DIM)) < HEAD_DIM

    xs = jnp.concatenate([x.reshape(n_lat, D), ctx.reshape(B * C, D)], axis=0)
    for l in range(depth):
        last = l == depth - 1
        mods = mods_all[l]
        lam_init = 0.8 - 0.6 * math.exp(-0.3 * l)
        lq1, lk1, lq2, lk2 = diff_lambda[l]
        lam = (jnp.exp(jnp.sum(lq1 * lk1)) - jnp.exp(jnp.sum(lq2 * lk2)) + lam_init).reshape(1).astype(F32)
        w = (jnp.take(w_in[l], cols, axis=1) * col_scale).astype(BF16)
        qg = jnp.tile(q_norm_g[l][de] * HEAD_DIM ** -0.5, PROJ_TN // HEAD_DIM).reshape(1, PROJ_TN)
        kg = jnp.where(is_key_lane, jnp.tile(k_norm_g[l][de], PROJ_TN // HEAD_DIM), 1.0).reshape(1, PROJ_TN)
        subln = (diff_subln_g[l] * (1.0 - lam_init)).reshape(1, 2 * HEAD_DIM)

        p = _project(xs, mods, norm_attn_g[l].reshape(1, D), w, cos_t, sin_t, ones_bd, swap_bd, qg, kg,
                     tm=tiles.tm_proj, lat_tiles=n_lat // tiles.tm_proj, tiles_per_seq=S // tiles.tm_proj,
                     n_batch=B)
        yd, yg = _attention(p, lam, subln, B=B, S=S, C=C, tiles=tiles, ctx_queries=not last)
        rows = n_lat if last else n_lat + B * C
        xs = _mixer_out(xs, yd, yg, p, mods, w_proj_diff[l].astype(BF16), w_proj_gqa[l].astype(BF16),
                        w_out[l].astype(BF16), rows=rows, tm=tiles.tm_mix, tiles_per_seq=S // tiles.tm_mix,
                        n_batch=B)
        i = l // 2
        if l % 2 == 0:
            xs = _ffn(xs, mods, norm_ffn_g[l].reshape(1, D), ffn_w1[i].astype(BF16), ffn_w3[i].astype(BF16),
                      ffn_w2[i].astype(BF16), tm=tiles.tm_ffn, tf=tiles.tf_ffn,
                      tiles_per_seq=S // tiles.tm_ffn, n_batch=B)
        else:
            r_pad = jnp.zeros((D, 128), F32).at[:, :N_EXPERTS].set(moe_router[i])
            r_hi = r_pad.astype(BF16)
            r_lo = (r_pad - r_hi.astype(F32)).astype(BF16)
            xs = _moe(xs, mods, norm_ffn_g[l].reshape(1, D), r_hi, r_lo, final_norm_g.reshape(1, D),
                      moe_w1[i].astype(BF16), moe_w3[i].astype(BF16), moe_w2[i].astype(BF16),
                      rows=rows, tm=tiles.tm_mix, tm_e=tiles.tm_moe, tf=tiles.tf_moe,
                      tiles_per_seq=S // tiles.tm_mix, n_batch=B)
    return xs.reshape(B, S, D)
```

```python
import functools
import math
from typing import NamedTuple

import numpy as np
import jax
import jax.numpy as jnp
from jax import lax
from jax.experimental import pallas as pl
from jax.experimental.pallas import tpu as pltpu

F32 = jnp.float32
BF16 = jnp.bfloat16

D_MODEL = 1024
HEAD_DIM = 64
N_DIFF_HEADS = 8
N_GQA_HEADS = 16
N_GQA_KV = 4
GQA_REP = N_GQA_HEADS // N_GQA_KV
N_EXPERTS = 8
GRID_W = 64
ROPE_THETA = 10000.0
EPS = 1e-6
N_MODS = 6
MOD_ROWS = 16
NEG_BIG = -1e30

GATE_OFF = 0
DQ_OFF = 2048
GQ_OFF = 3072
DK_OFF = 4096
DV_OFF = 5120
GKV_OFF = 6144
IN_W = 6656
PROJ_TN = 512
ROPE_TILES = (4, 5, 8, 9)
QNORM_TILES = (6, 7)
KV_TILE = 12

VMEM_LIMIT = 52 * 1024 * 1024


class Tiles(NamedTuple):
    tm_proj: int
    tq: int
    tk: int
    tm_mix: int
    tm_ffn: int
    tf_ffn: int
    tf_moe: int
    tm_moe: int


def _largest_divisor(n, candidates):
    for c in candidates:
        if n % c == 0:
            return c
    raise ValueError(f"no tile in {candidates} divides {n}")


def _pick_tiles(B, S, C, d_ff, d_ff_e):
    rows_common = math.gcd(S, B * C)
    tm = _largest_divisor(rows_common, (512, 256, 128))
    return Tiles(
        tm_proj=_largest_divisor(rows_common, (1024, 512, 256, 128)),
        tq=C,
        tk=_largest_divisor(math.gcd(S, C), (256, 128)),
        tm_mix=tm,
        tm_ffn=_largest_divisor(rows_common, (1024, 512, 256, 128)),
        tf_ffn=_largest_divisor(d_ff, (1408, 1024, 512, 256, 128)),
        tf_moe=_largest_divisor(d_ff_e, (896, 512, 256, 128)),
        tm_moe=_largest_divisor(2 * B * S, (512, 256)),
    )


def _split_bf16(v):
    hi = v.astype(BF16)
    lo = (v - hi.astype(F32)).astype(BF16)
    return hi, lo


def _ada_kernel(c_ref, w_ref, b_ref, o_ref):
    c = c_ref[...]
    s = c / (1.0 + jnp.exp(-c))
    s_hi, s_lo = _split_bf16(s)
    w_hi, w_lo = _split_bf16(w_ref[...])
    acc = jnp.dot(s_hi, w_hi, preferred_element_type=F32)
    acc += jnp.dot(s_lo, w_hi, preferred_element_type=F32)
    acc += jnp.dot(s_hi, w_lo, preferred_element_type=F32)
    o_ref[...] = acc + b_ref[...]


def _ada_mods(cvec, ada_w, ada_b):
    depth, d, n = ada_w.shape
    tn = _largest_divisor(n, (1536, 1024, 512))
    return pl.pallas_call(
        _ada_kernel,
        grid=(depth, n // tn),
        in_specs=[
            pl.BlockSpec((MOD_ROWS, d), lambda l, j: (0, 0)),
            pl.BlockSpec((None, d, tn), lambda l, j: (l, 0, j)),
            pl.BlockSpec((None, 1, tn), lambda l, j: (l, 0, j)),
        ],
        out_specs=pl.BlockSpec((None, MOD_ROWS, tn), lambda l, j: (l, 0, j)),
        out_shape=jax.ShapeDtypeStruct((depth, MOD_ROWS, n), F32),
        compiler_params=pltpu.CompilerParams(
            dimension_semantics=("arbitrary", "arbitrary"), vmem_limit_bytes=VMEM_LIMIT),
        name="ada_mods",
    )(cvec, ada_w, ada_b.reshape(depth, 1, n))


def _modulated_norm(x, g, scale, shift):
    ms = jnp.mean(x * x, axis=-1, keepdims=True)
    return (x * lax.rsqrt(ms + EPS) * g) * (1.0 + scale) + shift


def _silu(a):
    return a / (1.0 + jnp.exp(-a))


def _mod_spec(k, row_to_mod):
    return pl.BlockSpec((None, 1, D_MODEL), lambda i, *_: (row_to_mod(i) * N_MODS + k, 0, 0))


def _rope(z, cos, sin_signed, swap_ref):
    partner = jnp.dot(z.astype(BF16), swap_ref[...], preferred_element_type=F32)
    return z * cos + partner * sin_signed


def _head_rms(z, ones_ref, gain):
    ss = jnp.dot((z * z).astype(BF16), ones_ref[...], preferred_element_type=F32)
    return z * lax.rsqrt(ss * (1.0 / HEAD_DIM) + EPS) * gain


def _proj_kernel(x_ref, g_ref, sc_ref, sh_ref, w_ref, cos_ref, sin_ref, ones_ref, swap_ref, qg_ref, kg_ref,
                 o_ref, h_scr):
    j = pl.program_id(1)

    @pl.when(j == 0)
    def _():
        h_scr[...] = _modulated_norm(x_ref[...], g_ref[...], sc_ref[...], sh_ref[...]).astype(BF16)

    def project():
        z = jnp.dot(h_scr[...], w_ref[...], preferred_element_type=F32)
        return z, lax.broadcasted_iota(jnp.int32, z.shape, 1)

    is_rope = functools.reduce(jnp.logical_or, [j == t for t in ROPE_TILES])
    is_qnorm = functools.reduce(jnp.logical_or, [j == t for t in QNORM_TILES])
    is_kv = j == KV_TILE
    is_plain = jnp.logical_not(is_rope | is_qnorm | is_kv)

    @pl.when(is_plain)
    def _():
        z, _ = project()
        o_ref[...] = z.astype(BF16)

    @pl.when(is_rope)
    def _():
        z, lane = project()
        o_ref[...] = _rope(z, cos_ref[...], sin_ref[...], swap_ref).astype(BF16)

    @pl.when(is_qnorm)
    def _():
        z, lane = project()
        zn = _head_rms(z, ones_ref, qg_ref[...])
        o_ref[...] = _rope(zn, cos_ref[...], sin_ref[...], swap_ref).astype(BF16)

    @pl.when(is_kv)
    def _():
        z, lane = project()
        zn = _head_rms(z, ones_ref, kg_ref[...])
        zr = _rope(zn, cos_ref[...], sin_ref[...], swap_ref)
        is_key_lane = (lane & HEAD_DIM) == 0
        o_ref[...] = jnp.where(is_key_lane, zr, z).astype(BF16)


def _project(xs, mods, norm_g, w, cos_t, sin_t, ones_bd, swap_bd, qg, kg, *, tm, lat_tiles, tiles_per_seq, n_batch):
    rows = xs.shape[0]
    row_to_mod = lambda i: jnp.minimum(i // tiles_per_seq, n_batch)
    rope_row = lambda i: jnp.where(i < lat_tiles, i % tiles_per_seq, tiles_per_seq)
    const = lambda i, j: (0, 0)
    return pl.pallas_call(
        _proj_kernel,
        grid=(rows // tm, IN_W // PROJ_TN),
        in_specs=[
            pl.BlockSpec((tm, D_MODEL), lambda i, j: (i, 0)),
            pl.BlockSpec((1, D_MODEL), const),
            _mod_spec(1, row_to_mod),
            _mod_spec(0, row_to_mod),
            pl.BlockSpec((D_MODEL, PROJ_TN), lambda i, j: (0, j)),
            pl.BlockSpec((tm, PROJ_TN), lambda i, j: (rope_row(i), 0)),
            pl.BlockSpec((tm, PROJ_TN), lambda i, j: (rope_row(i), 0)),
            pl.BlockSpec((PROJ_TN, PROJ_TN), const),
            pl.BlockSpec((PROJ_TN, PROJ_TN), const),
            pl.BlockSpec((1, PROJ_TN), const),
            pl.BlockSpec((1, PROJ_TN), const),
        ],
        out_specs=pl.BlockSpec((tm, PROJ_TN), lambda i, j: (i, j)),
        out_shape=jax.ShapeDtypeStruct((rows, IN_W), BF16),
        scratch_shapes=[pltpu.VMEM((tm, D_MODEL), BF16)],
        compiler_params=pltpu.CompilerParams(
            dimension_semantics=("arbitrary", "arbitrary"), vmem_limit_bytes=VMEM_LIMIT),
        name="in_proj",
    )(xs, norm_g, mods, mods, w, cos_t, sin_t, ones_bd, swap_bd, qg, kg)


STAB_LANE = HEAD_DIM
L_FLOOR = 1e-26
DIFF_VT_ROWS = 2 * HEAD_DIM + 16
DIFF_HEADS_PER_STEP = 2
GQA_GROUPS_PER_STEP = 1


def _aug_keys(k):
    lane = lax.broadcasted_iota(jnp.int32, k.shape, 1)
    in_key = lane < HEAD_DIM
    sq = jnp.where(in_key, k * k, 0.0).astype(BF16)
    norms = jnp.dot(sq, jnp.ones((k.shape[1], k.shape[1]), BF16), preferred_element_type=F32)
    ka = jnp.where(in_key, k, jnp.where(lane == STAB_LANE, 1.0, 0.0)).astype(BF16)
    return ka, jnp.max(norms, axis=0, keepdims=True)


def _query_mats(slab, kmax2):
    lane = lax.broadcasted_iota(jnp.int32, slab.shape, 1)
    in_key = lane < HEAD_DIM
    sq = jnp.where(in_key, slab * slab, 0.0).astype(BF16)
    norms = jnp.dot(sq, jnp.ones((slab.shape[1], slab.shape[1]), BF16), preferred_element_type=F32)
    nk = norms * kmax2
    bound = nk * lax.rsqrt(nk + 1e-30)
    plain = jnp.where(in_key, slab, 0.0)
    return jnp.where(lane == STAB_LANE, -bound, plain).astype(BF16), plain.astype(BF16)


def _attend(ka_scr, vt_scr, mats, acc_scr, *, first, n_chunks, tk):
    def scores(r, c, qmat):
        off = pl.multiple_of(c * tk, tk)
        return lax.dot_general(ka_scr[r * ka_scr.shape[0] // len(mats), pl.ds(off, tk), :], qmat, (((1,), (1,)), ((), ())),
                               preferred_element_type=F32)

    acc_scr[...] = jnp.zeros(acc_scr.shape, F32)

    def all_scores(c):
        return tuple(scores(r, c, shifted) for r, (shifted, _) in enumerate(mats))

    def accumulate(c, s_all):
        for r, s in enumerate(s_all):
            vt = vt_scr[r * vt_scr.shape[0] // len(mats), c]
            acc_scr[r] += jnp.dot(vt, jnp.exp(s).astype(BF16), preferred_element_type=F32)

    def fast_body(c, s_cur):
        s_next = all_scores(c + 1)
        accumulate(c, s_cur)
        return s_next

    trips = n_chunks - 1 - first
    s_last = lax.fori_loop(first, n_chunks - 1, fast_body, all_scores(first),
                           unroll=_largest_divisor(trips, (16, 8, 4, 2, 1)) if trips > 0 else 1)
    accumulate(n_chunks - 1, s_last)


def _attend_fallback(ka_scr, vt_scr, mats, m_scr, acc_scr, *, first, n_chunks, tk):
    m_scr[...] = jnp.full(m_scr.shape, NEG_BIG, F32)
    acc_scr[...] = jnp.zeros(acc_scr.shape, F32)

    def body(c, carry):
        off = pl.multiple_of(c * tk, tk)
        for r, (_, plain) in enumerate(mats):
            s = lax.dot_general(ka_scr[r * ka_scr.shape[0] // len(mats), pl.ds(off, tk), :], plain, (((1,), (1,)), ((), ())),
                                preferred_element_type=F32)
            m_old = m_scr[r]
            m_new = jnp.maximum(m_old, jnp.max(s, axis=0, keepdims=True))
            p = jnp.exp(s - m_new).astype(BF16)
            vt = vt_scr[r * vt_scr.shape[0] // len(mats), c]
            acc_scr[r] = acc_scr[r] * jnp.exp(m_old - m_new) + jnp.dot(vt, p, preferred_element_type=F32)
            m_scr[r] = m_new
        return carry

    lax.fori_loop(first, n_chunks, body, 0)


def _softmax_tile(ka_scr, vt_scr, kmax_scr, slabs, m_scr, acc_scr, l_row, *, n_lat_chunks, n_chunks, tk, ctx_tile):
    mats = [_query_mats(slab, kmax_scr[r * kmax_scr.shape[0] // len(slabs)]) for r, slab in enumerate(slabs)]

    def run(first):
        _attend(ka_scr, vt_scr, mats, acc_scr, first=first, n_chunks=n_chunks, tk=tk)
        l_min = functools.reduce(
            jnp.minimum, [jnp.min(acc_scr[r, l_row:l_row + 1, :]) for r in range(len(slabs))])

        @pl.when(jnp.logical_not(l_min >= L_FLOOR))
        def _():
            _attend_fallback(ka_scr, vt_scr, mats, m_scr, acc_scr, first=first, n_chunks=n_chunks, tk=tk)

    if ctx_tile is None:
        run(0)
    else:
        pl.when(jnp.logical_not(ctx_tile))(functools.partial(run, 0))
        pl.when(ctx_tile)(functools.partial(run, n_lat_chunks))


def _chunk_rows(c, n_lat_chunks, tk, lat_ref, ctx_ref):
    if c < n_lat_chunks:
        return lat_ref[c * tk:(c + 1) * tk, :]
    return ctx_ref[(c - n_lat_chunks) * tk:(c - n_lat_chunks + 1) * tk, :]


def _diff_attn_kernel(lam_ref, q_ref, kl_ref, vl_ref, kc_ref, vc_ref, g_ref, o_ref,
                      ka_scr, vt_scr, kmax_scr, m_scr, acc_scr, *, nq_lat, n_lat_chunks, n_chunks, tk, ctx_queries):
    dv = 2 * HEAD_DIM
    heads = [slice(a * dv, (a + 1) * dv) for a in range(DIFF_HEADS_PER_STEP)]

    @pl.when(pl.program_id(2) == 0)
    def _():
        kmax = [None] * (2 * len(heads))
        for c in range(n_chunks):
            k_all = _chunk_rows(c, n_lat_chunks, tk, kl_ref, kc_ref).astype(F32)
            v_all = _chunk_rows(c, n_lat_chunks, tk, vl_ref, vc_ref)
            for a, cols in enumerate(heads):
                k = k_all[:, cols]
                for r, keys in ((2 * a, k), (2 * a + 1, pltpu.roll(k, HEAD_DIM, 1))):
                    ka, n2 = _aug_keys(keys)
                    ka_scr[r, c * tk:(c + 1) * tk, :] = ka
                    kmax[r] = n2 if kmax[r] is None else jnp.maximum(kmax[r], n2)
                vt_scr[a, c, :dv, :] = v_all[:, cols].T
                vt_scr[a, c, dv:, :] = jnp.ones((DIFF_VT_ROWS - dv, tk), BF16)
        for r, n2 in enumerate(kmax):
            kmax_scr[r] = n2

    q_all = q_ref[...].astype(F32)
    slabs = []
    for cols in heads:
        slabs += [q_all[:, cols], pltpu.roll(q_all[:, cols], HEAD_DIM, 1)]
    ctx_tile = (pl.program_id(2) == nq_lat) if ctx_queries else None
    _softmax_tile(ka_scr, vt_scr, kmax_scr, slabs, m_scr, acc_scr, dv,
                  n_lat_chunks=n_lat_chunks, n_chunks=n_chunks, tk=tk, ctx_tile=ctx_tile)
    for a, cols in enumerate(heads):
        a1, a2 = acc_scr[2 * a], acc_scr[2 * a + 1]
        yt = a1[:dv] * (1.0 / a1[dv:dv + 1]) - lam_ref[0] * (a2[:dv] * (1.0 / a2[dv:dv + 1]))
        y = yt.T
        ms = jnp.mean(y * y, axis=-1, keepdims=True)
        o_ref[:, cols] = (y * lax.rsqrt(ms + EPS) * g_ref[...]).astype(BF16)


def _gqa_attn_kernel(q_ref, kvl_ref, kvc_ref, o_ref, ka_scr, vt_scr, kmax_scr, m_scr, acc_scr,
                     *, nq_lat, n_lat_chunks, n_chunks, tk, ctx_queries):
    w2 = 2 * HEAD_DIM
    groups = [slice(g * w2, (g + 1) * w2) for g in range(GQA_GROUPS_PER_STEP)]
    n_heads = GQA_REP * len(groups)

    @pl.when(pl.program_id(2) == 0)
    def _():
        kmax = [None] * len(groups)
        for c in range(n_chunks):
            kv_all = _chunk_rows(c, n_lat_chunks, tk, kvl_ref, kvc_ref)
            for g, cols in enumerate(groups):
                kv = kv_all[:, cols]
                ka, n2 = _aug_keys(kv.astype(F32))
                ka_scr[g, c * tk:(c + 1) * tk, :] = ka
                kmax[g] = n2 if kmax[g] is None else jnp.maximum(kmax[g], n2)
                t = kv.T
                row = lax.broadcasted_iota(jnp.int32, t.shape, 0)
                vt_scr[g, c] = jnp.where(row < HEAD_DIM, jnp.ones_like(t), t)
        for g, n2 in enumerate(kmax):
            kmax_scr[g] = n2

    qf = q_ref[...].astype(F32)
    tq = qf.shape[0]
    lane = lax.broadcasted_iota(jnp.int32, (tq, w2), 1)
    slabs = []
    for r in range(n_heads):
        slab = qf[:, (r // 2) * w2:(r // 2 + 1) * w2]
        slabs.append(pltpu.roll(slab, HEAD_DIM, 1) if r % 2 else slab)
    ctx_tile = (pl.program_id(2) == nq_lat) if ctx_queries else None
    _softmax_tile(ka_scr, vt_scr, kmax_scr, slabs, m_scr, acc_scr, 0,
                  n_lat_chunks=n_lat_chunks, n_chunks=n_chunks, tk=tk, ctx_tile=ctx_tile)

    def head_out(r):
        acc = acc_scr[r]
        return (acc * (1.0 / acc[0:1])).T

    for j in range(n_heads // 2):
        pair = jnp.where(lane < HEAD_DIM, pltpu.roll(head_out(2 * j), HEAD_DIM, 1), head_out(2 * j + 1))
        o_ref[:, j * w2:(j + 1) * w2] = pair.astype(BF16)


def _attention(p, lam, subln_gain, *, B, S, C, tiles, ctx_queries):
    tq, tk = tiles.tq, tiles.tk
    nq_lat = S // tq
    nq = nq_lat + (1 if ctx_queries else 0)
    ctx_blk0 = (B * S) // C
    rows_out = B * S + (B * C if ctx_queries else 0)
    T = S + C
    statics = dict(nq_lat=nq_lat, n_lat_chunks=S // tk, n_chunks=T // tk, tk=tk, ctx_queries=ctx_queries)

    def q_row(b, qi):
        if not ctx_queries:
            return b * nq_lat + qi
        return jnp.where(qi < nq_lat, b * nq_lat + qi, ctx_blk0 + b)

    cparams = pltpu.CompilerParams(
        dimension_semantics=("arbitrary", "arbitrary", "arbitrary"), vmem_limit_bytes=VMEM_LIMIT)
    w2 = 2 * HEAD_DIM
    hps = DIFF_HEADS_PER_STEP
    wd = hps * w2
    yd = pl.pallas_call(
        functools.partial(_diff_attn_kernel, **statics),
        grid=(B, N_DIFF_HEADS // hps, nq),
        in_specs=[
            pl.BlockSpec(memory_space=pltpu.SMEM),
            pl.BlockSpec((tq, wd), lambda b, h, qi: (q_row(b, qi), DQ_OFF // wd + h)),
            pl.BlockSpec((S, wd), lambda b, h, qi: (b, DK_OFF // wd + h)),
            pl.BlockSpec((S, wd), lambda b, h, qi: (b, DV_OFF // wd + h)),
            pl.BlockSpec((C, wd), lambda b, h, qi: (ctx_blk0 + b, DK_OFF // wd + h)),
            pl.BlockSpec((C, wd), lambda b, h, qi: (ctx_blk0 + b, DV_OFF // wd + h)),
            pl.BlockSpec((1, w2), lambda b, h, qi: (0, 0)),
        ],
        out_specs=pl.BlockSpec((tq, wd), lambda b, h, qi: (q_row(b, qi), h)),
        out_shape=jax.ShapeDtypeStruct((rows_out, N_DIFF_HEADS * w2), BF16),
        scratch_shapes=[
            pltpu.VMEM((2 * hps, T, w2), BF16),
            pltpu.VMEM((hps, T // tk, DIFF_VT_ROWS, tk), BF16),
            pltpu.VMEM((2 * hps, 1, w2), F32),
            pltpu.VMEM((2 * hps, 1, tq), F32),
            pltpu.VMEM((2 * hps, DIFF_VT_ROWS, tq), F32),
        ],
        compiler_params=cparams,
        name="diff_attn",
    )(lam, p, p, p, p, p, subln_gain)

    gps = GQA_GROUPS_PER_STEP
    wq = gps * GQA_REP * HEAD_DIM
    wkv = gps * w2
    yg = pl.pallas_call(
        functools.partial(_gqa_attn_kernel, **statics),
        grid=(B, N_GQA_KV // gps, nq),
        in_specs=[
            pl.BlockSpec((tq, wq), lambda b, g, qi: (q_row(b, qi), GQ_OFF // wq + g)),
            pl.BlockSpec((S, wkv), lambda b, g, qi: (b, GKV_OFF // wkv + g)),
            pl.BlockSpec((C, wkv), lambda b, g, qi: (ctx_blk0 + b, GKV_OFF // wkv + g)),
        ],
        out_specs=pl.BlockSpec((tq, wq), lambda b, g, qi: (q_row(b, qi), g)),
        out_shape=jax.ShapeDtypeStruct((rows_out, N_GQA_HEADS * HEAD_DIM), BF16),
        scratch_shapes=[
            pltpu.VMEM((gps, T, w2), BF16),
            pltpu.VMEM((gps, T // tk, w2, tk), BF16),
            pltpu.VMEM((gps, 1, w2), F32),
            pltpu.VMEM((gps * GQA_REP, 1, tq), F32),
            pltpu.VMEM((gps * GQA_REP, w2, tq), F32),
        ],
        compiler_params=cparams,
        name="gqa_attn",
    )(p, p, p)
    return yd, yg


def _mixer_kernel(x_ref, yd_ref, yg_ref, gd_ref, gg_ref, gt_ref, wpd_ref, wpg_ref, wo_ref, o_ref):
    pd = jnp.dot(yd_ref[...], wpd_ref[...], preferred_element_type=F32)
    pg = jnp.dot(yg_ref[...], wpg_ref[...], preferred_element_type=F32)
    m = jax.nn.sigmoid(gd_ref[...].astype(F32)) * pd + jax.nn.sigmoid(gg_ref[...].astype(F32)) * pg
    o_ref[...] = x_ref[...] + gt_ref[...] * jnp.dot(m.astype(BF16), wo_ref[...], preferred_element_type=F32)


def _mixer_out(xs, yd, yg, p, mods, wpd, wpg, wo, *, rows, tm, tiles_per_seq, n_batch):
    row_to_mod = lambda i: jnp.minimum(i // tiles_per_seq, n_batch)
    row_tile = pl.BlockSpec((tm, D_MODEL), lambda i: (i, 0))
    weight = pl.BlockSpec((D_MODEL, D_MODEL), lambda i: (0, 0))
    return pl.pallas_call(
        _mixer_kernel,
        grid=(rows // tm,),
        in_specs=[
            row_tile, row_tile, row_tile,
            pl.BlockSpec((tm, D_MODEL), lambda i: (i, GATE_OFF // D_MODEL)),
            pl.BlockSpec((tm, D_MODEL), lambda i: (i, GATE_OFF // D_MODEL + 1)),
            _mod_spec(2, row_to_mod),
            weight, weight, weight,
        ],
        out_specs=row_tile,
        out_shape=jax.ShapeDtypeStruct((rows, D_MODEL), F32),
        compiler_params=pltpu.CompilerParams(
            dimension_semantics=("arbitrary",), vmem_limit_bytes=VMEM_LIMIT),
        name="mixer_out",
    )(xs, yd, yg, p, p, mods, wpd, wpg, wo)


def _ffn_kernel(x_ref, g_ref, sc_ref, sh_ref, gt_ref, w1_ref, w3_ref, w2_ref, o_ref, h_scr, acc_scr):
    f = pl.program_id(1)

    @pl.when(f == 0)
    def _():
        h_scr[...] = _modulated_norm(x_ref[...], g_ref[...], sc_ref[...], sh_ref[...]).astype(BF16)
        acc_scr[...] = jnp.zeros_like(acc_scr)

    h = h_scr[...]
    a = jnp.dot(h, w1_ref[...], preferred_element_type=F32)
    b = jnp.dot(h, w3_ref[...], preferred_element_type=F32)
    acc_scr[...] += jnp.dot((_silu(a) * b).astype(BF16), w2_ref[...], preferred_element_type=F32)

    @pl.when(f == pl.num_programs(1) - 1)
    def _():
        o_ref[...] = x_ref[...] + gt_ref[...] * acc_scr[...]


def _ffn(xs, mods, norm_g, w1, w3, w2, *, tm, tf, tiles_per_seq, n_batch):
    rows = xs.shape[0]
    d_ff = w1.shape[1]
    row_to_mod = lambda i: jnp.minimum(i // tiles_per_seq, n_batch)
    row_tile = pl.BlockSpec((tm, D_MODEL), lambda i, f: (i, 0))
    return pl.pallas_call(
        _ffn_kernel,
        grid=(rows // tm, d_ff // tf),
        in_specs=[
            row_tile,
            pl.BlockSpec((1, D_MODEL), lambda i, f: (0, 0)),
            _mod_spec(4, row_to_mod), _mod_spec(3, row_to_mod), _mod_spec(5, row_to_mod),
            pl.BlockSpec((D_MODEL, tf), lambda i, f: (0, f)),
            pl.BlockSpec((D_MODEL, tf), lambda i, f: (0, f)),
            pl.BlockSpec((tf, D_MODEL), lambda i, f: (f, 0)),
        ],
        out_specs=row_tile,
        out_shape=jax.ShapeDtypeStruct((rows, D_MODEL), F32),
        scratch_shapes=[pltpu.VMEM((tm, D_MODEL), BF16), pltpu.VMEM((tm, D_MODEL), F32)],
        compiler_params=pltpu.CompilerParams(
            dimension_semantics=("arbitrary", "arbitrary"), vmem_limit_bytes=VMEM_LIMIT),
        name="ffn_swiglu",
    )(xs, norm_g, mods, mods, mods, w1, w3, w2)


def _top2_gates(logits):
    lane = lax.broadcasted_iota(jnp.int32, logits.shape, 1)
    n_lanes = logits.shape[1]
    lg = jnp.where(lane < N_EXPERTS, logits, NEG_BIG)
    m1 = jnp.max(lg, axis=-1, keepdims=True)
    i1 = jnp.min(jnp.where(lg == m1, lane, n_lanes), axis=-1, keepdims=True)
    lg2 = jnp.where(lane == i1, NEG_BIG, lg)
    m2 = jnp.max(lg2, axis=-1, keepdims=True)
    i2 = jnp.min(jnp.where(lg2 == m2, lane, n_lanes), axis=-1, keepdims=True)
    e2 = jnp.exp(m2 - m1)
    w_top = 1.0 / (1.0 + e2)
    idx = jnp.where(lane == 0, i1, jnp.where(lane == 1, i2, 0))
    wts = jnp.where(lane == 0, w_top, jnp.where(lane == 1, e2 * w_top, 0.0))
    return idx, wts


def _router_kernel(x_ref, g_ref, sc_ref, sh_ref, rhi_ref, rlo_ref, h_ref, idx_ref, wts_ref):
    h = _modulated_norm(x_ref[...], g_ref[...], sc_ref[...], sh_ref[...])
    h_ref[...] = h
    h_hi, h_lo = _split_bf16(h)
    logits = jnp.dot(h_hi, rhi_ref[...], preferred_element_type=F32)
    logits += jnp.dot(h_lo, rhi_ref[...], preferred_element_type=F32)
    logits += jnp.dot(h_hi, rlo_ref[...], preferred_element_type=F32)
    idx_ref[...], wts_ref[...] = _top2_gates(logits)


def _router(xs, mods, norm_g, r_hi, r_lo, *, rows, tm, tiles_per_seq, n_batch):
    row_to_mod = lambda i: jnp.minimum(i // tiles_per_seq, n_batch)
    row_tile = pl.BlockSpec((tm, D_MODEL), lambda i: (i, 0))
    lanes = pl.BlockSpec((tm, 128), lambda i: (i, 0))
    router = pl.BlockSpec((D_MODEL, 128), lambda i: (0, 0))
    return pl.pallas_call(
        _router_kernel,
        grid=(rows // tm,),
        in_specs=[row_tile, pl.BlockSpec((1, D_MODEL), lambda i: (0, 0)),
                  _mod_spec(4, row_to_mod), _mod_spec(3, row_to_mod), router, router],
        out_specs=[row_tile, lanes, lanes],
        out_shape=[jax.ShapeDtypeStruct((rows, D_MODEL), F32),
                   jax.ShapeDtypeStruct((rows, 128), jnp.int32),
                   jax.ShapeDtypeStruct((rows, 128), F32)],
        compiler_params=pltpu.CompilerParams(
            dimension_semantics=("arbitrary",), vmem_limit_bytes=VMEM_LIMIT),
        name="moe_router",
    )(xs, norm_g, mods, mods, r_hi, r_lo)


def _route_plan(idx, n_tok, tm_e):
    n_pairs = 2 * n_tok
    n_tiles = n_pairs // tm_e + N_EXPERTS + 1
    n_rows = n_tiles * tm_e
    e_flat = idx.reshape(n_pairs)
    order = jnp.argsort(e_flat, stable=True).astype(jnp.int32)
    counts = jnp.sum((e_flat[:, None] == jnp.arange(N_EXPERTS, dtype=jnp.int32)[None, :]).astype(jnp.int32),
                     axis=0)
    padded = ((counts + tm_e - 1) // tm_e) * tm_e
    ends = jnp.cumsum(padded)
    tile_start = jnp.arange(n_tiles, dtype=jnp.int32) * tm_e
    tile_expert = jnp.minimum(jnp.sum((tile_start[:, None] >= ends[None, :]).astype(jnp.int32), axis=1),
                              N_EXPERTS - 1).astype(jnp.int32)
    n_used = (ends[-1] // tm_e).astype(jnp.int32).reshape(1)
    row = jnp.arange(n_rows, dtype=jnp.int32)
    row_expert = jnp.repeat(tile_expert, tm_e)
    offset = row - (ends - padded)[row_expert]
    valid = (offset < counts[row_expert]) & (row < ends[-1])
    pair = order[jnp.clip((jnp.cumsum(counts) - counts)[row_expert] + offset, 0, n_pairs - 1)]
    tok, choice = pair // 2, pair % 2
    src = jnp.where(valid, tok, 0)
    dump = n_pairs + jnp.cumsum(jnp.logical_not(valid).astype(jnp.int32)) - 1
    dst = jnp.where(valid, choice * n_tok + tok, dump)
    return tile_expert, n_used, src.reshape(n_tiles, 1, tm_e), dst.reshape(n_tiles, 1, tm_e)


def _experts_kernel(te_ref, nu_ref, src_ref, src_next_ref, dst_ref, h_hbm, w1_ref, w3_ref, w2_ref, y_hbm,
                    xbuf, xb_scr, acc_scr, ybuf, gsem, ssem, *, tm_e, n_chunks):
    t = pl.program_id(0)
    f = pl.program_id(1)
    nf = pl.num_programs(1)
    n_used = nu_ref[0]
    slot = t % 2

    def gather_rows(idx_ref, s):
        def body(r, carry):
            pltpu.make_async_copy(h_hbm.at[pl.ds(idx_ref[0, r], 1)], xbuf.at[s, pl.ds(r, 1)], gsem.at[s]).start()
            return carry
        lax.fori_loop(0, tm_e, body, 0, unroll=8)

    def wait_gather(s):
        pltpu.make_async_copy(h_hbm.at[pl.ds(0, tm_e)], xbuf.at[s], gsem.at[s]).wait()

    def scatter_rows(s):
        def body(r, carry):
            pltpu.make_async_copy(ybuf.at[s, pl.ds(r, 1)], y_hbm.at[pl.ds(dst_ref[0, r], 1)], ssem.at[s]).start()
            return carry
        lax.fori_loop(0, tm_e, body, 0, unroll=8)

    def wait_scatter(s):
        pltpu.make_async_copy(ybuf.at[s], y_hbm.at[pl.ds(0, tm_e)], ssem.at[s]).wait()

    @pl.when((f == 0) & (t <= n_used))
    def _():
        @pl.when(t == 0)
        def _():
            gather_rows(src_ref, 0)
        for s in range(2):
            @pl.when(slot == s)
            def _():
                wait_gather(s)

                @pl.when(t < n_used)
                def _():
                    xb_scr[...] = xbuf[s].astype(BF16)
                    acc_scr[...] = jnp.zeros_like(acc_scr)

    @pl.when(t < n_used)
    def _():
        x = xb_scr[...]
        a = jnp.dot(x, w1_ref[...].astype(BF16), preferred_element_type=F32)
        b = jnp.dot(x, w3_ref[...].astype(BF16), preferred_element_type=F32)
        acc_scr[...] += jnp.dot((_silu(a) * b).astype(BF16), w2_ref[...].astype(BF16),
                                preferred_element_type=F32)
        share = tm_e // n_chunks
        for i in range(share):
            r = f * share + i
            pltpu.make_async_copy(h_hbm.at[pl.ds(src_next_ref[0, r], 1)],
                                  xbuf.at[1 - slot, pl.ds(r, 1)], gsem.at[1 - slot]).start()

    @pl.when(f == nf - 1)
    def _():
        for s in range(2):
            @pl.when(slot == s)
            def _():
                @pl.when(t >= 2)
                def _():
                    wait_scatter(s)

                @pl.when(t < n_used)
                def _():
                    ybuf[s] = acc_scr[...]

                @pl.when(t >= n_used)
                def _():
                    ybuf[s] = jnp.zeros(ybuf.shape[1:], F32)

                scatter_rows(s)

                @pl.when(t == pl.num_programs(0) - 1)
                def _():
                    wait_scatter(s)
                    wait_scatter(1 - s)


def _experts(h, plan, w1, w3, w2, *, n_tok, tm_e, tf):
    tile_expert, n_used, src, dst = plan
    n_tiles = src.shape[0]
    d_ff = w1.shape[2]
    nf = d_ff // tf
    n_out = n_tiles * tm_e

    def chunk(t, f, nu):
        return jnp.where(t < nu[0], f, nf - 1)

    smem_tile = lambda shift: pl.BlockSpec(
        (None, 1, tm_e), lambda t, f, te, nu: (jnp.minimum(t + shift, n_tiles - 1), 0, 0),
        memory_space=pltpu.SMEM)
    grid_spec = pltpu.PrefetchScalarGridSpec(
        num_scalar_prefetch=2,
        grid=(n_tiles, nf),
        in_specs=[
            smem_tile(0), smem_tile(1), smem_tile(0),
            pl.BlockSpec(memory_space=pl.ANY),
            pl.BlockSpec((None, D_MODEL, tf), lambda t, f, te, nu: (te[t], 0, chunk(t, f, nu))),
            pl.BlockSpec((None, D_MODEL, tf), lambda t, f, te, nu: (te[t], 0, chunk(t, f, nu))),
            pl.BlockSpec((None, tf, D_MODEL), lambda t, f, te, nu: (te[t], chunk(t, f, nu), 0)),
        ],
        out_specs=pl.BlockSpec(memory_space=pl.ANY),
        scratch_shapes=[
            pltpu.VMEM((2, tm_e, D_MODEL), F32),
            pltpu.VMEM((tm_e, D_MODEL), BF16),
            pltpu.VMEM((tm_e, D_MODEL), F32),
            pltpu.VMEM((2, tm_e, D_MODEL), F32),
            pltpu.SemaphoreType.DMA((2,)),
            pltpu.SemaphoreType.DMA((2,)),
        ],
    )
    return pl.pallas_call(
        functools.partial(_experts_kernel, tm_e=tm_e, n_chunks=nf),
        grid_spec=grid_spec,
        out_shape=jax.ShapeDtypeStruct((n_out, D_MODEL), F32),
        compiler_params=pltpu.CompilerParams(
            dimension_semantics=("arbitrary", "arbitrary"), vmem_limit_bytes=VMEM_LIMIT),
        name="moe_experts",
    )(tile_expert, n_used, src, src, dst, h, w1, w3, w2)


def _combine_kernel(x_ref, y0_ref, y1_ref, wts_ref, gt_ref, fg_ref, o_ref):
    wts = wts_ref[...]
    moe = wts[:, 0:1] * y0_ref[...] + wts[:, 1:2] * y1_ref[...]
    y = x_ref[...] + gt_ref[...] * moe
    ms = jnp.mean(y * y, axis=-1, keepdims=True)
    o_ref[...] = y * lax.rsqrt(ms + EPS) * fg_ref[...]


def _combine(xs, y, wts, mods, final_g, *, rows, tm, tiles_per_seq, n_batch):
    row_to_mod = lambda i: jnp.minimum(i // tiles_per_seq, n_batch)
    row_tile = pl.BlockSpec((tm, D_MODEL), lambda i: (i, 0))
    return pl.pallas_call(
        _combine_kernel,
        grid=(rows // tm,),
        in_specs=[
            row_tile, row_tile,
            pl.BlockSpec((tm, D_MODEL), lambda i: (i + rows // tm, 0)),
            pl.BlockSpec((tm, 128), lambda i: (i, 0)),
            _mod_spec(5, row_to_mod),
            pl.BlockSpec((1, D_MODEL), lambda i: (0, 0)),
        ],
        out_specs=row_tile,
        out_shape=jax.ShapeDtypeStruct((rows, D_MODEL), F32),
        compiler_params=pltpu.CompilerParams(
            dimension_semantics=("arbitrary",), vmem_limit_bytes=VMEM_LIMIT),
        name="moe_combine",
    )(xs, y, y, wts, mods, final_g)


def _moe(xs, mods, norm_g, r_hi, r_lo, final_g, w1, w3, w2, *, rows, tm, tm_e, tf, tiles_per_seq, n_batch):
    h, idx, wts = _router(xs, mods, norm_g, r_hi, r_lo, rows=rows, tm=tm, tiles_per_seq=tiles_per_seq,
                          n_batch=n_batch)
    plan = _route_plan(idx[:, :2], rows, tm_e)
    y = _experts(h, plan, w1, w3, w2, n_tok=rows, tm_e=tm_e, tf=tf)
    return _combine(xs, y, wts, mods, final_g, rows=rows, tm=tm, tiles_per_seq=tiles_per_seq, n_batch=n_batch)


def _deinterleave(n=HEAD_DIM):
    return np.concatenate([np.arange(0, n, 2), np.arange(1, n, 2)])


def _proj_columns():
    de = _deinterleave()
    o_dq, o_gq, o_dk, o_dv, o_gk, o_gv, o_gate = 0, 1024, 2048, 3072, 4096, 4352, 4608
    cols = np.zeros(IN_W, np.int32)
    scale = np.ones(IN_W, np.float32)
    cols[GATE_OFF:GATE_OFF + 2048] = o_gate + np.arange(2048)
    for h in range(N_DIFF_HEADS):
        for c in range(2):
            dst = 128 * h + 64 * c
            cols[DQ_OFF + dst:DQ_OFF + dst + 64] = o_dq + dst + de
            cols[DK_OFF + dst:DK_OFF + dst + 64] = o_dk + dst + de
    scale[DQ_OFF:DQ_OFF + 1024] = HEAD_DIM ** -0.5
    for j in range(N_GQA_HEADS):
        cols[GQ_OFF + 64 * j:GQ_OFF + 64 * j + 64] = o_gq + 64 * j + de
    cols[DV_OFF:DV_OFF + 1024] = o_dv + np.arange(1024)
    for g in range(N_GQA_KV):
        cols[GKV_OFF + 128 * g:GKV_OFF + 128 * g + 64] = o_gk + 64 * g + de
        cols[GKV_OFF + 128 * g + 64:GKV_OFF + 128 * g + 128] = o_gv + 64 * g + np.arange(64)
    return cols, scale


def _rope_tables(S, pad_rows):
    rows = S // GRID_W
    row = jnp.repeat(jnp.arange(rows, dtype=F32), GRID_W)
    col = jnp.tile(jnp.arange(GRID_W, dtype=F32), rows)
    half = HEAD_DIM // 2
    inv_freq = ROPE_THETA ** (-jnp.arange(0, half, 2, dtype=F32) / half)
    ang = jnp.concatenate([row[:, None] * inv_freq, col[:, None] * inv_freq], axis=-1)
    cos, sin = jnp.cos(ang), jnp.sin(ang)
    reps = PROJ_TN // HEAD_DIM
    cos_t = jnp.tile(jnp.concatenate([cos, cos], axis=-1), (1, reps))
    sin_t = jnp.tile(jnp.concatenate([-sin, sin], axis=-1), (1, reps))
    cos_t = jnp.concatenate([cos_t, jnp.ones((pad_rows, PROJ_TN), F32)], axis=0)
    sin_t = jnp.concatenate([sin_t, jnp.zeros((pad_rows, PROJ_TN), F32)], axis=0)
    return cos_t, sin_t


def kernel(x, c, ctx, c_ctx, ada_w, ada_b, norm_attn_g, norm_ffn_g, w_in, q_norm_g, k_norm_g, diff_lambda,
           diff_subln_g, w_proj_diff, w_proj_gqa, w_out, ffn_w1, ffn_w3, ffn_w2, moe_router, moe_w1, moe_w3,
           moe_w2, final_norm_g):
    B, S, D = x.shape
    C = ctx.shape[1]
    depth = ada_w.shape[0]
    assert D == D_MODEL and depth == 2 and B + 1 <= MOD_ROWS
    assert w_in.shape[2] == IN_W and moe_router.shape[2] == N_EXPERTS
    tiles = _pick_tiles(B, S, C, ffn_w1.shape[2], moe_w1.shape[3])
    n_lat = B * S

    cvec = jnp.zeros((MOD_ROWS, D), F32).at[:B].set(c).at[B].set(c_ctx)
    mods_all = _ada_mods(cvec, ada_w, ada_b).reshape(depth, MOD_ROWS * N_MODS, 1, D)

    cols, col_scale = _proj_columns()
    de = _deinterleave()
    cos_t, sin_t = _rope_tables(S, tiles.tm_proj)
    blk = np.arange(PROJ_TN) // HEAD_DIM
    ones_bd = jnp.asarray(blk[:, None] == blk[None, :], BF16)
    col = np.arange(PROJ_TN)
    swap_bd = jnp.asarray(col[:, None] == (col[None, :] ^ (HEAD_DIM // 2)), BF16)
    is_key_lane = (np.arange(PROJ_TN) % (2 * HEAD_DIM)) < HEAD_DIM

    xs = jnp.concatenate([x.reshape(n_lat, D), ctx.reshape(B * C, D)], axis=0)
    for l in range(depth):
        last = l == depth - 1
        mods = mods_all[l]
        lam_init = 0.8 - 0.6 * math.exp(-0.3 * l)
        lq1, lk1, lq2, lk2 = diff_lambda[l]
        lam = (jnp.exp(jnp.sum(lq1 * lk1)) - jnp.exp(jnp.sum(lq2 * lk2)) + lam_init).reshape(1).astype(F32)
        w = (jnp.take(w_in[l], cols, axis=1) * col_scale).astype(BF16)
        qg = jnp.tile(q_norm_g[l][de] * HEAD_DIM ** -0.5, PROJ_TN // HEAD_DIM).reshape(1, PROJ_TN)
        kg = jnp.where(is_key_lane, jnp.tile(k_norm_g[l][de], PROJ_TN // HEAD_DIM), 1.0).reshape(1, PROJ_TN)
        subln = (diff_subln_g[l] * (1.0 - lam_init)).reshape(1, 2 * HEAD_DIM)

        p = _project(xs, mods, norm_attn_g[l].reshape(1, D), w, cos_t, sin_t, ones_bd, swap_bd, qg, kg,
                     tm=tiles.tm_proj, lat_tiles=n_lat // tiles.tm_proj, tiles_per_seq=S // tiles.tm_proj,
                     n_batch=B)
        yd, yg = _attention(p, lam, subln, B=B, S=S, C=C, tiles=tiles, ctx_queries=not last)
        rows = n_lat if last else n_lat + B * C
        xs = _mixer_out(xs, yd, yg, p, mods, w_proj_diff[l].astype(BF16), w_proj_gqa[l].astype(BF16),
                        w_out[l].astype(BF16), rows=rows, tm=tiles.tm_mix, tiles_per_seq=S // tiles.tm_mix,
                        n_batch=B)
        i = l // 2
        if l % 2 == 0:
            xs = _ffn(xs, mods, norm_ffn_g[l].reshape(1, D), ffn_w1[i].astype(BF16), ffn_w3[i].astype(BF16),
                      ffn_w2[i].astype(BF16), tm=tiles.tm_ffn, tf=tiles.tf_ffn,
                      tiles_per_seq=S // tiles.tm_ffn, n_batch=B)
        else:
            r_pad = jnp.zeros((D, 128), F32).at[:, :N_EXPERTS].set(moe_router[i])
            r_hi = r_pad.astype(BF16)
            r_lo = (r_pad - r_hi.astype(F32)).astype(BF16)
            xs = _moe(xs, mods, norm_ffn_g[l].reshape(1, D), r_hi, r_lo, final_norm_g.reshape(1, D),
                      moe_w1[i], moe_w3[i], moe_w2[i],
                      rows=rows, tm=tiles.tm_mix, tm_e=tiles.tm_moe, tf=tiles.tf_moe,
                      tiles_per_seq=S // tiles.tm_mix, n_batch=B)
    return xs.reshape(B, S, D)
```

```python
import functools
import math
from typing import NamedTuple

import numpy as np
import jax
import jax.numpy as jnp
from jax import lax
from jax.experimental import pallas as pl
from jax.experimental.pallas import tpu as pltpu

F32 = jnp.float32
BF16 = jnp.bfloat16

D_MODEL = 1024
HEAD_DIM = 64
N_DIFF_HEADS = 8
N_GQA_HEADS = 16
N_GQA_KV = 4
GQA_REP = N_GQA_HEADS // N_GQA_KV
N_EXPERTS = 8
GRID_W = 64
ROPE_THETA = 10000.0
EPS = 1e-6
N_MODS = 6
MOD_ROWS = 16
NEG_BIG = -1e30

GATE_OFF = 0
DQ_OFF = 2048
GQ_OFF = 3072
DK_OFF = 4096
DV_OFF = 5120
GKV_OFF = 6144
IN_W = 6656
PROJ_TN = 512
ROPE_TILES = (4, 5, 8, 9)
QNORM_TILES = (6, 7)
KV_TILE = 12

VMEM_LIMIT = 52 * 1024 * 1024


class Tiles(NamedTuple):
    tm_proj: int
    tq: int
    tk: int
    tm_mix: int
    tm_ffn: int
    tf_ffn: int
    tf_moe: int
    tm_moe: int


def _largest_divisor(n, candidates):
    for c in candidates:
        if n % c == 0:
            return c
    raise ValueError(f"no tile in {candidates} divides {n}")


def _pick_tiles(B, S, C, d_ff, d_ff_e):
    rows_common = math.gcd(S, B * C)
    tm = _largest_divisor(rows_common, (512, 256, 128))
    return Tiles(
        tm_proj=_largest_divisor(rows_common, (1024, 512, 256, 128)),
        tq=C,
        tk=_largest_divisor(math.gcd(S, C), (256, 128)),
        tm_mix=tm,
        tm_ffn=_largest_divisor(rows_common, (1024, 512, 256, 128)),
        tf_ffn=_largest_divisor(d_ff, (1408, 1024, 512, 256, 128)),
        tf_moe=_largest_divisor(d_ff_e, (896, 512, 256, 128)),
        tm_moe=_largest_divisor(2 * B * S, (512, 256)),
    )


def _split_bf16(v):
    hi = v.astype(BF16)
    lo = (v - hi.astype(F32)).astype(BF16)
    return hi, lo


def _ada_kernel(c_ref, w_ref, b_ref, o_ref):
    c = c_ref[...]
    s = c / (1.0 + jnp.exp(-c))
    s_hi, s_lo = _split_bf16(s)
    w_hi, w_lo = _split_bf16(w_ref[...])
    acc = jnp.dot(s_hi, w_hi, preferred_element_type=F32)
    acc += jnp.dot(s_lo, w_hi, preferred_element_type=F32)
    acc += jnp.dot(s_hi, w_lo, preferred_element_type=F32)
    o_ref[...] = acc + b_ref[...]


def _ada_mods(cvec, ada_w, ada_b):
    depth, d, n = ada_w.shape
    tn = _largest_divisor(n, (1536, 1024, 512))
    return pl.pallas_call(
        _ada_kernel,
        grid=(depth, n // tn),
        in_specs=[
            pl.BlockSpec((MOD_ROWS, d), lambda l, j: (0, 0)),
            pl.BlockSpec((None, d, tn), lambda l, j: (l, 0, j)),
            pl.BlockSpec((None, 1, tn), lambda l, j: (l, 0, j)),
        ],
        out_specs=pl.BlockSpec((None, MOD_ROWS, tn), lambda l, j: (l, 0, j)),
        out_shape=jax.ShapeDtypeStruct((depth, MOD_ROWS, n), F32),
        compiler_params=pltpu.CompilerParams(
            dimension_semantics=("arbitrary", "arbitrary"), vmem_limit_bytes=VMEM_LIMIT),
        name="ada_mods",
    )(cvec, ada_w, ada_b.reshape(depth, 1, n))


def _modulated_norm(x, g, scale, shift):
    ms = jnp.mean(x * x, axis=-1, keepdims=True)
    return (x * lax.rsqrt(ms + EPS) * g) * (1.0 + scale) + shift


def _silu(a):
    return a / (1.0 + jnp.exp(-a))


def _mod_spec(k, row_to_mod):
    return pl.BlockSpec((None, 1, D_MODEL), lambda i, *_: (row_to_mod(i) * N_MODS + k, 0, 0))


def _rope(z, cos, sin_signed, swap_ref):
    partner = jnp.dot(z.astype(BF16), swap_ref[...], preferred_element_type=F32)
    return z * cos + partner * sin_signed


def _head_rms(z, ones_ref, gain):
    ss = jnp.dot((z * z).astype(BF16), ones_ref[...], preferred_element_type=F32)
    return z * lax.rsqrt(ss * (1.0 / HEAD_DIM) + EPS) * gain


def _proj_kernel(x_ref, g_ref, sc_ref, sh_ref, w_ref, cos_ref, sin_ref, ones_ref, swap_ref, qg_ref, kg_ref,
                 o_ref, h_scr):
    j = pl.program_id(1)

    @pl.when(j == 0)
    def _():
        h_scr[...] = _modulated_norm(x_ref[...], g_ref[...], sc_ref[...], sh_ref[...]).astype(BF16)

    def project():
        z = jnp.dot(h_scr[...], w_ref[...], preferred_element_type=F32)
        return z, lax.broadcasted_iota(jnp.int32, z.shape, 1)

    is_rope = functools.reduce(jnp.logical_or, [j == t for t in ROPE_TILES])
    is_qnorm = functools.reduce(jnp.logical_or, [j == t for t in QNORM_TILES])
    is_kv = j == KV_TILE
    is_plain = jnp.logical_not(is_rope | is_qnorm | is_kv)

    @pl.when(is_plain)
    def _():
        z, _ = project()
        o_ref[...] = z.astype(BF16)

    @pl.when(is_rope)
    def _():
        z, lane = project()
        o_ref[...] = _rope(z, cos_ref[...], sin_ref[...], swap_ref).astype(BF16)

    @pl.when(is_qnorm)
    def _():
        z, lane = project()
        zn = _head_rms(z, ones_ref, qg_ref[...])
        o_ref[...] = _rope(zn, cos_ref[...], sin_ref[...], swap_ref).astype(BF16)

    @pl.when(is_kv)
    def _():
        z, lane = project()
        zn = _head_rms(z, ones_ref, kg_ref[...])
        zr = _rope(zn, cos_ref[...], sin_ref[...], swap_ref)
        is_key_lane = (lane & HEAD_DIM) == 0
        o_ref[...] = jnp.where(is_key_lane, zr, z).astype(BF16)


def _project(xs, mods, norm_g, w, cos_t, sin_t, ones_bd, swap_bd, qg, kg, *, tm, lat_tiles, tiles_per_seq, n_batch):
    rows = xs.shape[0]
    row_to_mod = lambda i: jnp.minimum(i // tiles_per_seq, n_batch)
    rope_row = lambda i: jnp.where(i < lat_tiles, i % tiles_per_seq, tiles_per_seq)
    const = lambda i, j: (0, 0)
    return pl.pallas_call(
        _proj_kernel,
        grid=(rows // tm, IN_W // PROJ_TN),
        in_specs=[
            pl.BlockSpec((tm, D_MODEL), lambda i, j: (i, 0)),
            pl.BlockSpec((1, D_MODEL), const),
            _mod_spec(1, row_to_mod),
            _mod_spec(0, row_to_mod),
            pl.BlockSpec((D_MODEL, PROJ_TN), lambda i, j: (0, j)),
            pl.BlockSpec((tm, PROJ_TN), lambda i, j: (rope_row(i), 0)),
            pl.BlockSpec((tm, PROJ_TN), lambda i, j: (rope_row(i), 0)),
            pl.BlockSpec((PROJ_TN, PROJ_TN), const),
            pl.BlockSpec((PROJ_TN, PROJ_TN), const),
            pl.BlockSpec((1, PROJ_TN), const),
            pl.BlockSpec((1, PROJ_TN), const),
        ],
        out_specs=pl.BlockSpec((tm, PROJ_TN), lambda i, j: (i, j)),
        out_shape=jax.ShapeDtypeStruct((rows, IN_W), BF16),
        scratch_shapes=[pltpu.VMEM((tm, D_MODEL), BF16)],
        compiler_params=pltpu.CompilerParams(
            dimension_semantics=("arbitrary", "arbitrary"), vmem_limit_bytes=VMEM_LIMIT),
        name="in_proj",
    )(xs, norm_g, mods, mods, w, cos_t, sin_t, ones_bd, swap_bd, qg, kg)


STAB_LANE = HEAD_DIM
L_FLOOR = 1e-26
L_CEIL = 1e30
FP8 = jnp.float8_e4m3fn
FP8_TARGET = 256.0
FP8_BOUND_SLACK = 1.125
LOG2_E = 1.4426950408889634
DIFF_VT_ROWS = 2 * HEAD_DIM + 16
DIFF_HEADS_PER_STEP = 2
GQA_GROUPS_PER_STEP = 1


def _aug_keys(k):
    lane = lax.broadcasted_iota(jnp.int32, k.shape, 1)
    in_key = lane < HEAD_DIM
    sq = jnp.where(in_key, k * k, 0.0).astype(BF16)
    norms = jnp.dot(sq, jnp.ones((k.shape[1], k.shape[1]), BF16), preferred_element_type=F32)
    ka = jnp.where(in_key, k, jnp.where(lane == STAB_LANE, 1.0, 0.0)).astype(BF16)
    return ka, jnp.max(norms, axis=0, keepdims=True)


def _pow2_floor(x):
    bits = lax.bitcast_convert_type(x, jnp.int32) & jnp.int32(0x7F800000)
    return lax.bitcast_convert_type(bits, F32)


def _fp8_keys(ka_scr, k8_scr, kscale_scr, kmax, *, n_chunks, tk):
    for r, kmax2 in enumerate(kmax):
        scale = _pow2_floor(FP8_TARGET * lax.rsqrt(jnp.maximum(kmax2, 1e-30)))
        kscale_scr[r] = scale
        for c in range(n_chunks):
            k = ka_scr[r, c * tk:(c + 1) * tk, :].astype(F32)
            lane = lax.broadcasted_iota(jnp.int32, k.shape, 1)
            k8_scr[r, c * tk:(c + 1) * tk, :] = jnp.where(lane == STAB_LANE, FP8_TARGET, k * scale).astype(FP8)


def _query_mats(slab, kmax2, kscale):
    lane = lax.broadcasted_iota(jnp.int32, slab.shape, 1)
    in_key = lane < HEAD_DIM
    sq = jnp.where(in_key, slab * slab, 0.0).astype(BF16)
    norms = jnp.dot(sq, jnp.ones((slab.shape[1], slab.shape[1]), BF16), preferred_element_type=F32)
    nk = norms * kmax2
    bound = FP8_BOUND_SLACK * nk * lax.rsqrt(nk + 1e-30)
    qscale = _pow2_floor(FP8_TARGET * lax.rsqrt(jnp.maximum(jnp.max(norms, axis=0, keepdims=True), 1e-30)))
    both = kscale * qscale
    plain = jnp.where(in_key, slab, 0.0)
    shifted = jnp.where(lane == STAB_LANE, -bound * (both * (1.0 / FP8_TARGET)), plain * qscale).astype(FP8)
    return shifted, plain.astype(BF16), (LOG2_E / both)[:, :1]


def _attend(k8_scr, vt_scr, mats, acc_scr, *, first, n_chunks, tk):
    def scores(r, c, qmat):
        off = pl.multiple_of(c * tk, tk)
        return lax.dot_general(k8_scr[r * k8_scr.shape[0] // len(mats), pl.ds(off, tk), :], qmat,
                               (((1,), (1,)), ((), ())), preferred_element_type=F32)

    acc_scr[...] = jnp.zeros(acc_scr.shape, F32)

    def all_scores(c):
        return tuple(scores(r, c, shifted) for r, (shifted, _, _) in enumerate(mats))

    def accumulate(c, s_all):
        for r, s in enumerate(s_all):
            vt = vt_scr[r * vt_scr.shape[0] // len(mats), c]
            p = jnp.exp2(s * mats[r][2]).astype(BF16)
            acc_scr[r] += jnp.dot(vt, p, preferred_element_type=F32)

    def fast_body(c, s_cur):
        s_next = all_scores(c + 1)
        accumulate(c, s_cur)
        return s_next

    trips = n_chunks - 1 - first
    s_last = lax.fori_loop(first, n_chunks - 1, fast_body, all_scores(first),
                           unroll=_largest_divisor(trips, (16, 8, 4, 2, 1)) if trips > 0 else 1)
    accumulate(n_chunks - 1, s_last)


def _attend_fallback(ka_scr, vt_scr, mats, m_scr, acc_scr, *, first, n_chunks, tk):
    m_scr[...] = jnp.full(m_scr.shape, NEG_BIG, F32)
    acc_scr[...] = jnp.zeros(acc_scr.shape, F32)

    def body(c, carry):
        off = pl.multiple_of(c * tk, tk)
        for r, (_, plain, _) in enumerate(mats):
            s = lax.dot_general(ka_scr[r * ka_scr.shape[0] // len(mats), pl.ds(off, tk), :], plain, (((1,), (1,)), ((), ())),
                                preferred_element_type=F32)
            m_old = m_scr[r]
            m_new = jnp.maximum(m_old, jnp.max(s, axis=0, keepdims=True))
            p = jnp.exp(s - m_new).astype(BF16)
            vt = vt_scr[r * vt_scr.shape[0] // len(mats), c]
            acc_scr[r] = acc_scr[r] * jnp.exp(m_old - m_new) + jnp.dot(vt, p, preferred_element_type=F32)
            m_scr[r] = m_new
        return carry

    lax.fori_loop(first, n_chunks, body, 0)


def _softmax_tile(ka_scr, k8_scr, vt_scr, kmax_scr, kscale_scr, slabs, m_scr, acc_scr, l_row,
                  *, n_lat_chunks, n_chunks, tk, ctx_tile):
    per_key = lambda scr, r: scr[r * scr.shape[0] // len(slabs)]
    mats = [_query_mats(slab, per_key(kmax_scr, r), per_key(kscale_scr, r)) for r, slab in enumerate(slabs)]

    def run(first):
        _attend(k8_scr, vt_scr, mats, acc_scr, first=first, n_chunks=n_chunks, tk=tk)
        dens = [acc_scr[r, l_row:l_row + 1, :] for r in range(len(slabs))]
        l_min = functools.reduce(jnp.minimum, [jnp.min(d) for d in dens])
        l_max = functools.reduce(jnp.maximum, [jnp.max(d) for d in dens])

        @pl.when(jnp.logical_not((l_min >= L_FLOOR) & (l_max <= L_CEIL)))
        def _():
            _attend_fallback(ka_scr, vt_scr, mats, m_scr, acc_scr, first=first, n_chunks=n_chunks, tk=tk)

    if ctx_tile is None:
        run(0)
    else:
        pl.when(jnp.logical_not(ctx_tile))(functools.partial(run, 0))
        pl.when(ctx_tile)(functools.partial(run, n_lat_chunks))


def _chunk_rows(c, n_lat_chunks, tk, lat_ref, ctx_ref):
    if c < n_lat_chunks:
        return lat_ref[c * tk:(c + 1) * tk, :]
    return ctx_ref[(c - n_lat_chunks) * tk:(c - n_lat_chunks + 1) * tk, :]


def _diff_attn_kernel(lam_ref, q_ref, kl_ref, vl_ref, kc_ref, vc_ref, g_ref, o_ref,
                      ka_scr, k8_scr, vt_scr, kmax_scr, kscale_scr, m_scr, acc_scr,
                      *, nq_lat, n_lat_chunks, n_chunks, tk, ctx_queries):
    dv = 2 * HEAD_DIM
    heads = [slice(a * dv, (a + 1) * dv) for a in range(DIFF_HEADS_PER_STEP)]

    @pl.when(pl.program_id(2) == 0)
    def _():
        kmax = [None] * (2 * len(heads))
        for c in range(n_chunks):
            k_all = _chunk_rows(c, n_lat_chunks, tk, kl_ref, kc_ref).astype(F32)
            v_all = _chunk_rows(c, n_lat_chunks, tk, vl_ref, vc_ref)
            for a, cols in enumerate(heads):
                k = k_all[:, cols]
                for r, keys in ((2 * a, k), (2 * a + 1, pltpu.roll(k, HEAD_DIM, 1))):
                    ka, n2 = _aug_keys(keys)
                    ka_scr[r, c * tk:(c + 1) * tk, :] = ka
                    kmax[r] = n2 if kmax[r] is None else jnp.maximum(kmax[r], n2)
                vt_scr[a, c, :dv, :] = v_all[:, cols].T
                vt_scr[a, c, dv:, :] = jnp.ones((DIFF_VT_ROWS - dv, tk), BF16)
        for r, n2 in enumerate(kmax):
            kmax_scr[r] = n2
        _fp8_keys(ka_scr, k8_scr, kscale_scr, kmax, n_chunks=n_chunks, tk=tk)

    q_all = q_ref[...].astype(F32)
    slabs = []
    for cols in heads:
        slabs += [q_all[:, cols], pltpu.roll(q_all[:, cols], HEAD_DIM, 1)]
    ctx_tile = (pl.program_id(2) == nq_lat) if ctx_queries else None
    _softmax_tile(ka_scr, k8_scr, vt_scr, kmax_scr, kscale_scr, slabs, m_scr, acc_scr, dv,
                  n_lat_chunks=n_lat_chunks, n_chunks=n_chunks, tk=tk, ctx_tile=ctx_tile)
    for a, cols in enumerate(heads):
        a1, a2 = acc_scr[2 * a], acc_scr[2 * a + 1]
        yt = a1[:dv] * (1.0 / a1[dv:dv + 1]) - lam_ref[0] * (a2[:dv] * (1.0 / a2[dv:dv + 1]))
        y = yt.T
        ms = jnp.mean(y * y, axis=-1, keepdims=True)
        o_ref[:, cols] = (y * lax.rsqrt(ms + EPS) * g_ref[...]).astype(BF16)


def _gqa_attn_kernel(q_ref, kvl_ref, kvc_ref, o_ref, ka_scr, k8_scr, vt_scr, kmax_scr, kscale_scr, m_scr, acc_scr,
                     *, nq_lat, n_lat_chunks, n_chunks, tk, ctx_queries):
    w2 = 2 * HEAD_DIM
    groups = [slice(g * w2, (g + 1) * w2) for g in range(GQA_GROUPS_PER_STEP)]
    n_heads = GQA_REP * len(groups)

    @pl.when(pl.program_id(2) == 0)
    def _():
        kmax = [None] * len(groups)
        for c in range(n_chunks):
            kv_all = _chunk_rows(c, n_lat_chunks, tk, kvl_ref, kvc_ref)
            for g, cols in enumerate(groups):
                kv = kv_all[:, cols]
                ka, n2 = _aug_keys(kv.astype(F32))
                ka_scr[g, c * tk:(c + 1) * tk, :] = ka
                kmax[g] = n2 if kmax[g] is None else jnp.maximum(kmax[g], n2)
                t = kv.T
                row = lax.broadcasted_iota(jnp.int32, t.shape, 0)
                vt_scr[g, c] = jnp.where(row < HEAD_DIM, jnp.ones_like(t), t)
        for g, n2 in enumerate(kmax):
            kmax_scr[g] = n2
        _fp8_keys(ka_scr, k8_scr, kscale_scr, kmax, n_chunks=n_chunks, tk=tk)

    qf = q_ref[...].astype(F32)
    tq = qf.shape[0]
    lane = lax.broadcasted_iota(jnp.int32, (tq, w2), 1)
    slabs = []
    for r in range(n_heads):
        slab = qf[:, (r // 2) * w2:(r // 2 + 1) * w2]
        slabs.append(pltpu.roll(slab, HEAD_DIM, 1) if r % 2 else slab)
    ctx_tile = (pl.program_id(2) == nq_lat) if ctx_queries else None
    _softmax_tile(ka_scr, k8_scr, vt_scr, kmax_scr, kscale_scr, slabs, m_scr, acc_scr, 0,
                  n_lat_chunks=n_lat_chunks, n_chunks=n_chunks, tk=tk, ctx_tile=ctx_tile)

    def head_out(r):
        acc = acc_scr[r]
        return (acc * (1.0 / acc[0:1])).T

    for j in range(n_heads // 2):
        pair = jnp.where(lane < HEAD_DIM, pltpu.roll(head_out(2 * j), HEAD_DIM, 1), head_out(2 * j + 1))
        o_ref[:, j * w2:(j + 1) * w2] = pair.astype(BF16)


def _attention(p, lam, subln_gain, *, B, S, C, tiles, ctx_queries):
    tq, tk = tiles.tq, tiles.tk
    nq_lat = S // tq
    nq = nq_lat + (1 if ctx_queries else 0)
    ctx_blk0 = (B * S) // C
    rows_out = B * S + (B * C if ctx_queries else 0)
    T = S + C
    statics = dict(nq_lat=nq_lat, n_lat_chunks=S // tk, n_chunks=T // tk, tk=tk, ctx_queries=ctx_queries)

    def q_row(b, qi):
        if not ctx_queries:
            return b * nq_lat + qi
        return jnp.where(qi < nq_lat, b * nq_lat + qi, ctx_blk0 + b)

    cparams = pltpu.CompilerParams(
        dimension_semantics=("arbitrary", "arbitrary", "arbitrary"), vmem_limit_bytes=VMEM_LIMIT)
    w2 = 2 * HEAD_DIM
    hps = DIFF_HEADS_PER_STEP
    wd = hps * w2
    yd = pl.pallas_call(
        functools.partial(_diff_attn_kernel, **statics),
        grid=(B, N_DIFF_HEADS // hps, nq),
        in_specs=[
            pl.BlockSpec(memory_space=pltpu.SMEM),
            pl.BlockSpec((tq, wd), lambda b, h, qi: (q_row(b, qi), DQ_OFF // wd + h)),
            pl.BlockSpec((S, wd), lambda b, h, qi: (b, DK_OFF // wd + h)),
            pl.BlockSpec((S, wd), lambda b, h, qi: (b, DV_OFF // wd + h)),
            pl.BlockSpec((C, wd), lambda b, h, qi: (ctx_blk0 + b, DK_OFF // wd + h)),
            pl.BlockSpec((C, wd), lambda b, h, qi: (ctx_blk0 + b, DV_OFF // wd + h)),
            pl.BlockSpec((1, w2), lambda b, h, qi: (0, 0)),
        ],
        out_specs=pl.BlockSpec((tq, wd), lambda b, h, qi: (q_row(b, qi), h)),
        out_shape=jax.ShapeDtypeStruct((rows_out, N_DIFF_HEADS * w2), BF16),
        scratch_shapes=[
            pltpu.VMEM((2 * hps, T, w2), BF16),
            pltpu.VMEM((2 * hps, T, w2), FP8),
            pltpu.VMEM((hps, T // tk, DIFF_VT_ROWS, tk), BF16),
            pltpu.VMEM((2 * hps, 1, w2), F32),
            pltpu.VMEM((2 * hps, 1, w2), F32),
            pltpu.VMEM((2 * hps, 1, tq), F32),
            pltpu.VMEM((2 * hps, DIFF_VT_ROWS, tq), F32),
        ],
        compiler_params=cparams,
        name="diff_attn",
    )(lam, p, p, p, p, p, subln_gain)

    gps = GQA_GROUPS_PER_STEP
    wq = gps * GQA_REP * HEAD_DIM
    wkv = gps * w2
    yg = pl.pallas_call(
        functools.partial(_gqa_attn_kernel, **statics),
        grid=(B, N_GQA_KV // gps, nq),
        in_specs=[
            pl.BlockSpec((tq, wq), lambda b, g, qi: (q_row(b, qi), GQ_OFF // wq + g)),
            pl.BlockSpec((S, wkv), lambda b, g, qi: (b, GKV_OFF // wkv + g)),
            pl.BlockSpec((C, wkv), lambda b, g, qi: (ctx_blk0 + b, GKV_OFF // wkv + g)),
        ],
        out_specs=pl.BlockSpec((tq, wq), lambda b, g, qi: (q_row(b, qi), g)),
        out_shape=jax.ShapeDtypeStruct((rows_out, N_GQA_HEADS * HEAD_DIM), BF16),
        scratch_shapes=[
            pltpu.VMEM((gps, T, w2), BF16),
            pltpu.VMEM((gps, T, w2), FP8),
            pltpu.VMEM((gps, T // tk, w2, tk), BF16),
            pltpu.VMEM((gps, 1, w2), F32),
            pltpu.VMEM((gps, 1, w2), F32),
            pltpu.VMEM((gps * GQA_REP, 1, tq), F32),
            pltpu.VMEM((gps * GQA_REP, w2, tq), F32),
        ],
        compiler_params=cparams,
        name="gqa_attn",
    )(p, p, p)
    return yd, yg


def _mixer_kernel(x_ref, yd_ref, yg_ref, gd_ref, gg_ref, gt_ref, wpd_ref, wpg_ref, wo_ref, o_ref):
    pd = jnp.dot(yd_ref[...], wpd_ref[...], preferred_element_type=F32)
    pg = jnp.dot(yg_ref[...], wpg_ref[...], preferred_element_type=F32)
    m = jax.nn.sigmoid(gd_ref[...].astype(F32)) * pd + jax.nn.sigmoid(gg_ref[...].astype(F32)) * pg
    o_ref[...] = x_ref[...] + gt_ref[...] * jnp.dot(m.astype(BF16), wo_ref[...], preferred_element_type=F32)


def _mixer_out(xs, yd, yg, p, mods, wpd, wpg, wo, *, rows, tm, tiles_per_seq, n_batch):
    row_to_mod = lambda i: jnp.minimum(i // tiles_per_seq, n_batch)
    row_tile = pl.BlockSpec((tm, D_MODEL), lambda i: (i, 0))
    weight = pl.BlockSpec((D_MODEL, D_MODEL), lambda i: (0, 0))
    return pl.pallas_call(
        _mixer_kernel,
        grid=(rows // tm,),
        in_specs=[
            row_tile, row_tile, row_tile,
            pl.BlockSpec((tm, D_MODEL), lambda i: (i, GATE_OFF // D_MODEL)),
            pl.BlockSpec((tm, D_MODEL), lambda i: (i, GATE_OFF // D_MODEL + 1)),
            _mod_spec(2, row_to_mod),
            weight, weight, weight,
        ],
        out_specs=row_tile,
        out_shape=jax.ShapeDtypeStruct((rows, D_MODEL), F32),
        compiler_params=pltpu.CompilerParams(
            dimension_semantics=("arbitrary",), vmem_limit_bytes=VMEM_LIMIT),
        name="mixer_out",
    )(xs, yd, yg, p, p, mods, wpd, wpg, wo)


def _ffn_kernel(x_ref, g_ref, sc_ref, sh_ref, gt_ref, w1_ref, w3_ref, w2_ref, o_ref, h_scr, acc_scr):
    f = pl.program_id(1)

    @pl.when(f == 0)
    def _():
        h_scr[...] = _modulated_norm(x_ref[...], g_ref[...], sc_ref[...], sh_ref[...]).astype(BF16)
        acc_scr[...] = jnp.zeros_like(acc_scr)

    h = h_scr[...]
    a = jnp.dot(h, w1_ref[...], preferred_element_type=F32)
    b = jnp.dot(h, w3_ref[...], preferred_element_type=F32)
    acc_scr[...] += jnp.dot((_silu(a) * b).astype(BF16), w2_ref[...], preferred_element_type=F32)

    @pl.when(f == pl.num_programs(1) - 1)
    def _():
        o_ref[...] = x_ref[...] + gt_ref[...] * acc_scr[...]


def _ffn(xs, mods, norm_g, w1, w3, w2, *, tm, tf, tiles_per_seq, n_batch):
    rows = xs.shape[0]
    d_ff = w1.shape[1]
    row_to_mod = lambda i: jnp.minimum(i // tiles_per_seq, n_batch)
    row_tile = pl.BlockSpec((tm, D_MODEL), lambda i, f: (i, 0))
    return pl.pallas_call(
        _ffn_kernel,
        grid=(rows // tm, d_ff // tf),
        in_specs=[
            row_tile,
            pl.BlockSpec((1, D_MODEL), lambda i, f: (0, 0)),
            _mod_spec(4, row_to_mod), _mod_spec(3, row_to_mod), _mod_spec(5, row_to_mod),
            pl.BlockSpec((D_MODEL, tf), lambda i, f: (0, f)),
            pl.BlockSpec((D_MODEL, tf), lambda i, f: (0, f)),
            pl.BlockSpec((tf, D_MODEL), lambda i, f: (f, 0)),
        ],
        out_specs=row_tile,
        out_shape=jax.ShapeDtypeStruct((rows, D_MODEL), F32),
        scratch_shapes=[pltpu.VMEM((tm, D_MODEL), BF16), pltpu.VMEM((tm, D_MODEL), F32)],
        compiler_params=pltpu.CompilerParams(
            dimension_semantics=("arbitrary", "arbitrary"), vmem_limit_bytes=VMEM_LIMIT),
        name="ffn_swiglu",
    )(xs, norm_g, mods, mods, mods, w1, w3, w2)


def _top2_gates(logits):
    lane = lax.broadcasted_iota(jnp.int32, logits.shape, 1)
    n_lanes = logits.shape[1]
    lg = jnp.where(lane < N_EXPERTS, logits, NEG_BIG)
    m1 = jnp.max(lg, axis=-1, keepdims=True)
    i1 = jnp.min(jnp.where(lg == m1, lane, n_lanes), axis=-1, keepdims=True)
    lg2 = jnp.where(lane == i1, NEG_BIG, lg)
    m2 = jnp.max(lg2, axis=-1, keepdims=True)
    i2 = jnp.min(jnp.where(lg2 == m2, lane, n_lanes), axis=-1, keepdims=True)
    e2 = jnp.exp(m2 - m1)
    w_top = 1.0 / (1.0 + e2)
    idx = jnp.where(lane == 0, i1, jnp.where(lane == 1, i2, 0))
    wts = jnp.where(lane == 0, w_top, jnp.where(lane == 1, e2 * w_top, 0.0))
    return idx, wts


def _router_kernel(x_ref, g_ref, sc_ref, sh_ref, rhi_ref, rlo_ref, h_ref, idx_ref, wts_ref):
    h = _modulated_norm(x_ref[...], g_ref[...], sc_ref[...], sh_ref[...])
    h_ref[...] = h
    h_hi, h_lo = _split_bf16(h)
    logits = jnp.dot(h_hi, rhi_ref[...], preferred_element_type=F32)
    logits += jnp.dot(h_lo, rhi_ref[...], preferred_element_type=F32)
    logits += jnp.dot(h_hi, rlo_ref[...], preferred_element_type=F32)
    idx_ref[...], wts_ref[...] = _top2_gates(logits)


def _router(xs, mods, norm_g, r_hi, r_lo, *, rows, tm, tiles_per_seq, n_batch):
    row_to_mod = lambda i: jnp.minimum(i // tiles_per_seq, n_batch)
    row_tile = pl.BlockSpec((tm, D_MODEL), lambda i: (i, 0))
    lanes = pl.BlockSpec((tm, 128), lambda i: (i, 0))
    router = pl.BlockSpec((D_MODEL, 128), lambda i: (0, 0))
    return pl.pallas_call(
        _router_kernel,
        grid=(rows // tm,),
        in_specs=[row_tile, pl.BlockSpec((1, D_MODEL), lambda i: (0, 0)),
                  _mod_spec(4, row_to_mod), _mod_spec(3, row_to_mod), router, router],
        out_specs=[row_tile, lanes, lanes],
        out_shape=[jax.ShapeDtypeStruct((rows, D_MODEL), F32),
                   jax.ShapeDtypeStruct((rows, 128), jnp.int32),
                   jax.ShapeDtypeStruct((rows, 128), F32)],
        compiler_params=pltpu.CompilerParams(
            dimension_semantics=("arbitrary",), vmem_limit_bytes=VMEM_LIMIT),
        name="moe_router",
    )(xs, norm_g, mods, mods, r_hi, r_lo)


def _route_plan(idx, n_tok, tm_e):
    n_pairs = 2 * n_tok
    n_tiles = n_pairs // tm_e + N_EXPERTS + 1
    n_rows = n_tiles * tm_e
    e_flat = idx.reshape(n_pairs)
    order = jnp.argsort(e_flat, stable=True).astype(jnp.int32)
    counts = jnp.sum((e_flat[:, None] == jnp.arange(N_EXPERTS, dtype=jnp.int32)[None, :]).astype(jnp.int32),
                     axis=0)
    padded = ((counts + tm_e - 1) // tm_e) * tm_e
    ends = jnp.cumsum(padded)
    tile_start = jnp.arange(n_tiles, dtype=jnp.int32) * tm_e
    tile_expert = jnp.minimum(jnp.sum((tile_start[:, None] >= ends[None, :]).astype(jnp.int32), axis=1),
                              N_EXPERTS - 1).astype(jnp.int32)
    n_used = (ends[-1] // tm_e).astype(jnp.int32).reshape(1)
    row = jnp.arange(n_rows, dtype=jnp.int32)
    row_expert = jnp.repeat(tile_expert, tm_e)
    offset = row - (ends - padded)[row_expert]
    valid = (offset < counts[row_expert]) & (row < ends[-1])
    pair = order[jnp.clip((jnp.cumsum(counts) - counts)[row_expert] + offset, 0, n_pairs - 1)]
    tok, choice = pair // 2, pair % 2
    src = jnp.where(valid, tok, 0)
    dump = n_pairs + jnp.cumsum(jnp.logical_not(valid).astype(jnp.int32)) - 1
    dst = jnp.where(valid, choice * n_tok + tok, dump)
    return tile_expert, n_used, src.reshape(n_tiles, 1, tm_e), dst.reshape(n_tiles, 1, tm_e)


def _experts_kernel(te_ref, nu_ref, src_ref, src_next_ref, dst_ref, h_hbm, w1_ref, w3_ref, w2_ref, y_hbm,
                    xbuf, xb_scr, acc_scr, ybuf, gsem, ssem, *, tm_e, n_chunks):
    t = pl.program_id(0)
    f = pl.program_id(1)
    nf = pl.num_programs(1)
    n_used = nu_ref[0]
    slot = t % 2

    def gather_rows(idx_ref, s):
        def body(r, carry):
            pltpu.make_async_copy(h_hbm.at[pl.ds(idx_ref[0, r], 1)], xbuf.at[s, pl.ds(r, 1)], gsem.at[s]).start()
            return carry
        lax.fori_loop(0, tm_e, body, 0, unroll=8)

    def wait_gather(s):
        pltpu.make_async_copy(h_hbm.at[pl.ds(0, tm_e)], xbuf.at[s], gsem.at[s]).wait()

    def scatter_rows(s):
        def body(r, carry):
            pltpu.make_async_copy(ybuf.at[s, pl.ds(r, 1)], y_hbm.at[pl.ds(dst_ref[0, r], 1)], ssem.at[s]).start()
            return carry
        lax.fori_loop(0, tm_e, body, 0, unroll=8)

    def wait_scatter(s):
        pltpu.make_async_copy(ybuf.at[s], y_hbm.at[pl.ds(0, tm_e)], ssem.at[s]).wait()

    @pl.when((f == 0) & (t <= n_used))
    def _():
        @pl.when(t == 0)
        def _():
            gather_rows(src_ref, 0)
        for s in range(2):
            @pl.when(slot == s)
            def _():
                wait_gather(s)

                @pl.when(t < n_used)
                def _():
                    xb_scr[...] = xbuf[s].astype(BF16)
                    acc_scr[...] = jnp.zeros_like(acc_scr)

    @pl.when(t < n_used)
    def _():
        x = xb_scr[...]
        a = jnp.dot(x, w1_ref[...].astype(BF16), preferred_element_type=F32)
        b = jnp.dot(x, w3_ref[...].astype(BF16), preferred_element_type=F32)
        acc_scr[...] += jnp.dot((_silu(a) * b).astype(BF16), w2_ref[...].astype(BF16),
                                preferred_element_type=F32)
        share = tm_e // n_chunks
        for i in range(share):
            r = f * share + i
            pltpu.make_async_copy(h_hbm.at[pl.ds(src_next_ref[0, r], 1)],
                                  xbuf.at[1 - slot, pl.ds(r, 1)], gsem.at[1 - slot]).start()

    @pl.when(f == nf - 1)
    def _():
        for s in range(2):
            @pl.when(slot == s)
            def _():
                @pl.when(t >= 2)
                def _():
                    wait_scatter(s)

                @pl.when(t < n_used)
                def _():
                    ybuf[s] = acc_scr[...]

                @pl.when(t >= n_used)
                def _():
                    ybuf[s] = jnp.zeros(ybuf.shape[1:], F32)

                scatter_rows(s)

                @pl.when(t == pl.num_programs(0) - 1)
                def _():
                    wait_scatter(s)
                    wait_scatter(1 - s)


def _experts(h, plan, w1, w3, w2, *, n_tok, tm_e, tf):
    tile_expert, n_used, src, dst = plan
    n_tiles = src.shape[0]
    d_ff = w1.shape[2]
    nf = d_ff // tf
    n_out = n_tiles * tm_e

    def chunk(t, f, nu):
        return jnp.where(t < nu[0], f, nf - 1)

    smem_tile = lambda shift: pl.BlockSpec(
        (None, 1, tm_e), lambda t, f, te, nu: (jnp.minimum(t + shift, n_tiles - 1), 0, 0),
        memory_space=pltpu.SMEM)
    grid_spec = pltpu.PrefetchScalarGridSpec(
        num_scalar_prefetch=2,
        grid=(n_tiles, nf),
        in_specs=[
            smem_tile(0), smem_tile(1), smem_tile(0),
            pl.BlockSpec(memory_space=pl.ANY),
            pl.BlockSpec((None, D_MODEL, tf), lambda t, f, te, nu: (te[t], 0, chunk(t, f, nu))),
            pl.BlockSpec((None, D_MODEL, tf), lambda t, f, te, nu: (te[t], 0, chunk(t, f, nu))),
            pl.BlockSpec((None, tf, D_MODEL), lambda t, f, te, nu: (te[t], chunk(t, f, nu), 0)),
        ],
        out_specs=pl.BlockSpec(memory_space=pl.ANY),
        scratch_shapes=[
            pltpu.VMEM((2, tm_e, D_MODEL), F32),
            pltpu.VMEM((tm_e, D_MODEL), BF16),
            pltpu.VMEM((tm_e, D_MODEL), F32),
            pltpu.VMEM((2, tm_e, D_MODEL), F32),
            pltpu.SemaphoreType.DMA((2,)),
            pltpu.SemaphoreType.DMA((2,)),
        ],
    )
    return pl.pallas_call(
        functools.partial(_experts_kernel, tm_e=tm_e, n_chunks=nf),
        grid_spec=grid_spec,
        out_shape=jax.ShapeDtypeStruct((n_out, D_MODEL), F32),
        compiler_params=pltpu.CompilerParams(
            dimension_semantics=("arbitrary", "arbitrary"), vmem_limit_bytes=VMEM_LIMIT),
        name="moe_experts",
    )(tile_expert, n_used, src, src, dst, h, w1, w3, w2)


def _combine_kernel(x_ref, y0_ref, y1_ref, wts_ref, gt_ref, fg_ref, o_ref):
    wts = wts_ref[...]
    moe = wts[:, 0:1] * y0_ref[...] + wts[:, 1:2] * y1_ref[...]
    y = x_ref[...] + gt_ref[...] * moe
    ms = jnp.mean(y * y, axis=-1, keepdims=True)
    o_ref[...] = y * lax.rsqrt(ms + EPS) * fg_ref[...]


def _combine(xs, y, wts, mods, final_g, *, rows, tm, tiles_per_seq, n_batch):
    row_to_mod = lambda i: jnp.minimum(i // tiles_per_seq, n_batch)
    row_tile = pl.BlockSpec((tm, D_MODEL), lambda i: (i, 0))
    return pl.pallas_call(
        _combine_kernel,
        grid=(rows // tm,),
        in_specs=[
            row_tile, row_tile,
            pl.BlockSpec((tm, D_MODEL), lambda i: (i + rows // tm, 0)),
            pl.BlockSpec((tm, 128), lambda i: (i, 0)),
            _mod_spec(5, row_to_mod),
            pl.BlockSpec((1, D_MODEL), lambda i: (0, 0)),
        ],
        out_specs=row_tile,
        out_shape=jax.ShapeDtypeStruct((rows, D_MODEL), F32),
        compiler_params=pltpu.CompilerParams(
            dimension_semantics=("arbitrary",), vmem_limit_bytes=VMEM_LIMIT),
        name="moe_combine",
    )(xs, y, y, wts, mods, final_g)


def _moe(xs, mods, norm_g, r_hi, r_lo, final_g, w1, w3, w2, *, rows, tm, tm_e, tf, tiles_per_seq, n_batch):
    h, idx, wts = _router(xs, mods, norm_g, r_hi, r_lo, rows=rows, tm=tm, tiles_per_seq=tiles_per_seq,
                          n_batch=n_batch)
    plan = _route_plan(idx[:, :2], rows, tm_e)
    y = _experts(h, plan, w1, w3, w2, n_tok=rows, tm_e=tm_e, tf=tf)
    return _combine(xs, y, wts, mods, final_g, rows=rows, tm=tm, tiles_per_seq=tiles_per_seq, n_batch=n_batch)


def _deinterleave(n=HEAD_DIM):
    return np.concatenate([np.arange(0, n, 2), np.arange(1, n, 2)])


def _proj_columns():
    de = _deinterleave()
    o_dq, o_gq, o_dk, o_dv, o_gk, o_gv, o_gate = 0, 1024, 2048, 3072, 4096, 4352, 4608
    cols = np.zeros(IN_W, np.int32)
    scale = np.ones(IN_W, np.float32)
    cols[GATE_OFF:GATE_OFF + 2048] = o_gate + np.arange(2048)
    for h in range(N_DIFF_HEADS):
        for c in range(2):
            dst = 128 * h + 64 * c
            cols[DQ_OFF + dst:DQ_OFF + dst + 64] = o_dq + dst + de
            cols[DK_OFF + dst:DK_OFF + dst + 64] = o_dk + dst + de
    scale[DQ_OFF:DQ_OFF + 1024] = HEAD_DIM ** -0.5
    for j in range(N_GQA_HEADS):
        cols[GQ_OFF + 64 * j:GQ_OFF + 64 * j + 64] = o_gq + 64 * j + de
    cols[DV_OFF:DV_OFF + 1024] = o_dv + np.arange(1024)
    for g in range(N_GQA_KV):
        cols[GKV_OFF + 128 * g:GKV_OFF + 128 * g + 64] = o_gk + 64 * g + de
        cols[GKV_OFF + 128 * g + 64:GKV_OFF + 128 * g + 128] = o_gv + 64 * g + np.arange(64)
    return cols, scale


def _rope_tables(S, pad_rows):
    rows = S // GRID_W
    row = jnp.repeat(jnp.arange(rows, dtype=F32), GRID_W)
    col = jnp.tile(jnp.arange(GRID_W, dtype=F32), rows)
    half = HEAD_DIM // 2
    inv_freq = ROPE_THETA ** (-jnp.arange(0, half, 2, dtype=F32) / half)
    ang = jnp.concatenate([row[:, None] * inv_freq, col[:, None] * inv_freq], axis=-1)
    cos, sin = jnp.cos(ang), jnp.sin(ang)
    reps = PROJ_TN // HEAD_DIM
    cos_t = jnp.tile(jnp.concatenate([cos, cos], axis=-1), (1, reps))
    sin_t = jnp.tile(jnp.concatenate([-sin, sin], axis=-1), (1, reps))
    cos_t = jnp.concatenate([cos_t, jnp.ones((pad_rows, PROJ_TN), F32)], axis=0)
    sin_t = jnp.concatenate([sin_t, jnp.zeros((pad_rows, PROJ_TN), F32)], axis=0)
    return cos_t, sin_t


def kernel(x, c, ctx, c_ctx, ada_w, ada_b, norm_attn_g, norm_ffn_g, w_in, q_norm_g, k_norm_g, diff_lambda,
           diff_subln_g, w_proj_diff, w_proj_gqa, w_out, ffn_w1, ffn_w3, ffn_w2, moe_router, moe_w1, moe_w3,
           moe_w2, final_norm_g):
    B, S, D = x.shape
    C = ctx.shape[1]
    depth = ada_w.shape[0]
    assert D == D_MODEL and depth == 2 and B + 1 <= MOD_ROWS
    assert w_in.shape[2] == IN_W and moe_router.shape[2] == N_EXPERTS
    tiles = _pick_tiles(B, S, C, ffn_w1.shape[2], moe_w1.shape[3])
    n_lat = B * S

    cvec = jnp.zeros((MOD_ROWS, D), F32).at[:B].set(c).at[B].set(c_ctx)
    mods_all = _ada_mods(cvec, ada_w, ada_b).reshape(depth, MOD_ROWS * N_MODS, 1, D)

    cols, col_scale = _proj_columns()
    de = _deinterleave()
    cos_t, sin_t = _rope_tables(S, tiles.tm_proj)
    blk = np.arange(PROJ_TN) // HEAD_DIM
    ones_bd = jnp.asarray(blk[:, None] == blk[None, :], BF16)
    col = np.arange(PROJ_TN)
    swap_bd = jnp.asarray(col[:, None] == (col[None, :] ^ (HEAD_DIM // 2)), BF16)
    is_key_lane = (np.arange(PROJ_TN) % (2 * HEAD_DIM)) < HEAD_DIM

    xs = jnp.concatenate([x.reshape(n_lat, D), ctx.reshape(B * C, D)], axis=0)
    for l in range(depth):
        last = l == depth - 1
        mods = mods_all[l]
        lam_init = 0.8 - 0.6 * math.exp(-0.3 * l)
        lq1, lk1, lq2, lk2 = diff_lambda[l]
        lam = (jnp.exp(jnp.sum(lq1 * lk1)) - jnp.exp(jnp.sum(lq2 * lk2)) + lam_init).reshape(1).astype(F32)
        w = (jnp.take(w_in[l], cols, axis=1) * col_scale).astype(BF16)
        qg = jnp.tile(q_norm_g[l][de] * HEAD_DIM ** -0.5, PROJ_TN // HEAD_DIM).reshape(1, PROJ_TN)
        kg = jnp.where(is_key_lane, jnp.tile(k_norm_g[l][de], PROJ_TN // HEAD_DIM), 1.0).reshape(1, PROJ_TN)
        subln = (diff_subln_g[l] * (1.0 - lam_init)).reshape(1, 2 * HEAD_DIM)

        p = _project(xs, mods, norm_attn_g[l].reshape(1, D), w, cos_t, sin_t, ones_bd, swap_bd, qg, kg,
                     tm=tiles.tm_proj, lat_tiles=n_lat // tiles.tm_proj, tiles_per_seq=S // tiles.tm_proj,
                     n_batch=B)
        yd, yg = _attention(p, lam, subln, B=B, S=S, C=C, tiles=tiles, ctx_queries=not last)
        rows = n_lat if last else n_lat + B * C
        xs = _mixer_out(xs, yd, yg, p, mods, w_proj_diff[l].astype(BF16), w_proj_gqa[l].astype(BF16),
                        w_out[l].astype(BF16), rows=rows, tm=tiles.tm_mix, tiles_per_seq=S // tiles.tm_mix,
                        n_batch=B)
        i = l // 2
        if l % 2 == 0:
            xs = _ffn(xs, mods, norm_ffn_g[l].reshape(1, D), ffn_w1[i].astype(BF16), ffn_w3[i].astype(BF16),
                      ffn_w2[i].astype(BF16), tm=tiles.tm_ffn, tf=tiles.tf_ffn,
                      tiles_per_seq=S // tiles.tm_ffn, n_batch=B)
        else:
            r_pad = jnp.zeros((D, 128), F32).at[:, :N_EXPERTS].set(moe_router[i])
            r_hi = r_pad.astype(BF16)
            r_lo = (r_pad - r_hi.astype(F32)).astype(BF16)
            xs = _moe(xs, mods, norm_ffn_g[l].reshape(1, D), r_hi, r_lo, final_norm_g.reshape(1, D),
                      moe_w1[i], moe_w3[i], moe_w2[i],
                      rows=rows, tm=tiles.tm_mix, tm_e=tiles.tm_moe, tf=tiles.tf_moe,
                      tiles_per_seq=S // tiles.tm_mix, n_batch=B)
    return xs.reshape(B, S, D)
```

```python
import functools
import math
from typing import NamedTuple

import numpy as np
import jax
import jax.numpy as jnp
from jax import lax
from jax.experimental import pallas as pl
from jax.experimental.pallas import tpu as pltpu

F32 = jnp.float32
BF16 = jnp.bfloat16

D_MODEL = 1024
HEAD_DIM = 64
N_DIFF_HEADS = 8
N_GQA_HEADS = 16
N_GQA_KV = 4
GQA_REP = N_GQA_HEADS // N_GQA_KV
N_EXPERTS = 8
GRID_W = 64
ROPE_THETA = 10000.0
EPS = 1e-6
N_MODS = 6
MOD_ROWS = 16
NEG_BIG = -1e30

GATE_OFF = 0
DQ_OFF = 2048
GQ_OFF = 3072
DK_OFF = 4096
DV_OFF = 5120
GKV_OFF = 6144
IN_W = 6656
PROJ_TN = 512
ROPE_TILES = (4, 5, 8, 9)
QNORM_TILES = (6, 7)
KV_TILE = 12

VMEM_LIMIT = 52 * 1024 * 1024


class Tiles(NamedTuple):
    tm_proj: int
    tq: int
    tk: int
    tm_mix: int
    tm_ffn: int
    tf_ffn: int
    tf_moe: int
    tm_moe: int


def _largest_divisor(n, candidates):
    for c in candidates:
        if n % c == 0:
            return c
    raise ValueError(f"no tile in {candidates} divides {n}")


def _pick_tiles(B, S, C, d_ff, d_ff_e):
    rows_common = math.gcd(S, B * C)
    tm = _largest_divisor(rows_common, (512, 256, 128))
    return Tiles(
        tm_proj=_largest_divisor(rows_common, (1024, 512, 256, 128)),
        tq=C,
        tk=_largest_divisor(math.gcd(S, C), (256, 128)),
        tm_mix=tm,
        tm_ffn=_largest_divisor(rows_common, (1024, 512, 256, 128)),
        tf_ffn=_largest_divisor(d_ff, (1408, 1024, 512, 256, 128)),
        tf_moe=_largest_divisor(d_ff_e, (896, 512, 256, 128)),
        tm_moe=_largest_divisor(2 * B * S, (512, 256)),
    )


def _split_bf16(v):
    hi = v.astype(BF16)
    lo = (v - hi.astype(F32)).astype(BF16)
    return hi, lo


def _ada_kernel(c_ref, w_ref, b_ref, o_ref):
    c = c_ref[...]
    s = c / (1.0 + jnp.exp(-c))
    s_hi, s_lo = _split_bf16(s)
    w_hi, w_lo = _split_bf16(w_ref[...])
    acc = jnp.dot(s_hi, w_hi, preferred_element_type=F32)
    acc += jnp.dot(s_lo, w_hi, preferred_element_type=F32)
    acc += jnp.dot(s_hi, w_lo, preferred_element_type=F32)
    o_ref[...] = acc + b_ref[...]


def _ada_mods(cvec, ada_w, ada_b):
    depth, d, n = ada_w.shape
    tn = _largest_divisor(n, (1536, 1024, 512))
    return pl.pallas_call(
        _ada_kernel,
        grid=(depth, n // tn),
        in_specs=[
            pl.BlockSpec((MOD_ROWS, d), lambda l, j: (0, 0)),
            pl.BlockSpec((None, d, tn), lambda l, j: (l, 0, j)),
            pl.BlockSpec((None, 1, tn), lambda l, j: (l, 0, j)),
        ],
        out_specs=pl.BlockSpec((None, MOD_ROWS, tn), lambda l, j: (l, 0, j)),
        out_shape=jax.ShapeDtypeStruct((depth, MOD_ROWS, n), F32),
        compiler_params=pltpu.CompilerParams(
            dimension_semantics=("arbitrary", "arbitrary"), vmem_limit_bytes=VMEM_LIMIT),
        name="ada_mods",
    )(cvec, ada_w, ada_b.reshape(depth, 1, n))


def _modulated_norm(x, g, scale, shift):
    ms = jnp.mean(x * x, axis=-1, keepdims=True)
    return (x * lax.rsqrt(ms + EPS) * g) * (1.0 + scale) + shift


def _silu(a):
    return a / (1.0 + jnp.exp(-a))


def _mod_spec(k, row_to_mod):
    return pl.BlockSpec((None, 1, D_MODEL), lambda i, *_: (row_to_mod(i) * N_MODS + k, 0, 0))


def _rope(z, cos, sin_signed, swap_ref):
    partner = jnp.dot(z.astype(BF16), swap_ref[...], preferred_element_type=F32)
    return z * cos + partner * sin_signed


def _head_rms(z, ones_ref, gain):
    ss = jnp.dot((z * z).astype(BF16), ones_ref[...], preferred_element_type=F32)
    return z * lax.rsqrt(ss * (1.0 / HEAD_DIM) + EPS) * gain


def _proj_kernel(x_ref, g_ref, sc_ref, sh_ref, w_ref, cos_ref, sin_ref, ones_ref, swap_ref, qg_ref, kg_ref,
                 o_ref, h_scr):
    j = pl.program_id(1)

    @pl.when(j == 0)
    def _():
        h_scr[...] = _modulated_norm(x_ref[...], g_ref[...], sc_ref[...], sh_ref[...]).astype(BF16)

    def project():
        z = jnp.dot(h_scr[...], w_ref[...], preferred_element_type=F32)
        return z, lax.broadcasted_iota(jnp.int32, z.shape, 1)

    is_rope = functools.reduce(jnp.logical_or, [j == t for t in ROPE_TILES])
    is_qnorm = functools.reduce(jnp.logical_or, [j == t for t in QNORM_TILES])
    is_kv = j == KV_TILE
    is_plain = jnp.logical_not(is_rope | is_qnorm | is_kv)

    @pl.when(is_plain)
    def _():
        z, _ = project()
        o_ref[...] = z.astype(BF16)

    @pl.when(is_rope)
    def _():
        z, lane = project()
        o_ref[...] = _rope(z, cos_ref[...], sin_ref[...], swap_ref).astype(BF16)

    @pl.when(is_qnorm)
    def _():
        z, lane = project()
        zn = _head_rms(z, ones_ref, qg_ref[...])
        o_ref[...] = _rope(zn, cos_ref[...], sin_ref[...], swap_ref).astype(BF16)

    @pl.when(is_kv)
    def _():
        z, lane = project()
        zn = _head_rms(z, ones_ref, kg_ref[...])
        zr = _rope(zn, cos_ref[...], sin_ref[...], swap_ref)
        is_key_lane = (lane & HEAD_DIM) == 0
        o_ref[...] = jnp.where(is_key_lane, zr, z).astype(BF16)


def _project(xs, mods, norm_g, w, cos_t, sin_t, ones_bd, swap_bd, qg, kg, *, tm, lat_tiles, tiles_per_seq, n_batch):
    rows = xs.shape[0]
    row_to_mod = lambda i: jnp.minimum(i // tiles_per_seq, n_batch)
    rope_row = lambda i: jnp.where(i < lat_tiles, i % tiles_per_seq, tiles_per_seq)
    const = lambda i, j: (0, 0)
    return pl.pallas_call(
        _proj_kernel,
        grid=(rows // tm, IN_W // PROJ_TN),
        in_specs=[
            pl.BlockSpec((tm, D_MODEL), lambda i, j: (i, 0)),
            pl.BlockSpec((1, D_MODEL), const),
            _mod_spec(1, row_to_mod),
            _mod_spec(0, row_to_mod),
            pl.BlockSpec((D_MODEL, PROJ_TN), lambda i, j: (0, j)),
            pl.BlockSpec((tm, PROJ_TN), lambda i, j: (rope_row(i), 0)),
            pl.BlockSpec((tm, PROJ_TN), lambda i, j: (rope_row(i), 0)),
            pl.BlockSpec((PROJ_TN, PROJ_TN), const),
            pl.BlockSpec((PROJ_TN, PROJ_TN), const),
            pl.BlockSpec((1, PROJ_TN), const),
            pl.BlockSpec((1, PROJ_TN), const),
        ],
        out_specs=pl.BlockSpec((tm, PROJ_TN), lambda i, j: (i, j)),
        out_shape=jax.ShapeDtypeStruct((rows, IN_W), BF16),
        scratch_shapes=[pltpu.VMEM((tm, D_MODEL), BF16)],
        compiler_params=pltpu.CompilerParams(
            dimension_semantics=("arbitrary", "arbitrary"), vmem_limit_bytes=VMEM_LIMIT),
        name="in_proj",
    )(xs, norm_g, mods, mods, w, cos_t, sin_t, ones_bd, swap_bd, qg, kg)


STAB_LANE = HEAD_DIM
L_FLOOR = 1e-26
L_CEIL = 1e30
FP8 = jnp.float8_e4m3fn
FP8_TARGET = 256.0
FP8_BOUND_SLACK = 1.125
LOG2_E = 1.4426950408889634
DIFF_VT_ROWS = 2 * HEAD_DIM + 16
DIFF_HEADS_PER_STEP = 2
GQA_GROUPS_PER_STEP = 1


def _aug_keys(k):
    lane = lax.broadcasted_iota(jnp.int32, k.shape, 1)
    in_key = lane < HEAD_DIM
    sq = jnp.where(in_key, k * k, 0.0).astype(BF16)
    norms = jnp.dot(sq, jnp.ones((k.shape[1], k.shape[1]), BF16), preferred_element_type=F32)
    ka = jnp.where(in_key, k, jnp.where(lane == STAB_LANE, 1.0, 0.0)).astype(BF16)
    return ka, jnp.max(norms, axis=0, keepdims=True)


def _pow2_floor(x):
    bits = lax.bitcast_convert_type(x, jnp.int32) & jnp.int32(0x7F800000)
    return lax.bitcast_convert_type(bits, F32)


def _fp8_keys(ka_scr, k8_scr, kscale_scr, kmax, *, n_chunks, tk):
    for r, kmax2 in enumerate(kmax):
        scale = _pow2_floor(FP8_TARGET * lax.rsqrt(jnp.maximum(kmax2, 1e-30)))
        kscale_scr[r] = scale
        for c in range(n_chunks):
            k = ka_scr[r, c * tk:(c + 1) * tk, :].astype(F32)
            lane = lax.broadcasted_iota(jnp.int32, k.shape, 1)
            k8_scr[r, c * tk:(c + 1) * tk, :] = jnp.where(lane == STAB_LANE, FP8_TARGET, k * scale).astype(FP8)


def _query_mats(slab, kmax2, kscale):
    lane = lax.broadcasted_iota(jnp.int32, slab.shape, 1)
    in_key = lane < HEAD_DIM
    sq = jnp.where(in_key, slab * slab, 0.0).astype(BF16)
    norms = jnp.dot(sq, jnp.ones((slab.shape[1], slab.shape[1]), BF16), preferred_element_type=F32)
    nk = norms * kmax2
    bound = FP8_BOUND_SLACK * nk * lax.rsqrt(nk + 1e-30)
    qscale = _pow2_floor(FP8_TARGET * lax.rsqrt(jnp.maximum(jnp.max(norms, axis=0, keepdims=True), 1e-30)))
    both = kscale * qscale
    plain = jnp.where(in_key, slab, 0.0)
    shifted = jnp.where(lane == STAB_LANE, -bound * (both * (1.0 / FP8_TARGET)), plain * qscale).astype(FP8)
    return shifted, plain.astype(BF16), (LOG2_E / both)[:, :1]


def _attend(k8_scr, vt_scr, mats, acc_scr, *, first, n_chunks, tk):
    def scores(r, c, qmat):
        off = pl.multiple_of(c * tk, tk)
        return lax.dot_general(k8_scr[r * k8_scr.shape[0] // len(mats), pl.ds(off, tk), :], qmat,
                               (((1,), (1,)), ((), ())), preferred_element_type=F32)

    acc_scr[...] = jnp.zeros(acc_scr.shape, F32)

    def all_scores(c):
        return tuple(scores(r, c, shifted) for r, (shifted, _, _) in enumerate(mats))

    def accumulate(c, s_all):
        for r, s in enumerate(s_all):
            vt = vt_scr[r * vt_scr.shape[0] // len(mats), c]
            p = jnp.exp2((s * mats[r][2]).astype(BF16))
            acc_scr[r] += jnp.dot(vt, p, preferred_element_type=F32)

    def fast_body(c, s_cur):
        s_next = all_scores(c + 1)
        accumulate(c, s_cur)
        return s_next

    trips = n_chunks - 1 - first
    s_last = lax.fori_loop(first, n_chunks - 1, fast_body, all_scores(first),
                           unroll=_largest_divisor(trips, (16, 8, 4, 2, 1)) if trips > 0 else 1)
    accumulate(n_chunks - 1, s_last)


def _attend_fallback(ka_scr, vt_scr, mats, m_scr, acc_scr, *, first, n_chunks, tk):
    m_scr[...] = jnp.full(m_scr.shape, NEG_BIG, F32)
    acc_scr[...] = jnp.zeros(acc_scr.shape, F32)

    def body(c, carry):
        off = pl.multiple_of(c * tk, tk)
        for r, (_, plain, _) in enumerate(mats):
            s = lax.dot_general(ka_scr[r * ka_scr.shape[0] // len(mats), pl.ds(off, tk), :], plain, (((1,), (1,)), ((), ())),
                                preferred_element_type=F32)
            m_old = m_scr[r]
            m_new = jnp.maximum(m_old, jnp.max(s, axis=0, keepdims=True))
            p = jnp.exp(s - m_new).astype(BF16)
            vt = vt_scr[r * vt_scr.shape[0] // len(mats), c]
            acc_scr[r] = acc_scr[r] * jnp.exp(m_old - m_new) + jnp.dot(vt, p, preferred_element_type=F32)
            m_scr[r] = m_new
        return carry

    lax.fori_loop(first, n_chunks, body, 0)


def _softmax_tile(ka_scr, k8_scr, vt_scr, kmax_scr, kscale_scr, slabs, m_scr, acc_scr, l_row,
                  *, n_lat_chunks, n_chunks, tk, ctx_tile):
    per_key = lambda scr, r: scr[r * scr.shape[0] // len(slabs)]
    mats = [_query_mats(slab, per_key(kmax_scr, r), per_key(kscale_scr, r)) for r, slab in enumerate(slabs)]

    def run(first):
        _attend(k8_scr, vt_scr, mats, acc_scr, first=first, n_chunks=n_chunks, tk=tk)
        dens = [acc_scr[r, l_row:l_row + 1, :] for r in range(len(slabs))]
        l_min = functools.reduce(jnp.minimum, [jnp.min(d) for d in dens])
        l_max = functools.reduce(jnp.maximum, [jnp.max(d) for d in dens])

        @pl.when(jnp.logical_not((l_min >= L_FLOOR) & (l_max <= L_CEIL)))
        def _():
            _attend_fallback(ka_scr, vt_scr, mats, m_scr, acc_scr, first=first, n_chunks=n_chunks, tk=tk)

    if ctx_tile is None:
        run(0)
    else:
        pl.when(jnp.logical_not(ctx_tile))(functools.partial(run, 0))
        pl.when(ctx_tile)(functools.partial(run, n_lat_chunks))


def _chunk_rows(c, n_lat_chunks, tk, lat_ref, ctx_ref):
    if c < n_lat_chunks:
        return lat_ref[c * tk:(c + 1) * tk, :]
    return ctx_ref[(c - n_lat_chunks) * tk:(c - n_lat_chunks + 1) * tk, :]


def _diff_attn_kernel(lam_ref, q_ref, kl_ref, vl_ref, kc_ref, vc_ref, g_ref, o_ref,
                      ka_scr, k8_scr, vt_scr, kmax_scr, kscale_scr, m_scr, acc_scr,
                      *, nq_lat, n_lat_chunks, n_chunks, tk, ctx_queries):
    dv = 2 * HEAD_DIM
    heads = [slice(a * dv, (a + 1) * dv) for a in range(DIFF_HEADS_PER_STEP)]

    @pl.when(pl.program_id(2) == 0)
    def _():
        kmax = [None] * (2 * len(heads))
        for c in range(n_chunks):
            k_all = _chunk_rows(c, n_lat_chunks, tk, kl_ref, kc_ref).astype(F32)
            v_all = _chunk_rows(c, n_lat_chunks, tk, vl_ref, vc_ref)
            for a, cols in enumerate(heads):
                k = k_all[:, cols]
                for r, keys in ((2 * a, k), (2 * a + 1, pltpu.roll(k, HEAD_DIM, 1))):
                    ka, n2 = _aug_keys(keys)
                    ka_scr[r, c * tk:(c + 1) * tk, :] = ka
                    kmax[r] = n2 if kmax[r] is None else jnp.maximum(kmax[r], n2)
                vt_scr[a, c, :dv, :] = v_all[:, cols].T
                vt_scr[a, c, dv:, :] = jnp.ones((DIFF_VT_ROWS - dv, tk), BF16)
        for r, n2 in enumerate(kmax):
            kmax_scr[r] = n2
        _fp8_keys(ka_scr, k8_scr, kscale_scr, kmax, n_chunks=n_chunks, tk=tk)

    q_all = q_ref[...].astype(F32)
    slabs = []
    for cols in heads:
        slabs += [q_all[:, cols], pltpu.roll(q_all[:, cols], HEAD_DIM, 1)]
    ctx_tile = (pl.program_id(2) == nq_lat) if ctx_queries else None
    _softmax_tile(ka_scr, k8_scr, vt_scr, kmax_scr, kscale_scr, slabs, m_scr, acc_scr, dv,
                  n_lat_chunks=n_lat_chunks, n_chunks=n_chunks, tk=tk, ctx_tile=ctx_tile)
    for a, cols in enumerate(heads):
        a1, a2 = acc_scr[2 * a], acc_scr[2 * a + 1]
        yt = a1[:dv] * (1.0 / a1[dv:dv + 1]) - lam_ref[0] * (a2[:dv] * (1.0 / a2[dv:dv + 1]))
        y = yt.T
        ms = jnp.mean(y * y, axis=-1, keepdims=True)
        o_ref[:, cols] = (y * lax.rsqrt(ms + EPS) * g_ref[...]).astype(BF16)


def _gqa_attn_kernel(q_ref, kvl_ref, kvc_ref, o_ref, ka_scr, k8_scr, vt_scr, kmax_scr, kscale_scr, m_scr, acc_scr,
                     *, nq_lat, n_lat_chunks, n_chunks, tk, ctx_queries):
    w2 = 2 * HEAD_DIM
    groups = [slice(g * w2, (g + 1) * w2) for g in range(GQA_GROUPS_PER_STEP)]
    n_heads = GQA_REP * len(groups)

    @pl.when(pl.program_id(2) == 0)
    def _():
        kmax = [None] * len(groups)
        for c in range(n_chunks):
            kv_all = _chunk_rows(c, n_lat_chunks, tk, kvl_ref, kvc_ref)
            for g, cols in enumerate(groups):
                kv = kv_all[:, cols]
                ka, n2 = _aug_keys(kv.astype(F32))
                ka_scr[g, c * tk:(c + 1) * tk, :] = ka
                kmax[g] = n2 if kmax[g] is None else jnp.maximum(kmax[g], n2)
                t = kv.T
                row = lax.broadcasted_iota(jnp.int32, t.shape, 0)
                vt_scr[g, c] = jnp.where(row < HEAD_DIM, jnp.ones_like(t), t)
        for g, n2 in enumerate(kmax):
            kmax_scr[g] = n2
        _fp8_keys(ka_scr, k8_scr, kscale_scr, kmax, n_chunks=n_chunks, tk=tk)

    qf = q_ref[...].astype(F32)
    tq = qf.shape[0]
    lane = lax.broadcasted_iota(jnp.int32, (tq, w2), 1)
    slabs = []
    for r in range(n_heads):
        slab = qf[:, (r // 2) * w2:(r // 2 + 1) * w2]
        slabs.append(pltpu.roll(slab, HEAD_DIM, 1) if r % 2 else slab)
    ctx_tile = (pl.program_id(2) == nq_lat) if ctx_queries else None
    _softmax_tile(ka_scr, k8_scr, vt_scr, kmax_scr, kscale_scr, slabs, m_scr, acc_scr, 0,
                  n_lat_chunks=n_lat_chunks, n_chunks=n_chunks, tk=tk, ctx_tile=ctx_tile)

    def head_out(r):
        acc = acc_scr[r]
        return (acc * (1.0 / acc[0:1])).T

    for j in range(n_heads // 2):
        pair = jnp.where(lane < HEAD_DIM, pltpu.roll(head_out(2 * j), HEAD_DIM, 1), head_out(2 * j + 1))
        o_ref[:, j * w2:(j + 1) * w2] = pair.astype(BF16)


def _attention(p, lam, subln_gain, *, B, S, C, tiles, ctx_queries):
    tq, tk = tiles.tq, tiles.tk
    nq_lat = S // tq
    nq = nq_lat + (1 if ctx_queries else 0)
    ctx_blk0 = (B * S) // C
    rows_out = B * S + (B * C if ctx_queries else 0)
    T = S + C
    statics = dict(nq_lat=nq_lat, n_lat_chunks=S // tk, n_chunks=T // tk, tk=tk, ctx_queries=ctx_queries)

    def q_row(b, qi):
        if not ctx_queries:
            return b * nq_lat + qi
        return jnp.where(qi < nq_lat, b * nq_lat + qi, ctx_blk0 + b)

    cparams = pltpu.CompilerParams(
        dimension_semantics=("arbitrary", "arbitrary", "arbitrary"), vmem_limit_bytes=VMEM_LIMIT)
    w2 = 2 * HEAD_DIM
    hps = DIFF_HEADS_PER_STEP
    wd = hps * w2
    yd = pl.pallas_call(
        functools.partial(_diff_attn_kernel, **statics),
        grid=(B, N_DIFF_HEADS // hps, nq),
        in_specs=[
            pl.BlockSpec(memory_space=pltpu.SMEM),
            pl.BlockSpec((tq, wd), lambda b, h, qi: (q_row(b, qi), DQ_OFF // wd + h)),
            pl.BlockSpec((S, wd), lambda b, h, qi: (b, DK_OFF // wd + h)),
            pl.BlockSpec((S, wd), lambda b, h, qi: (b, DV_OFF // wd + h)),
            pl.BlockSpec((C, wd), lambda b, h, qi: (ctx_blk0 + b, DK_OFF // wd + h)),
            pl.BlockSpec((C, wd), lambda b, h, qi: (ctx_blk0 + b, DV_OFF // wd + h)),
            pl.BlockSpec((1, w2), lambda b, h, qi: (0, 0)),
        ],
        out_specs=pl.BlockSpec((tq, wd), lambda b, h, qi: (q_row(b, qi), h)),
        out_shape=jax.ShapeDtypeStruct((rows_out, N_DIFF_HEADS * w2), BF16),
        scratch_shapes=[
            pltpu.VMEM((2 * hps, T, w2), BF16),
            pltpu.VMEM((2 * hps, T, w2), FP8),
            pltpu.VMEM((hps, T // tk, DIFF_VT_ROWS, tk), BF16),
            pltpu.VMEM((2 * hps, 1, w2), F32),
            pltpu.VMEM((2 * hps, 1, w2), F32),
            pltpu.VMEM((2 * hps, 1, tq), F32),
            pltpu.VMEM((2 * hps, DIFF_VT_ROWS, tq), F32),
        ],
        compiler_params=cparams,
        name="diff_attn",
    )(lam, p, p, p, p, p, subln_gain)

    gps = GQA_GROUPS_PER_STEP
    wq = gps * GQA_REP * HEAD_DIM
    wkv = gps * w2
    yg = pl.pallas_call(
        functools.partial(_gqa_attn_kernel, **statics),
        grid=(B, N_GQA_KV // gps, nq),
        in_specs=[
            pl.BlockSpec((tq, wq), lambda b, g, qi: (q_row(b, qi), GQ_OFF // wq + g)),
            pl.BlockSpec((S, wkv), lambda b, g, qi: (b, GKV_OFF // wkv + g)),
            pl.BlockSpec((C, wkv), lambda b, g, qi: (ctx_blk0 + b, GKV_OFF // wkv + g)),
        ],
        out_specs=pl.BlockSpec((tq, wq), lambda b, g, qi: (q_row(b, qi), g)),
        out_shape=jax.ShapeDtypeStruct((rows_out, N_GQA_HEADS * HEAD_DIM), BF16),
        scratch_shapes=[
            pltpu.VMEM((gps, T, w2), BF16),
            pltpu.VMEM((gps, T, w2), FP8),
            pltpu.VMEM((gps, T // tk, w2, tk), BF16),
            pltpu.VMEM((gps, 1, w2), F32),
            pltpu.VMEM((gps, 1, w2), F32),
            pltpu.VMEM((gps * GQA_REP, 1, tq), F32),
            pltpu.VMEM((gps * GQA_REP, w2, tq), F32),
        ],
        compiler_params=cparams,
        name="gqa_attn",
    )(p, p, p)
    return yd, yg


def _mixer_kernel(x_ref, yd_ref, yg_ref, gd_ref, gg_ref, gt_ref, wpd_ref, wpg_ref, wo_ref, o_ref):
    pd = jnp.dot(yd_ref[...], wpd_ref[...], preferred_element_type=F32)
    pg = jnp.dot(yg_ref[...], wpg_ref[...], preferred_element_type=F32)
    m = jax.nn.sigmoid(gd_ref[...].astype(F32)) * pd + jax.nn.sigmoid(gg_ref[...].astype(F32)) * pg
    o_ref[...] = x_ref[...] + gt_ref[...] * jnp.dot(m.astype(BF16), wo_ref[...], preferred_element_type=F32)


def _mixer_out(xs, yd, yg, p, mods, wpd, wpg, wo, *, rows, tm, tiles_per_seq, n_batch):
    row_to_mod = lambda i: jnp.minimum(i // tiles_per_seq, n_batch)
    row_tile = pl.BlockSpec((tm, D_MODEL), lambda i: (i, 0))
    weight = pl.BlockSpec((D_MODEL, D_MODEL), lambda i: (0, 0))
    return pl.pallas_call(
        _mixer_kernel,
        grid=(rows // tm,),
        in_specs=[
            row_tile, row_tile, row_tile,
            pl.BlockSpec((tm, D_MODEL), lambda i: (i, GATE_OFF // D_MODEL)),
            pl.BlockSpec((tm, D_MODEL), lambda i: (i, GATE_OFF // D_MODEL + 1)),
            _mod_spec(2, row_to_mod),
            weight, weight, weight,
        ],
        out_specs=row_tile,
        out_shape=jax.ShapeDtypeStruct((rows, D_MODEL), F32),
        compiler_params=pltpu.CompilerParams(
            dimension_semantics=("arbitrary",), vmem_limit_bytes=VMEM_LIMIT),
        name="mixer_out",
    )(xs, yd, yg, p, p, mods, wpd, wpg, wo)


def _ffn_kernel(x_ref, g_ref, sc_ref, sh_ref, gt_ref, w1_ref, w3_ref, w2_ref, o_ref, h_scr, acc_scr):
    f = pl.program_id(1)

    @pl.when(f == 0)
    def _():
        h_scr[...] = _modulated_norm(x_ref[...], g_ref[...], sc_ref[...], sh_ref[...]).astype(BF16)
        acc_scr[...] = jnp.zeros_like(acc_scr)

    h = h_scr[...]
    a = jnp.dot(h, w1_ref[...], preferred_element_type=F32)
    b = jnp.dot(h, w3_ref[...], preferred_element_type=F32)
    acc_scr[...] += jnp.dot((_silu(a) * b).astype(BF16), w2_ref[...], preferred_element_type=F32)

    @pl.when(f == pl.num_programs(1) - 1)
    def _():
        o_ref[...] = x_ref[...] + gt_ref[...] * acc_scr[...]


def _ffn(xs, mods, norm_g, w1, w3, w2, *, tm, tf, tiles_per_seq, n_batch):
    rows = xs.shape[0]
    d_ff = w1.shape[1]
    row_to_mod = lambda i: jnp.minimum(i // tiles_per_seq, n_batch)
    row_tile = pl.BlockSpec((tm, D_MODEL), lambda i, f: (i, 0))
    return pl.pallas_call(
        _ffn_kernel,
        grid=(rows // tm, d_ff // tf),
        in_specs=[
            row_tile,
            pl.BlockSpec((1, D_MODEL), lambda i, f: (0, 0)),
            _mod_spec(4, row_to_mod), _mod_spec(3, row_to_mod), _mod_spec(5, row_to_mod),
            pl.BlockSpec((D_MODEL, tf), lambda i, f: (0, f)),
            pl.BlockSpec((D_MODEL, tf), lambda i, f: (0, f)),
            pl.BlockSpec((tf, D_MODEL), lambda i, f: (f, 0)),
        ],
        out_specs=row_tile,
        out_shape=jax.ShapeDtypeStruct((rows, D_MODEL), F32),
        scratch_shapes=[pltpu.VMEM((tm, D_MODEL), BF16), pltpu.VMEM((tm, D_MODEL), F32)],
        compiler_params=pltpu.CompilerParams(
            dimension_semantics=("arbitrary", "arbitrary"), vmem_limit_bytes=VMEM_LIMIT),
        name="ffn_swiglu",
    )(xs, norm_g, mods, mods, mods, w1, w3, w2)


def _top2_gates(logits):
    lane = lax.broadcasted_iota(jnp.int32, logits.shape, 1)
    n_lanes = logits.shape[1]
    lg = jnp.where(lane < N_EXPERTS, logits, NEG_BIG)
    m1 = jnp.max(lg, axis=-1, keepdims=True)
    i1 = jnp.min(jnp.where(lg == m1, lane, n_lanes), axis=-1, keepdims=True)
    lg2 = jnp.where(lane == i1, NEG_BIG, lg)
    m2 = jnp.max(lg2, axis=-1, keepdims=True)
    i2 = jnp.min(jnp.where(lg2 == m2, lane, n_lanes), axis=-1, keepdims=True)
    e2 = jnp.exp(m2 - m1)
    w_top = 1.0 / (1.0 + e2)
    idx = jnp.where(lane == 0, i1, jnp.where(lane == 1, i2, 0))
    wts = jnp.where(lane == 0, w_top, jnp.where(lane == 1, e2 * w_top, 0.0))
    return idx, wts


def _router_kernel(x_ref, g_ref, sc_ref, sh_ref, rhi_ref, rlo_ref, h_ref, idx_ref, wts_ref):
    h = _modulated_norm(x_ref[...], g_ref[...], sc_ref[...], sh_ref[...])
    h_ref[...] = h
    h_hi, h_lo = _split_bf16(h)
    logits = jnp.dot(h_hi, rhi_ref[...], preferred_element_type=F32)
    logits += jnp.dot(h_lo, rhi_ref[...], preferred_element_type=F32)
    logits += jnp.dot(h_hi, rlo_ref[...], preferred_element_type=F32)
    idx_ref[...], wts_ref[...] = _top2_gates(logits)


def _router(xs, mods, norm_g, r_hi, r_lo, *, rows, tm, tiles_per_seq, n_batch):
    row_to_mod = lambda i: jnp.minimum(i // tiles_per_seq, n_batch)
    row_tile = pl.BlockSpec((tm, D_MODEL), lambda i: (i, 0))
    lanes = pl.BlockSpec((tm, 128), lambda i: (i, 0))
    router = pl.BlockSpec((D_MODEL, 128), lambda i: (0, 0))
    return pl.pallas_call(
        _router_kernel,
        grid=(rows // tm,),
        in_specs=[row_tile, pl.BlockSpec((1, D_MODEL), lambda i: (0, 0)),
                  _mod_spec(4, row_to_mod), _mod_spec(3, row_to_mod), router, router],
        out_specs=[row_tile, lanes, lanes],
        out_shape=[jax.ShapeDtypeStruct((rows, D_MODEL), F32),
                   jax.ShapeDtypeStruct((rows, 128), jnp.int32),
                   jax.ShapeDtypeStruct((rows, 128), F32)],
        compiler_params=pltpu.CompilerParams(
            dimension_semantics=("arbitrary",), vmem_limit_bytes=VMEM_LIMIT),
        name="moe_router",
    )(xs, norm_g, mods, mods, r_hi, r_lo)


def _route_plan(idx, n_tok, tm_e):
    n_pairs = 2 * n_tok
    n_tiles = n_pairs // tm_e + N_EXPERTS + 1
    n_rows = n_tiles * tm_e
    e_flat = idx.reshape(n_pairs)
    order = jnp.argsort(e_flat, stable=True).astype(jnp.int32)
    counts = jnp.sum((e_flat[:, None] == jnp.arange(N_EXPERTS, dtype=jnp.int32)[None, :]).astype(jnp.int32),
                     axis=0)
    padded = ((counts + tm_e - 1) // tm_e) * tm_e
    ends = jnp.cumsum(padded)
    tile_start = jnp.arange(n_tiles, dtype=jnp.int32) * tm_e
    tile_expert = jnp.minimum(jnp.sum((tile_start[:, None] >= ends[None, :]).astype(jnp.int32), axis=1),
                              N_EXPERTS - 1).astype(jnp.int32)
    n_used = (ends[-1] // tm_e).astype(jnp.int32).reshape(1)
    row = jnp.arange(n_rows, dtype=jnp.int32)
    row_expert = jnp.repeat(tile_expert, tm_e)
    offset = row - (ends - padded)[row_expert]
    valid = (offset < counts[row_expert]) & (row < ends[-1])
    pair = order[jnp.clip((jnp.cumsum(counts) - counts)[row_expert] + offset, 0, n_pairs - 1)]
    tok, choice = pair // 2, pair % 2
    src = jnp.where(valid, tok, 0)
    dump = n_pairs + jnp.cumsum(jnp.logical_not(valid).astype(jnp.int32)) - 1
    dst = jnp.where(valid, choice * n_tok + tok, dump)
    return tile_expert, n_used, src.reshape(n_tiles, 1, tm_e), dst.reshape(n_tiles, 1, tm_e)


def _experts_kernel(te_ref, nu_ref, src_ref, src_next_ref, dst_ref, h_hbm, w1_ref, w3_ref, w2_ref, y_hbm,
                    xbuf, xb_scr, acc_scr, ybuf, gsem, ssem, *, tm_e, n_chunks):
    t = pl.program_id(0)
    f = pl.program_id(1)
    nf = pl.num_programs(1)
    n_used = nu_ref[0]
    slot = t % 2

    def gather_rows(idx_ref, s):
        def body(r, carry):
            pltpu.make_async_copy(h_hbm.at[pl.ds(idx_ref[0, r], 1)], xbuf.at[s, pl.ds(r, 1)], gsem.at[s]).start()
            return carry
        lax.fori_loop(0, tm_e, body, 0, unroll=8)

    def wait_gather(s):
        pltpu.make_async_copy(h_hbm.at[pl.ds(0, tm_e)], xbuf.at[s], gsem.at[s]).wait()

    def scatter_rows(s):
        def body(r, carry):
            pltpu.make_async_copy(ybuf.at[s, pl.ds(r, 1)], y_hbm.at[pl.ds(dst_ref[0, r], 1)], ssem.at[s]).start()
            return carry
        lax.fori_loop(0, tm_e, body, 0, unroll=8)

    def wait_scatter(s):
        pltpu.make_async_copy(ybuf.at[s], y_hbm.at[pl.ds(0, tm_e)], ssem.at[s]).wait()

    @pl.when((f == 0) & (t <= n_used))
    def _():
        @pl.when(t == 0)
        def _():
            gather_rows(src_ref, 0)
        for s in range(2):
            @pl.when(slot == s)
            def _():
                wait_gather(s)

                @pl.when(t < n_used)
                def _():
                    xb_scr[...] = xbuf[s].astype(BF16)
                    acc_scr[...] = jnp.zeros_like(acc_scr)

    @pl.when(t < n_used)
    def _():
        x = xb_scr[...]
        a = jnp.dot(x, w1_ref[...].astype(BF16), preferred_element_type=F32)
        b = jnp.dot(x, w3_ref[...].astype(BF16), preferred_element_type=F32)
        acc_scr[...] += jnp.dot((_silu(a) * b).astype(BF16), w2_ref[...].astype(BF16),
                                preferred_element_type=F32)
        share = tm_e // n_chunks
        for i in range(share):
            r = f * share + i
            pltpu.make_async_copy(h_hbm.at[pl.ds(src_next_ref[0, r], 1)],
                                  xbuf.at[1 - slot, pl.ds(r, 1)], gsem.at[1 - slot]).start()

    @pl.when(f == nf - 1)
    def _():
        for s in range(2):
            @pl.when(slot == s)
            def _():
                @pl.when(t >= 2)
                def _():
                    wait_scatter(s)

                @pl.when(t < n_used)
                def _():
                    ybuf[s] = acc_scr[...]

                @pl.when(t >= n_used)
                def _():
                    ybuf[s] = jnp.zeros(ybuf.shape[1:], F32)

                scatter_rows(s)

                @pl.when(t == pl.num_programs(0) - 1)
                def _():
                    wait_scatter(s)
                    wait_scatter(1 - s)


def _experts(h, plan, w1, w3, w2, *, n_tok, tm_e, tf):
    tile_expert, n_used, src, dst = plan
    n_tiles = src.shape[0]
    d_ff = w1.shape[2]
    nf = d_ff // tf
    n_out = n_tiles * tm_e

    def chunk(t, f, nu):
        return jnp.where(t < nu[0], f, nf - 1)

    smem_tile = lambda shift: pl.BlockSpec(
        (None, 1, tm_e), lambda t, f, te, nu: (jnp.minimum(t + shift, n_tiles - 1), 0, 0),
        memory_space=pltpu.SMEM)
    grid_spec = pltpu.PrefetchScalarGridSpec(
        num_scalar_prefetch=2,
        grid=(n_tiles, nf),
        in_specs=[
            smem_tile(0), smem_tile(1), smem_tile(0),
            pl.BlockSpec(memory_space=pl.ANY),
            pl.BlockSpec((None, D_MODEL, tf), lambda t, f, te, nu: (te[t], 0, chunk(t, f, nu))),
            pl.BlockSpec((None, D_MODEL, tf), lambda t, f, te, nu: (te[t], 0, chunk(t, f, nu))),
            pl.BlockSpec((None, tf, D_MODEL), lambda t, f, te, nu: (te[t], chunk(t, f, nu), 0)),
        ],
        out_specs=pl.BlockSpec(memory_space=pl.ANY),
        scratch_shapes=[
            pltpu.VMEM((2, tm_e, D_MODEL), F32),
            pltpu.VMEM((tm_e, D_MODEL), BF16),
            pltpu.VMEM((tm_e, D_MODEL), F32),
            pltpu.VMEM((2, tm_e, D_MODEL), F32),
            pltpu.SemaphoreType.DMA((2,)),
            pltpu.SemaphoreType.DMA((2,)),
        ],
    )
    return pl.pallas_call(
        functools.partial(_experts_kernel, tm_e=tm_e, n_chunks=nf),
        grid_spec=grid_spec,
        out_shape=jax.ShapeDtypeStruct((n_out, D_MODEL), F32),
        compiler_params=pltpu.CompilerParams(
            dimension_semantics=("arbitrary", "arbitrary"), vmem_limit_bytes=VMEM_LIMIT),
        name="moe_experts",
    )(tile_expert, n_used, src, src, dst, h, w1, w3, w2)


def _combine_kernel(x_ref, y0_ref, y1_ref, wts_ref, gt_ref, fg_ref, o_ref):
    wts = wts_ref[...]
    moe = wts[:, 0:1] * y0_ref[...] + wts[:, 1:2] * y1_ref[...]
    y = x_ref[...] + gt_ref[...] * moe
    ms = jnp.mean(y * y, axis=-1, keepdims=True)
    o_ref[...] = y * lax.rsqrt(ms + EPS) * fg_ref[...]


def _combine(xs, y, wts, mods, final_g, *, rows, tm, tiles_per_seq, n_batch):
    row_to_mod = lambda i: jnp.minimum(i // tiles_per_seq, n_batch)
    row_tile = pl.BlockSpec((tm, D_MODEL), lambda i: (i, 0))
    return pl.pallas_call(
        _combine_kernel,
        grid=(rows // tm,),
        in_specs=[
            row_tile, row_tile,
            pl.BlockSpec((tm, D_MODEL), lambda i: (i + rows // tm, 0)),
            pl.BlockSpec((tm, 128), lambda i: (i, 0)),
            _mod_spec(5, row_to_mod),
            pl.BlockSpec((1, D_MODEL), lambda i: (0, 0)),
        ],
        out_specs=row_tile,
        out_shape=jax.ShapeDtypeStruct((rows, D_MODEL), F32),
        compiler_params=pltpu.CompilerParams(
            dimension_semantics=("arbitrary",), vmem_limit_bytes=VMEM_LIMIT),
        name="moe_combine",
    )(xs, y, y, wts, mods, final_g)


def _moe(xs, mods, norm_g, r_hi, r_lo, final_g, w1, w3, w2, *, rows, tm, tm_e, tf, tiles_per_seq, n_batch):
    h, idx, wts = _router(xs, mods, norm_g, r_hi, r_lo, rows=rows, tm=tm, tiles_per_seq=tiles_per_seq,
                          n_batch=n_batch)
    plan = _route_plan(idx[:, :2], rows, tm_e)
    y = _experts(h, plan, w1, w3, w2, n_tok=rows, tm_e=tm_e, tf=tf)
    return _combine(xs, y, wts, mods, final_g, rows=rows, tm=tm, tiles_per_seq=tiles_per_seq, n_batch=n_batch)


def _deinterleave(n=HEAD_DIM):
    return np.concatenate([np.arange(0, n, 2), np.arange(1, n, 2)])


def _proj_columns():
    de = _deinterleave()
    o_dq, o_gq, o_dk, o_dv, o_gk, o_gv, o_gate = 0, 1024, 2048, 3072, 4096, 4352, 4608
    cols = np.zeros(IN_W, np.int32)
    scale = np.ones(IN_W, np.float32)
    cols[GATE_OFF:GATE_OFF + 2048] = o_gate + np.arange(2048)
    for h in range(N_DIFF_HEADS):
        for c in range(2):
            dst = 128 * h + 64 * c
            cols[DQ_OFF + dst:DQ_OFF + dst + 64] = o_dq + dst + de
            cols[DK_OFF + dst:DK_OFF + dst + 64] = o_dk + dst + de
    scale[DQ_OFF:DQ_OFF + 1024] = HEAD_DIM ** -0.5
    for j in range(N_GQA_HEADS):
        cols[GQ_OFF + 64 * j:GQ_OFF + 64 * j + 64] = o_gq + 64 * j + de
    cols[DV_OFF:DV_OFF + 1024] = o_dv + np.arange(1024)
    for g in range(N_GQA_KV):
        cols[GKV_OFF + 128 * g:GKV_OFF + 128 * g + 64] = o_gk + 64 * g + de
        cols[GKV_OFF + 128 * g + 64:GKV_OFF + 128 * g + 128] = o_gv + 64 * g + np.arange(64)
    return cols, scale


def _rope_tables(S, pad_rows):
    rows = S // GRID_W
    row = jnp.repeat(jnp.arange(rows, dtype=F32), GRID_W)
    col = jnp.tile(jnp.arange(GRID_W, dtype=F32), rows)
    half = HEAD_DIM // 2
    inv_freq = ROPE_THETA ** (-jnp.arange(0, half, 2, dtype=F32) / half)
    ang = jnp.concatenate([row[:, None] * inv_freq, col[:, None] * inv_freq], axis=-1)
    cos, sin = jnp.cos(ang), jnp.sin(ang)
    reps = PROJ_TN // HEAD_DIM
    cos_t = jnp.tile(jnp.concatenate([cos, cos], axis=-1), (1, reps))
    sin_t = jnp.tile(jnp.concatenate([-sin, sin], axis=-1), (1, reps))
    cos_t = jnp.concatenate([cos_t, jnp.ones((pad_rows, PROJ_TN), F32)], axis=0)
    sin_t = jnp.concatenate([sin_t, jnp.zeros((pad_rows, PROJ_TN), F32)], axis=0)
    return cos_t, sin_t


def kernel(x, c, ctx, c_ctx, ada_w, ada_b, norm_attn_g, norm_ffn_g, w_in, q_norm_g, k_norm_g, diff_lambda,
           diff_subln_g, w_proj_diff, w_proj_gqa, w_out, ffn_w1, ffn_w3, ffn_w2, moe_router, moe_w1, moe_w3,
           moe_w2, final_norm_g):
    B, S, D = x.shape
    C = ctx.shape[1]
    depth = ada_w.shape[0]
    assert D == D_MODEL and depth == 2 and B + 1 <= MOD_ROWS
    assert w_in.shape[2] == IN_W and moe_router.shape[2] == N_EXPERTS
    tiles = _pick_tiles(B, S, C, ffn_w1.shape[2], moe_w1.shape[3])
    n_lat = B * S

    cvec = jnp.zeros((MOD_ROWS, D), F32).at[:B].set(c).at[B].set(c_ctx)
    mods_all = _ada_mods(cvec, ada_w, ada_b).reshape(depth, MOD_ROWS * N_MODS, 1, D)

    cols, col_scale = _proj_columns()
    de = _deinterleave()
    cos_t, sin_t = _rope_tables(S, tiles.tm_proj)
    blk = np.arange(PROJ_TN) // HEAD_DIM
    ones_bd = jnp.asarray(blk[:, None] == blk[None, :], BF16)
    col = np.arange(PROJ_TN)
    swap_bd = jnp.asarray(col[:, None] == (col[None, :] ^ (HEAD_DIM // 2)), BF16)
    is_key_lane = (np.arange(PROJ_TN) % (2 * HEAD_DIM)) < HEAD_DIM

    xs = jnp.concatenate([x.reshape(n_lat, D), ctx.reshape(B * C, D)], axis=0)
    for l in range(depth):
        last = l == depth - 1
        mods = mods_all[l]
        lam_init = 0.8 - 0.6 * math.exp(-0.3 * l)
        lq1, lk1, lq2, lk2 = diff_lambda[l]
        lam = (jnp.exp(jnp.sum(lq1 * lk1)) - jnp.exp(jnp.sum(lq2 * lk2)) + lam_init).reshape(1).astype(F32)
        w = (jnp.take(w_in[l], cols, axis=1) * col_scale).astype(BF16)
        qg = jnp.tile(q_norm_g[l][de] * HEAD_DIM ** -0.5, PROJ_TN // HEAD_DIM).reshape(1, PROJ_TN)
        kg = jnp.where(is_key_lane, jnp.tile(k_norm_g[l][de], PROJ_TN // HEAD_DIM), 1.0).reshape(1, PROJ_TN)
        subln = (diff_subln_g[l] * (1.0 - lam_init)).reshape(1, 2 * HEAD_DIM)

        p = _project(xs, mods, norm_attn_g[l].reshape(1, D), w, cos_t, sin_t, ones_bd, swap_bd, qg, kg,
                     tm=tiles.tm_proj, lat_tiles=n_lat // tiles.tm_proj, tiles_per_seq=S // tiles.tm_proj,
                     n_batch=B)
        yd, yg = _attention(p, lam, subln, B=B, S=S, C=C, tiles=tiles, ctx_queries=not last)
        rows = n_lat if last else n_lat + B * C
        xs = _mixer_out(xs, yd, yg, p, mods, w_proj_diff[l].astype(BF16), w_proj_gqa[l].astype(BF16),
                        w_out[l].astype(BF16), rows=rows, tm=tiles.tm_mix, tiles_per_seq=S // tiles.tm_mix,
                        n_batch=B)
        i = l // 2
        if l % 2 == 0:
            xs = _ffn(xs, mods, norm_ffn_g[l].reshape(1, D), ffn_w1[i].astype(BF16), ffn_w3[i].astype(BF16),
                      ffn_w2[i].astype(BF16), tm=tiles.tm_ffn, tf=tiles.tf_ffn,
                      tiles_per_seq=S // tiles.tm_ffn, n_batch=B)
        else:
            r_pad = jnp.zeros((D, 128), F32).at[:, :N_EXPERTS].set(moe_router[i])
            r_hi = r_pad.astype(BF16)
            r_lo = (r_pad - r_hi.astype(F32)).astype(BF16)
            xs = _moe(xs, mods, norm_ffn_g[l].reshape(1, D), r_hi, r_lo, final_norm_g.reshape(1, D),
                      moe_w1[i], moe_w3[i], moe_w2[i],
                      rows=rows, tm=tiles.tm_mix, tm_e=tiles.tm_moe, tf=tiles.tf_moe,
                      tiles_per_seq=S // tiles.tm_mix, n_batch=B)
    return xs.reshape(B, S, D)
```

```python
import functools
import math
from typing import NamedTuple

import numpy as np
import jax
import jax.numpy as jnp
from jax import lax
from jax.experimental import pallas as pl
from jax.experimental.pallas import tpu as pltpu

F32 = jnp.float32
BF16 = jnp.bfloat16

D_MODEL = 1024
HEAD_DIM = 64
N_DIFF_HEADS = 8
N_GQA_HEADS = 16
N_GQA_KV = 4
GQA_REP = N_GQA_HEADS // N_GQA_KV
N_EXPERTS = 8
GRID_W = 64
ROPE_THETA = 10000.0
EPS = 1e-6
N_MODS = 6
MOD_ROWS = 16
NEG_BIG = -1e30

GATE_OFF = 0
DQ_OFF = 2048
GQ_OFF = 3072
DK_OFF = 4096
DV_OFF = 5120
GKV_OFF = 6144
IN_W = 6656
PROJ_TN = 512
ROPE_TILES = (4, 5, 8, 9)
QNORM_TILES = (6, 7)
KV_TILE = 12

VMEM_LIMIT = 52 * 1024 * 1024


class Tiles(NamedTuple):
    tm_proj: int
    tq: int
    tk: int
    tm_mix: int
    tm_ffn: int
    tf_ffn: int
    tf_moe: int
    tm_moe: int


def _largest_divisor(n, candidates):
    for c in candidates:
        if n % c == 0:
            return c
    raise ValueError(f"no tile in {candidates} divides {n}")


def _pick_tiles(B, S, C, d_ff, d_ff_e):
    rows_common = math.gcd(S, B * C)
    tm = _largest_divisor(rows_common, (512, 256, 128))
    return Tiles(
        tm_proj=_largest_divisor(rows_common, (1024, 512, 256, 128)),
        tq=C,
        tk=_largest_divisor(math.gcd(S, C), (256, 128)),
        tm_mix=tm,
        tm_ffn=_largest_divisor(rows_common, (1024, 512, 256, 128)),
        tf_ffn=_largest_divisor(d_ff, (1408, 1024, 512, 256, 128)),
        tf_moe=_largest_divisor(d_ff_e, (896, 512, 256, 128)),
        tm_moe=_largest_divisor(2 * B * S, (512, 256)),
    )


def _split_bf16(v):
    hi = v.astype(BF16)
    lo = (v - hi.astype(F32)).astype(BF16)
    return hi, lo


def _ada_kernel(c_ref, w_ref, b_ref, o_ref):
    c = c_ref[...]
    s = c / (1.0 + jnp.exp(-c))
    s_hi, s_lo = _split_bf16(s)
    w_hi, w_lo = _split_bf16(w_ref[...])
    acc = jnp.dot(s_hi, w_hi, preferred_element_type=F32)
    acc += jnp.dot(s_lo, w_hi, preferred_element_type=F32)
    acc += jnp.dot(s_hi, w_lo, preferred_element_type=F32)
    o_ref[...] = acc + b_ref[...]


def _ada_mods(cvec, ada_w, ada_b):
    depth, d, n = ada_w.shape
    tn = _largest_divisor(n, (1536, 1024, 512))
    return pl.pallas_call(
        _ada_kernel,
        grid=(depth, n // tn),
        in_specs=[
            pl.BlockSpec((MOD_ROWS, d), lambda l, j: (0, 0)),
            pl.BlockSpec((None, d, tn), lambda l, j: (l, 0, j)),
            pl.BlockSpec((None, 1, tn), lambda l, j: (l, 0, j)),
        ],
        out_specs=pl.BlockSpec((None, MOD_ROWS, tn), lambda l, j: (l, 0, j)),
        out_shape=jax.ShapeDtypeStruct((depth, MOD_ROWS, n), F32),
        compiler_params=pltpu.CompilerParams(
            dimension_semantics=("arbitrary", "arbitrary"), vmem_limit_bytes=VMEM_LIMIT),
        name="ada_mods",
    )(cvec, ada_w, ada_b.reshape(depth, 1, n))


def _modulated_norm(x, g, scale, shift):
    ms = jnp.mean(x * x, axis=-1, keepdims=True)
    return (x * lax.rsqrt(ms + EPS) * g) * (1.0 + scale) + shift


def _silu(a):
    return a / (1.0 + jnp.exp(-a))


def _mod_spec(k, row_to_mod):
    return pl.BlockSpec((None, 1, D_MODEL), lambda i, *_: (row_to_mod(i) * N_MODS + k, 0, 0))


def _rope(z, cos, sin_signed, swap_ref):
    partner = jnp.dot(z.astype(BF16), swap_ref[...], preferred_element_type=F32)
    return z * cos + partner * sin_signed


def _head_rms(z, ones_ref, gain):
    ss = jnp.dot((z * z).astype(BF16), ones_ref[...], preferred_element_type=F32)
    return z * lax.rsqrt(ss * (1.0 / HEAD_DIM) + EPS) * gain


def _proj_kernel(x_ref, g_ref, sc_ref, sh_ref, w_ref, cos_ref, sin_ref, ones_ref, swap_ref, qg_ref, kg_ref,
                 o_ref, h_scr):
    j = pl.program_id(1)

    @pl.when(j == 0)
    def _():
        h_scr[...] = _modulated_norm(x_ref[...], g_ref[...], sc_ref[...], sh_ref[...]).astype(BF16)

    def project():
        z = jnp.dot(h_scr[...], w_ref[...], preferred_element_type=F32)
        return z, lax.broadcasted_iota(jnp.int32, z.shape, 1)

    is_rope = functools.reduce(jnp.logical_or, [j == t for t in ROPE_TILES])
    is_qnorm = functools.reduce(jnp.logical_or, [j == t for t in QNORM_TILES])
    is_kv = j == KV_TILE
    is_plain = jnp.logical_not(is_rope | is_qnorm | is_kv)

    @pl.when(is_plain)
    def _():
        z, _ = project()
        o_ref[...] = z.astype(BF16)

    @pl.when(is_rope)
    def _():
        z, lane = project()
        o_ref[...] = _rope(z, cos_ref[...], sin_ref[...], swap_ref).astype(BF16)

    @pl.when(is_qnorm)
    def _():
        z, lane = project()
        zn = _head_rms(z, ones_ref, qg_ref[...])
        o_ref[...] = _rope(zn, cos_ref[...], sin_ref[...], swap_ref).astype(BF16)

    @pl.when(is_kv)
    def _():
        z, lane = project()
        zn = _head_rms(z, ones_ref, kg_ref[...])
        zr = _rope(zn, cos_ref[...], sin_ref[...], swap_ref)
        is_key_lane = (lane & HEAD_DIM) == 0
        o_ref[...] = jnp.where(is_key_lane, zr, z).astype(BF16)


def _project(xs, mods, norm_g, w, cos_t, sin_t, ones_bd, swap_bd, qg, kg, *, tm, lat_tiles, tiles_per_seq, n_batch):
    rows = xs.shape[0]
    row_to_mod = lambda i: jnp.minimum(i // tiles_per_seq, n_batch)
    rope_row = lambda i: jnp.where(i < lat_tiles, i % tiles_per_seq, tiles_per_seq)
    const = lambda i, j: (0, 0)
    return pl.pallas_call(
        _proj_kernel,
        grid=(rows // tm, IN_W // PROJ_TN),
        in_specs=[
            pl.BlockSpec((tm, D_MODEL), lambda i, j: (i, 0)),
            pl.BlockSpec((1, D_MODEL), const),
            _mod_spec(1, row_to_mod),
            _mod_spec(0, row_to_mod),
            pl.BlockSpec((D_MODEL, PROJ_TN), lambda i, j: (0, j)),
            pl.BlockSpec((tm, PROJ_TN), lambda i, j: (rope_row(i), 0)),
            pl.BlockSpec((tm, PROJ_TN), lambda i, j: (rope_row(i), 0)),
            pl.BlockSpec((PROJ_TN, PROJ_TN), const),
            pl.BlockSpec((PROJ_TN, PROJ_TN), const),
            pl.BlockSpec((1, PROJ_TN), const),
            pl.BlockSpec((1, PROJ_TN), const),
        ],
        out_specs=pl.BlockSpec((tm, PROJ_TN), lambda i, j: (i, j)),
        out_shape=jax.ShapeDtypeStruct((rows, IN_W), BF16),
        scratch_shapes=[pltpu.VMEM((tm, D_MODEL), BF16)],
        compiler_params=pltpu.CompilerParams(
            dimension_semantics=("arbitrary", "arbitrary"), vmem_limit_bytes=VMEM_LIMIT),
        name="in_proj",
    )(xs, norm_g, mods, mods, w, cos_t, sin_t, ones_bd, swap_bd, qg, kg)


STAB_LANE = HEAD_DIM
L_FLOOR = 1e-26
L_CEIL = 1e30
FP8 = jnp.float8_e4m3fn
FP8_TARGET = 256.0
FP8_BOUND_SLACK = 1.125
LOG2_E = 1.4426950408889634
DIFF_VT_ROWS = 2 * HEAD_DIM + 16
DIFF_HEADS_PER_STEP = 2
GQA_GROUPS_PER_STEP = 1


def _aug_keys(k):
    lane = lax.broadcasted_iota(jnp.int32, k.shape, 1)
    in_key = lane < HEAD_DIM
    sq = jnp.where(in_key, k * k, 0.0).astype(BF16)
    norms = jnp.dot(sq, jnp.ones((k.shape[1], k.shape[1]), BF16), preferred_element_type=F32)
    ka = jnp.where(in_key, k, jnp.where(lane == STAB_LANE, 1.0, 0.0)).astype(BF16)
    return ka, jnp.max(norms, axis=0, keepdims=True)


def _pow2_floor(x):
    bits = lax.bitcast_convert_type(x, jnp.int32) & jnp.int32(0x7F800000)
    return lax.bitcast_convert_type(bits, F32)


def _fp8_keys(ka_scr, k8_scr, kscale_scr, kmax, *, n_chunks, tk):
    for r, kmax2 in enumerate(kmax):
        scale = _pow2_floor(FP8_TARGET * lax.rsqrt(jnp.maximum(kmax2, 1e-30)))
        kscale_scr[r] = scale
        for c in range(n_chunks):
            k = ka_scr[r, c * tk:(c + 1) * tk, :].astype(F32)
            lane = lax.broadcasted_iota(jnp.int32, k.shape, 1)
            k8_scr[r, c * tk:(c + 1) * tk, :] = jnp.where(lane == STAB_LANE, FP8_TARGET, k * scale).astype(FP8)


def _query_mats(slab, kmax2, kscale):
    lane = lax.broadcasted_iota(jnp.int32, slab.shape, 1)
    in_key = lane < HEAD_DIM
    sq = jnp.where(in_key, slab * slab, 0.0).astype(BF16)
    norms = jnp.dot(sq, jnp.ones((slab.shape[1], slab.shape[1]), BF16), preferred_element_type=F32)
    nk = norms * kmax2
    bound = FP8_BOUND_SLACK * nk * lax.rsqrt(nk + 1e-30)
    qscale = _pow2_floor(FP8_TARGET * lax.rsqrt(jnp.maximum(jnp.max(norms, axis=0, keepdims=True), 1e-30)))
    both = kscale * qscale
    plain = jnp.where(in_key, slab, 0.0)
    shifted = jnp.where(lane == STAB_LANE, -bound * (both * (1.0 / FP8_TARGET)), plain * qscale).astype(FP8)
    return shifted, plain.astype(BF16), (LOG2_E / both)[:, :1]


def _attend(k8_scr, vt_scr, mats, acc_scr, *, first, n_chunks, tk):
    def scores(r, c, qmat):
        off = pl.multiple_of(c * tk, tk)
        return lax.dot_general(k8_scr[r * k8_scr.shape[0] // len(mats), pl.ds(off, tk), :], qmat,
                               (((1,), (1,)), ((), ())), preferred_element_type=F32)

    acc_scr[...] = jnp.zeros(acc_scr.shape, F32)

    def all_scores(c):
        return tuple(scores(r, c, shifted) for r, (shifted, _, _) in enumerate(mats))

    def accumulate(c, s_all):
        for r, s in enumerate(s_all):
            vt = vt_scr[r * vt_scr.shape[0] // len(mats), c]
            p = jnp.exp2((s * mats[r][2]).astype(BF16))
            acc_scr[r] += jnp.dot(vt, p, preferred_element_type=F32)

    def fast_body(c, s_cur):
        s_next = all_scores(c + 1)
        accumulate(c, s_cur)
        return s_next

    trips = n_chunks - 1 - first
    s_last = lax.fori_loop(first, n_chunks - 1, fast_body, all_scores(first),
                           unroll=_largest_divisor(trips, (16, 8, 4, 2, 1)) if trips > 0 else 1)
    accumulate(n_chunks - 1, s_last)


def _attend_fallback(ka_scr, vt_scr, mats, m_scr, acc_scr, *, first, n_chunks, tk):
    m_scr[...] = jnp.full(m_scr.shape, NEG_BIG, F32)
    acc_scr[...] = jnp.zeros(acc_scr.shape, F32)

    def body(c, carry):
        off = pl.multiple_of(c * tk, tk)
        for r, (_, plain, _) in enumerate(mats):
            s = lax.dot_general(ka_scr[r * ka_scr.shape[0] // len(mats), pl.ds(off, tk), :], plain, (((1,), (1,)), ((), ())),
                                preferred_element_type=F32)
            m_old = m_scr[r]
            m_new = jnp.maximum(m_old, jnp.max(s, axis=0, keepdims=True))
            p = jnp.exp(s - m_new).astype(BF16)
            vt = vt_scr[r * vt_scr.shape[0] // len(mats), c]
            acc_scr[r] = acc_scr[r] * jnp.exp(m_old - m_new) + jnp.dot(vt, p, preferred_element_type=F32)
            m_scr[r] = m_new
        return carry

    lax.fori_loop(first, n_chunks, body, 0)


def _softmax_tile(ka_scr, k8_scr, vt_scr, kmax_scr, kscale_scr, slabs, m_scr, acc_scr, l_row,
                  *, n_chunks, tk):
    per_key = lambda scr, r: scr[r * scr.shape[0] // len(slabs)]
    mats = [_query_mats(slab, per_key(kmax_scr, r), per_key(kscale_scr, r)) for r, slab in enumerate(slabs)]
    _attend(k8_scr, vt_scr, mats, acc_scr, first=0, n_chunks=n_chunks, tk=tk)
    dens = [acc_scr[r, l_row:l_row + 1, :] for r in range(len(slabs))]
    l_min = functools.reduce(jnp.minimum, [jnp.min(d) for d in dens])
    l_max = functools.reduce(jnp.maximum, [jnp.max(d) for d in dens])

    @pl.when(jnp.logical_not((l_min >= L_FLOOR) & (l_max <= L_CEIL)))
    def _():
        _attend_fallback(ka_scr, vt_scr, mats, m_scr, acc_scr, first=0, n_chunks=n_chunks, tk=tk)


def _chunk_rows(c, n_lat_chunks, tk, lat_ref, ctx_ref):
    if c < n_lat_chunks:
        return lat_ref[c * tk:(c + 1) * tk, :]
    return ctx_ref[(c - n_lat_chunks) * tk:(c - n_lat_chunks + 1) * tk, :]


def _diff_attn_kernel(lam_ref, q_ref, kl_ref, vl_ref, kc_ref, vc_ref, g_ref, o_ref,
                      ka_scr, k8_scr, vt_scr, kmax_scr, kscale_scr, m_scr, acc_scr,
                      *, n_lat_chunks, n_chunks, tk):
    dv = 2 * HEAD_DIM
    heads = [slice(a * dv, (a + 1) * dv) for a in range(DIFF_HEADS_PER_STEP)]

    @pl.when(pl.program_id(2) == 0)
    def _():
        kmax = [None] * (2 * len(heads))
        for c in range(n_chunks):
            k_all = _chunk_rows(c, n_lat_chunks, tk, kl_ref, kc_ref).astype(F32)
            v_all = _chunk_rows(c, n_lat_chunks, tk, vl_ref, vc_ref)
            for a, cols in enumerate(heads):
                k = k_all[:, cols]
                for r, keys in ((2 * a, k), (2 * a + 1, pltpu.roll(k, HEAD_DIM, 1))):
                    ka, n2 = _aug_keys(keys)
                    ka_scr[r, c * tk:(c + 1) * tk, :] = ka
                    kmax[r] = n2 if kmax[r] is None else jnp.maximum(kmax[r], n2)
                vt_scr[a, c, :dv, :] = v_all[:, cols].T
                vt_scr[a, c, dv:, :] = jnp.ones((DIFF_VT_ROWS - dv, tk), BF16)
        for r, n2 in enumerate(kmax):
            kmax_scr[r] = n2
        _fp8_keys(ka_scr, k8_scr, kscale_scr, kmax, n_chunks=n_chunks, tk=tk)

    q_all = q_ref[...].astype(F32)
    slabs = []
    for cols in heads:
        slabs += [q_all[:, cols], pltpu.roll(q_all[:, cols], HEAD_DIM, 1)]
    _softmax_tile(ka_scr, k8_scr, vt_scr, kmax_scr, kscale_scr, slabs, m_scr, acc_scr, dv,
                  n_chunks=n_chunks, tk=tk)
    for a, cols in enumerate(heads):
        a1, a2 = acc_scr[2 * a], acc_scr[2 * a + 1]
        yt = a1[:dv] * (1.0 / a1[dv:dv + 1]) - lam_ref[0] * (a2[:dv] * (1.0 / a2[dv:dv + 1]))
        y = yt.T
        ms = jnp.mean(y * y, axis=-1, keepdims=True)
        o_ref[:, cols] = (y * lax.rsqrt(ms + EPS) * g_ref[...]).astype(BF16)


def _gqa_attn_kernel(q_ref, kvl_ref, kvc_ref, o_ref, ka_scr, k8_scr, vt_scr, kmax_scr, kscale_scr, m_scr, acc_scr,
                     *, n_lat_chunks, n_chunks, tk):
    w2 = 2 * HEAD_DIM
    groups = [slice(g * w2, (g + 1) * w2) for g in range(GQA_GROUPS_PER_STEP)]
    n_heads = GQA_REP * len(groups)

    @pl.when(pl.program_id(2) == 0)
    def _():
        kmax = [None] * len(groups)
        for c in range(n_chunks):
            kv_all = _chunk_rows(c, n_lat_chunks, tk, kvl_ref, kvc_ref)
            for g, cols in enumerate(groups):
                kv = kv_all[:, cols]
                ka, n2 = _aug_keys(kv.astype(F32))
                ka_scr[g, c * tk:(c + 1) * tk, :] = ka
                kmax[g] = n2 if kmax[g] is None else jnp.maximum(kmax[g], n2)
                t = kv.T
                row = lax.broadcasted_iota(jnp.int32, t.shape, 0)
                vt_scr[g, c] = jnp.where(row < HEAD_DIM, jnp.ones_like(t), t)
        for g, n2 in enumerate(kmax):
            kmax_scr[g] = n2
        _fp8_keys(ka_scr, k8_scr, kscale_scr, kmax, n_chunks=n_chunks, tk=tk)

    qf = q_ref[...].astype(F32)
    tq = qf.shape[0]
    lane = lax.broadcasted_iota(jnp.int32, (tq, w2), 1)
    slabs = []
    for r in range(n_heads):
        slab = qf[:, (r // 2) * w2:(r // 2 + 1) * w2]
        slabs.append(pltpu.roll(slab, HEAD_DIM, 1) if r % 2 else slab)
    _softmax_tile(ka_scr, k8_scr, vt_scr, kmax_scr, kscale_scr, slabs, m_scr, acc_scr, 0,
                  n_chunks=n_chunks, tk=tk)

    def head_out(r):
        acc = acc_scr[r]
        return (acc * (1.0 / acc[0:1])).T

    for j in range(n_heads // 2):
        pair = jnp.where(lane < HEAD_DIM, pltpu.roll(head_out(2 * j), HEAD_DIM, 1), head_out(2 * j + 1))
        o_ref[:, j * w2:(j + 1) * w2] = pair.astype(BF16)


def _attention(p, lam, subln_gain, *, B, S, C, tiles, ctx_queries):
    lat = _attention_calls(p, lam, subln_gain, B=B, S=S, C=C, tq=tiles.tq, tk=tiles.tk, latent=True)
    if not ctx_queries:
        return lat
    ctx = _attention_calls(p, lam, subln_gain, B=B, S=S, C=C, tq=tiles.tq, tk=tiles.tk, latent=False)
    return tuple(jnp.concatenate([a, b], axis=0) for a, b in zip(lat, ctx))


def _attention_calls(p, lam, subln_gain, *, B, S, C, tq, tk, latent):
    ctx_blk0 = (B * S) // C
    if latent:
        nq, rows_out, s_kv, n_lat_chunks = S // tq, B * S, S, S // tk
        q_row = out_row = lambda b, qi: b * nq + qi
        lat_blk = lambda b: b
    else:
        nq, rows_out, s_kv, n_lat_chunks = 1, B * C, C, 0
        q_row = lambda b, qi: ctx_blk0 + b
        out_row = lambda b, qi: b
        lat_blk = lambda b: ctx_blk0 + b
    T = n_lat_chunks * tk + C
    S = s_kv
    statics = dict(n_lat_chunks=n_lat_chunks, n_chunks=T // tk, tk=tk)

    cparams = pltpu.CompilerParams(
        dimension_semantics=("arbitrary", "arbitrary", "arbitrary"), vmem_limit_bytes=VMEM_LIMIT)
    w2 = 2 * HEAD_DIM
    hps = DIFF_HEADS_PER_STEP
    wd = hps * w2
    yd = pl.pallas_call(
        functools.partial(_diff_attn_kernel, **statics),
        grid=(B, N_DIFF_HEADS // hps, nq),
        in_specs=[
            pl.BlockSpec(memory_space=pltpu.SMEM),
            pl.BlockSpec((tq, wd), lambda b, h, qi: (q_row(b, qi), DQ_OFF // wd + h)),
            pl.BlockSpec((S, wd), lambda b, h, qi: (lat_blk(b), DK_OFF // wd + h)),
            pl.BlockSpec((S, wd), lambda b, h, qi: (lat_blk(b), DV_OFF // wd + h)),
            pl.BlockSpec((C, wd), lambda b, h, qi: (ctx_blk0 + b, DK_OFF // wd + h)),
            pl.BlockSpec((C, wd), lambda b, h, qi: (ctx_blk0 + b, DV_OFF // wd + h)),
            pl.BlockSpec((1, w2), lambda b, h, qi: (0, 0)),
        ],
        out_specs=pl.BlockSpec((tq, wd), lambda b, h, qi: (out_row(b, qi), h)),
        out_shape=jax.ShapeDtypeStruct((rows_out, N_DIFF_HEADS * w2), BF16),
        scratch_shapes=[
            pltpu.VMEM((2 * hps, T, w2), BF16),
            pltpu.VMEM((2 * hps, T, w2), FP8),
            pltpu.VMEM((hps, T // tk, DIFF_VT_ROWS, tk), BF16),
            pltpu.VMEM((2 * hps, 1, w2), F32),
            pltpu.VMEM((2 * hps, 1, w2), F32),
            pltpu.VMEM((2 * hps, 1, tq), F32),
            pltpu.VMEM((2 * hps, DIFF_VT_ROWS, tq), F32),
        ],
        compiler_params=cparams,
        name="diff_attn" if latent else "diff_attn_ctx",
    )(lam, p, p, p, p, p, subln_gain)

    gps = GQA_GROUPS_PER_STEP
    wq = gps * GQA_REP * HEAD_DIM
    wkv = gps * w2
    yg = pl.pallas_call(
        functools.partial(_gqa_attn_kernel, **statics),
        grid=(B, N_GQA_KV // gps, nq),
        in_specs=[
            pl.BlockSpec((tq, wq), lambda b, g, qi: (q_row(b, qi), GQ_OFF // wq + g)),
            pl.BlockSpec((S, wkv), lambda b, g, qi: (lat_blk(b), GKV_OFF // wkv + g)),
            pl.BlockSpec((C, wkv), lambda b, g, qi: (ctx_blk0 + b, GKV_OFF // wkv + g)),
        ],
        out_specs=pl.BlockSpec((tq, wq), lambda b, g, qi: (out_row(b, qi), g)),
        out_shape=jax.ShapeDtypeStruct((rows_out, N_GQA_HEADS * HEAD_DIM), BF16),
        scratch_shapes=[
            pltpu.VMEM((gps, T, w2), BF16),
            pltpu.VMEM((gps, T, w2), FP8),
            pltpu.VMEM((gps, T // tk, w2, tk), BF16),
            pltpu.VMEM((gps, 1, w2), F32),
            pltpu.VMEM((gps, 1, w2), F32),
            pltpu.VMEM((gps * GQA_REP, 1, tq), F32),
            pltpu.VMEM((gps * GQA_REP, w2, tq), F32),
        ],
        compiler_params=cparams,
        name="gqa_attn" if latent else "gqa_attn_ctx",
    )(p, p, p)
    return yd, yg


def _mixer_kernel(x_ref, yd_ref, yg_ref, gd_ref, gg_ref, gt_ref, wpd_ref, wpg_ref, wo_ref, o_ref):
    pd = jnp.dot(yd_ref[...], wpd_ref[...], preferred_element_type=F32)
    pg = jnp.dot(yg_ref[...], wpg_ref[...], preferred_element_type=F32)
    m = jax.nn.sigmoid(gd_ref[...].astype(F32)) * pd + jax.nn.sigmoid(gg_ref[...].astype(F32)) * pg
    o_ref[...] = x_ref[...] + gt_ref[...] * jnp.dot(m.astype(BF16), wo_ref[...], preferred_element_type=F32)


def _mixer_out(xs, yd, yg, p, mods, wpd, wpg, wo, *, rows, tm, tiles_per_seq, n_batch):
    row_to_mod = lambda i: jnp.minimum(i // tiles_per_seq, n_batch)
    row_tile = pl.BlockSpec((tm, D_MODEL), lambda i: (i, 0))
    weight = pl.BlockSpec((D_MODEL, D_MODEL), lambda i: (0, 0))
    return pl.pallas_call(
        _mixer_kernel,
        grid=(rows // tm,),
        in_specs=[
            row_tile, row_tile, row_tile,
            pl.BlockSpec((tm, D_MODEL), lambda i: (i, GATE_OFF // D_MODEL)),
            pl.BlockSpec((tm, D_MODEL), lambda i: (i, GATE_OFF // D_MODEL + 1)),
            _mod_spec(2, row_to_mod),
            weight, weight, weight,
        ],
        out_specs=row_tile,
        out_shape=jax.ShapeDtypeStruct((rows, D_MODEL), F32),
        compiler_params=pltpu.CompilerParams(
            dimension_semantics=("arbitrary",), vmem_limit_bytes=VMEM_LIMIT),
        name="mixer_out",
    )(xs, yd, yg, p, p, mods, wpd, wpg, wo)


def _ffn_kernel(x_ref, g_ref, sc_ref, sh_ref, gt_ref, w1_ref, w3_ref, w2_ref, o_ref, h_scr, acc_scr):
    f = pl.program_id(1)

    @pl.when(f == 0)
    def _():
        h_scr[...] = _modulated_norm(x_ref[...], g_ref[...], sc_ref[...], sh_ref[...]).astype(BF16)
        acc_scr[...] = jnp.zeros_like(acc_scr)

    h = h_scr[...]
    a = jnp.dot(h, w1_ref[...], preferred_element_type=F32)
    b = jnp.dot(h, w3_ref[...], preferred_element_type=F32)
    acc_scr[...] += jnp.dot((_silu(a) * b).astype(BF16), w2_ref[...], preferred_element_type=F32)

    @pl.when(f == pl.num_programs(1) - 1)
    def _():
        o_ref[...] = x_ref[...] + gt_ref[...] * acc_scr[...]


def _ffn(xs, mods, norm_g, w1, w3, w2, *, tm, tf, tiles_per_seq, n_batch):
    rows = xs.shape[0]
    d_ff = w1.shape[1]
    row_to_mod = lambda i: jnp.minimum(i // tiles_per_seq, n_batch)
    row_tile = pl.BlockSpec((tm, D_MODEL), lambda i, f: (i, 0))
    return pl.pallas_call(
        _ffn_kernel,
        grid=(rows // tm, d_ff // tf),
        in_specs=[
            row_tile,
            pl.BlockSpec((1, D_MODEL), lambda i, f: (0, 0)),
            _mod_spec(4, row_to_mod), _mod_spec(3, row_to_mod), _mod_spec(5, row_to_mod),
            pl.BlockSpec((D_MODEL, tf), lambda i, f: (0, f)),
            pl.BlockSpec((D_MODEL, tf), lambda i, f: (0, f)),
            pl.BlockSpec((tf, D_MODEL), lambda i, f: (f, 0)),
        ],
        out_specs=row_tile,
        out_shape=jax.ShapeDtypeStruct((rows, D_MODEL), F32),
        scratch_shapes=[pltpu.VMEM((tm, D_MODEL), BF16), pltpu.VMEM((tm, D_MODEL), F32)],
        compiler_params=pltpu.CompilerParams(
            dimension_semantics=("arbitrary", "arbitrary"), vmem_limit_bytes=VMEM_LIMIT),
        name="ffn_swiglu",
    )(xs, norm_g, mods, mods, mods, w1, w3, w2)


def _top2_gates(logits):
    lane = lax.broadcasted_iota(jnp.int32, logits.shape, 1)
    n_lanes = logits.shape[1]
    lg = jnp.where(lane < N_EXPERTS, logits, NEG_BIG)
    m1 = jnp.max(lg, axis=-1, keepdims=True)
    i1 = jnp.min(jnp.where(lg == m1, lane, n_lanes), axis=-1, keepdims=True)
    lg2 = jnp.where(lane == i1, NEG_BIG, lg)
    m2 = jnp.max(lg2, axis=-1, keepdims=True)
    i2 = jnp.min(jnp.where(lg2 == m2, lane, n_lanes), axis=-1, keepdims=True)
    e2 = jnp.exp(m2 - m1)
    w_top = 1.0 / (1.0 + e2)
    idx = jnp.where(lane == 0, i1, jnp.where(lane == 1, i2, 0))
    wts = jnp.where(lane == 0, w_top, jnp.where(lane == 1, e2 * w_top, 0.0))
    return idx, wts


def _router_kernel(x_ref, g_ref, sc_ref, sh_ref, rhi_ref, rlo_ref, h_ref, idx_ref, wts_ref):
    h = _modulated_norm(x_ref[...], g_ref[...], sc_ref[...], sh_ref[...])
    h_ref[...] = h
    h_hi, h_lo = _split_bf16(h)
    logits = jnp.dot(h_hi, rhi_ref[...], preferred_element_type=F32)
    logits += jnp.dot(h_lo, rhi_ref[...], preferred_element_type=F32)
    logits += jnp.dot(h_hi, rlo_ref[...], preferred_element_type=F32)
    idx_ref[...], wts_ref[...] = _top2_gates(logits)


def _router(xs, mods, norm_g, r_hi, r_lo, *, rows, tm, tiles_per_seq, n_batch):
    row_to_mod = lambda i: jnp.minimum(i // tiles_per_seq, n_batch)
    row_tile = pl.BlockSpec((tm, D_MODEL), lambda i: (i, 0))
    lanes = pl.BlockSpec((tm, 128), lambda i: (i, 0))
    router = pl.BlockSpec((D_MODEL, 128), lambda i: (0, 0))
    return pl.pallas_call(
        _router_kernel,
        grid=(rows // tm,),
        in_specs=[row_tile, pl.BlockSpec((1, D_MODEL), lambda i: (0, 0)),
                  _mod_spec(4, row_to_mod), _mod_spec(3, row_to_mod), router, router],
        out_specs=[row_tile, lanes, lanes],
        out_shape=[jax.ShapeDtypeStruct((rows, D_MODEL), F32),
                   jax.ShapeDtypeStruct((rows, 128), jnp.int32),
                   jax.ShapeDtypeStruct((rows, 128), F32)],
        compiler_params=pltpu.CompilerParams(
            dimension_semantics=("arbitrary",), vmem_limit_bytes=VMEM_LIMIT),
        name="moe_router",
    )(xs, norm_g, mods, mods, r_hi, r_lo)


def _route_plan(idx, n_tok, tm_e):
    n_pairs = 2 * n_tok
    n_tiles = n_pairs // tm_e + N_EXPERTS + 1
    n_rows = n_tiles * tm_e
    e_flat = idx.reshape(n_pairs)
    order = jnp.argsort(e_flat, stable=True).astype(jnp.int32)
    counts = jnp.sum((e_flat[:, None] == jnp.arange(N_EXPERTS, dtype=jnp.int32)[None, :]).astype(jnp.int32),
                     axis=0)
    padded = ((counts + tm_e - 1) // tm_e) * tm_e
    ends = jnp.cumsum(padded)
    tile_start = jnp.arange(n_tiles, dtype=jnp.int32) * tm_e
    tile_expert = jnp.minimum(jnp.sum((tile_start[:, None] >= ends[None, :]).astype(jnp.int32), axis=1),
                              N_EXPERTS - 1).astype(jnp.int32)
    n_used = (ends[-1] // tm_e).astype(jnp.int32).reshape(1)
    row = jnp.arange(n_rows, dtype=jnp.int32)
    row_expert = jnp.repeat(tile_expert, tm_e)
    offset = row - (ends - padded)[row_expert]
    valid = (offset < counts[row_expert]) & (row < ends[-1])
    pair = order[jnp.clip((jnp.cumsum(counts) - counts)[row_expert] + offset, 0, n_pairs - 1)]
    tok, choice = pair // 2, pair % 2
    src = jnp.where(valid, tok, 0)
    dump = n_pairs + jnp.cumsum(jnp.logical_not(valid).astype(jnp.int32)) - 1
    dst = jnp.where(valid, choice * n_tok + tok, dump)
    return tile_expert, n_used, src.reshape(n_tiles, 1, tm_e), dst.reshape(n_tiles, 1, tm_e)


def _experts_kernel(te_ref, nu_ref, src_ref, src_next_ref, dst_ref, h_hbm, w1_ref, w3_ref, w2_ref, y_hbm,
                    xbuf, xb_scr, acc_scr, ybuf, gsem, ssem, *, tm_e, n_chunks):
    t = pl.program_id(0)
    f = pl.program_id(1)
    nf = pl.num_programs(1)
    n_used = nu_ref[0]
    slot = t % 2

    def gather_rows(idx_ref, s):
        def body(r, carry):
            pltpu.make_async_copy(h_hbm.at[pl.ds(idx_ref[0, r], 1)], xbuf.at[s, pl.ds(r, 1)], gsem.at[s]).start()
            return carry
        lax.fori_loop(0, tm_e, body, 0, unroll=8)

    def wait_gather(s):
        pltpu.make_async_copy(h_hbm.at[pl.ds(0, tm_e)], xbuf.at[s], gsem.at[s]).wait()

    def scatter_rows(s):
        def body(r, carry):
            pltpu.make_async_copy(ybuf.at[s, pl.ds(r, 1)], y_hbm.at[pl.ds(dst_ref[0, r], 1)], ssem.at[s]).start()
            return carry
        lax.fori_loop(0, tm_e, body, 0, unroll=8)

    def wait_scatter(s):
        pltpu.make_async_copy(ybuf.at[s], y_hbm.at[pl.ds(0, tm_e)], ssem.at[s]).wait()

    @pl.when((f == 0) & (t <= n_used))
    def _():
        @pl.when(t == 0)
        def _():
            gather_rows(src_ref, 0)
        for s in range(2):
            @pl.when(slot == s)
            def _():
                wait_gather(s)

                @pl.when(t < n_used)
                def _():
                    xb_scr[...] = xbuf[s].astype(BF16)
                    acc_scr[...] = jnp.zeros_like(acc_scr)

    @pl.when(t < n_used)
    def _():
        x = xb_scr[...]
        a = jnp.dot(x, w1_ref[...].astype(BF16), preferred_element_type=F32)
        b = jnp.dot(x, w3_ref[...].astype(BF16), preferred_element_type=F32)
        acc_scr[...] += jnp.dot((_silu(a) * b).astype(BF16), w2_ref[...].astype(BF16),
                                preferred_element_type=F32)
        share = tm_e // n_chunks
        for i in range(share):
            r = f * share + i
            pltpu.make_async_copy(h_hbm.at[pl.ds(src_next_ref[0, r], 1)],
                                  xbuf.at[1 - slot, pl.ds(r, 1)], gsem.at[1 - slot]).start()

    @pl.when(f == nf - 1)
    def _():
        for s in range(2):
            @pl.when(slot == s)
            def _():
                @pl.when(t >= 2)
                def _():
                    wait_scatter(s)

                @pl.when(t < n_used)
                def _():
                    ybuf[s] = acc_scr[...]

                @pl.when(t >= n_used)
                def _():
                    ybuf[s] = jnp.zeros(ybuf.shape[1:], F32)

                scatter_rows(s)

                @pl.when(t == pl.num_programs(0) - 1)
                def _():
                    wait_scatter(s)
                    wait_scatter(1 - s)


def _experts(h, plan, w1, w3, w2, *, n_tok, tm_e, tf):
    tile_expert, n_used, src, dst = plan
    n_tiles = src.shape[0]
    d_ff = w1.shape[2]
    nf = d_ff // tf
    n_out = n_tiles * tm_e

    def chunk(t, f, nu):
        return jnp.where(t < nu[0], f, nf - 1)

    smem_tile = lambda shift: pl.BlockSpec(
        (None, 1, tm_e), lambda t, f, te, nu: (jnp.minimum(t + shift, n_tiles - 1), 0, 0),
        memory_space=pltpu.SMEM)
    grid_spec = pltpu.PrefetchScalarGridSpec(
        num_scalar_prefetch=2,
        grid=(n_tiles, nf),
        in_specs=[
            smem_tile(0), smem_tile(1), smem_tile(0),
            pl.BlockSpec(memory_space=pl.ANY),
            pl.BlockSpec((None, D_MODEL, tf), lambda t, f, te, nu: (te[t], 0, chunk(t, f, nu))),
            pl.BlockSpec((None, D_MODEL, tf), lambda t, f, te, nu: (te[t], 0, chunk(t, f, nu))),
            pl.BlockSpec((None, tf, D_MODEL), lambda t, f, te, nu: (te[t], chunk(t, f, nu), 0)),
        ],
        out_specs=pl.BlockSpec(memory_space=pl.ANY),
        scratch_shapes=[
            pltpu.VMEM((2, tm_e, D_MODEL), F32),
            pltpu.VMEM((tm_e, D_MODEL), BF16),
            pltpu.VMEM((tm_e, D_MODEL), F32),
            pltpu.VMEM((2, tm_e, D_MODEL), F32),
            pltpu.SemaphoreType.DMA((2,)),
            pltpu.SemaphoreType.DMA((2,)),
        ],
    )
    return pl.pallas_call(
        functools.partial(_experts_kernel, tm_e=tm_e, n_chunks=nf),
        grid_spec=grid_spec,
        out_shape=jax.ShapeDtypeStruct((n_out, D_MODEL), F32),
        compiler_params=pltpu.CompilerParams(
            dimension_semantics=("arbitrary", "arbitrary"), vmem_limit_bytes=VMEM_LIMIT),
        name="moe_experts",
    )(tile_expert, n_used, src, src, dst, h, w1, w3, w2)


def _combine_kernel(x_ref, y0_ref, y1_ref, wts_ref, gt_ref, fg_ref, o_ref):
    wts = wts_ref[...]
    moe = wts[:, 0:1] * y0_ref[...] + wts[:, 1:2] * y1_ref[...]
    y = x_ref[...] + gt_ref[...] * moe
    ms = jnp.mean(y * y, axis=-1, keepdims=True)
    o_ref[...] = y * lax.rsqrt(ms + EPS) * fg_ref[...]


def _combine(xs, y, wts, mods, final_g, *, rows, tm, tiles_per_seq, n_batch):
    row_to_mod = lambda i: jnp.minimum(i // tiles_per_seq, n_batch)
    row_tile = pl.BlockSpec((tm, D_MODEL), lambda i: (i, 0))
    return pl.pallas_call(
        _combine_kernel,
        grid=(rows // tm,),
        in_specs=[
            row_tile, row_tile,
            pl.BlockSpec((tm, D_MODEL), lambda i: (i + rows // tm, 0)),
            pl.BlockSpec((tm, 128), lambda i: (i, 0)),
            _mod_spec(5, row_to_mod),
            pl.BlockSpec((1, D_MODEL), lambda i: (0, 0)),
        ],
        out_specs=row_tile,
        out_shape=jax.ShapeDtypeStruct((rows, D_MODEL), F32),
        compiler_params=pltpu.CompilerParams(
            dimension_semantics=("arbitrary",), vmem_limit_bytes=VMEM_LIMIT),
        name="moe_combine",
    )(xs, y, y, wts, mods, final_g)


def _moe(xs, mods, norm_g, r_hi, r_lo, final_g, w1, w3, w2, *, rows, tm, tm_e, tf, tiles_per_seq, n_batch):
    h, idx, wts = _router(xs, mods, norm_g, r_hi, r_lo, rows=rows, tm=tm, tiles_per_seq=tiles_per_seq,
                          n_batch=n_batch)
    plan = _route_plan(idx[:, :2], rows, tm_e)
    y = _experts(h, plan, w1, w3, w2, n_tok=rows, tm_e=tm_e, tf=tf)
    return _combine(xs, y, wts, mods, final_g, rows=rows, tm=tm, tiles_per_seq=tiles_per_seq, n_batch=n_batch)


def _deinterleave(n=HEAD_DIM):
    return np.concatenate([np.arange(0, n, 2), np.arange(1, n, 2)])


def _proj_columns():
    de = _deinterleave()
    o_dq, o_gq, o_dk, o_dv, o_gk, o_gv, o_gate = 0, 1024, 2048, 3072, 4096, 4352, 4608
    cols = np.zeros(IN_W, np.int32)
    scale = np.ones(IN_W, np.float32)
    cols[GATE_OFF:GATE_OFF + 2048] = o_gate + np.arange(2048)
    for h in range(N_DIFF_HEADS):
        for c in range(2):
            dst = 128 * h + 64 * c
            cols[DQ_OFF + dst:DQ_OFF + dst + 64] = o_dq + dst + de
            cols[DK_OFF + dst:DK_OFF + dst + 64] = o_dk + dst + de
    scale[DQ_OFF:DQ_OFF + 1024] = HEAD_DIM ** -0.5
    for j in range(N_GQA_HEADS):
        cols[GQ_OFF + 64 * j:GQ_OFF + 64 * j + 64] = o_gq + 64 * j + de
    cols[DV_OFF:DV_OFF + 1024] = o_dv + np.arange(1024)
    for g in range(N_GQA_KV):
        cols[GKV_OFF + 128 * g:GKV_OFF + 128 * g + 64] = o_gk + 64 * g + de
        cols[GKV_OFF + 128 * g + 64:GKV_OFF + 128 * g + 128] = o_gv + 64 * g + np.arange(64)
    return cols, scale


def _rope_tables(S, pad_rows):
    rows = S // GRID_W
    row = jnp.repeat(jnp.arange(rows, dtype=F32), GRID_W)
    col = jnp.tile(jnp.arange(GRID_W, dtype=F32), rows)
    half = HEAD_DIM // 2
    inv_freq = ROPE_THETA ** (-jnp.arange(0, half, 2, dtype=F32) / half)
    ang = jnp.concatenate([row[:, None] * inv_freq, col[:, None] * inv_freq], axis=-1)
    cos, sin = jnp.cos(ang), jnp.sin(ang)
    reps = PROJ_TN // HEAD_DIM
    cos_t = jnp.tile(jnp.concatenate([cos, cos], axis=-1), (1, reps))
    sin_t = jnp.tile(jnp.concatenate([-sin, sin], axis=-1), (1, reps))
    cos_t = jnp.concatenate([cos_t, jnp.ones((pad_rows, PROJ_TN), F32)], axis=0)
    sin_t = jnp.concatenate([sin_t, jnp.zeros((pad_rows, PROJ_TN), F32)], axis=0)
    return cos_t, sin_t


def kernel(x, c, ctx, c_ctx, ada_w, ada_b, norm_attn_g, norm_ffn_g, w_in, q_norm_g, k_norm_g, diff_lambda,
           diff_subln_g, w_proj_diff, w_proj_gqa, w_out, ffn_w1, ffn_w3, ffn_w2, moe_router, moe_w1, moe_w3,
           moe_w2, final_norm_g):
    B, S, D = x.shape
    C = ctx.shape[1]
    depth = ada_w.shape[0]
    assert D == D_MODEL and depth == 2 and B + 1 <= MOD_ROWS
    assert w_in.shape[2] == IN_W and moe_router.shape[2] == N_EXPERTS
    tiles = _pick_tiles(B, S, C, ffn_w1.shape[2], moe_w1.shape[3])
    n_lat = B * S

    cvec = jnp.zeros((MOD_ROWS, D), F32).at[:B].set(c).at[B].set(c_ctx)
    mods_all = _ada_mods(cvec, ada_w, ada_b).reshape(depth, MOD_ROWS * N_MODS, 1, D)

    cols, col_scale = _proj_columns()
    de = _deinterleave()
    cos_t, sin_t = _rope_tables(S, tiles.tm_proj)
    blk = np.arange(PROJ_TN) // HEAD_DIM
    ones_bd = jnp.asarray(blk[:, None] == blk[None, :], BF16)
    col = np.arange(PROJ_TN)
    swap_bd = jnp.asarray(col[:, None] == (col[None, :] ^ (HEAD_DIM // 2)), BF16)
    is_key_lane = (np.arange(PROJ_TN) % (2 * HEAD_DIM)) < HEAD_DIM

    xs = jnp.concatenate([x.reshape(n_lat, D), ctx.reshape(B * C, D)], axis=0)
    for l in range(depth):
        last = l == depth - 1
        mods = mods_all[l]
        lam_init = 0.8 - 0.6 * math.exp(-0.3 * l)
        lq1, lk1, lq2, lk2 = diff_lambda[l]
        lam = (jnp.exp(jnp.sum(lq1 * lk1)) - jnp.exp(jnp.sum(lq2 * lk2)) + lam_init).reshape(1).astype(F32)
        w = (jnp.take(w_in[l], cols, axis=1) * col_scale).astype(BF16)
        qg = jnp.tile(q_norm_g[l][de] * HEAD_DIM ** -0.5, PROJ_TN // HEAD_DIM).reshape(1, PROJ_TN)
        kg = jnp.where(is_key_lane, jnp.tile(k_norm_g[l][de], PROJ_TN // HEAD_DIM), 1.0).reshape(1, PROJ_TN)
        subln = (diff_subln_g[l] * (1.0 - lam_init)).reshape(1, 2 * HEAD_DIM)

        p = _project(xs, mods, norm_attn_g[l].reshape(1, D), w, cos_t, sin_t, ones_bd, swap_bd, qg, kg,
                     tm=tiles.tm_proj, lat_tiles=n_lat // tiles.tm_proj, tiles_per_seq=S // tiles.tm_proj,
                     n_batch=B)
        yd, yg = _attention(p, lam, subln, B=B, S=S, C=C, tiles=tiles, ctx_queries=not last)
        rows = n_lat if last else n_lat + B * C
        xs = _mixer_out(xs, yd, yg, p, mods, w_proj_diff[l].astype(BF16), w_proj_gqa[l].astype(BF16),
                        w_out[l].astype(BF16), rows=rows, tm=tiles.tm_mix, tiles_per_seq=S // tiles.tm_mix,
                        n_batch=B)
        i = l // 2
        if l % 2 == 0:
            xs = _ffn(xs, mods, norm_ffn_g[l].reshape(1, D), ffn_w1[i].astype(BF16), ffn_w3[i].astype(BF16),
                      ffn_w2[i].astype(BF16), tm=tiles.tm_ffn, tf=tiles.tf_ffn,
                      tiles_per_seq=S // tiles.tm_ffn, n_batch=B)
        else:
            r_pad = jnp.zeros((D, 128), F32).at[:, :N_EXPERTS].set(moe_router[i])
            r_hi = r_pad.astype(BF16)
            r_lo = (r_pad - r_hi.astype(F32)).astype(BF16)
            xs = _moe(xs, mods, norm_ffn_g[l].reshape(1, D), r_hi, r_lo, final_norm_g.reshape(1, D),
                      moe_w1[i], moe_w3[i], moe_w2[i],
                      rows=rows, tm=tiles.tm_mix, tm_e=tiles.tm_moe, tf=tiles.tf_moe,
                      tiles_per_seq=S // tiles.tm_mix, n_batch=B)
    return xs.reshape(B, S, D)
```

```python
import functools
import math
from typing import NamedTuple

import numpy as np
import jax
import jax.numpy as jnp
from jax import lax
from jax.experimental import pallas as pl
from jax.experimental.pallas import tpu as pltpu

F32 = jnp.float32
BF16 = jnp.bfloat16

D_MODEL = 1024
HEAD_DIM = 64
N_DIFF_HEADS = 8
N_GQA_HEADS = 16
N_GQA_KV = 4
GQA_REP = N_GQA_HEADS // N_GQA_KV
N_EXPERTS = 8
GRID_W = 64
ROPE_THETA = 10000.0
EPS = 1e-6
N_MODS = 6
MOD_ROWS = 16
NEG_BIG = -1e30

GATE_OFF = 0
DQ_OFF = 2048
GQ_OFF = 3072
DK_OFF = 4096
DV_OFF = 5120
GKV_OFF = 6144
IN_W = 6656
PROJ_TN = 512
ROPE_TILES = (4, 5, 8, 9)
QNORM_TILES = (6, 7)
KV_TILE = 12

VMEM_LIMIT = 52 * 1024 * 1024


class Tiles(NamedTuple):
    tm_proj: int
    tq: int
    tk: int
    tm_mix: int
    tm_ffn: int
    tf_ffn: int
    tf_moe: int
    tm_moe: int


def _largest_divisor(n, candidates):
    for c in candidates:
        if n % c == 0:
            return c
    raise ValueError(f"no tile in {candidates} divides {n}")


def _pick_tiles(B, S, C, d_ff, d_ff_e):
    rows_common = math.gcd(S, B * C)
    tm = _largest_divisor(rows_common, (512, 256, 128))
    return Tiles(
        tm_proj=_largest_divisor(rows_common, (1024, 512, 256, 128)),
        tq=_largest_divisor(S, (256, 128)),
        tk=_largest_divisor(math.gcd(S, C), (256, 128)),
        tm_mix=tm,
        tm_ffn=_largest_divisor(rows_common, (1024, 512, 256, 128)),
        tf_ffn=_largest_divisor(d_ff, (1408, 1024, 512, 256, 128)),
        tf_moe=_largest_divisor(d_ff_e, (896, 512, 256, 128)),
        tm_moe=_largest_divisor(2 * B * S, (512, 256)),
    )


def _split_bf16(v):
    hi = v.astype(BF16)
    lo = (v - hi.astype(F32)).astype(BF16)
    return hi, lo


def _ada_kernel(c_ref, w_ref, b_ref, o_ref):
    c = c_ref[...]
    s = c / (1.0 + jnp.exp(-c))
    s_hi, s_lo = _split_bf16(s)
    w_hi, w_lo = _split_bf16(w_ref[...])
    acc = jnp.dot(s_hi, w_hi, preferred_element_type=F32)
    acc += jnp.dot(s_lo, w_hi, preferred_element_type=F32)
    acc += jnp.dot(s_hi, w_lo, preferred_element_type=F32)
    o_ref[...] = acc + b_ref[...]


def _ada_mods(cvec, ada_w, ada_b):
    depth, d, n = ada_w.shape
    tn = _largest_divisor(n, (1536, 1024, 512))
    return pl.pallas_call(
        _ada_kernel,
        grid=(depth, n // tn),
        in_specs=[
            pl.BlockSpec((MOD_ROWS, d), lambda l, j: (0, 0)),
            pl.BlockSpec((None, d, tn), lambda l, j: (l, 0, j)),
            pl.BlockSpec((None, 1, tn), lambda l, j: (l, 0, j)),
        ],
        out_specs=pl.BlockSpec((None, MOD_ROWS, tn), lambda l, j: (l, 0, j)),
        out_shape=jax.ShapeDtypeStruct((depth, MOD_ROWS, n), F32),
        compiler_params=pltpu.CompilerParams(
            dimension_semantics=("arbitrary", "arbitrary"), vmem_limit_bytes=VMEM_LIMIT),
        name="ada_mods",
    )(cvec, ada_w, ada_b.reshape(depth, 1, n))


def _modulated_norm(x, g, scale, shift):
    ms = jnp.mean(x * x, axis=-1, keepdims=True)
    return (x * lax.rsqrt(ms + EPS) * g) * (1.0 + scale) + shift


def _silu(a):
    return a / (1.0 + jnp.exp(-a))


def _mod_spec(k, row_to_mod):
    return pl.BlockSpec((None, 1, D_MODEL), lambda i, *_: (row_to_mod(i) * N_MODS + k, 0, 0))


def _rope(z, cos, sin_signed, swap_ref):
    partner = jnp.dot(z.astype(BF16), swap_ref[...], preferred_element_type=F32)
    return z * cos + partner * sin_signed


def _head_rms(z, ones_ref, gain):
    ss = jnp.dot((z * z).astype(BF16), ones_ref[...], preferred_element_type=F32)
    return z * lax.rsqrt(ss * (1.0 / HEAD_DIM) + EPS) * gain


def _split_rows(xs, tm):
    if isinstance(xs, tuple):
        return xs[0], xs[1], xs[0].shape[0] // tm
    return xs, xs, xs.shape[0] // tm


def _row_specs(tm, lat_tiles):
    return (pl.BlockSpec((tm, D_MODEL), lambda i, *_: (jnp.minimum(i, lat_tiles - 1), 0)),
            pl.BlockSpec((tm, D_MODEL), lambda i, *_: (jnp.maximum(i - lat_tiles, 0), 0)))


def _select_rows(lat_ref, ctx_ref, lat_tiles):
    return jnp.where(pl.program_id(0) >= lat_tiles, ctx_ref[...], lat_ref[...])


def _proj_kernel(xl_ref, xc_ref, g_ref, sc_ref, sh_ref, w_ref, cos_ref, sin_ref, ones_ref, swap_ref, qg_ref, kg_ref,
                 o_ref, h_scr, *, lat_tiles):
    j = pl.program_id(1)

    @pl.when(j == 0)
    def _():
        x = _select_rows(xl_ref, xc_ref, lat_tiles)
        h_scr[...] = _modulated_norm(x, g_ref[...], sc_ref[...], sh_ref[...]).astype(BF16)

    def project():
        z = jnp.dot(h_scr[...], w_ref[...], preferred_element_type=F32)
        return z, lax.broadcasted_iota(jnp.int32, z.shape, 1)

    is_rope = functools.reduce(jnp.logical_or, [j == t for t in ROPE_TILES])
    is_qnorm = functools.reduce(jnp.logical_or, [j == t for t in QNORM_TILES])
    is_kv = j == KV_TILE
    is_plain = jnp.logical_not(is_rope | is_qnorm | is_kv)

    @pl.when(is_plain)
    def _():
        z, _ = project()
        o_ref[...] = z.astype(BF16)

    @pl.when(is_rope)
    def _():
        z, lane = project()
        o_ref[...] = _rope(z, cos_ref[...], sin_ref[...], swap_ref).astype(BF16)

    @pl.when(is_qnorm)
    def _():
        z, lane = project()
        zn = _head_rms(z, ones_ref, qg_ref[...])
        o_ref[...] = _rope(zn, cos_ref[...], sin_ref[...], swap_ref).astype(BF16)

    @pl.when(is_kv)
    def _():
        z, lane = project()
        zn = _head_rms(z, ones_ref, kg_ref[...])
        zr = _rope(zn, cos_ref[...], sin_ref[...], swap_ref)
        is_key_lane = (lane & HEAD_DIM) == 0
        o_ref[...] = jnp.where(is_key_lane, zr, z).astype(BF16)


def _project(xs, mods, norm_g, w, cos_t, sin_t, ones_bd, swap_bd, qg, kg, *, tm, lat_tiles, tiles_per_seq, n_batch):
    x_lat, x_ctx, src_tiles = _split_rows(xs, tm)
    rows = x_lat.shape[0] + (x_ctx.shape[0] if isinstance(xs, tuple) else 0)
    row_to_mod = lambda i: jnp.minimum(i // tiles_per_seq, n_batch)
    rope_row = lambda i: jnp.where(i < lat_tiles, i % tiles_per_seq, tiles_per_seq)
    const = lambda i, j: (0, 0)
    return pl.pallas_call(
        functools.partial(_proj_kernel, lat_tiles=src_tiles),
        grid=(rows // tm, IN_W // PROJ_TN),
        in_specs=[
            *_row_specs(tm, src_tiles),
            pl.BlockSpec((1, D_MODEL), const),
            _mod_spec(1, row_to_mod),
            _mod_spec(0, row_to_mod),
            pl.BlockSpec((D_MODEL, PROJ_TN), lambda i, j: (0, j)),
            pl.BlockSpec((tm, PROJ_TN), lambda i, j: (rope_row(i), 0)),
            pl.BlockSpec((tm, PROJ_TN), lambda i, j: (rope_row(i), 0)),
            pl.BlockSpec((PROJ_TN, PROJ_TN), const),
            pl.BlockSpec((PROJ_TN, PROJ_TN), const),
            pl.BlockSpec((1, PROJ_TN), const),
            pl.BlockSpec((1, PROJ_TN), const),
        ],
        out_specs=pl.BlockSpec((tm, PROJ_TN), lambda i, j: (i, j)),
        out_shape=jax.ShapeDtypeStruct((rows, IN_W), BF16),
        scratch_shapes=[pltpu.VMEM((tm, D_MODEL), BF16)],
        compiler_params=pltpu.CompilerParams(
            dimension_semantics=("arbitrary", "arbitrary"), vmem_limit_bytes=VMEM_LIMIT),
        name="in_proj",
    )(x_lat, x_ctx, norm_g, mods, mods, w, cos_t, sin_t, ones_bd, swap_bd, qg, kg)


STAB_LANE = HEAD_DIM
L_FLOOR = 1e-26
L_CEIL = 1e30
FP8 = jnp.float8_e4m3fn
FP8_TARGET = 256.0
FP8_BOUND_SLACK = 1.125
LOG2_E = 1.4426950408889634
DIFF_VT_ROWS = 2 * HEAD_DIM + 16
DIFF_HEADS_PER_STEP = 2
GQA_GROUPS_PER_STEP = 1


def _aug_keys(k):
    lane = lax.broadcasted_iota(jnp.int32, k.shape, 1)
    in_key = lane < HEAD_DIM
    sq = jnp.where(in_key, k * k, 0.0).astype(BF16)
    norms = jnp.dot(sq, jnp.ones((k.shape[1], k.shape[1]), BF16), preferred_element_type=F32)
    ka = jnp.where(in_key, k, jnp.where(lane == STAB_LANE, 1.0, 0.0)).astype(BF16)
    return ka, jnp.max(norms, axis=0, keepdims=True)


def _pow2_floor(x):
    bits = lax.bitcast_convert_type(x, jnp.int32) & jnp.int32(0x7F800000)
    return lax.bitcast_convert_type(bits, F32)


def _fp8_keys(ka_scr, k8_scr, kscale_scr, kmax, *, n_chunks, tk):
    for r, kmax2 in enumerate(kmax):
        scale = _pow2_floor(FP8_TARGET * lax.rsqrt(jnp.maximum(kmax2, 1e-30)))
        kscale_scr[r] = scale
        for c in range(n_chunks):
            k = ka_scr[r, c * tk:(c + 1) * tk, :].astype(F32)
            lane = lax.broadcasted_iota(jnp.int32, k.shape, 1)
            k8_scr[r, c * tk:(c + 1) * tk, :] = jnp.where(lane == STAB_LANE, FP8_TARGET, k * scale).astype(FP8)


def _query_mats(slab, kmax2, kscale):
    lane = lax.broadcasted_iota(jnp.int32, slab.shape, 1)
    in_key = lane < HEAD_DIM
    sq = jnp.where(in_key, slab * slab, 0.0).astype(BF16)
    norms = jnp.dot(sq, jnp.ones((slab.shape[1], slab.shape[1]), BF16), preferred_element_type=F32)
    nk = norms * kmax2
    bound = FP8_BOUND_SLACK * nk * lax.rsqrt(nk + 1e-30)
    qscale = _pow2_floor(FP8_TARGET * lax.rsqrt(jnp.maximum(jnp.max(norms, axis=0, keepdims=True), 1e-30)))
    both = kscale * qscale
    plain = jnp.where(in_key, slab, 0.0)
    shifted = jnp.where(lane == STAB_LANE, -bound * (both * (1.0 / FP8_TARGET)), plain * qscale).astype(FP8)
    return shifted, plain.astype(BF16), (LOG2_E / both)[:, :1]


def _attend(k8_scr, vt_scr, mats, acc_scr, *, first, n_chunks, tk):
    def scores(r, c, qmat):
        off = pl.multiple_of(c * tk, tk)
        return lax.dot_general(k8_scr[r * k8_scr.shape[0] // len(mats), pl.ds(off, tk), :], qmat,
                               (((1,), (1,)), ((), ())), preferred_element_type=F32)

    acc_scr[...] = jnp.zeros(acc_scr.shape, F32)

    def all_scores(c):
        return tuple(scores(r, c, shifted) for r, (shifted, _, _) in enumerate(mats))

    def accumulate(c, s_all):
        for r, s in enumerate(s_all):
            vt = vt_scr[r * vt_scr.shape[0] // len(mats), c]
            p = jnp.exp2((s * mats[r][2]).astype(BF16))
            acc_scr[r] += jnp.dot(vt, p, preferred_element_type=F32)

    def fast_body(c, s_cur):
        s_next = all_scores(c + 1)
        accumulate(c, s_cur)
        return s_next

    trips = n_chunks - 1 - first
    s_last = lax.fori_loop(first, n_chunks - 1, fast_body, all_scores(first),
                           unroll=_largest_divisor(trips, (16, 8, 4, 2, 1)) if trips > 0 else 1)
    accumulate(n_chunks - 1, s_last)


def _attend_fallback(ka_scr, vt_scr, mats, m_scr, acc_scr, *, first, n_chunks, tk):
    m_scr[...] = jnp.full(m_scr.shape, NEG_BIG, F32)
    acc_scr[...] = jnp.zeros(acc_scr.shape, F32)

    def body(c, carry):
        off = pl.multiple_of(c * tk, tk)
        for r, (_, plain, _) in enumerate(mats):
            s = lax.dot_general(ka_scr[r * ka_scr.shape[0] // len(mats), pl.ds(off, tk), :], plain, (((1,), (1,)), ((), ())),
                                preferred_element_type=F32)
            m_old = m_scr[r]
            m_new = jnp.maximum(m_old, jnp.max(s, axis=0, keepdims=True))
            p = jnp.exp(s - m_new).astype(BF16)
            vt = vt_scr[r * vt_scr.shape[0] // len(mats), c]
            acc_scr[r] = acc_scr[r] * jnp.exp(m_old - m_new) + jnp.dot(vt, p, preferred_element_type=F32)
            m_scr[r] = m_new
        return carry

    lax.fori_loop(first, n_chunks, body, 0)


def _softmax_tile(ka_scr, k8_scr, vt_scr, kmax_scr, kscale_scr, slabs, m_scr, acc_scr, l_row,
                  *, n_chunks, tk):
    per_key = lambda scr, r: scr[r * scr.shape[0] // len(slabs)]
    mats = [_query_mats(slab, per_key(kmax_scr, r), per_key(kscale_scr, r)) for r, slab in enumerate(slabs)]
    _attend(k8_scr, vt_scr, mats, acc_scr, first=0, n_chunks=n_chunks, tk=tk)
    dens = [acc_scr[r, l_row:l_row + 1, :] for r in range(len(slabs))]
    l_min = functools.reduce(jnp.minimum, [jnp.min(d) for d in dens])
    l_max = functools.reduce(jnp.maximum, [jnp.max(d) for d in dens])

    @pl.when(jnp.logical_not((l_min >= L_FLOOR) & (l_max <= L_CEIL)))
    def _():
        _attend_fallback(ka_scr, vt_scr, mats, m_scr, acc_scr, first=0, n_chunks=n_chunks, tk=tk)


def _chunk_rows(c, n_lat_chunks, tk, lat_ref, ctx_ref):
    if c < n_lat_chunks:
        return lat_ref[c * tk:(c + 1) * tk, :]
    return ctx_ref[(c - n_lat_chunks) * tk:(c - n_lat_chunks + 1) * tk, :]


def _diff_attn_kernel(lam_ref, q_ref, kl_ref, vl_ref, kc_ref, vc_ref, g_ref, o_ref,
                      ka_scr, k8_scr, vt_scr, kmax_scr, kscale_scr, m_scr, acc_scr,
                      *, n_lat_chunks, n_chunks, tk):
    dv = 2 * HEAD_DIM
    heads = [slice(a * dv, (a + 1) * dv) for a in range(DIFF_HEADS_PER_STEP)]

    @pl.when(pl.program_id(2) == 0)
    def _():
        kmax = [None] * (2 * len(heads))
        for c in range(n_chunks):
            k_all = _chunk_rows(c, n_lat_chunks, tk, kl_ref, kc_ref).astype(F32)
            v_all = _chunk_rows(c, n_lat_chunks, tk, vl_ref, vc_ref)
            for a, cols in enumerate(heads):
                k = k_all[:, cols]
                for r, keys in ((2 * a, k), (2 * a + 1, pltpu.roll(k, HEAD_DIM, 1))):
                    ka, n2 = _aug_keys(keys)
                    ka_scr[r, c * tk:(c + 1) * tk, :] = ka
                    kmax[r] = n2 if kmax[r] is None else jnp.maximum(kmax[r], n2)
                vt_scr[a, c, :dv, :] = v_all[:, cols].T
                vt_scr[a, c, dv:, :] = jnp.ones((DIFF_VT_ROWS - dv, tk), BF16)
        for r, n2 in enumerate(kmax):
            kmax_scr[r] = n2
        _fp8_keys(ka_scr, k8_scr, kscale_scr, kmax, n_chunks=n_chunks, tk=tk)

    q_all = q_ref[...].astype(F32)
    slabs = []
    for cols in heads:
        slabs += [q_all[:, cols], pltpu.roll(q_all[:, cols], HEAD_DIM, 1)]
    _softmax_tile(ka_scr, k8_scr, vt_scr, kmax_scr, kscale_scr, slabs, m_scr, acc_scr, dv,
                  n_chunks=n_chunks, tk=tk)
    for a, cols in enumerate(heads):
        a1, a2 = acc_scr[2 * a], acc_scr[2 * a + 1]
        yt = a1[:dv] * (1.0 / a1[dv:dv + 1]) - lam_ref[0] * (a2[:dv] * (1.0 / a2[dv:dv + 1]))
        y = yt.T
        ms = jnp.mean(y * y, axis=-1, keepdims=True)
        o_ref[:, cols] = (y * lax.rsqrt(ms + EPS) * g_ref[...]).astype(BF16)


def _gqa_attn_kernel(q_ref, kvl_ref, kvc_ref, o_ref, ka_scr, k8_scr, vt_scr, kmax_scr, kscale_scr, m_scr, acc_scr,
                     *, n_lat_chunks, n_chunks, tk):
    w2 = 2 * HEAD_DIM
    groups = [slice(g * w2, (g + 1) * w2) for g in range(GQA_GROUPS_PER_STEP)]
    n_heads = GQA_REP * len(groups)

    @pl.when(pl.program_id(2) == 0)
    def _():
        kmax = [None] * len(groups)
        for c in range(n_chunks):
            kv_all = _chunk_rows(c, n_lat_chunks, tk, kvl_ref, kvc_ref)
            for g, cols in enumerate(groups):
                kv = kv_all[:, cols]
                ka, n2 = _aug_keys(kv.astype(F32))
                ka_scr[g, c * tk:(c + 1) * tk, :] = ka
                kmax[g] = n2 if kmax[g] is None else jnp.maximum(kmax[g], n2)
                t = kv.T
                row = lax.broadcasted_iota(jnp.int32, t.shape, 0)
                vt_scr[g, c] = jnp.where(row < HEAD_DIM, jnp.ones_like(t), t)
        for g, n2 in enumerate(kmax):
            kmax_scr[g] = n2
        _fp8_keys(ka_scr, k8_scr, kscale_scr, kmax, n_chunks=n_chunks, tk=tk)

    qf = q_ref[...].astype(F32)
    tq = qf.shape[0]
    lane = lax.broadcasted_iota(jnp.int32, (tq, w2), 1)
    slabs = []
    for r in range(n_heads):
        slab = qf[:, (r // 2) * w2:(r // 2 + 1) * w2]
        slabs.append(pltpu.roll(slab, HEAD_DIM, 1) if r % 2 else slab)
    _softmax_tile(ka_scr, k8_scr, vt_scr, kmax_scr, kscale_scr, slabs, m_scr, acc_scr, 0,
                  n_chunks=n_chunks, tk=tk)

    def head_out(r):
        acc = acc_scr[r]
        return (acc * (1.0 / acc[0:1])).T

    for j in range(n_heads // 2):
        pair = jnp.where(lane < HEAD_DIM, pltpu.roll(head_out(2 * j), HEAD_DIM, 1), head_out(2 * j + 1))
        o_ref[:, j * w2:(j + 1) * w2] = pair.astype(BF16)


def _attention(p, lam, subln_gain, *, B, S, C, tiles, ctx_queries):
    lat = _attention_calls(p, lam, subln_gain, B=B, S=S, C=C, tq=tiles.tq, tk=tiles.tk, latent=True)
    if not ctx_queries:
        return lat, None
    return lat, _attention_calls(p, lam, subln_gain, B=B, S=S, C=C, tq=C, tk=tiles.tk, latent=False)


def _attention_calls(p, lam, subln_gain, *, B, S, C, tq, tk, latent):
    ctx_blk0 = (B * S) // C
    if latent:
        nq, rows_out, s_kv, n_lat_chunks = S // tq, B * S, S, S // tk
        q_row = out_row = lambda b, qi: b * nq + qi
        lat_blk = lambda b: b
    else:
        nq, rows_out, s_kv, n_lat_chunks = 1, B * C, C, 0
        q_row = lambda b, qi: ctx_blk0 + b
        out_row = lambda b, qi: b
        lat_blk = lambda b: ctx_blk0 + b
    T = n_lat_chunks * tk + C
    S = s_kv
    statics = dict(n_lat_chunks=n_lat_chunks, n_chunks=T // tk, tk=tk)

    cparams = pltpu.CompilerParams(
        dimension_semantics=("arbitrary", "arbitrary", "arbitrary"), vmem_limit_bytes=VMEM_LIMIT)
    w2 = 2 * HEAD_DIM
    hps = DIFF_HEADS_PER_STEP
    wd = hps * w2
    yd = pl.pallas_call(
        functools.partial(_diff_attn_kernel, **statics),
        grid=(B, N_DIFF_HEADS // hps, nq),
        in_specs=[
            pl.BlockSpec(memory_space=pltpu.SMEM),
            pl.BlockSpec((tq, wd), lambda b, h, qi: (q_row(b, qi), DQ_OFF // wd + h)),
            pl.BlockSpec((S, wd), lambda b, h, qi: (lat_blk(b), DK_OFF // wd + h)),
            pl.BlockSpec((S, wd), lambda b, h, qi: (lat_blk(b), DV_OFF // wd + h)),
            pl.BlockSpec((C, wd), lambda b, h, qi: (ctx_blk0 + b, DK_OFF // wd + h)),
            pl.BlockSpec((C, wd), lambda b, h, qi: (ctx_blk0 + b, DV_OFF // wd + h)),
            pl.BlockSpec((1, w2), lambda b, h, qi: (0, 0)),
        ],
        out_specs=pl.BlockSpec((tq, wd), lambda b, h, qi: (out_row(b, qi), h)),
        out_shape=jax.ShapeDtypeStruct((rows_out, N_DIFF_HEADS * w2), BF16),
        scratch_shapes=[
            pltpu.VMEM((2 * hps, T, w2), BF16),
            pltpu.VMEM((2 * hps, T, w2), FP8),
            pltpu.VMEM((hps, T // tk, DIFF_VT_ROWS, tk), BF16),
            pltpu.VMEM((2 * hps, 1, w2), F32),
            pltpu.VMEM((2 * hps, 1, w2), F32),
            pltpu.VMEM((2 * hps, 1, tq), F32),
            pltpu.VMEM((2 * hps, DIFF_VT_ROWS, tq), F32),
        ],
        compiler_params=cparams,
        name="diff_attn" if latent else "diff_attn_ctx",
    )(lam, p, p, p, p, p, subln_gain)

    gps = GQA_GROUPS_PER_STEP
    wq = gps * GQA_REP * HEAD_DIM
    wkv = gps * w2
    yg = pl.pallas_call(
        functools.partial(_gqa_attn_kernel, **statics),
        grid=(B, N_GQA_KV // gps, nq),
        in_specs=[
            pl.BlockSpec((tq, wq), lambda b, g, qi: (q_row(b, qi), GQ_OFF // wq + g)),
            pl.BlockSpec((S, wkv), lambda b, g, qi: (lat_blk(b), GKV_OFF // wkv + g)),
            pl.BlockSpec((C, wkv), lambda b, g, qi: (ctx_blk0 + b, GKV_OFF // wkv + g)),
        ],
        out_specs=pl.BlockSpec((tq, wq), lambda b, g, qi: (out_row(b, qi), g)),
        out_shape=jax.ShapeDtypeStruct((rows_out, N_GQA_HEADS * HEAD_DIM), BF16),
        scratch_shapes=[
            pltpu.VMEM((gps, T, w2), BF16),
            pltpu.VMEM((gps, T, w2), FP8),
            pltpu.VMEM((gps, T // tk, w2, tk), BF16),
            pltpu.VMEM((gps, 1, w2), F32),
            pltpu.VMEM((gps, 1, w2), F32),
            pltpu.VMEM((gps * GQA_REP, 1, tq), F32),
            pltpu.VMEM((gps * GQA_REP, w2, tq), F32),
        ],
        compiler_params=cparams,
        name="gqa_attn" if latent else "gqa_attn_ctx",
    )(p, p, p)
    return yd, yg


def _mixer_kernel(xl_ref, xc_ref, ydl_ref, ygl_ref, ydc_ref, ygc_ref, gd_ref, gg_ref, gt_ref, wpd_ref, wpg_ref,
                  wo_ref, o_ref, *, x_lat_tiles, y_lat_tiles):
    x = _select_rows(xl_ref, xc_ref, x_lat_tiles)
    yd = _select_rows(ydl_ref, ydc_ref, y_lat_tiles)
    yg = _select_rows(ygl_ref, ygc_ref, y_lat_tiles)
    pd = jnp.dot(yd, wpd_ref[...], preferred_element_type=F32)
    pg = jnp.dot(yg, wpg_ref[...], preferred_element_type=F32)
    m = jax.nn.sigmoid(gd_ref[...].astype(F32)) * pd + jax.nn.sigmoid(gg_ref[...].astype(F32)) * pg
    o_ref[...] = x + gt_ref[...] * jnp.dot(m.astype(BF16), wo_ref[...], preferred_element_type=F32)


def _mixer_out(xs, lat, ctx, p, mods, wpd, wpg, wo, *, rows, tm, tiles_per_seq, n_batch):
    y_lat_tiles = lat[0].shape[0] // tm
    ctx = lat if ctx is None else ctx
    x_lat, x_ctx, x_lat_tiles = _split_rows(xs, tm)
    row_to_mod = lambda i: jnp.minimum(i // tiles_per_seq, n_batch)
    row_tile = pl.BlockSpec((tm, D_MODEL), lambda i: (i, 0))
    y_lat_spec, y_ctx_spec = _row_specs(tm, y_lat_tiles)
    weight = pl.BlockSpec((D_MODEL, D_MODEL), lambda i: (0, 0))
    return pl.pallas_call(
        functools.partial(_mixer_kernel, x_lat_tiles=x_lat_tiles, y_lat_tiles=y_lat_tiles),
        grid=(rows // tm,),
        in_specs=[
            *_row_specs(tm, x_lat_tiles), y_lat_spec, y_lat_spec, y_ctx_spec, y_ctx_spec,
            pl.BlockSpec((tm, D_MODEL), lambda i: (i, GATE_OFF // D_MODEL)),
            pl.BlockSpec((tm, D_MODEL), lambda i: (i, GATE_OFF // D_MODEL + 1)),
            _mod_spec(2, row_to_mod),
            weight, weight, weight,
        ],
        out_specs=row_tile,
        out_shape=jax.ShapeDtypeStruct((rows, D_MODEL), F32),
        compiler_params=pltpu.CompilerParams(
            dimension_semantics=("arbitrary",), vmem_limit_bytes=VMEM_LIMIT),
        name="mixer_out",
    )(x_lat, x_ctx, lat[0], lat[1], ctx[0], ctx[1], p, p, mods, wpd, wpg, wo)


def _ffn_kernel(x_ref, g_ref, sc_ref, sh_ref, gt_ref, w1_ref, w3_ref, w2_ref, o_ref, h_scr, acc_scr):
    f = pl.program_id(1)

    @pl.when(f == 0)
    def _():
        h_scr[...] = _modulated_norm(x_ref[...], g_ref[...], sc_ref[...], sh_ref[...]).astype(BF16)
        acc_scr[...] = jnp.zeros_like(acc_scr)

    h = h_scr[...]
    a = jnp.dot(h, w1_ref[...], preferred_element_type=F32)
    b = jnp.dot(h, w3_ref[...], preferred_element_type=F32)
    acc_scr[...] += jnp.dot((_silu(a) * b).astype(BF16), w2_ref[...], preferred_element_type=F32)

    @pl.when(f == pl.num_programs(1) - 1)
    def _():
        o_ref[...] = x_ref[...] + gt_ref[...] * acc_scr[...]


def _ffn(xs, mods, norm_g, w1, w3, w2, *, tm, tf, tiles_per_seq, n_batch):
    rows = xs.shape[0]
    d_ff = w1.shape[1]
    row_to_mod = lambda i: jnp.minimum(i // tiles_per_seq, n_batch)
    row_tile = pl.BlockSpec((tm, D_MODEL), lambda i, f: (i, 0))
    return pl.pallas_call(
        _ffn_kernel,
        grid=(rows // tm, d_ff // tf),
        in_specs=[
            row_tile,
            pl.BlockSpec((1, D_MODEL), lambda i, f: (0, 0)),
            _mod_spec(4, row_to_mod), _mod_spec(3, row_to_mod), _mod_spec(5, row_to_mod),
            pl.BlockSpec((D_MODEL, tf), lambda i, f: (0, f)),
            pl.BlockSpec((D_MODEL, tf), lambda i, f: (0, f)),
            pl.BlockSpec((tf, D_MODEL), lambda i, f: (f, 0)),
        ],
        out_specs=row_tile,
        out_shape=jax.ShapeDtypeStruct((rows, D_MODEL), F32),
        scratch_shapes=[pltpu.VMEM((tm, D_MODEL), BF16), pltpu.VMEM((tm, D_MODEL), F32)],
        compiler_params=pltpu.CompilerParams(
            dimension_semantics=("arbitrary", "arbitrary"), vmem_limit_bytes=VMEM_LIMIT),
        name="ffn_swiglu",
    )(xs, norm_g, mods, mods, mods, w1, w3, w2)


def _top2_gates(logits):
    lane = lax.broadcasted_iota(jnp.int32, logits.shape, 1)
    n_lanes = logits.shape[1]
    lg = jnp.where(lane < N_EXPERTS, logits, NEG_BIG)
    m1 = jnp.max(lg, axis=-1, keepdims=True)
    i1 = jnp.min(jnp.where(lg == m1, lane, n_lanes), axis=-1, keepdims=True)
    lg2 = jnp.where(lane == i1, NEG_BIG, lg)
    m2 = jnp.max(lg2, axis=-1, keepdims=True)
    i2 = jnp.min(jnp.where(lg2 == m2, lane, n_lanes), axis=-1, keepdims=True)
    e2 = jnp.exp(m2 - m1)
    w_top = 1.0 / (1.0 + e2)
    idx = jnp.where(lane == 0, i1, jnp.where(lane == 1, i2, 0))
    wts = jnp.where(lane == 0, w_top, jnp.where(lane == 1, e2 * w_top, 0.0))
    return idx, wts


def _router_kernel(x_ref, g_ref, sc_ref, sh_ref, rhi_ref, rlo_ref, h_ref, idx_ref, wts_ref):
    h = _modulated_norm(x_ref[...], g_ref[...], sc_ref[...], sh_ref[...])
    h_ref[...] = h
    h_hi, h_lo = _split_bf16(h)
    logits = jnp.dot(h_hi, rhi_ref[...], preferred_element_type=F32)
    logits += jnp.dot(h_lo, rhi_ref[...], preferred_element_type=F32)
    logits += jnp.dot(h_hi, rlo_ref[...], preferred_element_type=F32)
    idx_ref[...], wts_ref[...] = _top2_gates(logits)


def _router(xs, mods, norm_g, r_hi, r_lo, *, rows, tm, tiles_per_seq, n_batch):
    row_to_mod = lambda i: jnp.minimum(i // tiles_per_seq, n_batch)
    row_tile = pl.BlockSpec((tm, D_MODEL), lambda i: (i, 0))
    lanes = pl.BlockSpec((tm, 128), lambda i: (i, 0))
    router = pl.BlockSpec((D_MODEL, 128), lambda i: (0, 0))
    return pl.pallas_call(
        _router_kernel,
        grid=(rows // tm,),
        in_specs=[row_tile, pl.BlockSpec((1, D_MODEL), lambda i: (0, 0)),
                  _mod_spec(4, row_to_mod), _mod_spec(3, row_to_mod), router, router],
        out_specs=[row_tile, lanes, lanes],
        out_shape=[jax.ShapeDtypeStruct((rows, D_MODEL), F32),
                   jax.ShapeDtypeStruct((rows, 128), jnp.int32),
                   jax.ShapeDtypeStruct((rows, 128), F32)],
        compiler_params=pltpu.CompilerParams(
            dimension_semantics=("arbitrary",), vmem_limit_bytes=VMEM_LIMIT),
        name="moe_router",
    )(xs, norm_g, mods, mods, r_hi, r_lo)


def _route_plan(idx, n_tok, tm_e):
    n_pairs = 2 * n_tok
    n_tiles = n_pairs // tm_e + N_EXPERTS + 1
    n_rows = n_tiles * tm_e
    e_flat = idx.reshape(n_pairs)
    order = jnp.argsort(e_flat, stable=True).astype(jnp.int32)
    counts = jnp.sum((e_flat[:, None] == jnp.arange(N_EXPERTS, dtype=jnp.int32)[None, :]).astype(jnp.int32),
                     axis=0)
    padded = ((counts + tm_e - 1) // tm_e) * tm_e
    ends = jnp.cumsum(padded)
    tile_start = jnp.arange(n_tiles, dtype=jnp.int32) * tm_e
    tile_expert = jnp.minimum(jnp.sum((tile_start[:, None] >= ends[None, :]).astype(jnp.int32), axis=1),
                              N_EXPERTS - 1).astype(jnp.int32)
    n_used = (ends[-1] // tm_e).astype(jnp.int32).reshape(1)
    row = jnp.arange(n_rows, dtype=jnp.int32)
    row_expert = jnp.repeat(tile_expert, tm_e)
    offset = row - (ends - padded)[row_expert]
    valid = (offset < counts[row_expert]) & (row < ends[-1])
    pair = order[jnp.clip((jnp.cumsum(counts) - counts)[row_expert] + offset, 0, n_pairs - 1)]
    tok, choice = pair // 2, pair % 2
    src = jnp.where(valid, tok, 0)
    dump = n_pairs + jnp.cumsum(jnp.logical_not(valid).astype(jnp.int32)) - 1
    dst = jnp.where(valid, choice * n_tok + tok, dump)
    return tile_expert, n_used, src.reshape(n_tiles, 1, tm_e), dst.reshape(n_tiles, 1, tm_e)


def _experts_kernel(te_ref, nu_ref, src_ref, src_next_ref, dst_ref, h_hbm, w1_ref, w3_ref, w2_ref, y_hbm,
                    xbuf, xb_scr, acc_scr, ybuf, gsem, ssem, *, tm_e, n_chunks):
    t = pl.program_id(0)
    f = pl.program_id(1)
    nf = pl.num_programs(1)
    n_used = nu_ref[0]
    slot = t % 2

    def gather_rows(idx_ref, s):
        def body(r, carry):
            pltpu.make_async_copy(h_hbm.at[pl.ds(idx_ref[0, r], 1)], xbuf.at[s, pl.ds(r, 1)], gsem.at[s]).start()
            return carry
        lax.fori_loop(0, tm_e, body, 0, unroll=8)

    def wait_gather(s):
        pltpu.make_async_copy(h_hbm.at[pl.ds(0, tm_e)], xbuf.at[s], gsem.at[s]).wait()

    def scatter_rows(s):
        def body(r, carry):
            pltpu.make_async_copy(ybuf.at[s, pl.ds(r, 1)], y_hbm.at[pl.ds(dst_ref[0, r], 1)], ssem.at[s]).start()
            return carry
        lax.fori_loop(0, tm_e, body, 0, unroll=8)

    def wait_scatter(s):
        pltpu.make_async_copy(ybuf.at[s], y_hbm.at[pl.ds(0, tm_e)], ssem.at[s]).wait()

    @pl.when((f == 0) & (t <= n_used))
    def _():
        @pl.when(t == 0)
        def _():
            gather_rows(src_ref, 0)
        for s in range(2):
            @pl.when(slot == s)
            def _():
                wait_gather(s)

                @pl.when(t < n_used)
                def _():
                    xb_scr[...] = xbuf[s].astype(BF16)
                    acc_scr[...] = jnp.zeros_like(acc_scr)

    @pl.when(t < n_used)
    def _():
        x = xb_scr[...]
        a = jnp.dot(x, w1_ref[...].astype(BF16), preferred_element_type=F32)
        b = jnp.dot(x, w3_ref[...].astype(BF16), preferred_element_type=F32)
        acc_scr[...] += jnp.dot((_silu(a) * b).astype(BF16), w2_ref[...].astype(BF16),
                                preferred_element_type=F32)
        share = tm_e // n_chunks
        for i in range(share):
            r = f * share + i
            pltpu.make_async_copy(h_hbm.at[pl.ds(src_next_ref[0, r], 1)],
                                  xbuf.at[1 - slot, pl.ds(r, 1)], gsem.at[1 - slot]).start()

    @pl.when(f == nf - 1)
    def _():
        for s in range(2):
            @pl.when(slot == s)
            def _():
                @pl.when(t >= 2)
                def _():
                    wait_scatter(s)

                @pl.when(t < n_used)
                def _():
                    ybuf[s] = acc_scr[...]

                @pl.when(t >= n_used)
                def _():
                    ybuf[s] = jnp.zeros(ybuf.shape[1:], F32)

                scatter_rows(s)

                @pl.when(t == pl.num_programs(0) - 1)
                def _():
                    wait_scatter(s)
                    wait_scatter(1 - s)


def _experts(h, plan, w1, w3, w2, *, n_tok, tm_e, tf):
    tile_expert, n_used, src, dst = plan
    n_tiles = src.shape[0]
    d_ff = w1.shape[2]
    nf = d_ff // tf
    n_out = n_tiles * tm_e

    def chunk(t, f, nu):
        return jnp.where(t < nu[0], f, nf - 1)

    smem_tile = lambda shift: pl.BlockSpec(
        (None, 1, tm_e), lambda t, f, te, nu: (jnp.minimum(t + shift, n_tiles - 1), 0, 0),
        memory_space=pltpu.SMEM)
    grid_spec = pltpu.PrefetchScalarGridSpec(
        num_scalar_prefetch=2,
        grid=(n_tiles, nf),
        in_specs=[
            smem_tile(0), smem_tile(1), smem_tile(0),
            pl.BlockSpec(memory_space=pl.ANY),
            pl.BlockSpec((None, D_MODEL, tf), lambda t, f, te, nu: (te[t], 0, chunk(t, f, nu))),
            pl.BlockSpec((None, D_MODEL, tf), lambda t, f, te, nu: (te[t], 0, chunk(t, f, nu))),
            pl.BlockSpec((None, tf, D_MODEL), lambda t, f, te, nu: (te[t], chunk(t, f, nu), 0)),
        ],
        out_specs=pl.BlockSpec(memory_space=pl.ANY),
        scratch_shapes=[
            pltpu.VMEM((2, tm_e, D_MODEL), F32),
            pltpu.VMEM((tm_e, D_MODEL), BF16),
            pltpu.VMEM((tm_e, D_MODEL), F32),
            pltpu.VMEM((2, tm_e, D_MODEL), F32),
            pltpu.SemaphoreType.DMA((2,)),
            pltpu.SemaphoreType.DMA((2,)),
        ],
    )
    return pl.pallas_call(
        functools.partial(_experts_kernel, tm_e=tm_e, n_chunks=nf),
        grid_spec=grid_spec,
        out_shape=jax.ShapeDtypeStruct((n_out, D_MODEL), F32),
        compiler_params=pltpu.CompilerParams(
            dimension_semantics=("arbitrary", "arbitrary"), vmem_limit_bytes=VMEM_LIMIT),
        name="moe_experts",
    )(tile_expert, n_used, src, src, dst, h, w1, w3, w2)


def _combine_kernel(x_ref, y0_ref, y1_ref, wts_ref, gt_ref, fg_ref, o_ref):
    wts = wts_ref[...]
    moe = wts[:, 0:1] * y0_ref[...] + wts[:, 1:2] * y1_ref[...]
    y = x_ref[...] + gt_ref[...] * moe
    ms = jnp.mean(y * y, axis=-1, keepdims=True)
    o_ref[...] = y * lax.rsqrt(ms + EPS) * fg_ref[...]


def _combine(xs, y, wts, mods, final_g, *, rows, tm, tiles_per_seq, n_batch):
    row_to_mod = lambda i: jnp.minimum(i // tiles_per_seq, n_batch)
    row_tile = pl.BlockSpec((tm, D_MODEL), lambda i: (i, 0))
    return pl.pallas_call(
        _combine_kernel,
        grid=(rows // tm,),
        in_specs=[
            row_tile, row_tile,
            pl.BlockSpec((tm, D_MODEL), lambda i: (i + rows // tm, 0)),
            pl.BlockSpec((tm, 128), lambda i: (i, 0)),
            _mod_spec(5, row_to_mod),
            pl.BlockSpec((1, D_MODEL), lambda i: (0, 0)),
        ],
        out_specs=row_tile,
        out_shape=jax.ShapeDtypeStruct((rows, D_MODEL), F32),
        compiler_params=pltpu.CompilerParams(
            dimension_semantics=("arbitrary",), vmem_limit_bytes=VMEM_LIMIT),
        name="moe_combine",
    )(xs, y, y, wts, mods, final_g)


def _moe(xs, mods, norm_g, r_hi, r_lo, final_g, w1, w3, w2, *, rows, tm, tm_e, tf, tiles_per_seq, n_batch):
    h, idx, wts = _router(xs, mods, norm_g, r_hi, r_lo, rows=rows, tm=tm, tiles_per_seq=tiles_per_seq,
                          n_batch=n_batch)
    plan = _route_plan(idx[:, :2], rows, tm_e)
    y = _experts(h, plan, w1, w3, w2, n_tok=rows, tm_e=tm_e, tf=tf)
    return _combine(xs, y, wts, mods, final_g, rows=rows, tm=tm, tiles_per_seq=tiles_per_seq, n_batch=n_batch)


def _deinterleave(n=HEAD_DIM):
    return np.concatenate([np.arange(0, n, 2), np.arange(1, n, 2)])


def _proj_columns():
    de = _deinterleave()
    o_dq, o_gq, o_dk, o_dv, o_gk, o_gv, o_gate = 0, 1024, 2048, 3072, 4096, 4352, 4608
    cols = np.zeros(IN_W, np.int32)
    scale = np.ones(IN_W, np.float32)
    cols[GATE_OFF:GATE_OFF + 2048] = o_gate + np.arange(2048)
    for h in range(N_DIFF_HEADS):
        for c in range(2):
            dst = 128 * h + 64 * c
            cols[DQ_OFF + dst:DQ_OFF + dst + 64] = o_dq + dst + de
            cols[DK_OFF + dst:DK_OFF + dst + 64] = o_dk + dst + de
    scale[DQ_OFF:DQ_OFF + 1024] = HEAD_DIM ** -0.5
    for j in range(N_GQA_HEADS):
        cols[GQ_OFF + 64 * j:GQ_OFF + 64 * j + 64] = o_gq + 64 * j + de
    cols[DV_OFF:DV_OFF + 1024] = o_dv + np.arange(1024)
    for g in range(N_GQA_KV):
        cols[GKV_OFF + 128 * g:GKV_OFF + 128 * g + 64] = o_gk + 64 * g + de
        cols[GKV_OFF + 128 * g + 64:GKV_OFF + 128 * g + 128] = o_gv + 64 * g + np.arange(64)
    return cols, scale


def _rope_tables(S, pad_rows):
    rows = S // GRID_W
    row = jnp.repeat(jnp.arange(rows, dtype=F32), GRID_W)
    col = jnp.tile(jnp.arange(GRID_W, dtype=F32), rows)
    half = HEAD_DIM // 2
    inv_freq = ROPE_THETA ** (-jnp.arange(0, half, 2, dtype=F32) / half)
    ang = jnp.concatenate([row[:, None] * inv_freq, col[:, None] * inv_freq], axis=-1)
    cos, sin = jnp.cos(ang), jnp.sin(ang)
    reps = PROJ_TN // HEAD_DIM
    cos_t = jnp.tile(jnp.concatenate([cos, cos], axis=-1), (1, reps))
    sin_t = jnp.tile(jnp.concatenate([-sin, sin], axis=-1), (1, reps))
    cos_t = jnp.concatenate([cos_t, jnp.ones((pad_rows, PROJ_TN), F32)], axis=0)
    sin_t = jnp.concatenate([sin_t, jnp.zeros((pad_rows, PROJ_TN), F32)], axis=0)
    return cos_t, sin_t


def kernel(x, c, ctx, c_ctx, ada_w, ada_b, norm_attn_g, norm_ffn_g, w_in, q_norm_g, k_norm_g, diff_lambda,
           diff_subln_g, w_proj_diff, w_proj_gqa, w_out, ffn_w1, ffn_w3, ffn_w2, moe_router, moe_w1, moe_w3,
           moe_w2, final_norm_g):
    B, S, D = x.shape
    C = ctx.shape[1]
    depth = ada_w.shape[0]
    assert D == D_MODEL and depth == 2 and B + 1 <= MOD_ROWS
    assert w_in.shape[2] == IN_W and moe_router.shape[2] == N_EXPERTS
    tiles = _pick_tiles(B, S, C, ffn_w1.shape[2], moe_w1.shape[3])
    n_lat = B * S

    cvec = jnp.zeros((MOD_ROWS, D), F32).at[:B].set(c).at[B].set(c_ctx)
    mods_all = _ada_mods(cvec, ada_w, ada_b).reshape(depth, MOD_ROWS * N_MODS, 1, D)

    cols, col_scale = _proj_columns()
    de = _deinterleave()
    cos_t, sin_t = _rope_tables(S, tiles.tm_proj)
    blk = np.arange(PROJ_TN) // HEAD_DIM
    ones_bd = jnp.asarray(blk[:, None] == blk[None, :], BF16)
    col = np.arange(PROJ_TN)
    swap_bd = jnp.asarray(col[:, None] == (col[None, :] ^ (HEAD_DIM // 2)), BF16)
    is_key_lane = (np.arange(PROJ_TN) % (2 * HEAD_DIM)) < HEAD_DIM

    xs = (x.reshape(n_lat, D), ctx.reshape(B * C, D))
    for l in range(depth):
        last = l == depth - 1
        mods = mods_all[l]
        lam_init = 0.8 - 0.6 * math.exp(-0.3 * l)
        lq1, lk1, lq2, lk2 = diff_lambda[l]
        lam = (jnp.exp(jnp.sum(lq1 * lk1)) - jnp.exp(jnp.sum(lq2 * lk2)) + lam_init).reshape(1).astype(F32)
        w = (jnp.take(w_in[l], cols, axis=1) * col_scale).astype(BF16)
        qg = jnp.tile(q_norm_g[l][de] * HEAD_DIM ** -0.5, PROJ_TN // HEAD_DIM).reshape(1, PROJ_TN)
        kg = jnp.where(is_key_lane, jnp.tile(k_norm_g[l][de], PROJ_TN // HEAD_DIM), 1.0).reshape(1, PROJ_TN)
        subln = (diff_subln_g[l] * (1.0 - lam_init)).reshape(1, 2 * HEAD_DIM)

        p = _project(xs, mods, norm_attn_g[l].reshape(1, D), w, cos_t, sin_t, ones_bd, swap_bd, qg, kg,
                     tm=tiles.tm_proj, lat_tiles=n_lat // tiles.tm_proj, tiles_per_seq=S // tiles.tm_proj,
                     n_batch=B)
        y_lat, y_ctx = _attention(p, lam, subln, B=B, S=S, C=C, tiles=tiles, ctx_queries=not last)
        rows = n_lat if last else n_lat + B * C
        xs = _mixer_out(xs, y_lat, y_ctx, p, mods, w_proj_diff[l].astype(BF16), w_proj_gqa[l].astype(BF16),
                        w_out[l].astype(BF16), rows=rows, tm=tiles.tm_mix, tiles_per_seq=S // tiles.tm_mix,
                        n_batch=B)
        i = l // 2
        if l % 2 == 0:
            xs = _ffn(xs, mods, norm_ffn_g[l].reshape(1, D), ffn_w1[i].astype(BF16), ffn_w3[i].astype(BF16),
                      ffn_w2[i].astype(BF16), tm=tiles.tm_ffn, tf=tiles.tf_ffn,
                      tiles_per_seq=S // tiles.tm_ffn, n_batch=B)
        else:
            r_pad = jnp.zeros((D, 128), F32).at[:, :N_EXPERTS].set(moe_router[i])
            r_hi = r_pad.astype(BF16)
            r_lo = (r_pad - r_hi.astype(F32)).astype(BF16)
            xs = _moe(xs, mods, norm_ffn_g[l].reshape(1, D), r_hi, r_lo, final_norm_g.reshape(1, D),
                      moe_w1[i], moe_w3[i], moe_w2[i],
                      rows=rows, tm=tiles.tm_mix, tm_e=tiles.tm_moe, tf=tiles.tf_moe,
                      tiles_per_seq=S // tiles.tm_mix, n_batch=B)
    return xs.reshape(B, S, D)
```

```python
import functools
import math
from typing import NamedTuple

import numpy as np
import jax
import jax.numpy as jnp
from jax import lax
from jax.experimental import pallas as pl
from jax.experimental.pallas import tpu as pltpu

F32 = jnp.float32
BF16 = jnp.bfloat16

D_MODEL = 1024
HEAD_DIM = 64
N_DIFF_HEADS = 8
N_GQA_HEADS = 16
N_GQA_KV = 4
GQA_REP = N_GQA_HEADS // N_GQA_KV
N_EXPERTS = 8
GRID_W = 64
ROPE_THETA = 10000.0
EPS = 1e-6
N_MODS = 6
MOD_ROWS = 16
NEG_BIG = -1e30

GATE_OFF = 0
DQ_OFF = 2048
GQ_OFF = 3072
DK_OFF = 4096
DV_OFF = 5120
GKV_OFF = 6144
IN_W = 6656
PROJ_TN = 512
ROPE_TILES = (4, 5, 8, 9)
QNORM_TILES = (6, 7)
KV_TILE = 12

VMEM_LIMIT = 52 * 1024 * 1024


class Tiles(NamedTuple):
    tm_proj: int
    tq: int
    tk: int
    tm_mix: int
    tm_ffn: int
    tf_ffn: int
    tf_moe: int
    tm_moe: int


def _largest_divisor(n, candidates):
    for c in candidates:
        if n % c == 0:
            return c
    raise ValueError(f"no tile in {candidates} divides {n}")


def _pick_tiles(B, S, C, d_ff, d_ff_e):
    rows_common = math.gcd(S, B * C)
    tm = _largest_divisor(rows_common, (512, 256, 128))
    return Tiles(
        tm_proj=_largest_divisor(rows_common, (1024, 512, 256, 128)),
        tq=_largest_divisor(S, (256, 128)),
        tk=_largest_divisor(math.gcd(S, C), (256, 128)),
        tm_mix=tm,
        tm_ffn=_largest_divisor(rows_common, (1024, 512, 256, 128)),
        tf_ffn=_largest_divisor(d_ff, (1408, 1024, 512, 256, 128)),
        tf_moe=_largest_divisor(d_ff_e, (896, 512, 256, 128)),
        tm_moe=_largest_divisor(2 * B * S, (512, 256)),
    )


def _split_bf16(v):
    hi = v.astype(BF16)
    lo = (v - hi.astype(F32)).astype(BF16)
    return hi, lo


def _ada_kernel(c_ref, w_ref, b_ref, o_ref):
    c = c_ref[...]
    s = c / (1.0 + jnp.exp(-c))
    s_hi, s_lo = _split_bf16(s)
    w_hi, w_lo = _split_bf16(w_ref[...])
    acc = jnp.dot(s_hi, w_hi, preferred_element_type=F32)
    acc += jnp.dot(s_lo, w_hi, preferred_element_type=F32)
    acc += jnp.dot(s_hi, w_lo, preferred_element_type=F32)
    o_ref[...] = acc + b_ref[...]


def _ada_mods(cvec, ada_w, ada_b):
    depth, d, n = ada_w.shape
    tn = _largest_divisor(n, (1536, 1024, 512))
    return pl.pallas_call(
        _ada_kernel,
        grid=(depth, n // tn),
        in_specs=[
            pl.BlockSpec((MOD_ROWS, d), lambda l, j: (0, 0)),
            pl.BlockSpec((None, d, tn), lambda l, j: (l, 0, j)),
            pl.BlockSpec((None, 1, tn), lambda l, j: (l, 0, j)),
        ],
        out_specs=pl.BlockSpec((None, MOD_ROWS, tn), lambda l, j: (l, 0, j)),
        out_shape=jax.ShapeDtypeStruct((depth, MOD_ROWS, n), F32),
        compiler_params=pltpu.CompilerParams(
            dimension_semantics=("arbitrary", "arbitrary"), vmem_limit_bytes=VMEM_LIMIT),
        name="ada_mods",
    )(cvec, ada_w, ada_b.reshape(depth, 1, n))


def _modulated_norm(x, g, scale, shift):
    ms = jnp.mean(x * x, axis=-1, keepdims=True)
    return (x * lax.rsqrt(ms + EPS) * g) * (1.0 + scale) + shift


def _silu(a):
    return a / (1.0 + jnp.exp(-a))


def _mod_spec(k, row_to_mod):
    return pl.BlockSpec((None, 1, D_MODEL), lambda i, *_: (row_to_mod(i) * N_MODS + k, 0, 0))


def _rope(z, cos, sin_signed, swap_ref):
    partner = jnp.dot(z.astype(BF16), swap_ref[...], preferred_element_type=F32)
    return z * cos + partner * sin_signed


def _head_rms(z, ones_ref, gain):
    ss = jnp.dot((z * z).astype(BF16), ones_ref[...], preferred_element_type=F32)
    return z * lax.rsqrt(ss * (1.0 / HEAD_DIM) + EPS) * gain


def _split_rows(xs, tm):
    if isinstance(xs, tuple):
        return xs[0], xs[1], xs[0].shape[0] // tm
    return xs, xs, xs.shape[0] // tm


def _row_specs(tm, lat_tiles):
    return (pl.BlockSpec((tm, D_MODEL), lambda i, *_: (jnp.minimum(i, lat_tiles - 1), 0)),
            pl.BlockSpec((tm, D_MODEL), lambda i, *_: (jnp.maximum(i - lat_tiles, 0), 0)))


def _select_rows(lat_ref, ctx_ref, lat_tiles):
    return jnp.where(pl.program_id(0) >= lat_tiles, ctx_ref[...], lat_ref[...])


def _proj_kernel(xl_ref, xc_ref, g_ref, sc_ref, sh_ref, w_ref, cos_ref, sin_ref, ones_ref, swap_ref, qg_ref, kg_ref,
                 o_ref, h_scr, *, lat_tiles):
    j = pl.program_id(1)

    @pl.when(j == 0)
    def _():
        x = _select_rows(xl_ref, xc_ref, lat_tiles)
        h_scr[...] = _modulated_norm(x, g_ref[...], sc_ref[...], sh_ref[...]).astype(BF16)

    def project():
        z = jnp.dot(h_scr[...], w_ref[...], preferred_element_type=F32)
        return z, lax.broadcasted_iota(jnp.int32, z.shape, 1)

    is_rope = functools.reduce(jnp.logical_or, [j == t for t in ROPE_TILES])
    is_qnorm = functools.reduce(jnp.logical_or, [j == t for t in QNORM_TILES])
    is_kv = j == KV_TILE
    is_plain = jnp.logical_not(is_rope | is_qnorm | is_kv)

    @pl.when(is_plain)
    def _():
        z, _ = project()
        o_ref[...] = z.astype(BF16)

    @pl.when(is_rope)
    def _():
        z, lane = project()
        o_ref[...] = _rope(z, cos_ref[...], sin_ref[...], swap_ref).astype(BF16)

    @pl.when(is_qnorm)
    def _():
        z, lane = project()
        zn = _head_rms(z, ones_ref, qg_ref[...])
        o_ref[...] = _rope(zn, cos_ref[...], sin_ref[...], swap_ref).astype(BF16)

    @pl.when(is_kv)
    def _():
        z, lane = project()
        zn = _head_rms(z, ones_ref, kg_ref[...])
        zr = _rope(zn, cos_ref[...], sin_ref[...], swap_ref)
        is_key_lane = (lane & HEAD_DIM) == 0
        o_ref[...] = jnp.where(is_key_lane, zr, z).astype(BF16)


def _project(xs, mods, norm_g, w, cos_t, sin_t, ones_bd, swap_bd, qg, kg, *, tm, lat_tiles, tiles_per_seq, n_batch):
    x_lat, x_ctx, src_tiles = _split_rows(xs, tm)
    rows = x_lat.shape[0] + (x_ctx.shape[0] if isinstance(xs, tuple) else 0)
    row_to_mod = lambda i: jnp.minimum(i // tiles_per_seq, n_batch)
    rope_row = lambda i: jnp.where(i < lat_tiles, i % tiles_per_seq, tiles_per_seq)
    const = lambda i, j: (0, 0)
    return pl.pallas_call(
        functools.partial(_proj_kernel, lat_tiles=src_tiles),
        grid=(rows // tm, IN_W // PROJ_TN),
        in_specs=[
            *_row_specs(tm, src_tiles),
            pl.BlockSpec((1, D_MODEL), const),
            _mod_spec(1, row_to_mod),
            _mod_spec(0, row_to_mod),
            pl.BlockSpec((D_MODEL, PROJ_TN), lambda i, j: (0, j)),
            pl.BlockSpec((tm, PROJ_TN), lambda i, j: (rope_row(i), 0)),
            pl.BlockSpec((tm, PROJ_TN), lambda i, j: (rope_row(i), 0)),
            pl.BlockSpec((PROJ_TN, PROJ_TN), const),
            pl.BlockSpec((PROJ_TN, PROJ_TN), const),
            pl.BlockSpec((1, PROJ_TN), const),
            pl.BlockSpec((1, PROJ_TN), const),
        ],
        out_specs=pl.BlockSpec((tm, PROJ_TN), lambda i, j: (i, j)),
        out_shape=jax.ShapeDtypeStruct((rows, IN_W), BF16),
        scratch_shapes=[pltpu.VMEM((tm, D_MODEL), BF16)],
        compiler_params=pltpu.CompilerParams(
            dimension_semantics=("arbitrary", "arbitrary"), vmem_limit_bytes=VMEM_LIMIT),
        name="in_proj",
    )(x_lat, x_ctx, norm_g, mods, mods, w, cos_t, sin_t, ones_bd, swap_bd, qg, kg)


STAB_LANE = HEAD_DIM
L_FLOOR = 1e-26
L_CEIL = 1e30
FP8 = jnp.float8_e4m3fn
FP8_TARGET = 256.0
FP8_BOUND_SLACK = 1.125
LOG2_E = 1.4426950408889634
DIFF_VT_ROWS = 2 * HEAD_DIM + 16
DIFF_HEADS_PER_STEP = 2
GQA_GROUPS_PER_STEP = 1


def _aug_keys(k):
    lane = lax.broadcasted_iota(jnp.int32, k.shape, 1)
    in_key = lane < HEAD_DIM
    sq = jnp.where(in_key, k * k, 0.0).astype(BF16)
    norms = jnp.dot(sq, jnp.ones((k.shape[1], k.shape[1]), BF16), preferred_element_type=F32)
    ka = jnp.where(in_key, k, jnp.where(lane == STAB_LANE, 1.0, 0.0)).astype(BF16)
    return ka, jnp.max(norms, axis=0, keepdims=True)


def _pow2_floor(x):
    bits = lax.bitcast_convert_type(x, jnp.int32) & jnp.int32(0x7F800000)
    return lax.bitcast_convert_type(bits, F32)


def _fp8_keys(ka_scr, k8_scr, kscale_scr, kmax, *, n_chunks, tk):
    for r, kmax2 in enumerate(kmax):
        scale = _pow2_floor(FP8_TARGET * lax.rsqrt(jnp.maximum(kmax2, 1e-30)))
        kscale_scr[r] = scale
        for c in range(n_chunks):
            k = ka_scr[r, c * tk:(c + 1) * tk, :].astype(F32)
            lane = lax.broadcasted_iota(jnp.int32, k.shape, 1)
            k8_scr[r, c * tk:(c + 1) * tk, :] = jnp.where(lane == STAB_LANE, FP8_TARGET, k * scale).astype(FP8)


def _query_mats(slab, kmax2, kscale):
    lane = lax.broadcasted_iota(jnp.int32, slab.shape, 1)
    in_key = lane < HEAD_DIM
    sq = jnp.where(in_key, slab * slab, 0.0).astype(BF16)
    norms = jnp.dot(sq, jnp.ones((slab.shape[1], slab.shape[1]), BF16), preferred_element_type=F32)
    nk = norms * kmax2
    bound = FP8_BOUND_SLACK * nk * lax.rsqrt(nk + 1e-30)
    qscale = _pow2_floor(FP8_TARGET * lax.rsqrt(jnp.maximum(jnp.max(norms, axis=0, keepdims=True), 1e-30)))
    both = kscale * qscale
    plain = jnp.where(in_key, slab, 0.0)
    shifted = jnp.where(lane == STAB_LANE, -bound * (both * (1.0 / FP8_TARGET)), plain * qscale).astype(FP8)
    return shifted, plain.astype(BF16), (LOG2_E / both)[:, :1]


def _attend(k8_scr, vt_scr, mats, acc_scr, *, first, n_chunks, tk):
    def scores(r, c, qmat):
        off = pl.multiple_of(c * tk, tk)
        return lax.dot_general(k8_scr[r * k8_scr.shape[0] // len(mats), pl.ds(off, tk), :], qmat,
                               (((1,), (1,)), ((), ())), preferred_element_type=F32)

    acc_scr[...] = jnp.zeros(acc_scr.shape, F32)

    def all_scores(c):
        return tuple(scores(r, c, shifted) for r, (shifted, _, _) in enumerate(mats))

    def accumulate(c, s_all):
        for r, s in enumerate(s_all):
            vt = vt_scr[r * vt_scr.shape[0] // len(mats), c]
            p = jnp.exp2((s * mats[r][2]).astype(BF16))
            acc_scr[r] += jnp.dot(vt, p, preferred_element_type=F32)

    def fast_body(c, s_cur):
        s_next = all_scores(c + 1)
        accumulate(c, s_cur)
        return s_next

    trips = n_chunks - 1 - first
    s_last = lax.fori_loop(first, n_chunks - 1, fast_body, all_scores(first),
                           unroll=_largest_divisor(trips, (16, 8, 4, 2, 1)) if trips > 0 else 1)
    accumulate(n_chunks - 1, s_last)


def _attend_fallback(ka_scr, vt_scr, mats, m_scr, acc_scr, *, first, n_chunks, tk):
    m_scr[...] = jnp.full(m_scr.shape, NEG_BIG, F32)
    acc_scr[...] = jnp.zeros(acc_scr.shape, F32)

    def body(c, carry):
        off = pl.multiple_of(c * tk, tk)
        for r, (_, plain, _) in enumerate(mats):
            s = lax.dot_general(ka_scr[r * ka_scr.shape[0] // len(mats), pl.ds(off, tk), :], plain, (((1,), (1,)), ((), ())),
                                preferred_element_type=F32)
            m_old = m_scr[r]
            m_new = jnp.maximum(m_old, jnp.max(s, axis=0, keepdims=True))
            p = jnp.exp(s - m_new).astype(BF16)
            vt = vt_scr[r * vt_scr.shape[0] // len(mats), c]
            acc_scr[r] = acc_scr[r] * jnp.exp(m_old - m_new) + jnp.dot(vt, p, preferred_element_type=F32)
            m_scr[r] = m_new
        return carry

    lax.fori_loop(first, n_chunks, body, 0)


def _softmax_tile(ka_scr, k8_scr, vt_scr, kmax_scr, kscale_scr, slabs, m_scr, acc_scr, l_row,
                  *, n_chunks, tk):
    per_key = lambda scr, r: scr[r * scr.shape[0] // len(slabs)]
    mats = [_query_mats(slab, per_key(kmax_scr, r), per_key(kscale_scr, r)) for r, slab in enumerate(slabs)]
    _attend(k8_scr, vt_scr, mats, acc_scr, first=0, n_chunks=n_chunks, tk=tk)
    dens = [acc_scr[r, l_row:l_row + 1, :] for r in range(len(slabs))]
    l_min = functools.reduce(jnp.minimum, [jnp.min(d) for d in dens])
    l_max = functools.reduce(jnp.maximum, [jnp.max(d) for d in dens])

    @pl.when(jnp.logical_not((l_min >= L_FLOOR) & (l_max <= L_CEIL)))
    def _():
        _attend_fallback(ka_scr, vt_scr, mats, m_scr, acc_scr, first=0, n_chunks=n_chunks, tk=tk)


def _chunk_rows(c, n_lat_chunks, tk, lat_ref, ctx_ref):
    if c < n_lat_chunks:
        return lat_ref[c * tk:(c + 1) * tk, :]
    return ctx_ref[(c - n_lat_chunks) * tk:(c - n_lat_chunks + 1) * tk, :]


def _diff_attn_kernel(lam_ref, q_ref, kl_ref, vl_ref, kc_ref, vc_ref, g_ref, o_ref,
                      ka_scr, k8_scr, vt_scr, kmax_scr, kscale_scr, m_scr, acc_scr,
                      *, n_lat_chunks, n_chunks, tk):
    dv = 2 * HEAD_DIM
    heads = [slice(a * dv, (a + 1) * dv) for a in range(DIFF_HEADS_PER_STEP)]

    @pl.when(pl.program_id(2) == 0)
    def _():
        kmax = [None] * (2 * len(heads))
        for c in range(n_chunks):
            k_all = _chunk_rows(c, n_lat_chunks, tk, kl_ref, kc_ref).astype(F32)
            v_all = _chunk_rows(c, n_lat_chunks, tk, vl_ref, vc_ref)
            for a, cols in enumerate(heads):
                k = k_all[:, cols]
                for r, keys in ((2 * a, k), (2 * a + 1, pltpu.roll(k, HEAD_DIM, 1))):
                    ka, n2 = _aug_keys(keys)
                    ka_scr[r, c * tk:(c + 1) * tk, :] = ka
                    kmax[r] = n2 if kmax[r] is None else jnp.maximum(kmax[r], n2)
                vt_scr[a, c, :dv, :] = v_all[:, cols].T
                vt_scr[a, c, dv:, :] = jnp.ones((DIFF_VT_ROWS - dv, tk), BF16)
        for r, n2 in enumerate(kmax):
            kmax_scr[r] = n2
        _fp8_keys(ka_scr, k8_scr, kscale_scr, kmax, n_chunks=n_chunks, tk=tk)

    q_all = q_ref[...].astype(F32)
    slabs = []
    for cols in heads:
        slabs += [q_all[:, cols], pltpu.roll(q_all[:, cols], HEAD_DIM, 1)]
    _softmax_tile(ka_scr, k8_scr, vt_scr, kmax_scr, kscale_scr, slabs, m_scr, acc_scr, dv,
                  n_chunks=n_chunks, tk=tk)
    for a, cols in enumerate(heads):
        a1, a2 = acc_scr[2 * a], acc_scr[2 * a + 1]
        yt = a1[:dv] * (1.0 / a1[dv:dv + 1]) - lam_ref[0] * (a2[:dv] * (1.0 / a2[dv:dv + 1]))
        y = yt.T
        ms = jnp.mean(y * y, axis=-1, keepdims=True)
        o_ref[:, cols] = (y * lax.rsqrt(ms + EPS) * g_ref[...]).astype(BF16)


def _gqa_attn_kernel(q_ref, kvl_ref, kvc_ref, o_ref, ka_scr, k8_scr, vt_scr, kmax_scr, kscale_scr, m_scr, acc_scr,
                     *, n_lat_chunks, n_chunks, tk):
    w2 = 2 * HEAD_DIM
    groups = [slice(g * w2, (g + 1) * w2) for g in range(GQA_GROUPS_PER_STEP)]
    n_heads = GQA_REP * len(groups)

    @pl.when(pl.program_id(2) == 0)
    def _():
        kmax = [None] * len(groups)
        for c in range(n_chunks):
            kv_all = _chunk_rows(c, n_lat_chunks, tk, kvl_ref, kvc_ref)
            for g, cols in enumerate(groups):
                kv = kv_all[:, cols]
                ka, n2 = _aug_keys(kv.astype(F32))
                ka_scr[g, c * tk:(c + 1) * tk, :] = ka
                kmax[g] = n2 if kmax[g] is None else jnp.maximum(kmax[g], n2)
                t = kv.T
                row = lax.broadcasted_iota(jnp.int32, t.shape, 0)
                vt_scr[g, c] = jnp.where(row < HEAD_DIM, jnp.ones_like(t), t)
        for g, n2 in enumerate(kmax):
            kmax_scr[g] = n2
        _fp8_keys(ka_scr, k8_scr, kscale_scr, kmax, n_chunks=n_chunks, tk=tk)

    qf = q_ref[...].astype(F32)
    tq = qf.shape[0]
    lane = lax.broadcasted_iota(jnp.int32, (tq, w2), 1)
    slabs = []
    for r in range(n_heads):
        slab = qf[:, (r // 2) * w2:(r // 2 + 1) * w2]
        slabs.append(pltpu.roll(slab, HEAD_DIM, 1) if r % 2 else slab)
    _softmax_tile(ka_scr, k8_scr, vt_scr, kmax_scr, kscale_scr, slabs, m_scr, acc_scr, 0,
                  n_chunks=n_chunks, tk=tk)

    def head_out(r):
        acc = acc_scr[r]
        return (acc * (1.0 / acc[0:1])).T

    for j in range(n_heads // 2):
        pair = jnp.where(lane < HEAD_DIM, pltpu.roll(head_out(2 * j), HEAD_DIM, 1), head_out(2 * j + 1))
        o_ref[:, j * w2:(j + 1) * w2] = pair.astype(BF16)


def _attention(p, lam, subln_gain, *, B, S, C, tiles, ctx_queries):
    lat = _attention_calls(p, lam, subln_gain, B=B, S=S, C=C, tq=tiles.tq, tk=tiles.tk, latent=True)
    if not ctx_queries:
        return lat, None
    return lat, _attention_calls(p, lam, subln_gain, B=B, S=S, C=C, tq=C, tk=tiles.tk, latent=False)


def _attention_calls(p, lam, subln_gain, *, B, S, C, tq, tk, latent):
    ctx_blk0 = (B * S) // C
    if latent:
        nq, rows_out, s_kv, n_lat_chunks = S // tq, B * S, S, S // tk
        q_row = out_row = lambda b, qi: b * nq + qi
        lat_blk = lambda b: b
    else:
        nq, rows_out, s_kv, n_lat_chunks = 1, B * C, C, 0
        q_row = lambda b, qi: ctx_blk0 + b
        out_row = lambda b, qi: b
        lat_blk = lambda b: ctx_blk0 + b
    T = n_lat_chunks * tk + C
    S = s_kv
    statics = dict(n_lat_chunks=n_lat_chunks, n_chunks=T // tk, tk=tk)

    cparams = pltpu.CompilerParams(
        dimension_semantics=("arbitrary", "arbitrary", "arbitrary"), vmem_limit_bytes=VMEM_LIMIT)
    w2 = 2 * HEAD_DIM
    hps = DIFF_HEADS_PER_STEP
    wd = hps * w2
    yd = pl.pallas_call(
        functools.partial(_diff_attn_kernel, **statics),
        grid=(B, N_DIFF_HEADS // hps, nq),
        in_specs=[
            pl.BlockSpec(memory_space=pltpu.SMEM),
            pl.BlockSpec((tq, wd), lambda b, h, qi: (q_row(b, qi), DQ_OFF // wd + h)),
            pl.BlockSpec((S, wd), lambda b, h, qi: (lat_blk(b), DK_OFF // wd + h)),
            pl.BlockSpec((S, wd), lambda b, h, qi: (lat_blk(b), DV_OFF // wd + h)),
            pl.BlockSpec((C, wd), lambda b, h, qi: (ctx_blk0 + b, DK_OFF // wd + h)),
            pl.BlockSpec((C, wd), lambda b, h, qi: (ctx_blk0 + b, DV_OFF // wd + h)),
            pl.BlockSpec((1, w2), lambda b, h, qi: (0, 0)),
        ],
        out_specs=pl.BlockSpec((tq, wd), lambda b, h, qi: (out_row(b, qi), h)),
        out_shape=jax.ShapeDtypeStruct((rows_out, N_DIFF_HEADS * w2), BF16),
        scratch_shapes=[
            pltpu.VMEM((2 * hps, T, w2), BF16),
            pltpu.VMEM((2 * hps, T, w2), FP8),
            pltpu.VMEM((hps, T // tk, DIFF_VT_ROWS, tk), BF16),
            pltpu.VMEM((2 * hps, 1, w2), F32),
            pltpu.VMEM((2 * hps, 1, w2), F32),
            pltpu.VMEM((2 * hps, 1, tq), F32),
            pltpu.VMEM((2 * hps, DIFF_VT_ROWS, tq), F32),
        ],
        compiler_params=cparams,
        name="diff_attn" if latent else "diff_attn_ctx",
    )(lam, p, p, p, p, p, subln_gain)

    gps = GQA_GROUPS_PER_STEP
    wq = gps * GQA_REP * HEAD_DIM
    wkv = gps * w2
    yg = pl.pallas_call(
        functools.partial(_gqa_attn_kernel, **statics),
        grid=(B, N_GQA_KV // gps, nq),
        in_specs=[
            pl.BlockSpec((tq, wq), lambda b, g, qi: (q_row(b, qi), GQ_OFF // wq + g)),
            pl.BlockSpec((S, wkv), lambda b, g, qi: (lat_blk(b), GKV_OFF // wkv + g)),
            pl.BlockSpec((C, wkv), lambda b, g, qi: (ctx_blk0 + b, GKV_OFF // wkv + g)),
        ],
        out_specs=pl.BlockSpec((tq, wq), lambda b, g, qi: (out_row(b, qi), g)),
        out_shape=jax.ShapeDtypeStruct((rows_out, N_GQA_HEADS * HEAD_DIM), BF16),
        scratch_shapes=[
            pltpu.VMEM((gps, T, w2), BF16),
            pltpu.VMEM((gps, T, w2), FP8),
            pltpu.VMEM((gps, T // tk, w2, tk), BF16),
            pltpu.VMEM((gps, 1, w2), F32),
            pltpu.VMEM((gps, 1, w2), F32),
            pltpu.VMEM((gps * GQA_REP, 1, tq), F32),
            pltpu.VMEM((gps * GQA_REP, w2, tq), F32),
        ],
        compiler_params=cparams,
        name="gqa_attn" if latent else "gqa_attn_ctx",
    )(p, p, p)
    return yd, yg


def _mixer_kernel(xl_ref, xc_ref, ydl_ref, ygl_ref, ydc_ref, ygc_ref, gd_ref, gg_ref, gt_ref, wpd_ref, wpg_ref,
                  wo_ref, o_ref, *, x_lat_tiles, y_lat_tiles):
    x = _select_rows(xl_ref, xc_ref, x_lat_tiles)
    yd = _select_rows(ydl_ref, ydc_ref, y_lat_tiles)
    yg = _select_rows(ygl_ref, ygc_ref, y_lat_tiles)
    pd = jnp.dot(yd, wpd_ref[...], preferred_element_type=F32)
    pg = jnp.dot(yg, wpg_ref[...], preferred_element_type=F32)
    m = jax.nn.sigmoid(gd_ref[...].astype(F32)) * pd + jax.nn.sigmoid(gg_ref[...].astype(F32)) * pg
    o_ref[...] = x + gt_ref[...] * jnp.dot(m.astype(BF16), wo_ref[...], preferred_element_type=F32)


def _mixer_out(xs, lat, ctx, p, mods, wpd, wpg, wo, *, rows, tm, tiles_per_seq, n_batch):
    y_lat_tiles = lat[0].shape[0] // tm
    ctx = lat if ctx is None else ctx
    x_lat, x_ctx, x_lat_tiles = _split_rows(xs, tm)
    row_to_mod = lambda i: jnp.minimum(i // tiles_per_seq, n_batch)
    row_tile = pl.BlockSpec((tm, D_MODEL), lambda i: (i, 0))
    y_lat_spec, y_ctx_spec = _row_specs(tm, y_lat_tiles)
    weight = pl.BlockSpec((D_MODEL, D_MODEL), lambda i: (0, 0))
    return pl.pallas_call(
        functools.partial(_mixer_kernel, x_lat_tiles=x_lat_tiles, y_lat_tiles=y_lat_tiles),
        grid=(rows // tm,),
        in_specs=[
            *_row_specs(tm, x_lat_tiles), y_lat_spec, y_lat_spec, y_ctx_spec, y_ctx_spec,
            pl.BlockSpec((tm, D_MODEL), lambda i: (i, GATE_OFF // D_MODEL)),
            pl.BlockSpec((tm, D_MODEL), lambda i: (i, GATE_OFF // D_MODEL + 1)),
            _mod_spec(2, row_to_mod),
            weight, weight, weight,
        ],
        out_specs=row_tile,
        out_shape=jax.ShapeDtypeStruct((rows, D_MODEL), F32),
        compiler_params=pltpu.CompilerParams(
            dimension_semantics=("arbitrary",), vmem_limit_bytes=VMEM_LIMIT),
        name="mixer_out",
    )(x_lat, x_ctx, lat[0], lat[1], ctx[0], ctx[1], p, p, mods, wpd, wpg, wo)


def _ffn_kernel(x_ref, g_ref, sc_ref, sh_ref, gt_ref, w1_ref, w3_ref, w2_ref, o_ref, h_scr, acc_scr):
    f = pl.program_id(1)

    @pl.when(f == 0)
    def _():
        h_scr[...] = _modulated_norm(x_ref[...], g_ref[...], sc_ref[...], sh_ref[...]).astype(BF16)
        acc_scr[...] = jnp.zeros_like(acc_scr)

    h = h_scr[...]
    a = jnp.dot(h, w1_ref[...], preferred_element_type=F32)
    b = jnp.dot(h, w3_ref[...], preferred_element_type=F32)
    acc_scr[...] += jnp.dot((_silu(a) * b).astype(BF16), w2_ref[...], preferred_element_type=F32)

    @pl.when(f == pl.num_programs(1) - 1)
    def _():
        o_ref[...] = x_ref[...] + gt_ref[...] * acc_scr[...]


def _ffn(xs, mods, norm_g, w1, w3, w2, *, tm, tf, tiles_per_seq, n_batch):
    rows = xs.shape[0]
    d_ff = w1.shape[1]
    row_to_mod = lambda i: jnp.minimum(i // tiles_per_seq, n_batch)
    row_tile = pl.BlockSpec((tm, D_MODEL), lambda i, f: (i, 0))
    return pl.pallas_call(
        _ffn_kernel,
        grid=(rows // tm, d_ff // tf),
        in_specs=[
            row_tile,
            pl.BlockSpec((1, D_MODEL), lambda i, f: (0, 0)),
            _mod_spec(4, row_to_mod), _mod_spec(3, row_to_mod), _mod_spec(5, row_to_mod),
            pl.BlockSpec((D_MODEL, tf), lambda i, f: (0, f)),
            pl.BlockSpec((D_MODEL, tf), lambda i, f: (0, f)),
            pl.BlockSpec((tf, D_MODEL), lambda i, f: (f, 0)),
        ],
        out_specs=row_tile,
        out_shape=jax.ShapeDtypeStruct((rows, D_MODEL), F32),
        scratch_shapes=[pltpu.VMEM((tm, D_MODEL), BF16), pltpu.VMEM((tm, D_MODEL), F32)],
        compiler_params=pltpu.CompilerParams(
            dimension_semantics=("arbitrary", "arbitrary"), vmem_limit_bytes=VMEM_LIMIT),
        name="ffn_swiglu",
    )(xs, norm_g, mods, mods, mods, w1, w3, w2)


def _top2_gates(logits):
    lane = lax.broadcasted_iota(jnp.int32, logits.shape, 1)
    n_lanes = logits.shape[1]
    lg = jnp.where(lane < N_EXPERTS, logits, NEG_BIG)
    m1 = jnp.max(lg, axis=-1, keepdims=True)
    i1 = jnp.min(jnp.where(lg == m1, lane, n_lanes), axis=-1, keepdims=True)
    lg2 = jnp.where(lane == i1, NEG_BIG, lg)
    m2 = jnp.max(lg2, axis=-1, keepdims=True)
    i2 = jnp.min(jnp.where(lg2 == m2, lane, n_lanes), axis=-1, keepdims=True)
    e2 = jnp.exp(m2 - m1)
    w_top = 1.0 / (1.0 + e2)
    idx = jnp.where(lane == 0, i1, jnp.where(lane == 1, i2, 0))
    wts = jnp.where(lane == 0, w_top, jnp.where(lane == 1, e2 * w_top, 0.0))
    return idx, wts


def _router_kernel(x_ref, g_ref, sc_ref, sh_ref, rhi_ref, rlo_ref, h_ref, idx_ref, wts_ref):
    h = _modulated_norm(x_ref[...], g_ref[...], sc_ref[...], sh_ref[...])
    h_ref[...] = h
    h_hi, h_lo = _split_bf16(h)
    logits = jnp.dot(h_hi, rhi_ref[...], preferred_element_type=F32)
    logits += jnp.dot(h_lo, rhi_ref[...], preferred_element_type=F32)
    logits += jnp.dot(h_hi, rlo_ref[...], preferred_element_type=F32)
    idx_ref[...], wts_ref[...] = _top2_gates(logits)


def _router(xs, mods, norm_g, r_hi, r_lo, *, rows, tm, tiles_per_seq, n_batch):
    row_to_mod = lambda i: jnp.minimum(i // tiles_per_seq, n_batch)
    row_tile = pl.BlockSpec((tm, D_MODEL), lambda i: (i, 0))
    lanes = pl.BlockSpec((tm, 128), lambda i: (i, 0))
    router = pl.BlockSpec((D_MODEL, 128), lambda i: (0, 0))
    return pl.pallas_call(
        _router_kernel,
        grid=(rows // tm,),
        in_specs=[row_tile, pl.BlockSpec((1, D_MODEL), lambda i: (0, 0)),
                  _mod_spec(4, row_to_mod), _mod_spec(3, row_to_mod), router, router],
        out_specs=[row_tile, lanes, lanes],
        out_shape=[jax.ShapeDtypeStruct((rows, D_MODEL), F32),
                   jax.ShapeDtypeStruct((rows, 128), jnp.int32),
                   jax.ShapeDtypeStruct((rows, 128), F32)],
        compiler_params=pltpu.CompilerParams(
            dimension_semantics=("arbitrary",), vmem_limit_bytes=VMEM_LIMIT),
        name="moe_router",
    )(xs, norm_g, mods, mods, r_hi, r_lo)


def _route_plan(idx, n_tok, tm_e):
    n_pairs = 2 * n_tok
    n_tiles = n_pairs // tm_e + N_EXPERTS + 1
    n_rows = n_tiles * tm_e
    e_flat = idx.reshape(n_pairs)
    order = jnp.argsort(e_flat, stable=True).astype(jnp.int32)
    counts = jnp.sum((e_flat[:, None] == jnp.arange(N_EXPERTS, dtype=jnp.int32)[None, :]).astype(jnp.int32),
                     axis=0)
    padded = ((counts + tm_e - 1) // tm_e) * tm_e
    ends = jnp.cumsum(padded)
    tile_start = jnp.arange(n_tiles, dtype=jnp.int32) * tm_e
    tile_expert = jnp.minimum(jnp.sum((tile_start[:, None] >= ends[None, :]).astype(jnp.int32), axis=1),
                              N_EXPERTS - 1).astype(jnp.int32)
    n_used = (ends[-1] // tm_e).astype(jnp.int32).reshape(1)
    row = jnp.arange(n_rows, dtype=jnp.int32)
    row_expert = jnp.repeat(tile_expert, tm_e)
    offset = row - (ends - padded)[row_expert]
    valid = (offset < counts[row_expert]) & (row < ends[-1])
    pair = order[jnp.clip((jnp.cumsum(counts) - counts)[row_expert] + offset, 0, n_pairs - 1)]
    tok, choice = pair // 2, pair % 2
    src = jnp.where(valid, tok, 0)
    dump = n_pairs + jnp.cumsum(jnp.logical_not(valid).astype(jnp.int32)) - 1
    dst = jnp.where(valid, choice * n_tok + tok, dump)
    return tile_expert, n_used, src.reshape(n_tiles, 1, tm_e), dst.reshape(n_tiles, 1, tm_e)


def _experts_kernel(te_ref, nu_ref, src_ref, src_next_ref, dst_ref, h_hbm, w1_ref, w3_ref, w2_ref, y_hbm,
                    xbuf, xb_scr, acc_scr, ybuf, gsem, ssem, *, tm_e, n_chunks):
    t = pl.program_id(0)
    f = pl.program_id(1)
    nf = pl.num_programs(1)
    n_used = nu_ref[0]
    slot = t % 2

    def gather_rows(idx_ref, s):
        def body(r, carry):
            pltpu.make_async_copy(h_hbm.at[pl.ds(idx_ref[0, r], 1)], xbuf.at[s, pl.ds(r, 1)], gsem.at[s]).start()
            return carry
        lax.fori_loop(0, tm_e, body, 0, unroll=8)

    def wait_gather(s):
        pltpu.make_async_copy(h_hbm.at[pl.ds(0, tm_e)], xbuf.at[s], gsem.at[s]).wait()

    def scatter_rows(s):
        def body(r, carry):
            pltpu.make_async_copy(ybuf.at[s, pl.ds(r, 1)], y_hbm.at[pl.ds(dst_ref[0, r], 1)], ssem.at[s]).start()
            return carry
        lax.fori_loop(0, tm_e, body, 0, unroll=8)

    def wait_scatter(s):
        pltpu.make_async_copy(ybuf.at[s], y_hbm.at[pl.ds(0, tm_e)], ssem.at[s]).wait()

    @pl.when((f == 0) & (t <= n_used))
    def _():
        @pl.when(t == 0)
        def _():
            gather_rows(src_ref, 0)
        for s in range(2):
            @pl.when(slot == s)
            def _():
                wait_gather(s)

                @pl.when(t < n_used)
                def _():
                    xb_scr[...] = xbuf[s].astype(BF16)
                    acc_scr[...] = jnp.zeros_like(acc_scr)

    @pl.when(t < n_used)
    def _():
        x = xb_scr[...]
        a = jnp.dot(x, w1_ref[...].astype(BF16), preferred_element_type=F32)
        b = jnp.dot(x, w3_ref[...].astype(BF16), preferred_element_type=F32)
        acc_scr[...] += jnp.dot((_silu(a) * b).astype(BF16), w2_ref[...].astype(BF16),
                                preferred_element_type=F32)
        share = tm_e // n_chunks
        for i in range(share):
            r = f * share + i
            pltpu.make_async_copy(h_hbm.at[pl.ds(src_next_ref[0, r], 1)],
                                  xbuf.at[1 - slot, pl.ds(r, 1)], gsem.at[1 - slot]).start()

    @pl.when(f == nf - 1)
    def _():
        for s in range(2):
            @pl.when(slot == s)
            def _():
                @pl.when(t >= 2)
                def _():
                    wait_scatter(s)

                @pl.when(t < n_used)
                def _():
                    ybuf[s] = acc_scr[...]

                @pl.when(t >= n_used)
                def _():
                    ybuf[s] = jnp.zeros(ybuf.shape[1:], F32)

                scatter_rows(s)

                @pl.when(t == pl.num_programs(0) - 1)
                def _():
                    wait_scatter(s)
                    wait_scatter(1 - s)


def _experts(h, plan, w1, w3, w2, *, n_tok, tm_e, tf):
    tile_expert, n_used, src, dst = plan
    n_tiles = src.shape[0]
    d_ff = w1.shape[2]
    nf = d_ff // tf
    n_out = n_tiles * tm_e

    def chunk(t, f, nu):
        return jnp.where(t < nu[0], f, nf - 1)

    smem_tile = lambda shift: pl.BlockSpec(
        (None, 1, tm_e), lambda t, f, te, nu: (jnp.minimum(t + shift, n_tiles - 1), 0, 0),
        memory_space=pltpu.SMEM)
    grid_spec = pltpu.PrefetchScalarGridSpec(
        num_scalar_prefetch=2,
        grid=(n_tiles, nf),
        in_specs=[
            smem_tile(0), smem_tile(1), smem_tile(0),
            pl.BlockSpec(memory_space=pl.ANY),
            pl.BlockSpec((None, D_MODEL, tf), lambda t, f, te, nu: (te[t], 0, chunk(t, f, nu))),
            pl.BlockSpec((None, D_MODEL, tf), lambda t, f, te, nu: (te[t], 0, chunk(t, f, nu))),
            pl.BlockSpec((None, tf, D_MODEL), lambda t, f, te, nu: (te[t], chunk(t, f, nu), 0)),
        ],
        out_specs=pl.BlockSpec(memory_space=pl.ANY),
        scratch_shapes=[
            pltpu.VMEM((2, tm_e, D_MODEL), F32),
            pltpu.VMEM((tm_e, D_MODEL), BF16),
            pltpu.VMEM((tm_e, D_MODEL), F32),
            pltpu.VMEM((2, tm_e, D_MODEL), F32),
            pltpu.SemaphoreType.DMA((2,)),
            pltpu.SemaphoreType.DMA((2,)),
        ],
    )
    return pl.pallas_call(
        functools.partial(_experts_kernel, tm_e=tm_e, n_chunks=nf),
        grid_spec=grid_spec,
        out_shape=jax.ShapeDtypeStruct((n_out, D_MODEL), F32),
        compiler_params=pltpu.CompilerParams(
            dimension_semantics=("arbitrary", "arbitrary"), vmem_limit_bytes=VMEM_LIMIT),
        name="moe_experts",
    )(tile_expert, n_used, src, src, dst, h, w1, w3, w2)


def _combine_kernel(x_ref, y0_ref, y1_ref, wts_ref, gt_ref, fg_ref, o_ref):
    wts = wts_ref[...]
    moe = wts[:, 0:1] * y0_ref[...] + wts[:, 1:2] * y1_ref[...]
    y = x_ref[...] + gt_ref[...] * moe
    ms = jnp.mean(y * y, axis=-1, keepdims=True)
    o_ref[...] = y * lax.rsqrt(ms + EPS) * fg_ref[...]


def _combine(xs, y, wts, mods, final_g, *, rows, tm, tiles_per_seq, n_batch):
    row_to_mod = lambda i: jnp.minimum(i // tiles_per_seq, n_batch)
    row_tile = pl.BlockSpec((tm, D_MODEL), lambda i: (i, 0))
    return pl.pallas_call(
        _combine_kernel,
        grid=(rows // tm,),
        in_specs=[
            row_tile, row_tile,
            pl.BlockSpec((tm, D_MODEL), lambda i: (i + rows // tm, 0)),
            pl.BlockSpec((tm, 128), lambda i: (i, 0)),
            _mod_spec(5, row_to_mod),
            pl.BlockSpec((1, D_MODEL), lambda i: (0, 0)),
        ],
        out_specs=row_tile,
        out_shape=jax.ShapeDtypeStruct((rows, D_MODEL), F32),
        compiler_params=pltpu.CompilerParams(
            dimension_semantics=("arbitrary",), vmem_limit_bytes=VMEM_LIMIT),
        name="moe_combine",
    )(xs, y, y, wts, mods, final_g)


def _moe(xs, mods, norm_g, r_hi, r_lo, final_g, w1, w3, w2, *, rows, tm, tm_e, tf, tiles_per_seq, n_batch):
    h, idx, wts = _router(xs, mods, norm_g, r_hi, r_lo, rows=rows, tm=tm, tiles_per_seq=tiles_per_seq,
                          n_batch=n_batch)
    plan = _route_plan(idx[:, :2], rows, tm_e)
    y = _experts(h, plan, w1, w3, w2, n_tok=rows, tm_e=tm_e, tf=tf)
    return _combine(xs, y, wts, mods, final_g, rows=rows, tm=tm, tiles_per_seq=tiles_per_seq, n_batch=n_batch)


def _deinterleave(n=HEAD_DIM):
    return np.concatenate([np.arange(0, n, 2), np.arange(1, n, 2)])


def _relayout_w_in(w):
    d = w.shape[0]
    dq, gq, dk, dv, gk, gv, gate = jnp.split(w, [1024, 2048, 3072, 4096, 4352, 4608], axis=1)

    def deint(a):
        return a.reshape(d, -1, HEAD_DIM // 2, 2).swapaxes(2, 3).reshape(d, -1)

    gkv = jnp.concatenate([deint(gk).reshape(d, N_GQA_KV, HEAD_DIM), gv.reshape(d, N_GQA_KV, HEAD_DIM)], axis=2)
    parts = [gate, deint(dq) * HEAD_DIM ** -0.5, deint(gq), deint(dk), dv, gkv.reshape(d, -1)]
    return jnp.concatenate(parts, axis=1).astype(BF16)


def _rope_tables(S, pad_rows):
    rows = S // GRID_W
    row = jnp.repeat(jnp.arange(rows, dtype=F32), GRID_W)
    col = jnp.tile(jnp.arange(GRID_W, dtype=F32), rows)
    half = HEAD_DIM // 2
    inv_freq = ROPE_THETA ** (-jnp.arange(0, half, 2, dtype=F32) / half)
    ang = jnp.concatenate([row[:, None] * inv_freq, col[:, None] * inv_freq], axis=-1)
    cos, sin = jnp.cos(ang), jnp.sin(ang)
    reps = PROJ_TN // HEAD_DIM
    cos_t = jnp.tile(jnp.concatenate([cos, cos], axis=-1), (1, reps))
    sin_t = jnp.tile(jnp.concatenate([-sin, sin], axis=-1), (1, reps))
    cos_t = jnp.concatenate([cos_t, jnp.ones((pad_rows, PROJ_TN), F32)], axis=0)
    sin_t = jnp.concatenate([sin_t, jnp.zeros((pad_rows, PROJ_TN), F32)], axis=0)
    return cos_t, sin_t


def kernel(x, c, ctx, c_ctx, ada_w, ada_b, norm_attn_g, norm_ffn_g, w_in, q_norm_g, k_norm_g, diff_lambda,
           diff_subln_g, w_proj_diff, w_proj_gqa, w_out, ffn_w1, ffn_w3, ffn_w2, moe_router, moe_w1, moe_w3,
           moe_w2, final_norm_g):
    B, S, D = x.shape
    C = ctx.shape[1]
    depth = ada_w.shape[0]
    assert D == D_MODEL and depth == 2 and B + 1 <= MOD_ROWS
    assert w_in.shape[2] == IN_W and moe_router.shape[2] == N_EXPERTS
    tiles = _pick_tiles(B, S, C, ffn_w1.shape[2], moe_w1.shape[3])
    n_lat = B * S

    cvec = jnp.zeros((MOD_ROWS, D), F32).at[:B].set(c).at[B].set(c_ctx)
    mods_all = _ada_mods(cvec, ada_w, ada_b).reshape(depth, MOD_ROWS * N_MODS, 1, D)

    de = _deinterleave()
    cos_t, sin_t = _rope_tables(S, tiles.tm_proj)
    blk = np.arange(PROJ_TN) // HEAD_DIM
    ones_bd = jnp.asarray(blk[:, None] == blk[None, :], BF16)
    col = np.arange(PROJ_TN)
    swap_bd = jnp.asarray(col[:, None] == (col[None, :] ^ (HEAD_DIM // 2)), BF16)
    is_key_lane = (np.arange(PROJ_TN) % (2 * HEAD_DIM)) < HEAD_DIM

    xs = (x.reshape(n_lat, D), ctx.reshape(B * C, D))
    for l in range(depth):
        last = l == depth - 1
        mods = mods_all[l]
        lam_init = 0.8 - 0.6 * math.exp(-0.3 * l)
        lq1, lk1, lq2, lk2 = diff_lambda[l]
        lam = (jnp.exp(jnp.sum(lq1 * lk1)) - jnp.exp(jnp.sum(lq2 * lk2)) + lam_init).reshape(1).astype(F32)
        w = _relayout_w_in(w_in[l])
        qg = jnp.tile(q_norm_g[l][de] * HEAD_DIM ** -0.5, PROJ_TN // HEAD_DIM).reshape(1, PROJ_TN)
        kg = jnp.where(is_key_lane, jnp.tile(k_norm_g[l][de], PROJ_TN // HEAD_DIM), 1.0).reshape(1, PROJ_TN)
        subln = (diff_subln_g[l] * (1.0 - lam_init)).reshape(1, 2 * HEAD_DIM)

        p = _project(xs, mods, norm_attn_g[l].reshape(1, D), w, cos_t, sin_t, ones_bd, swap_bd, qg, kg,
                     tm=tiles.tm_proj, lat_tiles=n_lat // tiles.tm_proj, tiles_per_seq=S // tiles.tm_proj,
                     n_batch=B)
        y_lat, y_ctx = _attention(p, lam, subln, B=B, S=S, C=C, tiles=tiles, ctx_queries=not last)
        rows = n_lat if last else n_lat + B * C
        xs = _mixer_out(xs, y_lat, y_ctx, p, mods, w_proj_diff[l].astype(BF16), w_proj_gqa[l].astype(BF16),
                        w_out[l].astype(BF16), rows=rows, tm=tiles.tm_mix, tiles_per_seq=S // tiles.tm_mix,
                        n_batch=B)
        i = l // 2
        if l % 2 == 0:
            xs = _ffn(xs, mods, norm_ffn_g[l].reshape(1, D), ffn_w1[i].astype(BF16), ffn_w3[i].astype(BF16),
                      ffn_w2[i].astype(BF16), tm=tiles.tm_ffn, tf=tiles.tf_ffn,
                      tiles_per_seq=S // tiles.tm_ffn, n_batch=B)
        else:
            r_pad = jnp.zeros((D, 128), F32).at[:, :N_EXPERTS].set(moe_router[i])
            r_hi = r_pad.astype(BF16)
            r_lo = (r_pad - r_hi.astype(F32)).astype(BF16)
            xs = _moe(xs, mods, norm_ffn_g[l].reshape(1, D), r_hi, r_lo, final_norm_g.reshape(1, D),
                      moe_w1[i], moe_w3[i], moe_w2[i],
                      rows=rows, tm=tiles.tm_mix, tm_e=tiles.tm_moe, tf=tiles.tf_moe,
                      tiles_per_seq=S // tiles.tm_mix, n_batch=B)
    return xs.reshape(B, S, D)
```

```python
import functools
import math
from typing import NamedTuple

import numpy as np
import jax
import jax.numpy as jnp
from jax import lax
from jax.experimental import pallas as pl
from jax.experimental.pallas import tpu as pltpu

F32 = jnp.float32
BF16 = jnp.bfloat16

D_MODEL = 1024
HEAD_DIM = 64
N_DIFF_HEADS = 8
N_GQA_HEADS = 16
N_GQA_KV = 4
GQA_REP = N_GQA_HEADS // N_GQA_KV
N_EXPERTS = 8
GRID_W = 64
ROPE_THETA = 10000.0
EPS = 1e-6
N_MODS = 6
MOD_ROWS = 16
NEG_BIG = -1e30

GATE_OFF = 0
DQ_OFF = 2048
GQ_OFF = 3072
DK_OFF = 4096
DV_OFF = 5120
GKV_OFF = 6144
IN_W = 6656
PROJ_TN = 512
ROPE_TILES = (4, 5, 8, 9)
QNORM_TILES = (6, 7)
KV_TILE = 12

VMEM_LIMIT = 52 * 1024 * 1024


class Tiles(NamedTuple):
    tm_proj: int
    tq: int
    tk: int
    tm_mix: int
    tm_ffn: int
    tf_ffn: int
    tf_moe: int
    tm_moe: int


def _largest_divisor(n, candidates):
    for c in candidates:
        if n % c == 0:
            return c
    raise ValueError(f"no tile in {candidates} divides {n}")


def _pick_tiles(B, S, C, d_ff, d_ff_e):
    rows_common = math.gcd(S, B * C)
    tm = _largest_divisor(rows_common, (512, 256, 128))
    return Tiles(
        tm_proj=_largest_divisor(rows_common, (1024, 512, 256, 128)),
        tq=_largest_divisor(S, (256, 128)),
        tk=_largest_divisor(math.gcd(S, C), (256, 128)),
        tm_mix=tm,
        tm_ffn=_largest_divisor(rows_common, (1024, 512, 256, 128)),
        tf_ffn=_largest_divisor(d_ff, (1408, 1024, 512, 256, 128)),
        tf_moe=_largest_divisor(d_ff_e, (896, 512, 256, 128)),
        tm_moe=_largest_divisor(2 * B * S, (512, 256)),
    )


def _split_bf16(v):
    hi = v.astype(BF16)
    lo = (v - hi.astype(F32)).astype(BF16)
    return hi, lo


def _ada_kernel(c_ref, w_ref, b_ref, o_ref):
    c = c_ref[...]
    s = c / (1.0 + jnp.exp(-c))
    s_hi, s_lo = _split_bf16(s)
    w_hi, w_lo = _split_bf16(w_ref[...])
    acc = jnp.dot(s_hi, w_hi, preferred_element_type=F32)
    acc += jnp.dot(s_lo, w_hi, preferred_element_type=F32)
    acc += jnp.dot(s_hi, w_lo, preferred_element_type=F32)
    o_ref[...] = acc + b_ref[...]


def _ada_mods(cvec, ada_w, ada_b):
    depth, d, n = ada_w.shape
    tn = _largest_divisor(n, (1536, 1024, 512))
    return pl.pallas_call(
        _ada_kernel,
        grid=(depth, n // tn),
        in_specs=[
            pl.BlockSpec((MOD_ROWS, d), lambda l, j: (0, 0)),
            pl.BlockSpec((None, d, tn), lambda l, j: (l, 0, j)),
            pl.BlockSpec((None, 1, tn), lambda l, j: (l, 0, j)),
        ],
        out_specs=pl.BlockSpec((None, MOD_ROWS, tn), lambda l, j: (l, 0, j)),
        out_shape=jax.ShapeDtypeStruct((depth, MOD_ROWS, n), F32),
        compiler_params=pltpu.CompilerParams(
            dimension_semantics=("arbitrary", "arbitrary"), vmem_limit_bytes=VMEM_LIMIT),
        name="ada_mods",
    )(cvec, ada_w, ada_b.reshape(depth, 1, n))


def _modulated_norm(x, g, scale, shift):
    ms = jnp.mean(x * x, axis=-1, keepdims=True)
    return (x * lax.rsqrt(ms + EPS) * g) * (1.0 + scale) + shift


def _silu(a):
    return a / (1.0 + jnp.exp(-a))


def _mod_spec(k, row_to_mod):
    return pl.BlockSpec((None, 1, D_MODEL), lambda i, *_: (row_to_mod(i) * N_MODS + k, 0, 0))


def _rope(z, cos, sin_signed, swap_ref):
    partner = jnp.dot(z.astype(BF16), swap_ref[...], preferred_element_type=F32)
    return z * cos + partner * sin_signed


def _head_rms(z, ones_ref, gain):
    ss = jnp.dot((z * z).astype(BF16), ones_ref[...], preferred_element_type=F32)
    return z * lax.rsqrt(ss * (1.0 / HEAD_DIM) + EPS) * gain


def _split_rows(xs, tm):
    if isinstance(xs, tuple):
        return xs[0], xs[1], xs[0].shape[0] // tm
    return xs, xs, xs.shape[0] // tm


def _row_specs(tm, lat_tiles):
    return (pl.BlockSpec((tm, D_MODEL), lambda i, *_: (jnp.minimum(i, lat_tiles - 1), 0)),
            pl.BlockSpec((tm, D_MODEL), lambda i, *_: (jnp.maximum(i - lat_tiles, 0), 0)))


def _select_rows(lat_ref, ctx_ref, lat_tiles):
    return jnp.where(pl.program_id(0) >= lat_tiles, ctx_ref[...], lat_ref[...])


def _proj_kernel(xl_ref, xc_ref, g_ref, sc_ref, sh_ref, w_ref, cos_ref, sin_ref, ones_ref, swap_ref, qg_ref, kg_ref,
                 o_ref, h_scr, *, lat_tiles):
    j = pl.program_id(1)

    @pl.when(j == 0)
    def _():
        x = _select_rows(xl_ref, xc_ref, lat_tiles)
        h_scr[...] = _modulated_norm(x, g_ref[...], sc_ref[...], sh_ref[...]).astype(BF16)

    def project():
        z = jnp.dot(h_scr[...], w_ref[...], preferred_element_type=F32)
        return z, lax.broadcasted_iota(jnp.int32, z.shape, 1)

    is_rope = functools.reduce(jnp.logical_or, [j == t for t in ROPE_TILES])
    is_qnorm = functools.reduce(jnp.logical_or, [j == t for t in QNORM_TILES])
    is_kv = j == KV_TILE
    is_plain = jnp.logical_not(is_rope | is_qnorm | is_kv)

    @pl.when(is_plain)
    def _():
        z, _ = project()
        o_ref[...] = z.astype(BF16)

    @pl.when(is_rope)
    def _():
        z, lane = project()
        o_ref[...] = _rope(z, cos_ref[...], sin_ref[...], swap_ref).astype(BF16)

    @pl.when(is_qnorm)
    def _():
        z, lane = project()
        zn = _head_rms(z, ones_ref, qg_ref[...])
        o_ref[...] = _rope(zn, cos_ref[...], sin_ref[...], swap_ref).astype(BF16)

    @pl.when(is_kv)
    def _():
        z, lane = project()
        zn = _head_rms(z, ones_ref, kg_ref[...])
        zr = _rope(zn, cos_ref[...], sin_ref[...], swap_ref)
        is_key_lane = (lane & HEAD_DIM) == 0
        o_ref[...] = jnp.where(is_key_lane, zr, z).astype(BF16)


def _project(xs, mods, norm_g, w, cos_t, sin_t, ones_bd, swap_bd, qg, kg, *, tm, lat_tiles, tiles_per_seq, n_batch):
    x_lat, x_ctx, src_tiles = _split_rows(xs, tm)
    rows = x_lat.shape[0] + (x_ctx.shape[0] if isinstance(xs, tuple) else 0)
    row_to_mod = lambda i: jnp.minimum(i // tiles_per_seq, n_batch)
    rope_row = lambda i: jnp.where(i < lat_tiles, i % tiles_per_seq, tiles_per_seq)
    const = lambda i, j: (0, 0)
    return pl.pallas_call(
        functools.partial(_proj_kernel, lat_tiles=src_tiles),
        grid=(rows // tm, IN_W // PROJ_TN),
        in_specs=[
            *_row_specs(tm, src_tiles),
            pl.BlockSpec((1, D_MODEL), const),
            _mod_spec(1, row_to_mod),
            _mod_spec(0, row_to_mod),
            pl.BlockSpec((D_MODEL, PROJ_TN), lambda i, j: (0, j)),
            pl.BlockSpec((tm, PROJ_TN), lambda i, j: (rope_row(i), 0)),
            pl.BlockSpec((tm, PROJ_TN), lambda i, j: (rope_row(i), 0)),
            pl.BlockSpec((PROJ_TN, PROJ_TN), const),
            pl.BlockSpec((PROJ_TN, PROJ_TN), const),
            pl.BlockSpec((1, PROJ_TN), const),
            pl.BlockSpec((1, PROJ_TN), const),
        ],
        out_specs=pl.BlockSpec((tm, PROJ_TN), lambda i, j: (i, j)),
        out_shape=jax.ShapeDtypeStruct((rows, IN_W), BF16),
        scratch_shapes=[pltpu.VMEM((tm, D_MODEL), BF16)],
        compiler_params=pltpu.CompilerParams(
            dimension_semantics=("arbitrary", "arbitrary"), vmem_limit_bytes=VMEM_LIMIT),
        name="in_proj",
    )(x_lat, x_ctx, norm_g, mods, mods, w, cos_t, sin_t, ones_bd, swap_bd, qg, kg)


STAB_LANE = HEAD_DIM
L_FLOOR = 1e-26
L_CEIL = 1e30
FP8 = jnp.float8_e4m3fn
FP8_TARGET = 256.0
FP8_BOUND_SLACK = 1.125
LOG2_E = 1.4426950408889634
DIFF_VT_ROWS = 2 * HEAD_DIM + 16
DIFF_HEADS_PER_STEP = 2
GQA_GROUPS_PER_STEP = 1


def _aug_keys(k):
    lane = lax.broadcasted_iota(jnp.int32, k.shape, 1)
    in_key = lane < HEAD_DIM
    sq = jnp.where(in_key, k * k, 0.0).astype(BF16)
    norms = jnp.dot(sq, jnp.ones((k.shape[1], k.shape[1]), BF16), preferred_element_type=F32)
    ka = jnp.where(in_key, k, jnp.where(lane == STAB_LANE, 1.0, 0.0)).astype(BF16)
    return ka, jnp.max(norms, axis=0, keepdims=True)


def _pow2_floor(x):
    bits = lax.bitcast_convert_type(x, jnp.int32) & jnp.int32(0x7F800000)
    return lax.bitcast_convert_type(bits, F32)


def _fp8_keys(ka_scr, k8_scr, kscale_scr, kmax, *, n_chunks, tk):
    for r, kmax2 in enumerate(kmax):
        scale = _pow2_floor(FP8_TARGET * lax.rsqrt(jnp.maximum(kmax2, 1e-30)))
        kscale_scr[r] = scale
        for c in range(n_chunks):
            k = ka_scr[r, c * tk:(c + 1) * tk, :].astype(F32)
            lane = lax.broadcasted_iota(jnp.int32, k.shape, 1)
            k8_scr[r, c * tk:(c + 1) * tk, :] = jnp.where(lane == STAB_LANE, FP8_TARGET, k * scale).astype(FP8)


def _query_mats(slab, kmax2, kscale):
    lane = lax.broadcasted_iota(jnp.int32, slab.shape, 1)
    in_key = lane < HEAD_DIM
    sq = jnp.where(in_key, slab * slab, 0.0).astype(BF16)
    norms = jnp.dot(sq, jnp.ones((slab.shape[1], slab.shape[1]), BF16), preferred_element_type=F32)
    nk = norms * kmax2
    bound = FP8_BOUND_SLACK * nk * lax.rsqrt(nk + 1e-30)
    qscale = _pow2_floor(FP8_TARGET * lax.rsqrt(jnp.maximum(jnp.max(norms, axis=0, keepdims=True), 1e-30)))
    both = kscale * qscale
    plain = jnp.where(in_key, slab, 0.0)
    shifted = jnp.where(lane == STAB_LANE, -bound * (both * (1.0 / FP8_TARGET)), plain * qscale).astype(FP8)
    return shifted, plain.astype(BF16), (LOG2_E / both)[:, :1]


def _attend(k8_scr, vt_scr, mats, acc_scr, *, first, n_chunks, tk):
    def scores(r, c, qmat):
        off = pl.multiple_of(c * tk, tk)
        return lax.dot_general(k8_scr[r * k8_scr.shape[0] // len(mats), pl.ds(off, tk), :], qmat,
                               (((1,), (1,)), ((), ())), preferred_element_type=F32)

    acc_scr[...] = jnp.zeros(acc_scr.shape, F32)

    def all_scores(c):
        return tuple(scores(r, c, shifted) for r, (shifted, _, _) in enumerate(mats))

    def accumulate(c, s_all):
        for r, s in enumerate(s_all):
            vt = vt_scr[r * vt_scr.shape[0] // len(mats), c]
            p = jnp.exp2((s * mats[r][2]).astype(BF16))
            acc_scr[r] += jnp.dot(vt, p, preferred_element_type=F32)

    def fast_body(c, s_cur):
        s_next = all_scores(c + 1)
        accumulate(c, s_cur)
        return s_next

    trips = n_chunks - 1 - first
    s_last = lax.fori_loop(first, n_chunks - 1, fast_body, all_scores(first),
                           unroll=_largest_divisor(trips, (16, 8, 4, 2, 1)) if trips > 0 else 1)
    accumulate(n_chunks - 1, s_last)


def _attend_fallback(ka_scr, vt_scr, mats, m_scr, acc_scr, *, first, n_chunks, tk):
    m_scr[...] = jnp.full(m_scr.shape, NEG_BIG, F32)
    acc_scr[...] = jnp.zeros(acc_scr.shape, F32)

    def body(c, carry):
        off = pl.multiple_of(c * tk, tk)
        for r, (_, plain, _) in enumerate(mats):
            s = lax.dot_general(ka_scr[r * ka_scr.shape[0] // len(mats), pl.ds(off, tk), :], plain, (((1,), (1,)), ((), ())),
                                preferred_element_type=F32)
            m_old = m_scr[r]
            m_new = jnp.maximum(m_old, jnp.max(s, axis=0, keepdims=True))
            p = jnp.exp(s - m_new).astype(BF16)
            vt = vt_scr[r * vt_scr.shape[0] // len(mats), c]
            acc_scr[r] = acc_scr[r] * jnp.exp(m_old - m_new) + jnp.dot(vt, p, preferred_element_type=F32)
            m_scr[r] = m_new
        return carry

    lax.fori_loop(first, n_chunks, body, 0)


def _softmax_tile(ka_scr, k8_scr, vt_scr, kmax_scr, kscale_scr, slabs, m_scr, acc_scr, l_row,
                  *, n_chunks, tk):
    per_key = lambda scr, r: scr[r * scr.shape[0] // len(slabs)]
    mats = [_query_mats(slab, per_key(kmax_scr, r), per_key(kscale_scr, r)) for r, slab in enumerate(slabs)]
    _attend(k8_scr, vt_scr, mats, acc_scr, first=0, n_chunks=n_chunks, tk=tk)
    dens = [acc_scr[r, l_row:l_row + 1, :] for r in range(len(slabs))]
    l_min = functools.reduce(jnp.minimum, [jnp.min(d) for d in dens])
    l_max = functools.reduce(jnp.maximum, [jnp.max(d) for d in dens])

    @pl.when(jnp.logical_not((l_min >= L_FLOOR) & (l_max <= L_CEIL)))
    def _():
        _attend_fallback(ka_scr, vt_scr, mats, m_scr, acc_scr, first=0, n_chunks=n_chunks, tk=tk)


def _chunk_rows(c, n_lat_chunks, tk, lat_ref, ctx_ref):
    if c < n_lat_chunks:
        return lat_ref[c * tk:(c + 1) * tk, :]
    return ctx_ref[(c - n_lat_chunks) * tk:(c - n_lat_chunks + 1) * tk, :]


def _diff_attn_kernel(lam_ref, q_ref, kl_ref, vl_ref, kc_ref, vc_ref, g_ref, o_ref,
                      ka_scr, k8_scr, vt_scr, kmax_scr, kscale_scr, m_scr, acc_scr,
                      *, n_lat_chunks, n_chunks, tk):
    dv = 2 * HEAD_DIM
    heads = [slice(a * dv, (a + 1) * dv) for a in range(DIFF_HEADS_PER_STEP)]

    @pl.when(pl.program_id(2) == 0)
    def _():
        kmax = [None] * (2 * len(heads))
        for c in range(n_chunks):
            k_all = _chunk_rows(c, n_lat_chunks, tk, kl_ref, kc_ref).astype(F32)
            v_all = _chunk_rows(c, n_lat_chunks, tk, vl_ref, vc_ref)
            for a, cols in enumerate(heads):
                k = k_all[:, cols]
                for r, keys in ((2 * a, k), (2 * a + 1, pltpu.roll(k, HEAD_DIM, 1))):
                    ka, n2 = _aug_keys(keys)
                    ka_scr[r, c * tk:(c + 1) * tk, :] = ka
                    kmax[r] = n2 if kmax[r] is None else jnp.maximum(kmax[r], n2)
                vt_scr[a, c, :dv, :] = v_all[:, cols].T
                vt_scr[a, c, dv:, :] = jnp.ones((DIFF_VT_ROWS - dv, tk), BF16)
        for r, n2 in enumerate(kmax):
            kmax_scr[r] = n2
        _fp8_keys(ka_scr, k8_scr, kscale_scr, kmax, n_chunks=n_chunks, tk=tk)

    q_all = q_ref[...].astype(F32)
    slabs = []
    for cols in heads:
        slabs += [q_all[:, cols], pltpu.roll(q_all[:, cols], HEAD_DIM, 1)]
    _softmax_tile(ka_scr, k8_scr, vt_scr, kmax_scr, kscale_scr, slabs, m_scr, acc_scr, dv,
                  n_chunks=n_chunks, tk=tk)
    for a, cols in enumerate(heads):
        a1, a2 = acc_scr[2 * a], acc_scr[2 * a + 1]
        yt = a1[:dv] * (1.0 / a1[dv:dv + 1]) - lam_ref[0] * (a2[:dv] * (1.0 / a2[dv:dv + 1]))
        y = yt.T
        ms = jnp.mean(y * y, axis=-1, keepdims=True)
        o_ref[:, cols] = (y * lax.rsqrt(ms + EPS) * g_ref[...]).astype(BF16)


def _gqa_attn_kernel(q_ref, kvl_ref, kvc_ref, o_ref, ka_scr, k8_scr, vt_scr, kmax_scr, kscale_scr, m_scr, acc_scr,
                     *, n_lat_chunks, n_chunks, tk):
    w2 = 2 * HEAD_DIM
    groups = [slice(g * w2, (g + 1) * w2) for g in range(GQA_GROUPS_PER_STEP)]
    n_heads = GQA_REP * len(groups)

    @pl.when(pl.program_id(2) == 0)
    def _():
        kmax = [None] * len(groups)
        for c in range(n_chunks):
            kv_all = _chunk_rows(c, n_lat_chunks, tk, kvl_ref, kvc_ref)
            for g, cols in enumerate(groups):
                kv = kv_all[:, cols]
                ka, n2 = _aug_keys(kv.astype(F32))
                ka_scr[g, c * tk:(c + 1) * tk, :] = ka
                kmax[g] = n2 if kmax[g] is None else jnp.maximum(kmax[g], n2)
                t = kv.T
                row = lax.broadcasted_iota(jnp.int32, t.shape, 0)
                vt_scr[g, c] = jnp.where(row < HEAD_DIM, jnp.ones_like(t), t)
        for g, n2 in enumerate(kmax):
            kmax_scr[g] = n2
        _fp8_keys(ka_scr, k8_scr, kscale_scr, kmax, n_chunks=n_chunks, tk=tk)

    qf = q_ref[...].astype(F32)
    tq = qf.shape[0]
    lane = lax.broadcasted_iota(jnp.int32, (tq, w2), 1)
    slabs = []
    for r in range(n_heads):
        slab = qf[:, (r // 2) * w2:(r // 2 + 1) * w2]
        slabs.append(pltpu.roll(slab, HEAD_DIM, 1) if r % 2 else slab)
    _softmax_tile(ka_scr, k8_scr, vt_scr, kmax_scr, kscale_scr, slabs, m_scr, acc_scr, 0,
                  n_chunks=n_chunks, tk=tk)

    def head_out(r):
        acc = acc_scr[r]
        return (acc * (1.0 / acc[0:1])).T

    for j in range(n_heads // 2):
        pair = jnp.where(lane < HEAD_DIM, pltpu.roll(head_out(2 * j), HEAD_DIM, 1), head_out(2 * j + 1))
        o_ref[:, j * w2:(j + 1) * w2] = pair.astype(BF16)


def _attention(p, lam, subln_gain, *, B, S, C, tiles, ctx_queries):
    lat = _attention_calls(p, lam, subln_gain, B=B, S=S, C=C, tq=tiles.tq, tk=tiles.tk, latent=True)
    if not ctx_queries:
        return lat, None
    return lat, _attention_calls(p, lam, subln_gain, B=B, S=S, C=C, tq=C, tk=tiles.tk, latent=False)


def _attention_calls(p, lam, subln_gain, *, B, S, C, tq, tk, latent):
    ctx_blk0 = (B * S) // C
    if latent:
        nq, rows_out, s_kv, n_lat_chunks = S // tq, B * S, S, S // tk
        q_row = out_row = lambda b, qi: b * nq + qi
        lat_blk = lambda b: b
    else:
        nq, rows_out, s_kv, n_lat_chunks = 1, B * C, C, 0
        q_row = lambda b, qi: ctx_blk0 + b
        out_row = lambda b, qi: b
        lat_blk = lambda b: ctx_blk0 + b
    T = n_lat_chunks * tk + C
    S = s_kv
    statics = dict(n_lat_chunks=n_lat_chunks, n_chunks=T // tk, tk=tk)

    cparams = pltpu.CompilerParams(
        dimension_semantics=("arbitrary", "arbitrary", "arbitrary"), vmem_limit_bytes=VMEM_LIMIT)
    w2 = 2 * HEAD_DIM
    hps = DIFF_HEADS_PER_STEP
    wd = hps * w2
    yd = pl.pallas_call(
        functools.partial(_diff_attn_kernel, **statics),
        grid=(B, N_DIFF_HEADS // hps, nq),
        in_specs=[
            pl.BlockSpec(memory_space=pltpu.SMEM),
            pl.BlockSpec((tq, wd), lambda b, h, qi: (q_row(b, qi), DQ_OFF // wd + h)),
            pl.BlockSpec((S, wd), lambda b, h, qi: (lat_blk(b), DK_OFF // wd + h)),
            pl.BlockSpec((S, wd), lambda b, h, qi: (lat_blk(b), DV_OFF // wd + h)),
            pl.BlockSpec((C, wd), lambda b, h, qi: (ctx_blk0 + b, DK_OFF // wd + h)),
            pl.BlockSpec((C, wd), lambda b, h, qi: (ctx_blk0 + b, DV_OFF // wd + h)),
            pl.BlockSpec((1, w2), lambda b, h, qi: (0, 0)),
        ],
        out_specs=pl.BlockSpec((tq, wd), lambda b, h, qi: (out_row(b, qi), h)),
        out_shape=jax.ShapeDtypeStruct((rows_out, N_DIFF_HEADS * w2), BF16),
        scratch_shapes=[
            pltpu.VMEM((2 * hps, T, w2), BF16),
            pltpu.VMEM((2 * hps, T, w2), FP8),
            pltpu.VMEM((hps, T // tk, DIFF_VT_ROWS, tk), BF16),
            pltpu.VMEM((2 * hps, 1, w2), F32),
            pltpu.VMEM((2 * hps, 1, w2), F32),
            pltpu.VMEM((2 * hps, 1, tq), F32),
            pltpu.VMEM((2 * hps, DIFF_VT_ROWS, tq), F32),
        ],
        compiler_params=cparams,
        name="diff_attn" if latent else "diff_attn_ctx",
    )(lam, p, p, p, p, p, subln_gain)

    gps = GQA_GROUPS_PER_STEP
    wq = gps * GQA_REP * HEAD_DIM
    wkv = gps * w2
    yg = pl.pallas_call(
        functools.partial(_gqa_attn_kernel, **statics),
        grid=(B, N_GQA_KV // gps, nq),
        in_specs=[
            pl.BlockSpec((tq, wq), lambda b, g, qi: (q_row(b, qi), GQ_OFF // wq + g)),
            pl.BlockSpec((S, wkv), lambda b, g, qi: (lat_blk(b), GKV_OFF // wkv + g)),
            pl.BlockSpec((C, wkv), lambda b, g, qi: (ctx_blk0 + b, GKV_OFF // wkv + g)),
        ],
        out_specs=pl.BlockSpec((tq, wq), lambda b, g, qi: (out_row(b, qi), g)),
        out_shape=jax.ShapeDtypeStruct((rows_out, N_GQA_HEADS * HEAD_DIM), BF16),
        scratch_shapes=[
            pltpu.VMEM((gps, T, w2), BF16),
            pltpu.VMEM((gps, T, w2), FP8),
            pltpu.VMEM((gps, T // tk, w2, tk), BF16),
            pltpu.VMEM((gps, 1, w2), F32),
            pltpu.VMEM((gps, 1, w2), F32),
            pltpu.VMEM((gps * GQA_REP, 1, tq), F32),
            pltpu.VMEM((gps * GQA_REP, w2, tq), F32),
        ],
        compiler_params=cparams,
        name="gqa_attn" if latent else "gqa_attn_ctx",
    )(p, p, p)
    return yd, yg


def _mixer_kernel(xl_ref, xc_ref, ydl_ref, ygl_ref, ydc_ref, ygc_ref, gd_ref, gg_ref, gt_ref, wpd_ref, wpg_ref,
                  wo_ref, o_ref, *, x_lat_tiles, y_lat_tiles):
    x = _select_rows(xl_ref, xc_ref, x_lat_tiles)
    yd = _select_rows(ydl_ref, ydc_ref, y_lat_tiles)
    yg = _select_rows(ygl_ref, ygc_ref, y_lat_tiles)
    pd = jnp.dot(yd, wpd_ref[...], preferred_element_type=F32)
    pg = jnp.dot(yg, wpg_ref[...], preferred_element_type=F32)
    m = jax.nn.sigmoid(gd_ref[...].astype(F32)) * pd + jax.nn.sigmoid(gg_ref[...].astype(F32)) * pg
    o_ref[...] = x + gt_ref[...] * jnp.dot(m.astype(BF16), wo_ref[...], preferred_element_type=F32)


def _mixer_out(xs, lat, ctx, p, mods, wpd, wpg, wo, *, rows, tm, tiles_per_seq, n_batch):
    y_lat_tiles = lat[0].shape[0] // tm
    ctx = lat if ctx is None else ctx
    x_lat, x_ctx, x_lat_tiles = _split_rows(xs, tm)
    row_to_mod = lambda i: jnp.minimum(i // tiles_per_seq, n_batch)
    row_tile = pl.BlockSpec((tm, D_MODEL), lambda i: (i, 0))
    y_lat_spec, y_ctx_spec = _row_specs(tm, y_lat_tiles)
    weight = pl.BlockSpec((D_MODEL, D_MODEL), lambda i: (0, 0))
    return pl.pallas_call(
        functools.partial(_mixer_kernel, x_lat_tiles=x_lat_tiles, y_lat_tiles=y_lat_tiles),
        grid=(rows // tm,),
        in_specs=[
            *_row_specs(tm, x_lat_tiles), y_lat_spec, y_lat_spec, y_ctx_spec, y_ctx_spec,
            pl.BlockSpec((tm, D_MODEL), lambda i: (i, GATE_OFF // D_MODEL)),
            pl.BlockSpec((tm, D_MODEL), lambda i: (i, GATE_OFF // D_MODEL + 1)),
            _mod_spec(2, row_to_mod),
            weight, weight, weight,
        ],
        out_specs=row_tile,
        out_shape=jax.ShapeDtypeStruct((rows, D_MODEL), F32),
        compiler_params=pltpu.CompilerParams(
            dimension_semantics=("arbitrary",), vmem_limit_bytes=VMEM_LIMIT),
        name="mixer_out",
    )(x_lat, x_ctx, lat[0], lat[1], ctx[0], ctx[1], p, p, mods, wpd, wpg, wo)


def _ffn_kernel(x_ref, g_ref, sc_ref, sh_ref, gt_ref, w1_ref, w3_ref, w2_ref, o_ref, h_scr, acc_scr):
    f = pl.program_id(1)

    @pl.when(f == 0)
    def _():
        h_scr[...] = _modulated_norm(x_ref[...], g_ref[...], sc_ref[...], sh_ref[...]).astype(BF16)
        acc_scr[...] = jnp.zeros_like(acc_scr)

    h = h_scr[...]
    a = jnp.dot(h, w1_ref[...], preferred_element_type=F32)
    b = jnp.dot(h, w3_ref[...], preferred_element_type=F32)
    acc_scr[...] += jnp.dot((_silu(a) * b).astype(BF16), w2_ref[...], preferred_element_type=F32)

    @pl.when(f == pl.num_programs(1) - 1)
    def _():
        o_ref[...] = x_ref[...] + gt_ref[...] * acc_scr[...]


def _ffn(xs, mods, norm_g, w1, w3, w2, *, tm, tf, tiles_per_seq, n_batch):
    rows = xs.shape[0]
    d_ff = w1.shape[1]
    row_to_mod = lambda i: jnp.minimum(i // tiles_per_seq, n_batch)
    row_tile = pl.BlockSpec((tm, D_MODEL), lambda i, f: (i, 0))
    return pl.pallas_call(
        _ffn_kernel,
        grid=(rows // tm, d_ff // tf),
        in_specs=[
            row_tile,
            pl.BlockSpec((1, D_MODEL), lambda i, f: (0, 0)),
            _mod_spec(4, row_to_mod), _mod_spec(3, row_to_mod), _mod_spec(5, row_to_mod),
            pl.BlockSpec((D_MODEL, tf), lambda i, f: (0, f)),
            pl.BlockSpec((D_MODEL, tf), lambda i, f: (0, f)),
            pl.BlockSpec((tf, D_MODEL), lambda i, f: (f, 0)),
        ],
        out_specs=row_tile,
        out_shape=jax.ShapeDtypeStruct((rows, D_MODEL), F32),
        scratch_shapes=[pltpu.VMEM((tm, D_MODEL), BF16), pltpu.VMEM((tm, D_MODEL), F32)],
        compiler_params=pltpu.CompilerParams(
            dimension_semantics=("arbitrary", "arbitrary"), vmem_limit_bytes=VMEM_LIMIT),
        name="ffn_swiglu",
    )(xs, norm_g, mods, mods, mods, w1, w3, w2)


def _top2_gates(logits):
    lane = lax.broadcasted_iota(jnp.int32, logits.shape, 1)
    n_lanes = logits.shape[1]
    lg = jnp.where(lane < N_EXPERTS, logits, NEG_BIG)
    m1 = jnp.max(lg, axis=-1, keepdims=True)
    i1 = jnp.min(jnp.where(lg == m1, lane, n_lanes), axis=-1, keepdims=True)
    lg2 = jnp.where(lane == i1, NEG_BIG, lg)
    m2 = jnp.max(lg2, axis=-1, keepdims=True)
    i2 = jnp.min(jnp.where(lg2 == m2, lane, n_lanes), axis=-1, keepdims=True)
    e2 = jnp.exp(m2 - m1)
    w_top = 1.0 / (1.0 + e2)
    idx = jnp.where(lane == 0, i1, jnp.where(lane == 1, i2, 0))
    wts = jnp.where(lane == 0, w_top, jnp.where(lane == 1, e2 * w_top, 0.0))
    return idx, wts


def _router_kernel(x_ref, g_ref, sc_ref, sh_ref, rhi_ref, rlo_ref, h_ref, idx_ref, wts_ref):
    h = _modulated_norm(x_ref[...], g_ref[...], sc_ref[...], sh_ref[...])
    h_ref[...] = h
    h_hi, h_lo = _split_bf16(h)
    logits = jnp.dot(h_hi, rhi_ref[...], preferred_element_type=F32)
    logits += jnp.dot(h_lo, rhi_ref[...], preferred_element_type=F32)
    logits += jnp.dot(h_hi, rlo_ref[...], preferred_element_type=F32)
    idx_ref[...], wts_ref[...] = _top2_gates(logits)


def _router(xs, mods, norm_g, r_hi, r_lo, *, rows, tm, tiles_per_seq, n_batch):
    row_to_mod = lambda i: jnp.minimum(i // tiles_per_seq, n_batch)
    row_tile = pl.BlockSpec((tm, D_MODEL), lambda i: (i, 0))
    lanes = pl.BlockSpec((tm, 128), lambda i: (i, 0))
    router = pl.BlockSpec((D_MODEL, 128), lambda i: (0, 0))
    return pl.pallas_call(
        _router_kernel,
        grid=(rows // tm,),
        in_specs=[row_tile, pl.BlockSpec((1, D_MODEL), lambda i: (0, 0)),
                  _mod_spec(4, row_to_mod), _mod_spec(3, row_to_mod), router, router],
        out_specs=[row_tile, lanes, lanes],
        out_shape=[jax.ShapeDtypeStruct((rows, D_MODEL), F32),
                   jax.ShapeDtypeStruct((rows, 128), jnp.int32),
                   jax.ShapeDtypeStruct((rows, 128), F32)],
        compiler_params=pltpu.CompilerParams(
            dimension_semantics=("arbitrary",), vmem_limit_bytes=VMEM_LIMIT),
        name="moe_router",
    )(xs, norm_g, mods, mods, r_hi, r_lo)


def _route_plan(idx, n_tok, tm_e):
    n_pairs = 2 * n_tok
    n_tiles = n_pairs // tm_e + N_EXPERTS + 1
    n_rows = n_tiles * tm_e
    e_flat = idx.reshape(n_pairs)
    order = jnp.argsort(e_flat, stable=True).astype(jnp.int32)
    counts = jnp.sum((e_flat[:, None] == jnp.arange(N_EXPERTS, dtype=jnp.int32)[None, :]).astype(jnp.int32),
                     axis=0)
    padded = ((counts + tm_e - 1) // tm_e) * tm_e
    ends = jnp.cumsum(padded)
    tile_start = jnp.arange(n_tiles, dtype=jnp.int32) * tm_e
    tile_expert = jnp.minimum(jnp.sum((tile_start[:, None] >= ends[None, :]).astype(jnp.int32), axis=1),
                              N_EXPERTS - 1).astype(jnp.int32)
    n_used = (ends[-1] // tm_e).astype(jnp.int32).reshape(1)
    row = jnp.arange(n_rows, dtype=jnp.int32)
    row_expert = jnp.repeat(tile_expert, tm_e)
    offset = row - (ends - padded)[row_expert]
    valid = (offset < counts[row_expert]) & (row < ends[-1])
    pair = order[jnp.clip((jnp.cumsum(counts) - counts)[row_expert] + offset, 0, n_pairs - 1)]
    tok, choice = pair // 2, pair % 2
    src = jnp.where(valid, tok, 0)
    dump = n_pairs + jnp.cumsum(jnp.logical_not(valid).astype(jnp.int32)) - 1
    dst = jnp.where(valid, choice * n_tok + tok, dump)
    return tile_expert, n_used, src.reshape(n_tiles, 1, tm_e), dst.reshape(n_tiles, 1, tm_e)


def _experts_kernel(te_ref, nu_ref, src_ref, src_next_ref, dst_ref, h_hbm, w1_ref, w3_ref, w2_ref, y_hbm,
                    xbuf, xb_scr, acc_scr, ybuf, gsem, ssem, *, tm_e, n_chunks):
    t = pl.program_id(0)
    f = pl.program_id(1)
    nf = pl.num_programs(1)
    n_used = nu_ref[0]
    slot = t % 2

    def gather_rows(idx_ref, s):
        def body(r, carry):
            pltpu.make_async_copy(h_hbm.at[pl.ds(idx_ref[0, r], 1)], xbuf.at[s, pl.ds(r, 1)], gsem.at[s]).start()
            return carry
        lax.fori_loop(0, tm_e, body, 0, unroll=8)

    def wait_gather(s):
        pltpu.make_async_copy(h_hbm.at[pl.ds(0, tm_e)], xbuf.at[s], gsem.at[s]).wait()

    def scatter_rows(s):
        def body(r, carry):
            pltpu.make_async_copy(ybuf.at[s, pl.ds(r, 1)], y_hbm.at[pl.ds(dst_ref[0, r], 1)], ssem.at[s]).start()
            return carry
        lax.fori_loop(0, tm_e, body, 0, unroll=8)

    def wait_scatter(s):
        pltpu.make_async_copy(ybuf.at[s], y_hbm.at[pl.ds(0, tm_e)], ssem.at[s]).wait()

    @pl.when((f == 0) & (t <= n_used))
    def _():
        @pl.when(t == 0)
        def _():
            gather_rows(src_ref, 0)
        for s in range(2):
            @pl.when(slot == s)
            def _():
                wait_gather(s)

                @pl.when(t < n_used)
                def _():
                    xb_scr[...] = xbuf[s].astype(BF16)
                    acc_scr[...] = jnp.zeros_like(acc_scr)

    @pl.when(t < n_used)
    def _():
        x = xb_scr[...]
        a = jnp.dot(x, w1_ref[...].astype(BF16), preferred_element_type=F32)
        b = jnp.dot(x, w3_ref[...].astype(BF16), preferred_element_type=F32)
        acc_scr[...] += jnp.dot((_silu(a) * b).astype(BF16), w2_ref[...].astype(BF16),
                                preferred_element_type=F32)
        share = tm_e // n_chunks
        for i in range(share):
            r = f * share + i
            pltpu.make_async_copy(h_hbm.at[pl.ds(src_next_ref[0, r], 1)],
                                  xbuf.at[1 - slot, pl.ds(r, 1)], gsem.at[1 - slot]).start()

    @pl.when(f == nf - 1)
    def _():
        for s in range(2):
            @pl.when(slot == s)
            def _():
                @pl.when(t >= 2)
                def _():
                    wait_scatter(s)

                @pl.when(t < n_used)
                def _():
                    ybuf[s] = acc_scr[...]

                @pl.when(t >= n_used)
                def _():
                    ybuf[s] = jnp.zeros(ybuf.shape[1:], F32)

                scatter_rows(s)

                @pl.when(t == pl.num_programs(0) - 1)
                def _():
                    wait_scatter(s)
                    wait_scatter(1 - s)


def _experts(h, plan, w1, w3, w2, *, n_tok, tm_e, tf):
    tile_expert, n_used, src, dst = plan
    n_tiles = src.shape[0]
    d_ff = w1.shape[2]
    nf = d_ff // tf
    n_out = n_tiles * tm_e

    def chunk(t, f, nu):
        return jnp.where(t < nu[0], f, nf - 1)

    smem_tile = lambda shift: pl.BlockSpec(
        (None, 1, tm_e), lambda t, f, te, nu: (jnp.minimum(t + shift, n_tiles - 1), 0, 0),
        memory_space=pltpu.SMEM)
    grid_spec = pltpu.PrefetchScalarGridSpec(
        num_scalar_prefetch=2,
        grid=(n_tiles, nf),
        in_specs=[
            smem_tile(0), smem_tile(1), smem_tile(0),
            pl.BlockSpec(memory_space=pl.ANY),
            pl.BlockSpec((None, D_MODEL, tf), lambda t, f, te, nu: (te[t], 0, chunk(t, f, nu))),
            pl.BlockSpec((None, D_MODEL, tf), lambda t, f, te, nu: (te[t], 0, chunk(t, f, nu))),
            pl.BlockSpec((None, tf, D_MODEL), lambda t, f, te, nu: (te[t], chunk(t, f, nu), 0)),
        ],
        out_specs=pl.BlockSpec(memory_space=pl.ANY),
        scratch_shapes=[
            pltpu.VMEM((2, tm_e, D_MODEL), F32),
            pltpu.VMEM((tm_e, D_MODEL), BF16),
            pltpu.VMEM((tm_e, D_MODEL), F32),
            pltpu.VMEM((2, tm_e, D_MODEL), F32),
            pltpu.SemaphoreType.DMA((2,)),
            pltpu.SemaphoreType.DMA((2,)),
        ],
    )
    return pl.pallas_call(
        functools.partial(_experts_kernel, tm_e=tm_e, n_chunks=nf),
        grid_spec=grid_spec,
        out_shape=jax.ShapeDtypeStruct((n_out, D_MODEL), F32),
        compiler_params=pltpu.CompilerParams(
            dimension_semantics=("arbitrary", "arbitrary"), vmem_limit_bytes=VMEM_LIMIT),
        name="moe_experts",
    )(tile_expert, n_used, src, src, dst, h, w1, w3, w2)


def _combine_kernel(x_ref, y0_ref, y1_ref, wts_ref, gt_ref, fg_ref, o_ref):
    wts = wts_ref[...]
    moe = wts[:, 0:1] * y0_ref[...] + wts[:, 1:2] * y1_ref[...]
    y = x_ref[...] + gt_ref[...] * moe
    ms = jnp.mean(y * y, axis=-1, keepdims=True)
    o_ref[...] = y * lax.rsqrt(ms + EPS) * fg_ref[...]


def _combine(xs, y, wts, mods, final_g, *, rows, tm, tiles_per_seq, n_batch):
    row_to_mod = lambda i: jnp.minimum(i // tiles_per_seq, n_batch)
    row_tile = pl.BlockSpec((tm, D_MODEL), lambda i: (i, 0))
    return pl.pallas_call(
        _combine_kernel,
        grid=(rows // tm,),
        in_specs=[
            row_tile, row_tile,
            pl.BlockSpec((tm, D_MODEL), lambda i: (i + rows // tm, 0)),
            pl.BlockSpec((tm, 128), lambda i: (i, 0)),
            _mod_spec(5, row_to_mod),
            pl.BlockSpec((1, D_MODEL), lambda i: (0, 0)),
        ],
        out_specs=row_tile,
        out_shape=jax.ShapeDtypeStruct((rows, D_MODEL), F32),
        compiler_params=pltpu.CompilerParams(
            dimension_semantics=("arbitrary",), vmem_limit_bytes=VMEM_LIMIT),
        name="moe_combine",
    )(xs, y, y, wts, mods, final_g)


def _moe(xs, mods, norm_g, r_hi, r_lo, final_g, w1, w3, w2, *, rows, tm, tm_e, tf, tiles_per_seq, n_batch):
    h, idx, wts = _router(xs, mods, norm_g, r_hi, r_lo, rows=rows, tm=tm, tiles_per_seq=tiles_per_seq,
                          n_batch=n_batch)
    plan = _route_plan(idx[:, :2], rows, tm_e)
    y = _experts(h, plan, w1, w3, w2, n_tok=rows, tm_e=tm_e, tf=tf)
    return _combine(xs, y, wts, mods, final_g, rows=rows, tm=tm, tiles_per_seq=tiles_per_seq, n_batch=n_batch)


def _deinterleave(n=HEAD_DIM):
    return np.concatenate([np.arange(0, n, 2), np.arange(1, n, 2)])


def _proj_columns():
    de = _deinterleave()
    o_dq, o_gq, o_dk, o_dv, o_gk, o_gv, o_gate = 0, 1024, 2048, 3072, 4096, 4352, 4608
    cols = np.zeros(IN_W, np.int32)
    scale = np.ones(IN_W, np.float32)
    cols[GATE_OFF:GATE_OFF + 2048] = o_gate + np.arange(2048)
    for h in range(N_DIFF_HEADS):
        for c in range(2):
            dst = 128 * h + 64 * c
            cols[DQ_OFF + dst:DQ_OFF + dst + 64] = o_dq + dst + de
            cols[DK_OFF + dst:DK_OFF + dst + 64] = o_dk + dst + de
    scale[DQ_OFF:DQ_OFF + 1024] = HEAD_DIM ** -0.5
    for j in range(N_GQA_HEADS):
        cols[GQ_OFF + 64 * j:GQ_OFF + 64 * j + 64] = o_gq + 64 * j + de
    cols[DV_OFF:DV_OFF + 1024] = o_dv + np.arange(1024)
    for g in range(N_GQA_KV):
        cols[GKV_OFF + 128 * g:GKV_OFF + 128 * g + 64] = o_gk + 64 * g + de
        cols[GKV_OFF + 128 * g + 64:GKV_OFF + 128 * g + 128] = o_gv + 64 * g + np.arange(64)
    return cols, scale


def _rope_tables(S, pad_rows):
    rows = S // GRID_W
    row = jnp.repeat(jnp.arange(rows, dtype=F32), GRID_W)
    col = jnp.tile(jnp.arange(GRID_W, dtype=F32), rows)
    half = HEAD_DIM // 2
    inv_freq = ROPE_THETA ** (-jnp.arange(0, half, 2, dtype=F32) / half)
    ang = jnp.concatenate([row[:, None] * inv_freq, col[:, None] * inv_freq], axis=-1)
    cos, sin = jnp.cos(ang), jnp.sin(ang)
    reps = PROJ_TN // HEAD_DIM
    cos_t = jnp.tile(jnp.concatenate([cos, cos], axis=-1), (1, reps))
    sin_t = jnp.tile(jnp.concatenate([-sin, sin], axis=-1), (1, reps))
    cos_t = jnp.concatenate([cos_t, jnp.ones((pad_rows, PROJ_TN), F32)], axis=0)
    sin_t = jnp.concatenate([sin_t, jnp.zeros((pad_rows, PROJ_TN), F32)], axis=0)
    return cos_t, sin_t


def kernel(x, c, ctx, c_ctx, ada_w, ada_b, norm_attn_g, norm_ffn_g, w_in, q_norm_g, k_norm_g, diff_lambda,
           diff_subln_g, w_proj_diff, w_proj_gqa, w_out, ffn_w1, ffn_w3, ffn_w2, moe_router, moe_w1, moe_w3,
           moe_w2, final_norm_g):
    B, S, D = x.shape
    C = ctx.shape[1]
    depth = ada_w.shape[0]
    assert D == D_MODEL and depth == 2 and B + 1 <= MOD_ROWS
    assert w_in.shape[2] == IN_W and moe_router.shape[2] == N_EXPERTS
    tiles = _pick_tiles(B, S, C, ffn_w1.shape[2], moe_w1.shape[3])
    n_lat = B * S

    cvec = jnp.zeros((MOD_ROWS, D), F32).at[:B].set(c).at[B].set(c_ctx)
    mods_all = _ada_mods(cvec, ada_w, ada_b).reshape(depth, MOD_ROWS * N_MODS, 1, D)

    cols, col_scale = _proj_columns()
    de = _deinterleave()
    cos_t, sin_t = _rope_tables(S, tiles.tm_proj)
    blk = np.arange(PROJ_TN) // HEAD_DIM
    ones_bd = jnp.asarray(blk[:, None] == blk[None, :], BF16)
    col = np.arange(PROJ_TN)
    swap_bd = jnp.asarray(col[:, None] == (col[None, :] ^ (HEAD_DIM // 2)), BF16)
    is_key_lane = (np.arange(PROJ_TN) % (2 * HEAD_DIM)) < HEAD_DIM

    xs = (x.reshape(n_lat, D), ctx.reshape(B * C, D))
    for l in range(depth):
        last = l == depth - 1
        mods = mods_all[l]
        lam_init = 0.8 - 0.6 * math.exp(-0.3 * l)
        lq1, lk1, lq2, lk2 = diff_lambda[l]
        lam = (jnp.exp(jnp.sum(lq1 * lk1)) - jnp.exp(jnp.sum(lq2 * lk2)) + lam_init).reshape(1).astype(F32)
        w = (jnp.take(w_in[l], cols, axis=1) * col_scale).astype(BF16)
        qg = jnp.tile(q_norm_g[l][de] * HEAD_DIM ** -0.5, PROJ_TN // HEAD_DIM).reshape(1, PROJ_TN)
        kg = jnp.where(is_key_lane, jnp.tile(k_norm_g[l][de], PROJ_TN // HEAD_DIM), 1.0).reshape(1, PROJ_TN)
        subln = (diff_subln_g[l] * (1.0 - lam_init)).reshape(1, 2 * HEAD_DIM)

        p = _project(xs, mods, norm_attn_g[l].reshape(1, D), w, cos_t, sin_t, ones_bd, swap_bd, qg, kg,
                     tm=tiles.tm_proj, lat_tiles=n_lat // tiles.tm_proj, tiles_per_seq=S // tiles.tm_proj,
                     n_batch=B)
        y_lat, y_ctx = _attention(p, lam, subln, B=B, S=S, C=C, tiles=tiles, ctx_queries=not last)
        rows = n_lat if last else n_lat + B * C
        xs = _mixer_out(xs, y_lat, y_ctx, p, mods, w_proj_diff[l].astype(BF16), w_proj_gqa[l].astype(BF16),
                        w_out[l].astype(BF16), rows=rows, tm=tiles.tm_mix, tiles_per_seq=S // tiles.tm_mix,
                        n_batch=B)
        i = l // 2
        if l % 2 == 0:
            xs = _ffn(xs, mods, norm_ffn_g[l].reshape(1, D), ffn_w1[i].astype(BF16), ffn_w3[i].astype(BF16),
                      ffn_w2[i].astype(BF16), tm=tiles.tm_ffn, tf=tiles.tf_ffn,
                      tiles_per_seq=S // tiles.tm_ffn, n_batch=B)
        else:
            r_pad = jnp.zeros((D, 128), F32).at[:, :N_EXPERTS].set(moe_router[i])
            r_hi = r_pad.astype(BF16)
            r_lo = (r_pad - r_hi.astype(F32)).astype(BF16)
            xs = _moe(xs, mods, norm_ffn_g[l].reshape(1, D), r_hi, r_lo, final_norm_g.reshape(1, D),
                      moe_w1[i], moe_w3[i], moe_w2[i],
                      rows=rows, tm=tiles.tm_mix, tm_e=tiles.tm_moe, tf=tiles.tf_moe,
                      tiles_per_seq=S // tiles.tm_mix, n_batch=B)
    return xs.reshape(B, S, D)
```

```python
import functools
import math
from typing import NamedTuple

import numpy as np
import jax
import jax.numpy as jnp
from jax import lax
from jax.experimental import pallas as pl
from jax.experimental.pallas import tpu as pltpu

F32 = jnp.float32
BF16 = jnp.bfloat16

D_MODEL = 1024
HEAD_DIM = 64
N_DIFF_HEADS = 8
N_GQA_HEADS = 16
N_GQA_KV = 4
GQA_REP = N_GQA_HEADS // N_GQA_KV
N_EXPERTS = 8
GRID_W = 64
ROPE_THETA = 10000.0
EPS = 1e-6
N_MODS = 6
MOD_ROWS = 16
NEG_BIG = -1e30

GATE_OFF = 0
DQ_OFF = 2048
GQ_OFF = 3072
DK_OFF = 4096
DV_OFF = 5120
GKV_OFF = 6144
IN_W = 6656
PROJ_TN = 512
ROPE_TILES = (4, 5, 8, 9)
QNORM_TILES = (6, 7)
KV_TILE = 12

VMEM_LIMIT = 52 * 1024 * 1024


class Tiles(NamedTuple):
    tm_proj: int
    tq: int
    tk: int
    tm_mix: int
    tm_ffn: int
    tf_ffn: int
    tf_moe: int
    tm_moe: int


def _largest_divisor(n, candidates):
    for c in candidates:
        if n % c == 0:
            return c
    raise ValueError(f"no tile in {candidates} divides {n}")


def _pick_tiles(B, S, C, d_ff, d_ff_e):
    rows_common = math.gcd(S, B * C)
    tm = _largest_divisor(rows_common, (512, 256, 128))
    return Tiles(
        tm_proj=_largest_divisor(rows_common, (1024, 512, 256, 128)),
        tq=_largest_divisor(S, (256, 128)),
        tk=_largest_divisor(math.gcd(S, C), (256, 128)),
        tm_mix=tm,
        tm_ffn=_largest_divisor(rows_common, (1024, 512, 256, 128)),
        tf_ffn=_largest_divisor(d_ff, (1408, 1024, 512, 256, 128)),
        tf_moe=_largest_divisor(d_ff_e, (896, 512, 256, 128)),
        tm_moe=_largest_divisor(2 * B * S, (512, 256)),
    )


def _split_bf16(v):
    hi = v.astype(BF16)
    lo = (v - hi.astype(F32)).astype(BF16)
    return hi, lo


def _ada_kernel(c_ref, w_ref, b_ref, o_ref):
    c = c_ref[...]
    s = c / (1.0 + jnp.exp(-c))
    s_hi, s_lo = _split_bf16(s)
    w_hi, w_lo = _split_bf16(w_ref[...])
    acc = jnp.dot(s_hi, w_hi, preferred_element_type=F32)
    acc += jnp.dot(s_lo, w_hi, preferred_element_type=F32)
    acc += jnp.dot(s_hi, w_lo, preferred_element_type=F32)
    o_ref[...] = acc + b_ref[...]


def _ada_mods(cvec, ada_w, ada_b):
    depth, d, n = ada_w.shape
    tn = _largest_divisor(n, (1536, 1024, 512))
    return pl.pallas_call(
        _ada_kernel,
        grid=(depth, n // tn),
        in_specs=[
            pl.BlockSpec((MOD_ROWS, d), lambda l, j: (0, 0)),
            pl.BlockSpec((None, d, tn), lambda l, j: (l, 0, j)),
            pl.BlockSpec((None, 1, tn), lambda l, j: (l, 0, j)),
        ],
        out_specs=pl.BlockSpec((None, MOD_ROWS, tn), lambda l, j: (l, 0, j)),
        out_shape=jax.ShapeDtypeStruct((depth, MOD_ROWS, n), F32),
        compiler_params=pltpu.CompilerParams(
            dimension_semantics=("arbitrary", "arbitrary"), vmem_limit_bytes=VMEM_LIMIT),
        name="ada_mods",
    )(cvec, ada_w, ada_b.reshape(depth, 1, n))


def _modulated_norm(x, g, scale, shift):
    ms = jnp.mean(x * x, axis=-1, keepdims=True)
    return (x * lax.rsqrt(ms + EPS) * g) * (1.0 + scale) + shift


def _silu(a):
    return a / (1.0 + jnp.exp(-a))


def _mod_spec(k, row_to_mod):
    return pl.BlockSpec((None, 1, D_MODEL), lambda i, *_: (row_to_mod(i) * N_MODS + k, 0, 0))


def _rope(z, cos, sin_signed, swap_ref):
    partner = jnp.dot(z.astype(BF16), swap_ref[...], preferred_element_type=F32)
    return z * cos + partner * sin_signed


def _head_rms(z, ones_ref, gain):
    ss = jnp.dot((z * z).astype(BF16), ones_ref[...], preferred_element_type=F32)
    return z * lax.rsqrt(ss * (1.0 / HEAD_DIM) + EPS) * gain


def _split_rows(xs, tm):
    if isinstance(xs, tuple):
        return xs[0], xs[1], xs[0].shape[0] // tm
    return xs, xs, xs.shape[0] // tm


def _row_specs(tm, lat_tiles):
    return (pl.BlockSpec((tm, D_MODEL), lambda i, *_: (jnp.minimum(i, lat_tiles - 1), 0)),
            pl.BlockSpec((tm, D_MODEL), lambda i, *_: (jnp.maximum(i - lat_tiles, 0), 0)))


def _select_rows(lat_ref, ctx_ref, lat_tiles):
    return jnp.where(pl.program_id(0) >= lat_tiles, ctx_ref[...], lat_ref[...])


def _proj_kernel(xl_ref, xc_ref, g_ref, sc_ref, sh_ref, w_ref, cos_ref, sin_ref, ones_ref, swap_ref, qg_ref, kg_ref,
                 o_ref, h_scr, *, lat_tiles):
    j = pl.program_id(1)

    @pl.when(j == 0)
    def _():
        x = _select_rows(xl_ref, xc_ref, lat_tiles)
        h_scr[...] = _modulated_norm(x, g_ref[...], sc_ref[...], sh_ref[...]).astype(BF16)

    def project():
        z = jnp.dot(h_scr[...], w_ref[...], preferred_element_type=F32)
        return z, lax.broadcasted_iota(jnp.int32, z.shape, 1)

    is_rope = functools.reduce(jnp.logical_or, [j == t for t in ROPE_TILES])
    is_qnorm = functools.reduce(jnp.logical_or, [j == t for t in QNORM_TILES])
    is_kv = j == KV_TILE
    is_plain = jnp.logical_not(is_rope | is_qnorm | is_kv)

    @pl.when(is_plain)
    def _():
        z, _ = project()
        o_ref[...] = z.astype(BF16)

    @pl.when(is_rope)
    def _():
        z, lane = project()
        o_ref[...] = _rope(z, cos_ref[...], sin_ref[...], swap_ref).astype(BF16)

    @pl.when(is_qnorm)
    def _():
        z, lane = project()
        zn = _head_rms(z, ones_ref, qg_ref[...])
        o_ref[...] = _rope(zn, cos_ref[...], sin_ref[...], swap_ref).astype(BF16)

    @pl.when(is_kv)
    def _():
        z, lane = project()
        zn = _head_rms(z, ones_ref, kg_ref[...])
        zr = _rope(zn, cos_ref[...], sin_ref[...], swap_ref)
        is_key_lane = (lane & HEAD_DIM) == 0
        o_ref[...] = jnp.where(is_key_lane, zr, z).astype(BF16)


def _project(xs, mods, norm_g, w, cos_t, sin_t, ones_bd, swap_bd, qg, kg, *, tm, lat_tiles, tiles_per_seq, n_batch):
    x_lat, x_ctx, src_tiles = _split_rows(xs, tm)
    rows = x_lat.shape[0] + (x_ctx.shape[0] if isinstance(xs, tuple) else 0)
    row_to_mod = lambda i: jnp.minimum(i // tiles_per_seq, n_batch)
    rope_row = lambda i: jnp.where(i < lat_tiles, i % tiles_per_seq, tiles_per_seq)
    const = lambda i, j: (0, 0)
    return pl.pallas_call(
        functools.partial(_proj_kernel, lat_tiles=src_tiles),
        grid=(rows // tm, IN_W // PROJ_TN),
        in_specs=[
            *_row_specs(tm, src_tiles),
            pl.BlockSpec((1, D_MODEL), const),
            _mod_spec(1, row_to_mod),
            _mod_spec(0, row_to_mod),
            pl.BlockSpec((D_MODEL, PROJ_TN), lambda i, j: (0, j)),
            pl.BlockSpec((tm, PROJ_TN), lambda i, j: (rope_row(i), 0)),
            pl.BlockSpec((tm, PROJ_TN), lambda i, j: (rope_row(i), 0)),
            pl.BlockSpec((PROJ_TN, PROJ_TN), const),
            pl.BlockSpec((PROJ_TN, PROJ_TN), const),
            pl.BlockSpec((1, PROJ_TN), const),
            pl.BlockSpec((1, PROJ_TN), const),
        ],
        out_specs=pl.BlockSpec((tm, PROJ_TN), lambda i, j: (i, j)),
        out_shape=jax.ShapeDtypeStruct((rows, IN_W), BF16),
        scratch_shapes=[pltpu.VMEM((tm, D_MODEL), BF16)],
        compiler_params=pltpu.CompilerParams(
            dimension_semantics=("arbitrary", "arbitrary"), vmem_limit_bytes=VMEM_LIMIT),
        name="in_proj",
    )(x_lat, x_ctx, norm_g, mods, mods, w, cos_t, sin_t, ones_bd, swap_bd, qg, kg)


STAB_LANE = HEAD_DIM
L_FLOOR = 1e-26
L_CEIL = 1e30
FP8 = jnp.float8_e4m3fn
FP8_TARGET = 256.0
FP8_BOUND_SLACK = 1.125
LOG2_E = 1.4426950408889634
DIFF_VT_ROWS = 2 * HEAD_DIM + 16
DIFF_HEADS_PER_STEP = 2
GQA_GROUPS_PER_STEP = 1


def _aug_keys(k):
    lane = lax.broadcasted_iota(jnp.int32, k.shape, 1)
    in_key = lane < HEAD_DIM
    sq = jnp.where(in_key, k * k, 0.0).astype(BF16)
    norms = jnp.dot(sq, jnp.ones((k.shape[1], k.shape[1]), BF16), preferred_element_type=F32)
    ka = jnp.where(in_key, k, jnp.where(lane == STAB_LANE, 1.0, 0.0)).astype(BF16)
    return ka, jnp.max(norms, axis=0, keepdims=True)


def _pow2_floor(x):
    bits = lax.bitcast_convert_type(x, jnp.int32) & jnp.int32(0x7F800000)
    return lax.bitcast_convert_type(bits, F32)


def _fp8_keys(ka_scr, k8_scr, kscale_scr, kmax, *, n_chunks, tk):
    for r, kmax2 in enumerate(kmax):
        scale = _pow2_floor(FP8_TARGET * lax.rsqrt(jnp.maximum(kmax2, 1e-30)))
        kscale_scr[r] = scale
        for c in range(n_chunks):
            k = ka_scr[r, c * tk:(c + 1) * tk, :].astype(F32)
            lane = lax.broadcasted_iota(jnp.int32, k.shape, 1)
            k8_scr[r, c * tk:(c + 1) * tk, :] = jnp.where(lane == STAB_LANE, FP8_TARGET, k * scale).astype(FP8)


def _query_mats(slab, kmax2, kscale):
    lane = lax.broadcasted_iota(jnp.int32, slab.shape, 1)
    in_key = lane < HEAD_DIM
    sq = jnp.where(in_key, slab * slab, 0.0).astype(BF16)
    norms = jnp.dot(sq, jnp.ones((slab.shape[1], slab.shape[1]), BF16), preferred_element_type=F32)
    nk = norms * kmax2
    bound = FP8_BOUND_SLACK * nk * lax.rsqrt(nk + 1e-30)
    qscale = _pow2_floor(FP8_TARGET * lax.rsqrt(jnp.maximum(jnp.max(norms, axis=0, keepdims=True), 1e-30)))
    both = kscale * qscale
    plain = jnp.where(in_key, slab, 0.0)
    shifted = jnp.where(lane == STAB_LANE, -bound * (both * (1.0 / FP8_TARGET)), plain * qscale).astype(FP8)
    return shifted, plain.astype(BF16), (LOG2_E / both)[:, :1]


def _attend(k8_scr, vt_scr, mats, acc_scr, *, first, n_chunks, tk):
    def scores(r, c, qmat):
        off = pl.multiple_of(c * tk, tk)
        return lax.dot_general(k8_scr[r * k8_scr.shape[0] // len(mats), pl.ds(off, tk), :], qmat,
                               (((1,), (1,)), ((), ())), preferred_element_type=F32)

    acc_scr[...] = jnp.zeros(acc_scr.shape, F32)

    def all_scores(c):
        return tuple(scores(r, c, shifted) for r, (shifted, _, _) in enumerate(mats))

    def accumulate(c, s_all):
        for r, s in enumerate(s_all):
            vt = vt_scr[r * vt_scr.shape[0] // len(mats), c]
            p = jnp.exp2((s * mats[r][2]).astype(BF16))
            acc_scr[r] += jnp.dot(vt, p, preferred_element_type=F32)

    def fast_body(c, s_cur):
        s_next = all_scores(c + 1)
        accumulate(c, s_cur)
        return s_next

    trips = n_chunks - 1 - first
    s_last = lax.fori_loop(first, n_chunks - 1, fast_body, all_scores(first),
                           unroll=_largest_divisor(trips, (16, 8, 4, 2, 1)) if trips > 0 else 1)
    accumulate(n_chunks - 1, s_last)


def _attend_fallback(ka_scr, vt_scr, mats, m_scr, acc_scr, *, first, n_chunks, tk):
    m_scr[...] = jnp.full(m_scr.shape, NEG_BIG, F32)
    acc_scr[...] = jnp.zeros(acc_scr.shape, F32)

    def body(c, carry):
        off = pl.multiple_of(c * tk, tk)
        for r, (_, plain, _) in enumerate(mats):
            s = lax.dot_general(ka_scr[r * ka_scr.shape[0] // len(mats), pl.ds(off, tk), :], plain, (((1,), (1,)), ((), ())),
                                preferred_element_type=F32)
            m_old = m_scr[r]
            m_new = jnp.maximum(m_old, jnp.max(s, axis=0, keepdims=True))
            p = jnp.exp(s - m_new).astype(BF16)
            vt = vt_scr[r * vt_scr.shape[0] // len(mats), c]
            acc_scr[r] = acc_scr[r] * jnp.exp(m_old - m_new) + jnp.dot(vt, p, preferred_element_type=F32)
            m_scr[r] = m_new
        return carry

    lax.fori_loop(first, n_chunks, body, 0)


def _softmax_tile(ka_scr, k8_scr, vt_scr, kmax_scr, kscale_scr, slabs, m_scr, acc_scr, l_row,
                  *, n_chunks, tk):
    per_key = lambda scr, r: scr[r * scr.shape[0] // len(slabs)]
    mats = [_query_mats(slab, per_key(kmax_scr, r), per_key(kscale_scr, r)) for r, slab in enumerate(slabs)]
    _attend(k8_scr, vt_scr, mats, acc_scr, first=0, n_chunks=n_chunks, tk=tk)
    dens = [acc_scr[r, l_row:l_row + 1, :] for r in range(len(slabs))]
    l_min = functools.reduce(jnp.minimum, [jnp.min(d) for d in dens])
    l_max = functools.reduce(jnp.maximum, [jnp.max(d) for d in dens])

    @pl.when(jnp.logical_not((l_min >= L_FLOOR) & (l_max <= L_CEIL)))
    def _():
        _attend_fallback(ka_scr, vt_scr, mats, m_scr, acc_scr, first=0, n_chunks=n_chunks, tk=tk)


def _chunk_rows(c, n_lat_chunks, tk, lat_ref, ctx_ref):
    if c < n_lat_chunks:
        return lat_ref[c * tk:(c + 1) * tk, :]
    return ctx_ref[(c - n_lat_chunks) * tk:(c - n_lat_chunks + 1) * tk, :]


def _diff_attn_kernel(lam_ref, q_ref, kl_ref, vl_ref, kc_ref, vc_ref, g_ref, o_ref,
                      ka_scr, k8_scr, vt_scr, kmax_scr, kscale_scr, m_scr, acc_scr,
                      *, n_lat_chunks, n_chunks, tk):
    dv = 2 * HEAD_DIM
    heads = [slice(a * dv, (a + 1) * dv) for a in range(DIFF_HEADS_PER_STEP)]

    @pl.when(pl.program_id(2) == 0)
    def _():
        kmax = [None] * (2 * len(heads))
        for c in range(n_chunks):
            k_all = _chunk_rows(c, n_lat_chunks, tk, kl_ref, kc_ref).astype(F32)
            v_all = _chunk_rows(c, n_lat_chunks, tk, vl_ref, vc_ref)
            for a, cols in enumerate(heads):
                k = k_all[:, cols]
                for r, keys in ((2 * a, k), (2 * a + 1, pltpu.roll(k, HEAD_DIM, 1))):
                    ka, n2 = _aug_keys(keys)
                    ka_scr[r, c * tk:(c + 1) * tk, :] = ka
                    kmax[r] = n2 if kmax[r] is None else jnp.maximum(kmax[r], n2)
                vt_scr[a, c, :dv, :] = v_all[:, cols].T
                vt_scr[a, c, dv:, :] = jnp.ones((DIFF_VT_ROWS - dv, tk), BF16)
        for r, n2 in enumerate(kmax):
            kmax_scr[r] = n2
        _fp8_keys(ka_scr, k8_scr, kscale_scr, kmax, n_chunks=n_chunks, tk=tk)

    q_all = q_ref[...].astype(F32)
    slabs = []
    for cols in heads:
        slabs += [q_all[:, cols], pltpu.roll(q_all[:, cols], HEAD_DIM, 1)]
    _softmax_tile(ka_scr, k8_scr, vt_scr, kmax_scr, kscale_scr, slabs, m_scr, acc_scr, dv,
                  n_chunks=n_chunks, tk=tk)
    for a, cols in enumerate(heads):
        a1, a2 = acc_scr[2 * a], acc_scr[2 * a + 1]
        yt = a1[:dv] * (1.0 / a1[dv:dv + 1]) - lam_ref[0] * (a2[:dv] * (1.0 / a2[dv:dv + 1]))
        y = yt.T
        ms = jnp.mean(y * y, axis=-1, keepdims=True)
        o_ref[:, cols] = (y * lax.rsqrt(ms + EPS) * g_ref[...]).astype(BF16)


def _gqa_attn_kernel(q_ref, kvl_ref, kvc_ref, o_ref, ka_scr, k8_scr, vt_scr, kmax_scr, kscale_scr, m_scr, acc_scr,
                     *, n_lat_chunks, n_chunks, tk):
    w2 = 2 * HEAD_DIM
    groups = [slice(g * w2, (g + 1) * w2) for g in range(GQA_GROUPS_PER_STEP)]
    n_heads = GQA_REP * len(groups)

    @pl.when(pl.program_id(2) == 0)
    def _():
        kmax = [None] * len(groups)
        for c in range(n_chunks):
            kv_all = _chunk_rows(c, n_lat_chunks, tk, kvl_ref, kvc_ref)
            for g, cols in enumerate(groups):
                kv = kv_all[:, cols]
                ka, n2 = _aug_keys(kv.astype(F32))
                ka_scr[g, c * tk:(c + 1) * tk, :] = ka
                kmax[g] = n2 if kmax[g] is None else jnp.maximum(kmax[g], n2)
                t = kv.T
                row = lax.broadcasted_iota(jnp.int32, t.shape, 0)
                vt_scr[g, c] = jnp.where(row < HEAD_DIM, jnp.ones_like(t), t)
        for g, n2 in enumerate(kmax):
            kmax_scr[g] = n2
        _fp8_keys(ka_scr, k8_scr, kscale_scr, kmax, n_chunks=n_chunks, tk=tk)

    qf = q_ref[...].astype(F32)
    tq = qf.shape[0]
    lane = lax.broadcasted_iota(jnp.int32, (tq, w2), 1)
    slabs = []
    for r in range(n_heads):
        slab = qf[:, (r // 2) * w2:(r // 2 + 1) * w2]
        slabs.append(pltpu.roll(slab, HEAD_DIM, 1) if r % 2 else slab)
    _softmax_tile(ka_scr, k8_scr, vt_scr, kmax_scr, kscale_scr, slabs, m_scr, acc_scr, 0,
                  n_chunks=n_chunks, tk=tk)

    def head_out(r):
        acc = acc_scr[r]
        return (acc * (1.0 / acc[0:1])).T

    for j in range(n_heads // 2):
        pair = jnp.where(lane < HEAD_DIM, pltpu.roll(head_out(2 * j), HEAD_DIM, 1), head_out(2 * j + 1))
        o_ref[:, j * w2:(j + 1) * w2] = pair.astype(BF16)


def _attention(p, lam, subln_gain, *, B, S, C, tiles, ctx_queries):
    lat = _attention_calls(p, lam, subln_gain, B=B, S=S, C=C, tq=tiles.tq, tk=tiles.tk, latent=True)
    if not ctx_queries:
        return lat, None
    return lat, _attention_calls(p, lam, subln_gain, B=B, S=S, C=C, tq=C, tk=tiles.tk, latent=False)


def _attention_calls(p, lam, subln_gain, *, B, S, C, tq, tk, latent):
    ctx_blk0 = (B * S) // C
    if latent:
        nq, rows_out, s_kv, n_lat_chunks = S // tq, B * S, S, S // tk
        q_row = out_row = lambda b, qi: b * nq + qi
        lat_blk = lambda b: b
    else:
        nq, rows_out, s_kv, n_lat_chunks = 1, B * C, C, 0
        q_row = lambda b, qi: ctx_blk0 + b
        out_row = lambda b, qi: b
        lat_blk = lambda b: ctx_blk0 + b
    T = n_lat_chunks * tk + C
    S = s_kv
    statics = dict(n_lat_chunks=n_lat_chunks, n_chunks=T // tk, tk=tk)

    cparams = pltpu.CompilerParams(
        dimension_semantics=("arbitrary", "arbitrary", "arbitrary"), vmem_limit_bytes=VMEM_LIMIT)
    w2 = 2 * HEAD_DIM
    hps = DIFF_HEADS_PER_STEP
    wd = hps * w2
    yd = pl.pallas_call(
        functools.partial(_diff_attn_kernel, **statics),
        grid=(B, N_DIFF_HEADS // hps, nq),
        in_specs=[
            pl.BlockSpec(memory_space=pltpu.SMEM),
            pl.BlockSpec((tq, wd), lambda b, h, qi: (q_row(b, qi), DQ_OFF // wd + h)),
            pl.BlockSpec((S, wd), lambda b, h, qi: (lat_blk(b), DK_OFF // wd + h)),
            pl.BlockSpec((S, wd), lambda b, h, qi: (lat_blk(b), DV_OFF // wd + h)),
            pl.BlockSpec((C, wd), lambda b, h, qi: (ctx_blk0 + b, DK_OFF // wd + h)),
            pl.BlockSpec((C, wd), lambda b, h, qi: (ctx_blk0 + b, DV_OFF // wd + h)),
            pl.BlockSpec((1, w2), lambda b, h, qi: (0, 0)),
        ],
        out_specs=pl.BlockSpec((tq, wd), lambda b, h, qi: (out_row(b, qi), h)),
        out_shape=jax.ShapeDtypeStruct((rows_out, N_DIFF_HEADS * w2), BF16),
        scratch_shapes=[
            pltpu.VMEM((2 * hps, T, w2), BF16),
            pltpu.VMEM((2 * hps, T, w2), FP8),
            pltpu.VMEM((hps, T // tk, DIFF_VT_ROWS, tk), BF16),
            pltpu.VMEM((2 * hps, 1, w2), F32),
            pltpu.VMEM((2 * hps, 1, w2), F32),
            pltpu.VMEM((2 * hps, 1, tq), F32),
            pltpu.VMEM((2 * hps, DIFF_VT_ROWS, tq), F32),
        ],
        compiler_params=cparams,
        name="diff_attn" if latent else "diff_attn_ctx",
    )(lam, p, p, p, p, p, subln_gain)

    gps = GQA_GROUPS_PER_STEP
    wq = gps * GQA_REP * HEAD_DIM
    wkv = gps * w2
    yg = pl.pallas_call(
        functools.partial(_gqa_attn_kernel, **statics),
        grid=(B, N_GQA_KV // gps, nq),
        in_specs=[
            pl.BlockSpec((tq, wq), lambda b, g, qi: (q_row(b, qi), GQ_OFF // wq + g)),
            pl.BlockSpec((S, wkv), lambda b, g, qi: (lat_blk(b), GKV_OFF // wkv + g)),
            pl.BlockSpec((C, wkv), lambda b, g, qi: (ctx_blk0 + b, GKV_OFF // wkv + g)),
        ],
        out_specs=pl.BlockSpec((tq, wq), lambda b, g, qi: (out_row(b, qi), g)),
        out_shape=jax.ShapeDtypeStruct((rows_out, N_GQA_HEADS * HEAD_DIM), BF16),
        scratch_shapes=[
            pltpu.VMEM((gps, T, w2), BF16),
            pltpu.VMEM((gps, T, w2), FP8),
            pltpu.VMEM((gps, T // tk, w2, tk), BF16),
            pltpu.VMEM((gps, 1, w2), F32),
            pltpu.VMEM((gps, 1, w2), F32),
            pltpu.VMEM((gps * GQA_REP, 1, tq), F32),
            pltpu.VMEM((gps * GQA_REP, w2, tq), F32),
        ],
        compiler_params=cparams,
        name="gqa_attn" if latent else "gqa_attn_ctx",
    )(p, p, p)
    return yd, yg


def _mixer_kernel(xl_ref, xc_ref, ydl_ref, ygl_ref, ydc_ref, ygc_ref, gd_ref, gg_ref, gt_ref, wpd_ref, wpg_ref,
                  wo_ref, o_ref, *, x_lat_tiles, y_lat_tiles):
    x = _select_rows(xl_ref, xc_ref, x_lat_tiles)
    yd = _select_rows(ydl_ref, ydc_ref, y_lat_tiles)
    yg = _select_rows(ygl_ref, ygc_ref, y_lat_tiles)
    pd = jnp.dot(yd, wpd_ref[...], preferred_element_type=F32)
    pg = jnp.dot(yg, wpg_ref[...], preferred_element_type=F32)
    m = jax.nn.sigmoid(gd_ref[...].astype(F32)) * pd + jax.nn.sigmoid(gg_ref[...].astype(F32)) * pg
    o_ref[...] = x + gt_ref[...] * jnp.dot(m.astype(BF16), wo_ref[...], preferred_element_type=F32)


def _mixer_out(xs, lat, ctx, p, mods, wpd, wpg, wo, *, rows, tm, tiles_per_seq, n_batch):
    y_lat_tiles = lat[0].shape[0] // tm
    ctx = lat if ctx is None else ctx
    x_lat, x_ctx, x_lat_tiles = _split_rows(xs, tm)
    row_to_mod = lambda i: jnp.minimum(i // tiles_per_seq, n_batch)
    row_tile = pl.BlockSpec((tm, D_MODEL), lambda i: (i, 0))
    y_lat_spec, y_ctx_spec = _row_specs(tm, y_lat_tiles)
    weight = pl.BlockSpec((D_MODEL, D_MODEL), lambda i: (0, 0))
    return pl.pallas_call(
        functools.partial(_mixer_kernel, x_lat_tiles=x_lat_tiles, y_lat_tiles=y_lat_tiles),
        grid=(rows // tm,),
        in_specs=[
            *_row_specs(tm, x_lat_tiles), y_lat_spec, y_lat_spec, y_ctx_spec, y_ctx_spec,
            pl.BlockSpec((tm, D_MODEL), lambda i: (i, GATE_OFF // D_MODEL)),
            pl.BlockSpec((tm, D_MODEL), lambda i: (i, GATE_OFF // D_MODEL + 1)),
            _mod_spec(2, row_to_mod),
            weight, weight, weight,
        ],
        out_specs=row_tile,
        out_shape=jax.ShapeDtypeStruct((rows, D_MODEL), F32),
        compiler_params=pltpu.CompilerParams(
            dimension_semantics=("arbitrary",), vmem_limit_bytes=VMEM_LIMIT),
        name="mixer_out",
    )(x_lat, x_ctx, lat[0], lat[1], ctx[0], ctx[1], p, p, mods, wpd, wpg, wo)


def _ffn_kernel(x_ref, g_ref, sc_ref, sh_ref, gt_ref, w1_ref, w3_ref, w2_ref, o_ref, h_scr, acc_scr):
    f = pl.program_id(1)

    @pl.when(f == 0)
    def _():
        h_scr[...] = _modulated_norm(x_ref[...], g_ref[...], sc_ref[...], sh_ref[...]).astype(BF16)
        acc_scr[...] = jnp.zeros_like(acc_scr)

    h = h_scr[...]
    a = jnp.dot(h, w1_ref[...], preferred_element_type=F32)
    b = jnp.dot(h, w3_ref[...], preferred_element_type=F32)
    acc_scr[...] += jnp.dot((_silu(a) * b).astype(BF16), w2_ref[...], preferred_element_type=F32)

    @pl.when(f == pl.num_programs(1) - 1)
    def _():
        o_ref[...] = x_ref[...] + gt_ref[...] * acc_scr[...]


def _ffn(xs, mods, norm_g, w1, w3, w2, *, tm, tf, tiles_per_seq, n_batch):
    rows = xs.shape[0]
    d_ff = w1.shape[1]
    row_to_mod = lambda i: jnp.minimum(i // tiles_per_seq, n_batch)
    row_tile = pl.BlockSpec((tm, D_MODEL), lambda i, f: (i, 0))
    return pl.pallas_call(
        _ffn_kernel,
        grid=(rows // tm, d_ff // tf),
        in_specs=[
            row_tile,
            pl.BlockSpec((1, D_MODEL), lambda i, f: (0, 0)),
            _mod_spec(4, row_to_mod), _mod_spec(3, row_to_mod), _mod_spec(5, row_to_mod),
            pl.BlockSpec((D_MODEL, tf), lambda i, f: (0, f)),
            pl.BlockSpec((D_MODEL, tf), lambda i, f: (0, f)),
            pl.BlockSpec((tf, D_MODEL), lambda i, f: (f, 0)),
        ],
        out_specs=row_tile,
        out_shape=jax.ShapeDtypeStruct((rows, D_MODEL), F32),
        scratch_shapes=[pltpu.VMEM((tm, D_MODEL), BF16), pltpu.VMEM((tm, D_MODEL), F32)],
        compiler_params=pltpu.CompilerParams(
            dimension_semantics=("arbitrary", "arbitrary"), vmem_limit_bytes=VMEM_LIMIT),
        name="ffn_swiglu",
    )(xs, norm_g, mods, mods, mods, w1, w3, w2)


def _top2_gates(logits):
    lane = lax.broadcasted_iota(jnp.int32, logits.shape, 1)
    n_lanes = logits.shape[1]
    lg = jnp.where(lane < N_EXPERTS, logits, NEG_BIG)
    m1 = jnp.max(lg, axis=-1, keepdims=True)
    i1 = jnp.min(jnp.where(lg == m1, lane, n_lanes), axis=-1, keepdims=True)
    lg2 = jnp.where(lane == i1, NEG_BIG, lg)
    m2 = jnp.max(lg2, axis=-1, keepdims=True)
    i2 = jnp.min(jnp.where(lg2 == m2, lane, n_lanes), axis=-1, keepdims=True)
    e2 = jnp.exp(m2 - m1)
    w_top = 1.0 / (1.0 + e2)
    idx = jnp.where(lane == 0, i1, jnp.where(lane == 1, i2, 0))
    wts = jnp.where(lane == 0, w_top, jnp.where(lane == 1, e2 * w_top, 0.0))
    return idx, wts


def _router_kernel(x_ref, g_ref, sc_ref, sh_ref, rhi_ref, rlo_ref, h_ref, idx_ref, wts_ref):
    h = _modulated_norm(x_ref[...], g_ref[...], sc_ref[...], sh_ref[...])
    h_ref[...] = h
    h_hi, h_lo = _split_bf16(h)
    logits = jnp.dot(h_hi, rhi_ref[...], preferred_element_type=F32)
    logits += jnp.dot(h_lo, rhi_ref[...], preferred_element_type=F32)
    logits += jnp.dot(h_hi, rlo_ref[...], preferred_element_type=F32)
    idx_ref[...], wts_ref[...] = _top2_gates(logits)


def _router(xs, mods, norm_g, r_hi, r_lo, *, rows, tm, tiles_per_seq, n_batch):
    row_to_mod = lambda i: jnp.minimum(i // tiles_per_seq, n_batch)
    row_tile = pl.BlockSpec((tm, D_MODEL), lambda i: (i, 0))
    lanes = pl.BlockSpec((tm, 128), lambda i: (i, 0))
    router = pl.BlockSpec((D_MODEL, 128), lambda i: (0, 0))
    return pl.pallas_call(
        _router_kernel,
        grid=(rows // tm,),
        in_specs=[row_tile, pl.BlockSpec((1, D_MODEL), lambda i: (0, 0)),
                  _mod_spec(4, row_to_mod), _mod_spec(3, row_to_mod), router, router],
        out_specs=[row_tile, lanes, lanes],
        out_shape=[jax.ShapeDtypeStruct((rows, D_MODEL), F32),
                   jax.ShapeDtypeStruct((rows, 128), jnp.int32),
                   jax.ShapeDtypeStruct((rows, 128), F32)],
        compiler_params=pltpu.CompilerParams(
            dimension_semantics=("arbitrary",), vmem_limit_bytes=VMEM_LIMIT),
        name="moe_router",
    )(xs, norm_g, mods, mods, r_hi, r_lo)


def _route_plan(idx, n_tok, tm_e):
    n_pairs = 2 * n_tok
    n_tiles = n_pairs // tm_e + N_EXPERTS + 1
    n_rows = n_tiles * tm_e
    e_flat = idx.reshape(n_pairs)
    order = jnp.argsort(e_flat, stable=True).astype(jnp.int32)
    counts = jnp.sum((e_flat[:, None] == jnp.arange(N_EXPERTS, dtype=jnp.int32)[None, :]).astype(jnp.int32),
                     axis=0)
    padded = ((counts + tm_e - 1) // tm_e) * tm_e
    ends = jnp.cumsum(padded)
    tile_start = jnp.arange(n_tiles, dtype=jnp.int32) * tm_e
    tile_expert = jnp.minimum(jnp.sum((tile_start[:, None] >= ends[None, :]).astype(jnp.int32), axis=1),
                              N_EXPERTS - 1).astype(jnp.int32)
    n_used = (ends[-1] // tm_e).astype(jnp.int32).reshape(1)
    row = jnp.arange(n_rows, dtype=jnp.int32)
    row_expert = jnp.repeat(tile_expert, tm_e)
    offset = row - (ends - padded)[row_expert]
    valid = (offset < counts[row_expert]) & (row < ends[-1])
    pair = order[jnp.clip((jnp.cumsum(counts) - counts)[row_expert] + offset, 0, n_pairs - 1)]
    tok, choice = pair // 2, pair % 2
    src = jnp.where(valid, tok, 0)
    dump = n_pairs + jnp.cumsum(jnp.logical_not(valid).astype(jnp.int32)) - 1
    dst = jnp.where(valid, choice * n_tok + tok, dump)
    return tile_expert, n_used, src.reshape(n_tiles, 1, tm_e), dst.reshape(n_tiles, 1, tm_e)


def _experts_kernel(te_ref, nu_ref, src_ref, src_next_ref, dst_ref, h_hbm, w1_ref, w3_ref, w2_ref, y_hbm,
                    xbuf, xb_scr, acc_scr, ybuf, gsem, ssem, *, tm_e, n_chunks):
    t = pl.program_id(0)
    f = pl.program_id(1)
    nf = pl.num_programs(1)
    n_used = nu_ref[0]
    slot = t % 2

    def gather_rows(idx_ref, s):
        def body(j, carry):
            for prio in range(2):
                r = 2 * j + prio
                pltpu.make_async_copy(h_hbm.at[pl.ds(idx_ref[0, r], 1)], xbuf.at[s, pl.ds(r, 1)],
                                      gsem.at[s]).start(priority=prio)
            return carry
        lax.fori_loop(0, tm_e // 2, body, 0, unroll=4)

    def wait_gather(s):
        pltpu.make_async_copy(h_hbm.at[pl.ds(0, tm_e)], xbuf.at[s], gsem.at[s]).wait()

    def scatter_rows(s):
        def body(j, carry):
            for prio in range(2):
                r = 2 * j + prio
                pltpu.make_async_copy(ybuf.at[s, pl.ds(r, 1)], y_hbm.at[pl.ds(dst_ref[0, r], 1)],
                                      ssem.at[s]).start(priority=prio)
            return carry
        lax.fori_loop(0, tm_e // 2, body, 0, unroll=4)

    def wait_scatter(s):
        pltpu.make_async_copy(ybuf.at[s], y_hbm.at[pl.ds(0, tm_e)], ssem.at[s]).wait()

    @pl.when((f == 0) & (t <= n_used))
    def _():
        @pl.when(t == 0)
        def _():
            gather_rows(src_ref, 0)
        for s in range(2):
            @pl.when(slot == s)
            def _():
                wait_gather(s)

                @pl.when(t < n_used)
                def _():
                    xb_scr[...] = xbuf[s].astype(BF16)
                    acc_scr[...] = jnp.zeros_like(acc_scr)

    @pl.when(t < n_used)
    def _():
        x = xb_scr[...]
        a = jnp.dot(x, w1_ref[...].astype(BF16), preferred_element_type=F32)
        b = jnp.dot(x, w3_ref[...].astype(BF16), preferred_element_type=F32)
        acc_scr[...] += jnp.dot((_silu(a) * b).astype(BF16), w2_ref[...].astype(BF16),
                                preferred_element_type=F32)
        share = tm_e // n_chunks
        for i in range(share):
            r = f * share + i
            pltpu.make_async_copy(h_hbm.at[pl.ds(src_next_ref[0, r], 1)],
                                  xbuf.at[1 - slot, pl.ds(r, 1)], gsem.at[1 - slot]).start(priority=i % 2)

    @pl.when(f == nf - 1)
    def _():
        for s in range(2):
            @pl.when(slot == s)
            def _():
                @pl.when(t >= 2)
                def _():
                    wait_scatter(s)

                @pl.when(t < n_used)
                def _():
                    ybuf[s] = acc_scr[...]

                @pl.when(t >= n_used)
                def _():
                    ybuf[s] = jnp.zeros(ybuf.shape[1:], F32)

                scatter_rows(s)

                @pl.when(t == pl.num_programs(0) - 1)
                def _():
                    wait_scatter(s)
                    wait_scatter(1 - s)


def _experts(h, plan, w1, w3, w2, *, n_tok, tm_e, tf):
    tile_expert, n_used, src, dst = plan
    n_tiles = src.shape[0]
    d_ff = w1.shape[2]
    nf = d_ff // tf
    n_out = n_tiles * tm_e

    def chunk(t, f, nu):
        return jnp.where(t < nu[0], f, nf - 1)

    smem_tile = lambda shift: pl.BlockSpec(
        (None, 1, tm_e), lambda t, f, te, nu: (jnp.minimum(t + shift, n_tiles - 1), 0, 0),
        memory_space=pltpu.SMEM)
    grid_spec = pltpu.PrefetchScalarGridSpec(
        num_scalar_prefetch=2,
        grid=(n_tiles, nf),
        in_specs=[
            smem_tile(0), smem_tile(1), smem_tile(0),
            pl.BlockSpec(memory_space=pl.ANY),
            pl.BlockSpec((None, D_MODEL, tf), lambda t, f, te, nu: (te[t], 0, chunk(t, f, nu))),
            pl.BlockSpec((None, D_MODEL, tf), lambda t, f, te, nu: (te[t], 0, chunk(t, f, nu))),
            pl.BlockSpec((None, tf, D_MODEL), lambda t, f, te, nu: (te[t], chunk(t, f, nu), 0)),
        ],
        out_specs=pl.BlockSpec(memory_space=pl.ANY),
        scratch_shapes=[
            pltpu.VMEM((2, tm_e, D_MODEL), F32),
            pltpu.VMEM((tm_e, D_MODEL), BF16),
            pltpu.VMEM((tm_e, D_MODEL), F32),
            pltpu.VMEM((2, tm_e, D_MODEL), F32),
            pltpu.SemaphoreType.DMA((2,)),
            pltpu.SemaphoreType.DMA((2,)),
        ],
    )
    return pl.pallas_call(
        functools.partial(_experts_kernel, tm_e=tm_e, n_chunks=nf),
        grid_spec=grid_spec,
        out_shape=jax.ShapeDtypeStruct((n_out, D_MODEL), F32),
        compiler_params=pltpu.CompilerParams(
            dimension_semantics=("arbitrary", "arbitrary"), vmem_limit_bytes=VMEM_LIMIT),
        name="moe_experts",
    )(tile_expert, n_used, src, src, dst, h, w1, w3, w2)


def _combine_kernel(x_ref, y0_ref, y1_ref, wts_ref, gt_ref, fg_ref, o_ref):
    wts = wts_ref[...]
    moe = wts[:, 0:1] * y0_ref[...] + wts[:, 1:2] * y1_ref[...]
    y = x_ref[...] + gt_ref[...] * moe
    ms = jnp.mean(y * y, axis=-1, keepdims=True)
    o_ref[...] = y * lax.rsqrt(ms + EPS) * fg_ref[...]


def _combine(xs, y, wts, mods, final_g, *, rows, tm, tiles_per_seq, n_batch):
    row_to_mod = lambda i: jnp.minimum(i // tiles_per_seq, n_batch)
    row_tile = pl.BlockSpec((tm, D_MODEL), lambda i: (i, 0))
    return pl.pallas_call(
        _combine_kernel,
        grid=(rows // tm,),
        in_specs=[
            row_tile, row_tile,
            pl.BlockSpec((tm, D_MODEL), lambda i: (i + rows // tm, 0)),
            pl.BlockSpec((tm, 128), lambda i: (i, 0)),
            _mod_spec(5, row_to_mod),
            pl.BlockSpec((1, D_MODEL), lambda i: (0, 0)),
        ],
        out_specs=row_tile,
        out_shape=jax.ShapeDtypeStruct((rows, D_MODEL), F32),
        compiler_params=pltpu.CompilerParams(
            dimension_semantics=("arbitrary",), vmem_limit_bytes=VMEM_LIMIT),
        name="moe_combine",
    )(xs, y, y, wts, mods, final_g)


def _moe(xs, mods, norm_g, r_hi, r_lo, final_g, w1, w3, w2, *, rows, tm, tm_e, tf, tiles_per_seq, n_batch):
    h, idx, wts = _router(xs, mods, norm_g, r_hi, r_lo, rows=rows, tm=tm, tiles_per_seq=tiles_per_seq,
                          n_batch=n_batch)
    plan = _route_plan(idx[:, :2], rows, tm_e)
    y = _experts(h, plan, w1, w3, w2, n_tok=rows, tm_e=tm_e, tf=tf)
    return _combine(xs, y, wts, mods, final_g, rows=rows, tm=tm, tiles_per_seq=tiles_per_seq, n_batch=n_batch)


def _deinterleave(n=HEAD_DIM):
    return np.concatenate([np.arange(0, n, 2), np.arange(1, n, 2)])


def _proj_columns():
    de = _deinterleave()
    o_dq, o_gq, o_dk, o_dv, o_gk, o_gv, o_gate = 0, 1024, 2048, 3072, 4096, 4352, 4608
    cols = np.zeros(IN_W, np.int32)
    scale = np.ones(IN_W, np.float32)
    cols[GATE_OFF:GATE_OFF + 2048] = o_gate + np.arange(2048)
    for h in range(N_DIFF_HEADS):
        for c in range(2):
            dst = 128 * h + 64 * c
            cols[DQ_OFF + dst:DQ_OFF + dst + 64] = o_dq + dst + de
            cols[DK_OFF + dst:DK_OFF + dst + 64] = o_dk + dst + de
    scale[DQ_OFF:DQ_OFF + 1024] = HEAD_DIM ** -0.5
    for j in range(N_GQA_HEADS):
        cols[GQ_OFF + 64 * j:GQ_OFF + 64 * j + 64] = o_gq + 64 * j + de
    cols[DV_OFF:DV_OFF + 1024] = o_dv + np.arange(1024)
    for g in range(N_GQA_KV):
        cols[GKV_OFF + 128 * g:GKV_OFF + 128 * g + 64] = o_gk + 64 * g + de
        cols[GKV_OFF + 128 * g + 64:GKV_OFF + 128 * g + 128] = o_gv + 64 * g + np.arange(64)
    return cols, scale


def _rope_tables(S, pad_rows):
    rows = S // GRID_W
    row = jnp.repeat(jnp.arange(rows, dtype=F32), GRID_W)
    col = jnp.tile(jnp.arange(GRID_W, dtype=F32), rows)
    half = HEAD_DIM // 2
    inv_freq = ROPE_THETA ** (-jnp.arange(0, half, 2, dtype=F32) / half)
    ang = jnp.concatenate([row[:, None] * inv_freq, col[:, None] * inv_freq], axis=-1)
    cos, sin = jnp.cos(ang), jnp.sin(ang)
    reps = PROJ_TN // HEAD_DIM
    cos_t = jnp.tile(jnp.concatenate([cos, cos], axis=-1), (1, reps))
    sin_t = jnp.tile(jnp.concatenate([-sin, sin], axis=-1), (1, reps))
    cos_t = jnp.concatenate([cos_t, jnp.ones((pad_rows, PROJ_TN), F32)], axis=0)
    sin_t = jnp.concatenate([sin_t, jnp.zeros((pad_rows, PROJ_TN), F32)], axis=0)
    return cos_t, sin_t


def kernel(x, c, ctx, c_ctx, ada_w, ada_b, norm_attn_g, norm_ffn_g, w_in, q_norm_g, k_norm_g, diff_lambda,
           diff_subln_g, w_proj_diff, w_proj_gqa, w_out, ffn_w1, ffn_w3, ffn_w2, moe_router, moe_w1, moe_w3,
           moe_w2, final_norm_g):
    B, S, D = x.shape
    C = ctx.shape[1]
    depth = ada_w.shape[0]
    assert D == D_MODEL and depth == 2 and B + 1 <= MOD_ROWS
    assert w_in.shape[2] == IN_W and moe_router.shape[2] == N_EXPERTS
    tiles = _pick_tiles(B, S, C, ffn_w1.shape[2], moe_w1.shape[3])
    n_lat = B * S

    cvec = jnp.zeros((MOD_ROWS, D), F32).at[:B].set(c).at[B].set(c_ctx)
    mods_all = _ada_mods(cvec, ada_w, ada_b).reshape(depth, MOD_ROWS * N_MODS, 1, D)

    cols, col_scale = _proj_columns()
    de = _deinterleave()
    cos_t, sin_t = _rope_tables(S, tiles.tm_proj)
    blk = np.arange(PROJ_TN) // HEAD_DIM
    ones_bd = jnp.asarray(blk[:, None] == blk[None, :], BF16)
    col = np.arange(PROJ_TN)
    swap_bd = jnp.asarray(col[:, None] == (col[None, :] ^ (HEAD_DIM // 2)), BF16)
    is_key_lane = (np.arange(PROJ_TN) % (2 * HEAD_DIM)) < HEAD_DIM

    xs = (x.reshape(n_lat, D), ctx.reshape(B * C, D))
    for l in range(depth):
        last = l == depth - 1
        mods = mods_all[l]
        lam_init = 0.8 - 0.6 * math.exp(-0.3 * l)
        lq1, lk1, lq2, lk2 = diff_lambda[l]
        lam = (jnp.exp(jnp.sum(lq1 * lk1)) - jnp.exp(jnp.sum(lq2 * lk2)) + lam_init).reshape(1).astype(F32)
        w = (jnp.take(w_in[l], cols, axis=1) * col_scale).astype(BF16)
        qg = jnp.tile(q_norm_g[l][de] * HEAD_DIM ** -0.5, PROJ_TN // HEAD_DIM).reshape(1, PROJ_TN)
        kg = jnp.where(is_key_lane, jnp.tile(k_norm_g[l][de], PROJ_TN // HEAD_DIM), 1.0).reshape(1, PROJ_TN)
        subln = (diff_subln_g[l] * (1.0 - lam_init)).reshape(1, 2 * HEAD_DIM)

        p = _project(xs, mods, norm_attn_g[l].reshape(1, D), w, cos_t, sin_t, ones_bd, swap_bd, qg, kg,
                     tm=tiles.tm_proj, lat_tiles=n_lat // tiles.tm_proj, tiles_per_seq=S // tiles.tm_proj,
                     n_batch=B)
        y_lat, y_ctx = _attention(p, lam, subln, B=B, S=S, C=C, tiles=tiles, ctx_queries=not last)
        rows = n_lat if last else n_lat + B * C
        xs = _mixer_out(xs, y_lat, y_ctx, p, mods, w_proj_diff[l].astype(BF16), w_proj_gqa[l].astype(BF16),
                        w_out[l].astype(BF16), rows=rows, tm=tiles.tm_mix, tiles_per_seq=S // tiles.tm_mix,
                        n_batch=B)
        i = l // 2
        if l % 2 == 0:
            xs = _ffn(xs, mods, norm_ffn_g[l].reshape(1, D), ffn_w1[i].astype(BF16), ffn_w3[i].astype(BF16),
                      ffn_w2[i].astype(BF16), tm=tiles.tm_ffn, tf=tiles.tf_ffn,
                      tiles_per_seq=S // tiles.tm_ffn, n_batch=B)
        else:
            r_pad = jnp.zeros((D, 128), F32).at[:, :N_EXPERTS].set(moe_router[i])
            r_hi = r_pad.astype(BF16)
            r_lo = (r_pad - r_hi.astype(F32)).astype(BF16)
            xs = _moe(xs, mods, norm_ffn_g[l].reshape(1, D), r_hi, r_lo, final_norm_g.reshape(1, D),
                      moe_w1[i], moe_w3[i], moe_w2[i],
                      rows=rows, tm=tiles.tm_mix, tm_e=tiles.tm_moe, tf=tiles.tf_moe,
                      tiles_per_seq=S // tiles.tm_mix, n_batch=B)
    return xs.reshape(B, S, D)
```
